```python
import numpy as np
import jax
import jax.numpy as jnp
from jax import lax

D_MODEL = 1024
BATCH = 8
SEQ = 4096
DEPTH = 2

HEAD_DIM = 64
N_MIXERS = 4
GROUP_W = D_MODEL // N_MIXERS
N_HEADS = GROUP_W // HEAD_DIM
MIX_W = N_MIXERS * GROUP_W

RWKV_DECAY_RANK = 64
RWKV_AAA_RANK = 64
RWKV_GATE_RANK = 128
RWKV_DECAY_SCALE = 0.606531
RWKV_GN_EPS = 64e-5

CONV_W = 3

DIL_CFG = ((128, 1), (512, 4), (2048, 16))

CMP_LEN = 32
CMP_STRIDE = 16
CMP_HIDDEN = 256
SEL_BLK = 64
SEL_TOP = 16
SWA_WIN = 512
FORCE_SCORE = 1e4

N_EXPERTS = 32
TOP_K = 4
EXPERT_FF = D_MODEL
SWIGLU_LIMIT = 7.0
SWIGLU_ALPHA = 1.702
MOE_BLK = 128

ROPE_THETA = 10000.0
NORM_EPS = 1e-6
QBLK = 128
NEG = -1e30

A_WIDTHS = (GROUP_W, GROUP_W, GROUP_W, RWKV_DECAY_RANK, RWKV_AAA_RANK, RWKV_GATE_RANK)
B_WIDTHS = (GROUP_W, GROUP_W, GROUP_W)
C_WIDTHS = (GROUP_W, GROUP_W, GROUP_W)
D_WIDTHS = (GROUP_W, HEAD_DIM, HEAD_DIM, HEAD_DIM, HEAD_DIM, HEAD_DIM, HEAD_DIM, 3 * N_HEADS)
SECTION_WIDTHS = (sum(A_WIDTHS), sum(B_WIDTHS), sum(C_WIDTHS), sum(D_WIDTHS))
PROJ_W = sum(SECTION_WIDTHS)

kernel_name = 'hybrid_rwkv7_shortconv_dilattn_nsa_moe'


def split_cols(z, widths):
    offs = np.cumsum((0,) + tuple(widths))
    return [z[..., int(offs[i]):int(offs[i + 1])] for i in range(len(widths))]


def rmsnorm(x, g):
    xf = x.astype(jnp.float32)
    y = xf * lax.rsqrt(jnp.mean(xf * xf, axis=-1, keepdims=True) + NORM_EPS)
    return (y * g.astype(jnp.float32)).astype(x.dtype)


def rope(x, pos):
    half = x.shape[-1] // 2
    inv = ROPE_THETA ** (-jnp.arange(half, dtype=jnp.float32) / half)
    ang = pos.astype(jnp.float32)[:, :, None, None] * inv
    cos, sin = jnp.cos(ang), jnp.sin(ang)
    xf = x.astype(jnp.float32)
    x1, x2 = xf[..., :half], xf[..., half:]
    return jnp.concatenate([x1 * cos - x2 * sin, x2 * cos + x1 * sin], axis=-1).astype(x.dtype)


def token_shift(z):
    return jnp.pad(z, ((0, 0), (1, 0), (0, 0)))[:, :-1]


def banded_attention(q, k, v, window):
    B, L, Hq, hd = q.shape
    Hk = k.shape[2]
    G = Hq // Hk
    nb = -(-L // QBLK)
    Lp = nb * QBLK
    npv = -(-window // QBLK)
    pad = ((0, 0), (0, Lp - L), (0, 0), (0, 0))
    q, k, v = jnp.pad(q, pad), jnp.pad(k, pad), jnp.pad(v, pad)
    bpad = ((0, 0), (npv, 0), (0, 0), (0, 0), (0, 0))
    kb = jnp.pad(k.reshape(B, nb, QBLK, Hk, hd), bpad)
    vb = jnp.pad(v.reshape(B, nb, QBLK, Hk, hd), bpad)
    bidx = np.arange(nb)[:, None] + np.arange(npv + 1)[None, :]
    J = (npv + 1) * QBLK
    kband = kb[:, bidx].reshape(B, nb, J, Hk, hd)
    vband = vb[:, bidx].reshape(B, nb, J, Hk, hd)
    qb = q.reshape(B, nb, QBLK, Hk, G, hd)
    s = jnp.einsum('bnqkgd,bnjkd->bnkgqj', qb, kband).astype(jnp.float32) * (hd ** -0.5)
    qpos = np.arange(nb)[:, None] * QBLK + np.arange(QBLK)[None, :]
    kpos = (np.arange(nb)[:, None] - npv) * QBLK + np.arange(J)[None, :]
    dist = qpos[:, :, None] - kpos[:, None, :]
    mask = (dist >= 0) & (dist <= window) & (kpos[:, None, :] >= 0)
    s = jnp.where(mask[None, :, None, None], s, NEG)
    lse = jax.nn.logsumexp(s, axis=-1)
    p = jnp.exp(s - lse[..., None]).astype(v.dtype)
    o = jnp.einsum('bnkgqj,bnjkd->bnqkgd', p, vband).reshape(B, Lp, Hq, hd)[:, :L]
    lse = jnp.transpose(lse, (0, 1, 4, 2, 3)).reshape(B, Lp, Hq)[:, :L]
    return o, lse


def decimate(t, d):
    B, S = t.shape[:2]
    t = t.reshape((B, S // d, d) + t.shape[2:])
    return jnp.moveaxis(t, 2, 1).reshape((B * d, S // d) + t.shape[3:])


def undecimate(t, d, B):
    Sd = t.shape[1]
    t = t.reshape((B, d, Sd) + t.shape[2:])
    return jnp.moveaxis(t, 1, 2).reshape((B, Sd * d) + t.shape[3:])


def rwkv7_scan(r, w, k, v, kk, b):
    B, S, H, N = r.shape
    tm = lambda t: jnp.moveaxis(t, 1, 0)

    def step(st, inp):
        r_t, w_t, k_t, v_t, kk_t, b_t = inp
        sa = jnp.einsum('bhij,bhj->bhi', st, -kk_t)
        st = st * w_t[:, :, None, :] + sa[..., None] * b_t[:, :, None, :] + v_t[..., None] * k_t[:, :, None, :]
        return st, jnp.einsum('bhij,bhj->bhi', st, r_t)

    s0 = jnp.zeros((B, H, N, N), jnp.float32)
    _, ys = lax.scan(step, s0, (tm(r), tm(w), tm(k), tm(v), tm(kk), tm(b)))
    return jnp.moveaxis(ys, 0, 1)


def rwkv7_mix(r, k, v, wd, ad, gd, w0, w2, a0, a2, g2, k_k, k_a, r_k, gn_w, gn_b):
    B, S, C = r.shape
    H = C // HEAD_DIM
    f32 = jnp.float32
    heads = lambda t: t.reshape(B, S, H, HEAD_DIM).astype(f32)
    w = jnp.exp(-RWKV_DECAY_SCALE * jax.nn.sigmoid((w0 + jnp.tanh(wd) @ w2).astype(f32)))
    a = jax.nn.sigmoid(a0 + ad @ a2)
    g = jax.nn.sigmoid(gd) @ g2
    kk = heads(k * k_k)
    kk = kk * lax.rsqrt(jnp.sum(kk * kk, axis=-1, keepdims=True) + 1e-12)
    k = k * (1.0 + (a - 1.0) * k_a)
    y = rwkv7_scan(heads(r), heads(w), heads(k), heads(v), kk, kk * heads(a))
    mu = jnp.mean(y, axis=-1, keepdims=True)
    var = jnp.mean(jnp.square(y - mu), axis=-1, keepdims=True)
    y = (y - mu) * lax.rsqrt(var + RWKV_GN_EPS) * gn_w.reshape(H, HEAD_DIM).astype(f32) + gn_b.reshape(H, HEAD_DIM).astype(f32)
    y = y + jnp.sum(heads(r) * heads(k) * r_k.astype(f32), axis=-1, keepdims=True) * heads(v)
    return (y.reshape(B, S, C) * g.astype(f32)).astype(r.dtype)


def short_conv_mix(bg, cg, xin, conv_w):
    u = cg * xin
    y = lax.conv_general_dilated(u, conv_w[:, None, :].astype(u.dtype), window_strides=(1,),
                                 padding=((CONV_W - 1, 0),), dimension_numbers=('NWC', 'WIO', 'NWC'),
                                 feature_group_count=u.shape[-1])
    return bg * y


def dilated_mix(q, k, v, q_g, k_g, pos):
    B, S, H, hd = q.shape
    q = rope(rmsnorm(q, q_g), pos)
    k = rope(rmsnorm(k, k_g), pos)
    outs, lses = [], []
    for window, dil in DIL_CFG:
        o, lse = banded_attention(decimate(q, dil), decimate(k, dil), decimate(v, dil), window // dil)
        outs.append(undecimate(o, dil, B))
        lses.append(undecimate(lse, dil, B))
    wts = jax.nn.softmax(jnp.stack(lses), axis=0)
    o = jnp.einsum('cbsh,cbshd->bshd', wts.astype(q.dtype), jnp.stack(outs))
    return o.reshape(B, S, H * hd)


def selection_overlap(n_c, n_sel):
    r = SEL_BLK // CMP_STRIDE
    m = CMP_LEN // CMP_STRIDE
    diff = np.arange(n_c)[:, None] - r * np.arange(n_sel)[None, :]
    offs = (np.arange(r)[:, None] - np.arange(m)[None, :]).reshape(-1)
    return (diff[..., None] == offs).sum(-1).astype(np.float32)


def nsa_mix(q, kc, vc, ks, vs, kw, vw, gl, q_g, kc_g, ks_g, kw_g, pe_k, pe_v, wk1, wk2, wv1, wv2, pos):
    B, S, H, hd = q.shape
    f32 = jnp.float32
    scale = hd ** -0.5
    q_n = rmsnorm(q, q_g)
    q_r = rope(q_n, pos)
    qidx = np.arange(S)
    n_c = (S - CMP_LEN) // CMP_STRIDE + 1
    cidx = np.arange(n_c)[:, None] * CMP_STRIDE + np.arange(CMP_LEN)[None, :]

    def compress(t, pe, w1, w2):
        blk = (t[:, cidx] + pe).reshape(B, n_c, CMP_LEN * hd)
        return jax.nn.gelu(blk @ w1) @ w2

    k_cmp = rmsnorm(compress(kc, pe_k, wk1, wk2), kc_g)
    v_cmp = compress(vc, pe_v, wv1, wv2)
    s = jnp.einsum('bshd,bcd->bhsc', q_n, k_cmp).astype(f32) * scale
    valid = cidx[:, -1][None, :] <= qidx[:, None]
    s = jnp.where(valid, s, NEG)
    p_cmp = jax.nn.softmax(s, axis=-1) * valid.any(-1, keepdims=True).astype(np.float32)
    o_cmp = jnp.einsum('bhsc,bcd->bshd', p_cmp.astype(v_cmp.dtype), v_cmp)
    n_sel = S // SEL_BLK
    n_top = min(SEL_TOP, n_sel)
    imp = jnp.einsum('bhsc,cj->bsj', p_cmp, selection_overlap(n_c, n_sel))
    cur = qidx[:, None] // SEL_BLK
    jb = np.arange(n_sel)[None, :]
    future = jb > cur
    forced = (jb == 0) | (jb == cur) | (jb == cur - 1)
    score = jnp.where(future, -1.0, jnp.where(forced, FORCE_SCORE, imp))
    _, sel = lax.top_k(score, n_top)
    ks_r = rope(rmsnorm(ks[:, :, None, :], ks_g), pos)[:, :, 0]
    kb = ks_r.reshape(B, n_sel, SEL_BLK, hd)
    vb = vs.reshape(B, n_sel, SEL_BLK, hd)
    nq = S // QBLK
    gather = jax.vmap(lambda t, i: t[i])

    def sel_block(args):
        qc, ic, tc = args
        kg = gather(kb, ic).reshape(B, QBLK, n_top * SEL_BLK, hd)
        vg = gather(vb, ic).reshape(B, QBLK, n_top * SEL_BLK, hd)
        kpos = (ic[..., None] * SEL_BLK + jnp.arange(SEL_BLK)).reshape(B, QBLK, n_top * SEL_BLK)
        sc = jnp.einsum('bqhd,bqjd->bhqj', qc, kg).astype(f32) * scale
        sc = jnp.where((kpos <= tc[None, :, None])[:, None], sc, NEG)
        pc = jax.nn.softmax(sc, axis=-1).astype(vg.dtype)
        return jnp.einsum('bhqj,bqjd->bqhd', pc, vg)

    o_slc = lax.map(sel_block, (jnp.moveaxis(q_r.reshape(B, nq, QBLK, H, hd), 1, 0),
                                jnp.moveaxis(sel.reshape(B, nq, QBLK, n_top), 1, 0),
                                jnp.asarray(qidx.reshape(nq, QBLK))))
    o_slc = jnp.moveaxis(o_slc, 0, 1).reshape(B, S, H, hd)
    kw_r = rope(rmsnorm(kw[:, :, None, :], kw_g), pos)
    o_swa, _ = banded_attention(q_r, kw_r, vw[:, :, None, :], SWA_WIN - 1)
    gt = jax.nn.sigmoid(gl.astype(f32)).reshape(B, S, H, 3).astype(q.dtype)
    o = gt[..., 0:1] * o_cmp + gt[..., 1:2] * o_slc + gt[..., 2:3] * o_swa
    return o.reshape(B, S, H * hd)


def hybrid_mixer(h, positions, w_in, w_out, rwkv_mu, rwkv_w0, rwkv_w2, rwkv_a0, rwkv_a2, rwkv_g2,
                 rwkv_kk, rwkv_ka, rwkv_rk, rwkv_gn_w, rwkv_gn_b, conv_w, dil_q_g, dil_k_g,
                 nsa_q_g, nsa_kc_g, nsa_ks_g, nsa_kw_g, nsa_pe_k, nsa_pe_v, nsa_wk1, nsa_wk2,
                 nsa_wv1, nsa_wv2, onorm_g):
    B, S, _ = h.shape
    proj = h @ w_in
    za, zb, zc, zd = split_cols(proj, SECTION_WIDTHS)
    za = za + (token_shift(za) - za) * rwkv_mu
    r, k, v, wd, ad, gd = split_cols(za, A_WIDTHS)
    y_a = rwkv7_mix(r, k, v, wd, ad, gd, rwkv_w0, rwkv_w2, rwkv_a0, rwkv_a2, rwkv_g2,
                    rwkv_kk, rwkv_ka, rwkv_rk, rwkv_gn_w, rwkv_gn_b)
    bg, cg, xin = split_cols(zb, B_WIDTHS)
    y_b = short_conv_mix(bg, cg, xin, conv_w)
    cq, ck, cv = [t.reshape(B, S, N_HEADS, HEAD_DIM) for t in split_cols(zc, C_WIDTHS)]
    y_c = dilated_mix(cq, ck, cv, dil_q_g, dil_k_g, positions)
    dq, dkc, dvc, dks, dvs, dkw, dvw, dg = split_cols(zd, D_WIDTHS)
    y_d = nsa_mix(dq.reshape(B, S, N_HEADS, HEAD_DIM), dkc, dvc, dks, dvs, dkw, dvw, dg,
                  nsa_q_g, nsa_kc_g, nsa_ks_g, nsa_kw_g, nsa_pe_k, nsa_pe_v,
                  nsa_wk1, nsa_wk2, nsa_wv1, nsa_wv2, positions)
    y_bcd = jnp.concatenate([y_b, y_c, y_d], axis=-1).reshape(B, S, 3 * N_HEADS, HEAD_DIM)
    y_bcd = rmsnorm(y_bcd, onorm_g.reshape(3 * N_HEADS, HEAD_DIM)).reshape(B, S, 3 * GROUP_W)
    return jnp.concatenate([y_a, y_bcd], axis=-1) @ w_out


def moe(h, w_router, b_router, w_gu, b_gu, w_down, b_down):
    B, S, D = h.shape
    T = B * S
    TK = T * TOP_K
    xt = h.reshape(T, D)
    logits = (xt @ w_router + b_router).astype(jnp.float32)
    top_val, top_idx = lax.top_k(logits, TOP_K)
    gates = jax.nn.softmax(top_val, axis=-1)
    e_flat = top_idx.reshape(TK)
    order = jnp.argsort(e_flat)
    e_s = e_flat[order]
    tok_s = order // TOP_K
    g_s = gates.reshape(TK)[order]
    counts = jnp.bincount(e_flat, length=N_EXPERTS)
    padded = (counts + MOE_BLK - 1) // MOE_BLK * MOE_BLK
    pad_end = jnp.cumsum(padded)
    pad_start = pad_end - padded
    start = jnp.cumsum(counts) - counts
    dest = pad_start[e_s] + jnp.arange(TK) - start[e_s]
    n_rows = TK + N_EXPERTS * MOE_BLK
    n_blk = n_rows // MOE_BLK
    buf = jnp.zeros((n_rows, D), h.dtype).at[dest].set(xt[tok_s])
    blk_e = jnp.minimum(jnp.searchsorted(pad_end, jnp.arange(n_blk) * MOE_BLK, side='right'), N_EXPERTS - 1)

    def expert_block(args):
        xb, e = args
        gu = xb @ w_gu[e] + b_gu[e]
        gate, up = gu[:, :EXPERT_FF], gu[:, EXPERT_FF:]
        gate = jnp.minimum(gate, SWIGLU_LIMIT)
        up = jnp.clip(up, -SWIGLU_LIMIT, SWIGLU_LIMIT)
        act = gate * jax.nn.sigmoid(SWIGLU_ALPHA * gate) * (up + 1.0)
        return act @ w_down[e] + b_down[e]

    ybuf = lax.map(expert_block, (buf.reshape(n_blk, MOE_BLK, D), blk_e)).reshape(n_rows, D)
    y = ybuf[dest] * g_s[:, None].astype(h.dtype)
    return jax.ops.segment_sum(y, tok_s, num_segments=T).reshape(B, S, D)


def setup_inputs(seed: int = 0) -> dict:
    key = jax.random.key(seed)
    ks = iter(jax.random.split(key, 64))

    def nrm(shape, scale):
        return scale * jax.random.normal(next(ks), shape, jnp.float32)

    L, H, hd, D, F, E = DEPTH, N_HEADS, HEAD_DIM, D_MODEL, EXPERT_FF, N_EXPERTS
    x = nrm((BATCH, SEQ, D), 1.0)
    c = nrm((BATCH, D), 1.0)
    positions = (jax.random.randint(next(ks), (BATCH, 1), 0, 1024) + jnp.arange(SEQ)[None, :]).astype(jnp.int32)
    return {
        'x': x,
        'c': c,
        'positions': positions,
        'w_ada': nrm((L, D, 6 * D), 0.5 * D ** -0.5),
        'b_ada': nrm((L, 6 * D), 0.01),
        'norm1_g': 1.0 + nrm((L, D), 0.05),
        'norm2_g': 1.0 + nrm((L, D), 0.05),
        'w_in': nrm((L, D, PROJ_W), D ** -0.5),
        'w_out': nrm((L, MIX_W, D), MIX_W ** -0.5),
        'rwkv_mu': jax.random.uniform(next(ks), (L, SECTION_WIDTHS[0]), jnp.float32),
        'rwkv_w0': nrm((L, GROUP_W), 0.5),
        'rwkv_w2': nrm((L, RWKV_DECAY_RANK, GROUP_W), 0.5 * RWKV_DECAY_RANK ** -0.5),
        'rwkv_a0': nrm((L, GROUP_W), 0.5),
        'rwkv_a2': nrm((L, RWKV_AAA_RANK, GROUP_W), 0.5 * RWKV_AAA_RANK ** -0.5),
        'rwkv_g2': nrm((L, RWKV_GATE_RANK, GROUP_W), RWKV_GATE_RANK ** -0.5),
        'rwkv_kk': 0.85 + nrm((L, GROUP_W), 0.05),
        'rwkv_ka': 1.0 + nrm((L, GROUP_W), 0.05),
        'rwkv_rk': nrm((L, H, hd), 0.1),
        'rwkv_gn_w': 1.0 + nrm((L, GROUP_W), 0.05),
        'rwkv_gn_b': nrm((L, GROUP_W), 0.01),
        'conv_w': nrm((L, CONV_W, GROUP_W), CONV_W ** -0.5),
        'dil_q_g': 1.0 + nrm((L, hd), 0.05),
        'dil_k_g': 1.0 + nrm((L, hd), 0.05),
        'nsa_q_g': 1.0 + nrm((L, hd), 0.05),
        'nsa_kc_g': 1.0 + nrm((L, hd), 0.05),
        'nsa_ks_g': 1.0 + nrm((L, hd), 0.05),
        'nsa_kw_g': 1.0 + nrm((L, hd), 0.05),
        'nsa_pe_k': nrm((L, CMP_LEN, hd), 0.1),
        'nsa_pe_v': nrm((L, CMP_LEN, hd), 0.1),
        'nsa_wk1': nrm((L, CMP_LEN * hd, CMP_HIDDEN), (CMP_LEN * hd) ** -0.5),
        'nsa_wk2': nrm((L, CMP_HIDDEN, hd), CMP_HIDDEN ** -0.5),
        'nsa_wv1': nrm((L, CMP_LEN * hd, CMP_HIDDEN), (CMP_LEN * hd) ** -0.5),
        'nsa_wv2': nrm((L, CMP_HIDDEN, hd), CMP_HIDDEN ** -0.5),
        'onorm_g': 1.0 + nrm((L, 3 * GROUP_W), 0.05),
        'w_router': nrm((L, D, E), D ** -0.5),
        'b_router': nrm((L, E), 0.01),
        'w_gu': nrm((L, E, D, 2 * F), D ** -0.5),
        'b_gu': nrm((L, E, 2 * F), 0.01),
        'w_down': nrm((L, E, F, D), F ** -0.5),
        'b_down': nrm((L, E, D), 0.01),
    }


def reference(x, c, positions, w_ada, b_ada, norm1_g, norm2_g, w_in, w_out, rwkv_mu, rwkv_w0,
              rwkv_w2, rwkv_a0, rwkv_a2, rwkv_g2, rwkv_kk, rwkv_ka, rwkv_rk, rwkv_gn_w, rwkv_gn_b,
              conv_w, dil_q_g, dil_k_g, nsa_q_g, nsa_kc_g, nsa_ks_g, nsa_kw_g, nsa_pe_k, nsa_pe_v,
              nsa_wk1, nsa_wk2, nsa_wv1, nsa_wv2, onorm_g, w_router, b_router, w_gu, b_gu,
              w_down, b_down):
    for l in range(DEPTH):
        mod = jax.nn.silu(c) @ w_ada[l] + b_ada[l]
        sh1, sc1, gt1, sh2, sc2, gt2 = [m[:, None, :] for m in jnp.split(mod, 6, axis=-1)]
        h = rmsnorm(x, norm1_g[l]) * (1.0 + sc1) + sh1
        y = hybrid_mixer(h, positions, w_in[l], w_out[l], rwkv_mu[l], rwkv_w0[l], rwkv_w2[l],
                         rwkv_a0[l], rwkv_a2[l], rwkv_g2[l], rwkv_kk[l], rwkv_ka[l], rwkv_rk[l],
                         rwkv_gn_w[l], rwkv_gn_b[l], conv_w[l], dil_q_g[l], dil_k_g[l],
                         nsa_q_g[l], nsa_kc_g[l], nsa_ks_g[l], nsa_kw_g[l], nsa_pe_k[l], nsa_pe_v[l],
                         nsa_wk1[l], nsa_wk2[l], nsa_wv1[l], nsa_wv2[l], onorm_g[l])
        x = x + gt1 * y
        h = rmsnorm(x, norm2_g[l]) * (1.0 + sc2) + sh2
        x = x + gt2 * moe(h, w_router[l], b_router[l], w_gu[l], b_gu[l], w_down[l], b_down[l])
    return x
```

```python
import functools

import numpy as np
import jax
import jax.numpy as jnp
from jax import lax
from jax.experimental import pallas as pl
from jax.experimental.pallas import tpu as pltpu

D_MODEL = 1024
DEPTH = 2

HEAD_DIM = 64
N_MIXERS = 4
GROUP_W = D_MODEL // N_MIXERS
N_HEADS = GROUP_W // HEAD_DIM
MIX_W = N_MIXERS * GROUP_W

RWKV_DECAY_RANK = 64
RWKV_AAA_RANK = 64
RWKV_GATE_RANK = 128
RWKV_DECAY_SCALE = 0.606531
RWKV_GN_EPS = 64e-5

CONV_W = 3

DIL_CFG = ((128, 1), (512, 4), (2048, 16))

CMP_LEN = 32
CMP_STRIDE = 16
CMP_HIDDEN = 256
SEL_BLK = 64
SEL_TOP = 16
SWA_WIN = 512
FORCE_SCORE = 1e4

N_EXPERTS = 32
TOP_K = 4
EXPERT_FF = D_MODEL
SWIGLU_LIMIT = 7.0
SWIGLU_ALPHA = 1.702
MOE_BLK = 128

ROPE_THETA = 10000.0
NORM_EPS = 1e-6
QBLK = 128
NEG = -1e30

A_WIDTHS = (GROUP_W, GROUP_W, GROUP_W, RWKV_DECAY_RANK, RWKV_AAA_RANK, RWKV_GATE_RANK)
B_WIDTHS = (GROUP_W, GROUP_W, GROUP_W)
C_WIDTHS = (GROUP_W, GROUP_W, GROUP_W)
D_WIDTHS = (GROUP_W, HEAD_DIM, HEAD_DIM, HEAD_DIM, HEAD_DIM, HEAD_DIM, HEAD_DIM, 3 * N_HEADS)
SECTION_WIDTHS = (sum(A_WIDTHS), sum(B_WIDTHS), sum(C_WIDTHS), sum(D_WIDTHS))
PROJ_W = sum(SECTION_WIDTHS)


def _matmul_kernel(x_ref, w_ref, o_ref):
    o_ref[...] = jnp.dot(x_ref[...].astype(jnp.bfloat16), w_ref[...],
                         preferred_element_type=jnp.float32)


def matmul(x, w, tm=512, tn=256):
    M, K = x.shape
    N = w.shape[1]
    n_pad = -(-N // tn) * tn
    wb = jnp.pad(w.astype(jnp.bfloat16), ((0, 0), (0, n_pad - N)))
    out = pl.pallas_call(
        _matmul_kernel,
        grid=(M // tm, n_pad // tn),
        in_specs=[pl.BlockSpec((tm, K), lambda i, j: (i, 0)),
                  pl.BlockSpec((K, tn), lambda i, j: (0, j))],
        out_specs=pl.BlockSpec((tm, tn), lambda i, j: (i, j)),
        out_shape=jax.ShapeDtypeStruct((M, n_pad), jnp.float32),
        name="matmul",
    )(x, wb)
    return out[:, :N]


def split_cols(z, widths):
    offs = np.cumsum((0,) + tuple(widths))
    return [z[..., int(offs[i]):int(offs[i + 1])] for i in range(len(widths))]


def rmsnorm(x, g):
    xf = x.astype(jnp.float32)
    y = xf * lax.rsqrt(jnp.mean(xf * xf, axis=-1, keepdims=True) + NORM_EPS)
    return (y * g.astype(jnp.float32)).astype(x.dtype)


def rope(x, pos):
    half = x.shape[-1] // 2
    inv = ROPE_THETA ** (-jnp.arange(half, dtype=jnp.float32) / half)
    ang = pos.astype(jnp.float32)[:, :, None, None] * inv
    cos, sin = jnp.cos(ang), jnp.sin(ang)
    xf = x.astype(jnp.float32)
    x1, x2 = xf[..., :half], xf[..., half:]
    return jnp.concatenate([x1 * cos - x2 * sin, x2 * cos + x1 * sin], axis=-1).astype(x.dtype)


def token_shift(z):
    return jnp.pad(z, ((0, 0), (1, 0), (0, 0)))[:, :-1]


def banded_attention(q, k, v, window):
    B, L, Hq, hd = q.shape
    Hk = k.shape[2]
    G = Hq // Hk
    nb = -(-L // QBLK)
    Lp = nb * QBLK
    npv = -(-window // QBLK)
    pad = ((0, 0), (0, Lp - L), (0, 0), (0, 0))
    q, k, v = jnp.pad(q, pad), jnp.pad(k, pad), jnp.pad(v, pad)
    bpad = ((0, 0), (npv, 0), (0, 0), (0, 0), (0, 0))
    kb = jnp.pad(k.reshape(B, nb, QBLK, Hk, hd), bpad)
    vb = jnp.pad(v.reshape(B, nb, QBLK, Hk, hd), bpad)
    bidx = np.arange(nb)[:, None] + np.arange(npv + 1)[None, :]
    J = (npv + 1) * QBLK
    kband = kb[:, bidx].reshape(B, nb, J, Hk, hd)
    vband = vb[:, bidx].reshape(B, nb, J, Hk, hd)
    qb = q.reshape(B, nb, QBLK, Hk, G, hd)
    s = jnp.einsum('bnqkgd,bnjkd->bnkgqj', qb, kband).astype(jnp.float32) * (hd ** -0.5)
    qpos = np.arange(nb)[:, None] * QBLK + np.arange(QBLK)[None, :]
    kpos = (np.arange(nb)[:, None] - npv) * QBLK + np.arange(J)[None, :]
    dist = qpos[:, :, None] - kpos[:, None, :]
    mask = (dist >= 0) & (dist <= window) & (kpos[:, None, :] >= 0)
    s = jnp.where(mask[None, :, None, None], s, NEG)
    lse = jax.nn.logsumexp(s, axis=-1)
    p = jnp.exp(s - lse[..., None]).astype(v.dtype)
    o = jnp.einsum('bnkgqj,bnjkd->bnqkgd', p, vband).reshape(B, Lp, Hq, hd)[:, :L]
    lse = jnp.transpose(lse, (0, 1, 4, 2, 3)).reshape(B, Lp, Hq)[:, :L]
    return o, lse


def decimate(t, d):
    B, S = t.shape[:2]
    t = t.reshape((B, S // d, d) + t.shape[2:])
    return jnp.moveaxis(t, 2, 1).reshape((B * d, S // d) + t.shape[3:])


def undecimate(t, d, B):
    Sd = t.shape[1]
    t = t.reshape((B, d, Sd) + t.shape[2:])
    return jnp.moveaxis(t, 1, 2).reshape((B, Sd * d) + t.shape[3:])


def rwkv7_scan(r, w, k, v, kk, b):
    B, S, H, N = r.shape
    tm = lambda t: jnp.moveaxis(t, 1, 0)

    def step(st, inp):
        r_t, w_t, k_t, v_t, kk_t, b_t = inp
        sa = jnp.einsum('bhij,bhj->bhi', st, -kk_t)
        st = st * w_t[:, :, None, :] + sa[..., None] * b_t[:, :, None, :] + v_t[..., None] * k_t[:, :, None, :]
        return st, jnp.einsum('bhij,bhj->bhi', st, r_t)

    s0 = jnp.zeros((B, H, N, N), jnp.float32)
    _, ys = lax.scan(step, s0, (tm(r), tm(w), tm(k), tm(v), tm(kk), tm(b)))
    return jnp.moveaxis(ys, 0, 1)


def rwkv7_mix(r, k, v, wd, ad, gd, w0, w2, a0, a2, g2, k_k, k_a, r_k, gn_w, gn_b):
    B, S, C = r.shape
    H = C // HEAD_DIM
    f32 = jnp.float32
    heads = lambda t: t.reshape(B, S, H, HEAD_DIM).astype(f32)
    w = jnp.exp(-RWKV_DECAY_SCALE * jax.nn.sigmoid((w0 + jnp.tanh(wd) @ w2).astype(f32)))
    a = jax.nn.sigmoid(a0 + ad @ a2)
    g = jax.nn.sigmoid(gd) @ g2
    kk = heads(k * k_k)
    kk = kk * lax.rsqrt(jnp.sum(kk * kk, axis=-1, keepdims=True) + 1e-12)
    k = k * (1.0 + (a - 1.0) * k_a)
    y = rwkv7_scan(heads(r), heads(w), heads(k), heads(v), kk, kk * heads(a))
    mu = jnp.mean(y, axis=-1, keepdims=True)
    var = jnp.mean(jnp.square(y - mu), axis=-1, keepdims=True)
    y = (y - mu) * lax.rsqrt(var + RWKV_GN_EPS) * gn_w.reshape(H, HEAD_DIM).astype(f32) + gn_b.reshape(H, HEAD_DIM).astype(f32)
    y = y + jnp.sum(heads(r) * heads(k) * r_k.astype(f32), axis=-1, keepdims=True) * heads(v)
    return (y.reshape(B, S, C) * g.astype(f32)).astype(r.dtype)


def short_conv_mix(bg, cg, xin, conv_w):
    u = cg * xin
    y = lax.conv_general_dilated(u, conv_w[:, None, :].astype(u.dtype), window_strides=(1,),
                                 padding=((CONV_W - 1, 0),), dimension_numbers=('NWC', 'WIO', 'NWC'),
                                 feature_group_count=u.shape[-1])
    return bg * y


def dilated_mix(q, k, v, q_g, k_g, pos):
    B, S, H, hd = q.shape
    q = rope(rmsnorm(q, q_g), pos)
    k = rope(rmsnorm(k, k_g), pos)
    outs, lses = [], []
    for window, dil in DIL_CFG:
        o, lse = banded_attention(decimate(q, dil), decimate(k, dil), decimate(v, dil), window // dil)
        outs.append(undecimate(o, dil, B))
        lses.append(undecimate(lse, dil, B))
    wts = jax.nn.softmax(jnp.stack(lses), axis=0)
    o = jnp.einsum('cbsh,cbshd->bshd', wts.astype(q.dtype), jnp.stack(outs))
    return o.reshape(B, S, H * hd)


def selection_overlap(n_c, n_sel):
    r = SEL_BLK // CMP_STRIDE
    m = CMP_LEN // CMP_STRIDE
    diff = np.arange(n_c)[:, None] - r * np.arange(n_sel)[None, :]
    offs = (np.arange(r)[:, None] - np.arange(m)[None, :]).reshape(-1)
    return (diff[..., None] == offs).sum(-1).astype(np.float32)


def nsa_mix(q, kc, vc, ks, vs, kw, vw, gl, q_g, kc_g, ks_g, kw_g, pe_k, pe_v, wk1, wk2, wv1, wv2, pos):
    B, S, H, hd = q.shape
    f32 = jnp.float32
    scale = hd ** -0.5
    q_n = rmsnorm(q, q_g)
    q_r = rope(q_n, pos)
    qidx = np.arange(S)
    n_c = (S - CMP_LEN) // CMP_STRIDE + 1
    cidx = np.arange(n_c)[:, None] * CMP_STRIDE + np.arange(CMP_LEN)[None, :]

    def compress(t, pe, w1, w2):
        blk = (t[:, cidx] + pe).reshape(B, n_c, CMP_LEN * hd)
        return jax.nn.gelu(blk @ w1) @ w2

    k_cmp = rmsnorm(compress(kc, pe_k, wk1, wk2), kc_g)
    v_cmp = compress(vc, pe_v, wv1, wv2)
    s = jnp.einsum('bshd,bcd->bhsc', q_n, k_cmp).astype(f32) * scale
    valid = cidx[:, -1][None, :] <= qidx[:, None]
    s = jnp.where(valid, s, NEG)
    p_cmp = jax.nn.softmax(s, axis=-1) * valid.any(-1, keepdims=True).astype(np.float32)
    o_cmp = jnp.einsum('bhsc,bcd->bshd', p_cmp.astype(v_cmp.dtype), v_cmp)
    n_sel = S // SEL_BLK
    n_top = min(SEL_TOP, n_sel)
    imp = jnp.einsum('bhsc,cj->bsj', p_cmp, selection_overlap(n_c, n_sel))
    cur = qidx[:, None] // SEL_BLK
    jb = np.arange(n_sel)[None, :]
    future = jb > cur
    forced = (jb == 0) | (jb == cur) | (jb == cur - 1)
    score = jnp.where(future, -1.0, jnp.where(forced, FORCE_SCORE, imp))
    _, sel = lax.top_k(score, n_top)
    ks_r = rope(rmsnorm(ks[:, :, None, :], ks_g), pos)[:, :, 0]
    kb = ks_r.reshape(B, n_sel, SEL_BLK, hd)
    vb = vs.reshape(B, n_sel, SEL_BLK, hd)
    nq = S // QBLK
    gather = jax.vmap(lambda t, i: t[i])

    def sel_block(args):
        qc, ic, tc = args
        kg = gather(kb, ic).reshape(B, QBLK, n_top * SEL_BLK, hd)
        vg = gather(vb, ic).reshape(B, QBLK, n_top * SEL_BLK, hd)
        kpos = (ic[..., None] * SEL_BLK + jnp.arange(SEL_BLK)).reshape(B, QBLK, n_top * SEL_BLK)
        sc = jnp.einsum('bqhd,bqjd->bhqj', qc, kg).astype(f32) * scale
        sc = jnp.where((kpos <= tc[None, :, None])[:, None], sc, NEG)
        pc = jax.nn.softmax(sc, axis=-1).astype(vg.dtype)
        return jnp.einsum('bhqj,bqjd->bqhd', pc, vg)

    o_slc = lax.map(sel_block, (jnp.moveaxis(q_r.reshape(B, nq, QBLK, H, hd), 1, 0),
                                jnp.moveaxis(sel.reshape(B, nq, QBLK, n_top), 1, 0),
                                jnp.asarray(qidx.reshape(nq, QBLK))))
    o_slc = jnp.moveaxis(o_slc, 0, 1).reshape(B, S, H, hd)
    kw_r = rope(rmsnorm(kw[:, :, None, :], kw_g), pos)
    o_swa, _ = banded_attention(q_r, kw_r, vw[:, :, None, :], SWA_WIN - 1)
    gt = jax.nn.sigmoid(gl.astype(f32)).reshape(B, S, H, 3).astype(q.dtype)
    o = gt[..., 0:1] * o_cmp + gt[..., 1:2] * o_slc + gt[..., 2:3] * o_swa
    return o.reshape(B, S, H * hd)


def hybrid_mixer(h, positions, w_in, w_out, rwkv_mu, rwkv_w0, rwkv_w2, rwkv_a0, rwkv_a2, rwkv_g2,
                 rwkv_kk, rwkv_ka, rwkv_rk, rwkv_gn_w, rwkv_gn_b, conv_w, dil_q_g, dil_k_g,
                 nsa_q_g, nsa_kc_g, nsa_ks_g, nsa_kw_g, nsa_pe_k, nsa_pe_v, nsa_wk1, nsa_wk2,
                 nsa_wv1, nsa_wv2, onorm_g):
    B, S, D = h.shape
    proj = matmul(h.reshape(B * S, D), w_in).reshape(B, S, PROJ_W)
    za, zb, zc, zd = split_cols(proj, SECTION_WIDTHS)
    za = za + (token_shift(za) - za) * rwkv_mu
    r, k, v, wd, ad, gd = split_cols(za, A_WIDTHS)
    y_a = rwkv7_mix(r, k, v, wd, ad, gd, rwkv_w0, rwkv_w2, rwkv_a0, rwkv_a2, rwkv_g2,
                    rwkv_kk, rwkv_ka, rwkv_rk, rwkv_gn_w, rwkv_gn_b)
    bg, cg, xin = split_cols(zb, B_WIDTHS)
    y_b = short_conv_mix(bg, cg, xin, conv_w)
    cq, ck, cv = [t.reshape(B, S, N_HEADS, HEAD_DIM) for t in split_cols(zc, C_WIDTHS)]
    y_c = dilated_mix(cq, ck, cv, dil_q_g, dil_k_g, positions)
    dq, dkc, dvc, dks, dvs, dkw, dvw, dg = split_cols(zd, D_WIDTHS)
    y_d = nsa_mix(dq.reshape(B, S, N_HEADS, HEAD_DIM), dkc, dvc, dks, dvs, dkw, dvw, dg,
                  nsa_q_g, nsa_kc_g, nsa_ks_g, nsa_kw_g, nsa_pe_k, nsa_pe_v,
                  nsa_wk1, nsa_wk2, nsa_wv1, nsa_wv2, positions)
    y_bcd = jnp.concatenate([y_b, y_c, y_d], axis=-1).reshape(B, S, 3 * N_HEADS, HEAD_DIM)
    y_bcd = rmsnorm(y_bcd, onorm_g.reshape(3 * N_HEADS, HEAD_DIM)).reshape(B, S, 3 * GROUP_W)
    y = jnp.concatenate([y_a, y_bcd], axis=-1)
    return matmul(y.reshape(B * S, MIX_W), w_out).reshape(B, S, D)


def moe(h, w_router, b_router, w_gu, b_gu, w_down, b_down):
    B, S, D = h.shape
    T = B * S
    TK = T * TOP_K
    xt = h.reshape(T, D)
    logits = (xt @ w_router + b_router).astype(jnp.float32)
    top_val, top_idx = lax.top_k(logits, TOP_K)
    gates = jax.nn.softmax(top_val, axis=-1)
    e_flat = top_idx.reshape(TK)
    order = jnp.argsort(e_flat)
    e_s = e_flat[order]
    tok_s = order // TOP_K
    g_s = gates.reshape(TK)[order]
    counts = jnp.bincount(e_flat, length=N_EXPERTS)
    padded = (counts + MOE_BLK - 1) // MOE_BLK * MOE_BLK
    pad_end = jnp.cumsum(padded)
    pad_start = pad_end - padded
    start = jnp.cumsum(counts) - counts
    dest = pad_start[e_s] + jnp.arange(TK) - start[e_s]
    n_rows = TK + N_EXPERTS * MOE_BLK
    n_blk = n_rows // MOE_BLK
    buf = jnp.zeros((n_rows, D), h.dtype).at[dest].set(xt[tok_s])
    blk_e = jnp.minimum(jnp.searchsorted(pad_end, jnp.arange(n_blk) * MOE_BLK, side='right'), N_EXPERTS - 1)

    def expert_block(args):
        xb, e = args
        gu = xb @ w_gu[e] + b_gu[e]
        gate, up = gu[:, :EXPERT_FF], gu[:, EXPERT_FF:]
        gate = jnp.minimum(gate, SWIGLU_LIMIT)
        up = jnp.clip(up, -SWIGLU_LIMIT, SWIGLU_LIMIT)
        act = gate * jax.nn.sigmoid(SWIGLU_ALPHA * gate) * (up + 1.0)
        return act @ w_down[e] + b_down[e]

    ybuf = lax.map(expert_block, (buf.reshape(n_blk, MOE_BLK, D), blk_e)).reshape(n_rows, D)
    y = ybuf[dest] * g_s[:, None].astype(h.dtype)
    return jax.ops.segment_sum(y, tok_s, num_segments=T).reshape(B, S, D)


def kernel(x, c, positions, w_ada, b_ada, norm1_g, norm2_g, w_in, w_out, rwkv_mu, rwkv_w0, rwkv_w2, rwkv_a0, rwkv_a2, rwkv_g2, rwkv_kk, rwkv_ka, rwkv_rk, rwkv_gn_w, rwkv_gn_b, conv_w, dil_q_g, dil_k_g, nsa_q_g, nsa_kc_g, nsa_ks_g, nsa_kw_g, nsa_pe_k, nsa_pe_v, nsa_wk1, nsa_wk2, nsa_wv1, nsa_wv2, onorm_g, w_router, b_router, w_gu, b_gu, w_down, b_down):
    for l in range(DEPTH):
        mod = jax.nn.silu(c) @ w_ada[l] + b_ada[l]
        sh1, sc1, gt1, sh2, sc2, gt2 = [m[:, None, :] for m in jnp.split(mod, 6, axis=-1)]
        h = rmsnorm(x, norm1_g[l]) * (1.0 + sc1) + sh1
        y = hybrid_mixer(h, positions, w_in[l], w_out[l], rwkv_mu[l], rwkv_w0[l], rwkv_w2[l],
                         rwkv_a0[l], rwkv_a2[l], rwkv_g2[l], rwkv_kk[l], rwkv_ka[l], rwkv_rk[l],
                         rwkv_gn_w[l], rwkv_gn_b[l], conv_w[l], dil_q_g[l], dil_k_g[l],
                         nsa_q_g[l], nsa_kc_g[l], nsa_ks_g[l], nsa_kw_g[l], nsa_pe_k[l], nsa_pe_v[l],
                         nsa_wk1[l], nsa_wk2[l], nsa_wv1[l], nsa_wv2[l], onorm_g[l])
        x = x + gt1 * y
        h = rmsnorm(x, norm2_g[l]) * (1.0 + sc2) + sh2
        x = x + gt2 * moe(h, w_router[l], b_router[l], w_gu[l], b_gu[l], w_down[l], b_down[l])
    return x
```

```python
import functools

import numpy as np
import jax
import jax.numpy as jnp
from jax import lax
from jax.experimental import pallas as pl
from jax.experimental.pallas import tpu as pltpu

D_MODEL = 1024
DEPTH = 2

HEAD_DIM = 64
N_MIXERS = 4
GROUP_W = D_MODEL // N_MIXERS
N_HEADS = GROUP_W // HEAD_DIM
MIX_W = N_MIXERS * GROUP_W

RWKV_DECAY_RANK = 64
RWKV_AAA_RANK = 64
RWKV_GATE_RANK = 128
RWKV_DECAY_SCALE = 0.606531
RWKV_GN_EPS = 64e-5

CONV_W = 3

DIL_CFG = ((128, 1), (512, 4), (2048, 16))

CMP_LEN = 32
CMP_STRIDE = 16
CMP_HIDDEN = 256
SEL_BLK = 64
SEL_TOP = 16
SWA_WIN = 512
FORCE_SCORE = 1e4

N_EXPERTS = 32
TOP_K = 4
EXPERT_FF = D_MODEL
SWIGLU_LIMIT = 7.0
SWIGLU_ALPHA = 1.702
MOE_BLK = 128

ROPE_THETA = 10000.0
NORM_EPS = 1e-6
QBLK = 128
NEG = -1e30

A_WIDTHS = (GROUP_W, GROUP_W, GROUP_W, RWKV_DECAY_RANK, RWKV_AAA_RANK, RWKV_GATE_RANK)
B_WIDTHS = (GROUP_W, GROUP_W, GROUP_W)
C_WIDTHS = (GROUP_W, GROUP_W, GROUP_W)
D_WIDTHS = (GROUP_W, HEAD_DIM, HEAD_DIM, HEAD_DIM, HEAD_DIM, HEAD_DIM, HEAD_DIM, 3 * N_HEADS)
SECTION_WIDTHS = (sum(A_WIDTHS), sum(B_WIDTHS), sum(C_WIDTHS), sum(D_WIDTHS))
PROJ_W = sum(SECTION_WIDTHS)


def _matmul_kernel(x_ref, w_ref, o_ref, *, exact):
    if exact:
        o_ref[...] = jnp.dot(x_ref[...], w_ref[...], precision=lax.Precision.HIGHEST,
                             preferred_element_type=jnp.float32)
    else:
        o_ref[...] = jnp.dot(x_ref[...].astype(jnp.bfloat16), w_ref[...],
                             preferred_element_type=jnp.float32)


def matmul(x, w, tm=512, tn=256, exact=False):
    M, K = x.shape
    N = w.shape[1]
    n_pad = -(-N // tn) * tn
    wb = jnp.pad(w if exact else w.astype(jnp.bfloat16), ((0, 0), (0, n_pad - N)))
    out = pl.pallas_call(
        functools.partial(_matmul_kernel, exact=exact),
        grid=(M // tm, n_pad // tn),
        in_specs=[pl.BlockSpec((tm, K), lambda i, j: (i, 0)),
                  pl.BlockSpec((K, tn), lambda i, j: (0, j))],
        out_specs=pl.BlockSpec((tm, tn), lambda i, j: (i, j)),
        out_shape=jax.ShapeDtypeStruct((M, n_pad), jnp.float32),
        name="matmul",
    )(x, wb)
    return out[:, :N]


RWKV_CHUNK = 64
_F32 = jnp.float32
_BF16 = jnp.bfloat16
_NT = (((1,), (1,)), ((), ()))
_TN = (((0,), (0,)), ((), ()))


def _dot_hi(a, b, dims=None):
    if dims is None:
        return jnp.dot(a, b, precision=lax.Precision.HIGHEST, preferred_element_type=_F32)
    return lax.dot_general(a, b, dims, precision=lax.Precision.HIGHEST, preferred_element_type=_F32)


def _dot_lo(a, b, dims=None):
    a, b = a.astype(_BF16), b.astype(_BF16)
    if dims is None:
        return jnp.dot(a, b, preferred_element_type=_F32)
    return lax.dot_general(a, b, dims, preferred_element_type=_F32)


def _group_indicator(n, group):
    r = lax.broadcasted_iota(jnp.int32, (n, n), 0) // group
    c = lax.broadcasted_iota(jnp.int32, (n, n), 1) // group
    return r == c


def _rwkv_kernel(za_ref, mu_ref, w0_ref, w2_ref, a0_ref, a2_ref, g2_ref, kk_ref, ka_ref, rk_ref,
                 gnw_ref, gnb_ref, o_ref, prev_sc, h_sc):
    C = RWKV_CHUNK
    c = pl.program_id(1)

    @pl.when(c == 0)
    def _():
        prev_sc[...] = jnp.zeros_like(prev_sc)
        h_sc[...] = jnp.zeros_like(h_sc)

    z = za_ref[0]
    row = lax.broadcasted_iota(jnp.int32, z.shape, 0)
    zs = jnp.where(row == 0, prev_sc[...], pltpu.roll(z, 1, axis=0))
    prev_sc[...] = z[C - 1:C, :]
    z = z + (zs - z) * mu_ref[...]

    G = GROUP_W
    r, k, v = z[:, 0:G], z[:, G:2 * G], z[:, 2 * G:3 * G]
    o = 3 * G
    wd = z[:, o:o + RWKV_DECAY_RANK]
    ad = z[:, o + RWKV_DECAY_RANK:o + RWKV_DECAY_RANK + RWKV_AAA_RANK]
    gd = z[:, o + RWKV_DECAY_RANK + RWKV_AAA_RANK:]

    lw = -RWKV_DECAY_SCALE * jax.nn.sigmoid(w0_ref[...] + _dot_lo(jnp.tanh(wd), w2_ref[...]))
    a = jax.nn.sigmoid(a0_ref[...] + _dot_lo(ad, a2_ref[...]))
    g = _dot_lo(jax.nn.sigmoid(gd), g2_ref[...])

    head_sum = _group_indicator(G, HEAD_DIM).astype(_F32)
    kk = k * kk_ref[...]
    kk = kk * lax.rsqrt(_dot_hi(kk * kk, head_sum) + 1e-12)
    k = k * (1.0 + (a - 1.0) * ka_ref[...])
    b = kk * a

    ti = lax.broadcasted_iota(jnp.int32, (C, C), 0)
    tj = lax.broadcasted_iota(jnp.int32, (C, C), 1)
    incl = ti >= tj
    strict = ti > tj
    eye = ti == tj
    blk16 = (ti // 16) == (tj // 16)
    blk32 = (ti // 32) == (tj // 32)
    eye_f = eye.astype(_F32)

    cum = _dot_hi(incl.astype(_F32), lw)
    g_in = jnp.exp(cum)
    g_ex = jnp.exp(cum - lw)
    g_inv = jnp.exp(-cum)
    g_end = g_in[C - 1:C, :]
    A_all = -kk * g_ex
    R_all = r * g_in
    B_all = b * g_inv
    K_all = k * g_inv
    Bh_all = B_all * g_end
    Kh_all = K_all * g_end

    ys = []
    for h in range(N_HEADS):
        sl = slice(h * HEAD_DIM, (h + 1) * HEAD_DIM)
        A, R, B, Kt, Bh, Kh, vh = (A_all[:, sl], R_all[:, sl], B_all[:, sl], K_all[:, sl],
                                   Bh_all[:, sl], Kh_all[:, sl], v[:, sl])
        gram = _dot_lo(jnp.concatenate([A, R], axis=0), jnp.concatenate([B, Kt], axis=0), _NT)
        l_ab = jnp.where(strict, gram[0:C, 0:C], 0.0)
        l_ak = jnp.where(strict, gram[0:C, C:2 * C], 0.0)
        m_rb = jnp.where(incl, gram[C:2 * C, 0:C], 0.0)
        m_rk = jnp.where(incl, gram[C:2 * C, C:2 * C], 0.0)
        p = jnp.where(blk16, l_ab, 0.0)
        x = eye_f + p
        for _ in range(3):
            p = _dot_lo(p, p)
            x = _dot_lo(x, eye_f + p)
        x = x + _dot_lo(_dot_lo(x, jnp.where(blk32 & ~blk16, l_ab, 0.0)), x)
        t_inv = x + _dot_lo(_dot_lo(x, jnp.where(~blk32, l_ab, 0.0)), x)
        tap = _dot_lo(t_inv, jnp.concatenate([A, _dot_lo(l_ak, vh)], axis=1))
        m1 = _dot_lo(m_rb, tap)
        w_y = R + m1[:, 0:HEAD_DIM]
        y0 = m1[:, HEAD_DIM:] + _dot_lo(m_rk, vh)
        bt = _dot_lo(Bh, tap, _TN)
        w_h = jnp.where(eye, g_end[:, sl], 0.0) + bt[:, 0:HEAD_DIM]
        h_add = bt[:, HEAD_DIM:] + _dot_lo(Kh, vh, _TN)
        nxt = _dot_hi(jnp.concatenate([w_y, w_h], axis=0), h_sc[h])
        ys.append(nxt[0:C] + y0)
        h_sc[h] = nxt[C:2 * C] + h_add
    y = jnp.concatenate(ys, axis=1)

    head_mean = head_sum * (1.0 / HEAD_DIM)
    mu = _dot_hi(y, head_mean)
    d = y - mu
    var = _dot_hi(d * d, head_mean)
    y = d * lax.rsqrt(var + RWKV_GN_EPS) * gnw_ref[...] + gnb_ref[...]
    y = y + _dot_hi(r * k * rk_ref[...], head_sum) * v
    o_ref[0] = y * g


def rwkv7_pallas(za, mu, w0, w2, a0, a2, g2, k_k, k_a, r_k, gn_w, gn_b):
    B, S, W = za.shape
    C = RWKV_CHUNK
    row = lambda t: t.reshape(1, -1).astype(_F32)
    full = lambda t: pl.BlockSpec(t.shape, lambda b, c: (0,) * t.ndim)
    params = [row(mu), row(w0), w2.astype(_BF16), row(a0), a2.astype(_BF16), g2.astype(_BF16),
              row(k_k), row(k_a), row(r_k), row(gn_w), row(gn_b)]
    return pl.pallas_call(
        _rwkv_kernel,
        grid=(B, S // C),
        in_specs=[pl.BlockSpec((1, C, W), lambda b, c: (b, c, 0))] + [full(p) for p in params],
        out_specs=pl.BlockSpec((1, C, GROUP_W), lambda b, c: (b, c, 0)),
        out_shape=jax.ShapeDtypeStruct((B, S, GROUP_W), _F32),
        scratch_shapes=[pltpu.VMEM((1, W), _F32), pltpu.VMEM((N_HEADS, HEAD_DIM, HEAD_DIM), _F32)],
        compiler_params=pltpu.CompilerParams(dimension_semantics=("parallel", "arbitrary")),
        name="rwkv7",
    )(za, *params)


MOE_TM = 512
MOE_FC = 512


def _moe_ffn_kernel(blk_e_ref, n_used_ref, x_ref, wgu_ref, bgu_ref, wd_ref, bd_ref, o_ref):
    i = pl.program_id(0)

    @pl.when(i < n_used_ref[0])
    def _():
        x = x_ref[...]
        F = EXPERT_FF
        acc = None
        for c in range(F // MOE_FC):
            lo = c * MOE_FC
            gate = jnp.dot(x, wgu_ref[0, :, lo:lo + MOE_FC], preferred_element_type=_F32) + bgu_ref[0, :, lo:lo + MOE_FC]
            up = jnp.dot(x, wgu_ref[0, :, F + lo:F + lo + MOE_FC], preferred_element_type=_F32) + bgu_ref[0, :, F + lo:F + lo + MOE_FC]
            gate = jnp.minimum(gate, SWIGLU_LIMIT)
            up = jnp.clip(up, -SWIGLU_LIMIT, SWIGLU_LIMIT)
            act = gate * jax.nn.sigmoid(SWIGLU_ALPHA * gate) * (up + 1.0)
            part = jnp.dot(act.astype(_BF16), wd_ref[0, lo:lo + MOE_FC, :], preferred_element_type=_F32)
            acc = part if acc is None else acc + part
        o_ref[...] = acc + bd_ref[0]


def moe_ffn(xs, blk_e, n_used, w_gu, b_gu, w_down, b_down):
    n_rows, D = xs.shape
    E, _, F2 = w_gu.shape
    n_blk = n_rows // MOE_TM
    grid_spec = pltpu.PrefetchScalarGridSpec(
        num_scalar_prefetch=2,
        grid=(n_blk,),
        in_specs=[pl.BlockSpec((MOE_TM, D), lambda i, e, n: (i, 0)),
                  pl.BlockSpec((1, D, F2), lambda i, e, n: (e[i], 0, 0)),
                  pl.BlockSpec((1, 1, F2), lambda i, e, n: (e[i], 0, 0)),
                  pl.BlockSpec((1, F2 // 2, D), lambda i, e, n: (e[i], 0, 0)),
                  pl.BlockSpec((1, 1, D), lambda i, e, n: (e[i], 0, 0))],
        out_specs=pl.BlockSpec((MOE_TM, D), lambda i, e, n: (i, 0)),
    )
    return pl.pallas_call(
        _moe_ffn_kernel,
        grid_spec=grid_spec,
        out_shape=jax.ShapeDtypeStruct((n_rows, D), _F32),
        compiler_params=pltpu.CompilerParams(dimension_semantics=("arbitrary",),
                                             vmem_limit_bytes=48 * 1024 * 1024),
        name="moe_ffn",
    )(blk_e, n_used, xs, w_gu.astype(_BF16), b_gu.reshape(E, 1, F2), w_down.astype(_BF16),
      b_down.reshape(E, 1, D))


def moe_pallas(h, w_router, b_router, w_gu, b_gu, w_down, b_down):
    B, S, D = h.shape
    T = B * S
    TK = T * TOP_K
    TM = MOE_TM
    xt = h.reshape(T, D)
    logits = matmul(xt, w_router, tn=128, exact=True) + b_router
    top_val, top_idx = lax.top_k(logits, TOP_K)
    gates = jax.nn.softmax(top_val, axis=-1)
    e_flat = top_idx.reshape(TK)
    order = jnp.argsort(e_flat)
    e_s = e_flat[order]
    counts = jnp.bincount(e_flat, length=N_EXPERTS)
    padded = (counts + TM - 1) // TM * TM
    pad_end = jnp.cumsum(padded)
    pad_start = pad_end - padded
    start = jnp.cumsum(counts) - counts
    dest = pad_start[e_s] + jnp.arange(TK) - start[e_s]
    n_rows = TK + N_EXPERTS * TM
    n_blk = n_rows // TM
    src_tok = jnp.zeros((n_rows,), jnp.int32).at[dest].set((order // TOP_K).astype(jnp.int32))
    row_of = jnp.zeros((TK,), jnp.int32).at[order].set(dest.astype(jnp.int32)).reshape(T, TOP_K)
    n_used = (pad_end[-1] // TM).astype(jnp.int32).reshape(1)
    blk_e = jnp.minimum(jnp.searchsorted(pad_end, jnp.arange(n_blk) * TM, side='right'), N_EXPERTS - 1)
    blk_e = jnp.where(jnp.arange(n_blk) < n_used[0], blk_e, blk_e[jnp.maximum(n_used[0] - 1, 0)]).astype(jnp.int32)
    xs = jnp.take(xt.astype(_BF16), src_tok, axis=0)
    ybuf = moe_ffn(xs, blk_e, n_used, w_gu, b_gu, w_down, b_down)
    y = jnp.take(ybuf, row_of.reshape(TK), axis=0).reshape(T, TOP_K, D) * gates[:, :, None]
    return jnp.sum(y, axis=1).reshape(B, S, D)


ATT_T = 128


def _softmax_tile_update(s, wgt, v, m_ref, l_ref, acc_ref, rows, weighted):
    s = jnp.where(wgt > 0.0, s, NEG)
    m_prev = m_ref[rows, :]
    m_new = jnp.maximum(m_prev, jnp.max(s, axis=-1, keepdims=True))
    alpha = jnp.exp(m_prev - m_new)
    p = jnp.exp(s - m_new)
    if weighted:
        p = p * wgt
    l_ref[rows, :] = alpha * l_ref[rows, :] + jnp.sum(p, axis=-1, keepdims=True)
    acc_ref[rows, :] = alpha * acc_ref[rows, :] + jnp.dot(p.astype(_BF16), v, preferred_element_type=_F32)
    m_ref[rows, :] = m_new


def _tile_iotas():
    r = lax.broadcasted_iota(jnp.int32, (ATT_T, ATT_T), 0)
    c = lax.broadcasted_iota(jnp.int32, (ATT_T, ATT_T), 1)
    return r, c


def _stack_heads(q):
    return jnp.concatenate([q[:, h * HEAD_DIM:(h + 1) * HEAD_DIM] for h in range(N_HEADS)], axis=0)


def _unstack_heads(o):
    return jnp.concatenate([o[h * ATT_T:(h + 1) * ATT_T, :] for h in range(N_HEADS)], axis=1)


def _nsa_attn_kernel(q_ref, ks_ref, vs_ref, kw_ref, vw_ref, sel_ref, oslc_ref, oswa_ref, m_sc, l_sc, acc_sc):
    T = ATT_T
    R = N_HEADS * T
    qi = pl.program_id(1)
    q = _stack_heads(q_ref[0])
    r, c = _tile_iotas()
    rows = slice(0, R)
    bpt = T // SEL_BLK
    jb = lax.broadcasted_iota(jnp.int32, (NSA_NSEL, T), 0)
    cc = lax.broadcasted_iota(jnp.int32, (NSA_NSEL, T), 1) // SEL_BLK
    sel = sel_ref[0]

    def init():
        m_sc[...] = jnp.full_like(m_sc, NEG)
        l_sc[...] = jnp.zeros_like(l_sc)
        acc_sc[...] = jnp.zeros_like(acc_sc)

    def finish(o_ref):
        o_ref[0] = _unstack_heads(acc_sc[...] / l_sc[...])

    def scores(k_ref, kt):
        k = k_ref[0, pl.ds(pl.multiple_of(kt * T, T), T), :]
        return lax.dot_general(q, k, _NT, preferred_element_type=_F32)

    def tile4(w):
        return jnp.concatenate([w] * N_HEADS, axis=0)

    init()

    def slc_step(kt, carry):
        expand = (jb == cc + kt * bpt).astype(_BF16)
        w = jnp.dot(sel, expand, preferred_element_type=_F32)
        w = jnp.where((kt - qi) * T + c <= r, w, 0.0)
        v = vs_ref[0, pl.ds(pl.multiple_of(kt * T, T), T), :]
        _softmax_tile_update(scores(ks_ref, kt), tile4(w), v, m_sc, l_sc, acc_sc, rows, False)
        return carry

    lax.fori_loop(0, qi + 1, slc_step, 0)
    finish(oslc_ref)

    init()

    def swa_step(kt, carry):
        d = (qi - kt) * T + r - c
        w = ((d >= 0) & (d <= SWA_WIN - 1)).astype(_F32)
        v = vw_ref[0, pl.ds(pl.multiple_of(kt * T, T), T), :]
        _softmax_tile_update(scores(kw_ref, kt), tile4(w), v, m_sc, l_sc, acc_sc, rows, False)
        return carry

    lax.fori_loop(jnp.maximum(qi - (SWA_WIN - 1 + T - 1) // T, 0), qi + 1, swa_step, 0)
    finish(oswa_ref)


NSA_NSEL = 64


def nsa_attn_pallas(q_r, ks_r, vs, kw_r, vw, sel_mask):
    B, S, W = q_r.shape
    assert S // SEL_BLK == NSA_NSEL
    T = ATT_T
    R = N_HEADS * T
    bf = lambda t: t.astype(_BF16)
    kv_spec = pl.BlockSpec((1, S, HEAD_DIM), lambda b, i: (b, 0, 0))
    q_spec = pl.BlockSpec((1, T, W), lambda b, i: (b, i, 0))
    return pl.pallas_call(
        _nsa_attn_kernel,
        grid=(B, S // T),
        in_specs=[q_spec, kv_spec, kv_spec, kv_spec, kv_spec,
                  pl.BlockSpec((1, T, NSA_NSEL), lambda b, i: (b, i, 0))],
        out_specs=[q_spec, q_spec],
        out_shape=[jax.ShapeDtypeStruct((B, S, W), _F32)] * 2,
        scratch_shapes=[pltpu.VMEM((R, 1), _F32), pltpu.VMEM((R, 1), _F32), pltpu.VMEM((R, HEAD_DIM), _F32)],
        compiler_params=pltpu.CompilerParams(dimension_semantics=("parallel", "arbitrary")),
        name="nsa_attn",
    )(bf(q_r * (HEAD_DIM ** -0.5)), bf(ks_r), bf(vs), bf(kw_r), bf(vw), bf(sel_mask))


DIL_SPAN_TILES = DIL_CFG[-1][0] // ATT_T


def _dil_attn_kernel(q_ref, k_ref, v_ref, o_ref, m_sc, l_sc, acc_sc):
    T = ATT_T
    qi = pl.program_id(1)
    r, c = _tile_iotas()
    m_sc[...] = jnp.full_like(m_sc, NEG)
    l_sc[...] = jnp.zeros_like(l_sc)
    acc_sc[...] = jnp.zeros_like(acc_sc)
    qs = [q_ref[0, :, h * HEAD_DIM:(h + 1) * HEAD_DIM] for h in range(N_HEADS)]

    def step(kt, carry):
        d = (qi - kt) * T + r - c
        w = jnp.zeros((T, T), _F32)
        for window, dil in DIL_CFG:
            w = w + ((d >= 0) & (d <= window) & ((d & (dil - 1)) == 0)).astype(_F32)
        base = pl.multiple_of(kt * T, T)
        for h in range(N_HEADS):
            k = k_ref[0, pl.ds(base, T), h * HEAD_DIM:(h + 1) * HEAD_DIM]
            v = v_ref[0, pl.ds(base, T), h * HEAD_DIM:(h + 1) * HEAD_DIM]
            s = lax.dot_general(qs[h], k, _NT, preferred_element_type=_F32)
            _softmax_tile_update(s, w, v, m_sc, l_sc, acc_sc, slice(h * T, (h + 1) * T), True)
        return carry

    lax.fori_loop(jnp.maximum(qi - DIL_SPAN_TILES, 0), qi + 1, step, 0)
    o_ref[0] = _unstack_heads(acc_sc[...] / l_sc[...])


def dil_attn_pallas(q_r, k_r, v):
    B, S, W = q_r.shape
    T = ATT_T
    R = N_HEADS * T
    bf = lambda t: t.astype(_BF16)
    kv_spec = pl.BlockSpec((1, S, W), lambda b, i: (b, 0, 0))
    q_spec = pl.BlockSpec((1, T, W), lambda b, i: (b, i, 0))
    return pl.pallas_call(
        _dil_attn_kernel,
        grid=(B, S // T),
        in_specs=[q_spec, kv_spec, kv_spec],
        out_specs=q_spec,
        out_shape=jax.ShapeDtypeStruct((B, S, W), _F32),
        scratch_shapes=[pltpu.VMEM((R, 1), _F32), pltpu.VMEM((R, 1), _F32), pltpu.VMEM((R, HEAD_DIM), _F32)],
        compiler_params=pltpu.CompilerParams(dimension_semantics=("parallel", "arbitrary")),
        name="dil_attn",
    )(bf(q_r * (HEAD_DIM ** -0.5)), bf(k_r), bf(v))


def split_cols(z, widths):
    offs = np.cumsum((0,) + tuple(widths))
    return [z[..., int(offs[i]):int(offs[i + 1])] for i in range(len(widths))]


def rmsnorm(x, g):
    xf = x.astype(jnp.float32)
    y = xf * lax.rsqrt(jnp.mean(xf * xf, axis=-1, keepdims=True) + NORM_EPS)
    return (y * g.astype(jnp.float32)).astype(x.dtype)


def rope(x, pos):
    half = x.shape[-1] // 2
    inv = ROPE_THETA ** (-jnp.arange(half, dtype=jnp.float32) / half)
    ang = pos.astype(jnp.float32)[:, :, None, None] * inv
    cos, sin = jnp.cos(ang), jnp.sin(ang)
    xf = x.astype(jnp.float32)
    x1, x2 = xf[..., :half], xf[..., half:]
    return jnp.concatenate([x1 * cos - x2 * sin, x2 * cos + x1 * sin], axis=-1).astype(x.dtype)


def token_shift(z):
    return jnp.pad(z, ((0, 0), (1, 0), (0, 0)))[:, :-1]


def banded_attention(q, k, v, window):
    B, L, Hq, hd = q.shape
    Hk = k.shape[2]
    G = Hq // Hk
    nb = -(-L // QBLK)
    Lp = nb * QBLK
    npv = -(-window // QBLK)
    pad = ((0, 0), (0, Lp - L), (0, 0), (0, 0))
    q, k, v = jnp.pad(q, pad), jnp.pad(k, pad), jnp.pad(v, pad)
    bpad = ((0, 0), (npv, 0), (0, 0), (0, 0), (0, 0))
    kb = jnp.pad(k.reshape(B, nb, QBLK, Hk, hd), bpad)
    vb = jnp.pad(v.reshape(B, nb, QBLK, Hk, hd), bpad)
    bidx = np.arange(nb)[:, None] + np.arange(npv + 1)[None, :]
    J = (npv + 1) * QBLK
    kband = kb[:, bidx].reshape(B, nb, J, Hk, hd)
    vband = vb[:, bidx].reshape(B, nb, J, Hk, hd)
    qb = q.reshape(B, nb, QBLK, Hk, G, hd)
    s = jnp.einsum('bnqkgd,bnjkd->bnkgqj', qb, kband).astype(jnp.float32) * (hd ** -0.5)
    qpos = np.arange(nb)[:, None] * QBLK + np.arange(QBLK)[None, :]
    kpos = (np.arange(nb)[:, None] - npv) * QBLK + np.arange(J)[None, :]
    dist = qpos[:, :, None] - kpos[:, None, :]
    mask = (dist >= 0) & (dist <= window) & (kpos[:, None, :] >= 0)
    s = jnp.where(mask[None, :, None, None], s, NEG)
    lse = jax.nn.logsumexp(s, axis=-1)
    p = jnp.exp(s - lse[..., None]).astype(v.dtype)
    o = jnp.einsum('bnkgqj,bnjkd->bnqkgd', p, vband).reshape(B, Lp, Hq, hd)[:, :L]
    lse = jnp.transpose(lse, (0, 1, 4, 2, 3)).reshape(B, Lp, Hq)[:, :L]
    return o, lse


def decimate(t, d):
    B, S = t.shape[:2]
    t = t.reshape((B, S // d, d) + t.shape[2:])
    return jnp.moveaxis(t, 2, 1).reshape((B * d, S // d) + t.shape[3:])


def undecimate(t, d, B):
    Sd = t.shape[1]
    t = t.reshape((B, d, Sd) + t.shape[2:])
    return jnp.moveaxis(t, 1, 2).reshape((B, Sd * d) + t.shape[3:])


def rwkv7_scan(r, w, k, v, kk, b):
    B, S, H, N = r.shape
    tm = lambda t: jnp.moveaxis(t, 1, 0)

    def step(st, inp):
        r_t, w_t, k_t, v_t, kk_t, b_t = inp
        sa = jnp.einsum('bhij,bhj->bhi', st, -kk_t)
        st = st * w_t[:, :, None, :] + sa[..., None] * b_t[:, :, None, :] + v_t[..., None] * k_t[:, :, None, :]
        return st, jnp.einsum('bhij,bhj->bhi', st, r_t)

    s0 = jnp.zeros((B, H, N, N), jnp.float32)
    _, ys = lax.scan(step, s0, (tm(r), tm(w), tm(k), tm(v), tm(kk), tm(b)))
    return jnp.moveaxis(ys, 0, 1)


def rwkv7_mix(r, k, v, wd, ad, gd, w0, w2, a0, a2, g2, k_k, k_a, r_k, gn_w, gn_b):
    B, S, C = r.shape
    H = C // HEAD_DIM
    f32 = jnp.float32
    heads = lambda t: t.reshape(B, S, H, HEAD_DIM).astype(f32)
    w = jnp.exp(-RWKV_DECAY_SCALE * jax.nn.sigmoid((w0 + jnp.tanh(wd) @ w2).astype(f32)))
    a = jax.nn.sigmoid(a0 + ad @ a2)
    g = jax.nn.sigmoid(gd) @ g2
    kk = heads(k * k_k)
    kk = kk * lax.rsqrt(jnp.sum(kk * kk, axis=-1, keepdims=True) + 1e-12)
    k = k * (1.0 + (a - 1.0) * k_a)
    y = rwkv7_scan(heads(r), heads(w), heads(k), heads(v), kk, kk * heads(a))
    mu = jnp.mean(y, axis=-1, keepdims=True)
    var = jnp.mean(jnp.square(y - mu), axis=-1, keepdims=True)
    y = (y - mu) * lax.rsqrt(var + RWKV_GN_EPS) * gn_w.reshape(H, HEAD_DIM).astype(f32) + gn_b.reshape(H, HEAD_DIM).astype(f32)
    y = y + jnp.sum(heads(r) * heads(k) * r_k.astype(f32), axis=-1, keepdims=True) * heads(v)
    return (y.reshape(B, S, C) * g.astype(f32)).astype(r.dtype)


def short_conv_mix(bg, cg, xin, conv_w):
    u = cg * xin
    y = lax.conv_general_dilated(u, conv_w[:, None, :].astype(u.dtype), window_strides=(1,),
                                 padding=((CONV_W - 1, 0),), dimension_numbers=('NWC', 'WIO', 'NWC'),
                                 feature_group_count=u.shape[-1])
    return bg * y


def dilated_mix(q, k, v, q_g, k_g, pos):
    B, S, H, hd = q.shape
    q = rope(rmsnorm(q, q_g), pos)
    k = rope(rmsnorm(k, k_g), pos)
    return dil_attn_pallas(q.reshape(B, S, H * hd), k.reshape(B, S, H * hd), v.reshape(B, S, H * hd))


def selection_overlap(n_c, n_sel):
    r = SEL_BLK // CMP_STRIDE
    m = CMP_LEN // CMP_STRIDE
    diff = np.arange(n_c)[:, None] - r * np.arange(n_sel)[None, :]
    offs = (np.arange(r)[:, None] - np.arange(m)[None, :]).reshape(-1)
    return (diff[..., None] == offs).sum(-1).astype(np.float32)


def nsa_mix(q, kc, vc, ks, vs, kw, vw, gl, q_g, kc_g, ks_g, kw_g, pe_k, pe_v, wk1, wk2, wv1, wv2, pos):
    B, S, H, hd = q.shape
    f32 = jnp.float32
    scale = hd ** -0.5
    q_n = rmsnorm(q, q_g)
    q_r = rope(q_n, pos)
    qidx = np.arange(S)
    n_c = (S - CMP_LEN) // CMP_STRIDE + 1
    cidx = np.arange(n_c)[:, None] * CMP_STRIDE + np.arange(CMP_LEN)[None, :]

    def compress(t, pe, w1, w2):
        blk = (t[:, cidx] + pe).reshape(B, n_c, CMP_LEN * hd)
        return jax.nn.gelu(blk @ w1) @ w2

    k_cmp = rmsnorm(compress(kc, pe_k, wk1, wk2), kc_g)
    v_cmp = compress(vc, pe_v, wv1, wv2)
    s = jnp.einsum('bshd,bcd->bhsc', q_n, k_cmp).astype(f32) * scale
    valid = cidx[:, -1][None, :] <= qidx[:, None]
    s = jnp.where(valid, s, NEG)
    p_cmp = jax.nn.softmax(s, axis=-1) * valid.any(-1, keepdims=True).astype(np.float32)
    o_cmp = jnp.einsum('bhsc,bcd->bshd', p_cmp.astype(v_cmp.dtype), v_cmp)
    n_sel = S // SEL_BLK
    n_top = min(SEL_TOP, n_sel)
    imp = jnp.einsum('bhsc,cj->bsj', p_cmp, selection_overlap(n_c, n_sel))
    cur = qidx[:, None] // SEL_BLK
    jb = np.arange(n_sel)[None, :]
    future = jb > cur
    forced = (jb == 0) | (jb == cur) | (jb == cur - 1)
    score = jnp.where(future, -1.0, jnp.where(forced, FORCE_SCORE, imp))
    _, sel = lax.top_k(score, n_top)
    ks_r = rope(rmsnorm(ks[:, :, None, :], ks_g), pos)[:, :, 0]
    kw_r = rope(rmsnorm(kw[:, :, None, :], kw_g), pos)[:, :, 0]
    sel_mask = (sel[..., None] == jnp.arange(n_sel)).any(-2)
    o_slc, o_swa = nsa_attn_pallas(q_r.reshape(B, S, H * hd), ks_r, vs, kw_r, vw, sel_mask)
    o_slc = o_slc.reshape(B, S, H, hd)
    o_swa = o_swa.reshape(B, S, H, hd)
    gt = jax.nn.sigmoid(gl.astype(f32)).reshape(B, S, H, 3).astype(q.dtype)
    o = gt[..., 0:1] * o_cmp + gt[..., 1:2] * o_slc + gt[..., 2:3] * o_swa
    return o.reshape(B, S, H * hd)


def hybrid_mixer(h, positions, w_in, w_out, rwkv_mu, rwkv_w0, rwkv_w2, rwkv_a0, rwkv_a2, rwkv_g2,
                 rwkv_kk, rwkv_ka, rwkv_rk, rwkv_gn_w, rwkv_gn_b, conv_w, dil_q_g, dil_k_g,
                 nsa_q_g, nsa_kc_g, nsa_ks_g, nsa_kw_g, nsa_pe_k, nsa_pe_v, nsa_wk1, nsa_wk2,
                 nsa_wv1, nsa_wv2, onorm_g):
    B, S, D = h.shape
    proj = matmul(h.reshape(B * S, D), w_in).reshape(B, S, PROJ_W)
    za, zb, zc, zd = split_cols(proj, SECTION_WIDTHS)
    y_a = rwkv7_pallas(za, rwkv_mu, rwkv_w0, rwkv_w2, rwkv_a0, rwkv_a2, rwkv_g2,
                       rwkv_kk, rwkv_ka, rwkv_rk, rwkv_gn_w, rwkv_gn_b)
    bg, cg, xin = split_cols(zb, B_WIDTHS)
    y_b = short_conv_mix(bg, cg, xin, conv_w)
    cq, ck, cv = [t.reshape(B, S, N_HEADS, HEAD_DIM) for t in split_cols(zc, C_WIDTHS)]
    y_c = dilated_mix(cq, ck, cv, dil_q_g, dil_k_g, positions)
    dq, dkc, dvc, dks, dvs, dkw, dvw, dg = split_cols(zd, D_WIDTHS)
    y_d = nsa_mix(dq.reshape(B, S, N_HEADS, HEAD_DIM), dkc, dvc, dks, dvs, dkw, dvw, dg,
                  nsa_q_g, nsa_kc_g, nsa_ks_g, nsa_kw_g, nsa_pe_k, nsa_pe_v,
                  nsa_wk1, nsa_wk2, nsa_wv1, nsa_wv2, positions)
    y_bcd = jnp.concatenate([y_b, y_c, y_d], axis=-1).reshape(B, S, 3 * N_HEADS, HEAD_DIM)
    y_bcd = rmsnorm(y_bcd, onorm_g.reshape(3 * N_HEADS, HEAD_DIM)).reshape(B, S, 3 * GROUP_W)
    y = jnp.concatenate([y_a, y_bcd], axis=-1)
    return matmul(y.reshape(B * S, MIX_W), w_out).reshape(B, S, D)


def moe(h, w_router, b_router, w_gu, b_gu, w_down, b_down):
    B, S, D = h.shape
    T = B * S
    TK = T * TOP_K
    xt = h.reshape(T, D)
    logits = (xt @ w_router + b_router).astype(jnp.float32)
    top_val, top_idx = lax.top_k(logits, TOP_K)
    gates = jax.nn.softmax(top_val, axis=-1)
    e_flat = top_idx.reshape(TK)
    order = jnp.argsort(e_flat)
    e_s = e_flat[order]
    tok_s = order // TOP_K
    g_s = gates.reshape(TK)[order]
    counts = jnp.bincount(e_flat, length=N_EXPERTS)
    padded = (counts + MOE_BLK - 1) // MOE_BLK * MOE_BLK
    pad_end = jnp.cumsum(padded)
    pad_start = pad_end - padded
    start = jnp.cumsum(counts) - counts
    dest = pad_start[e_s] + jnp.arange(TK) - start[e_s]
    n_rows = TK + N_EXPERTS * MOE_BLK
    n_blk = n_rows // MOE_BLK
    buf = jnp.zeros((n_rows, D), h.dtype).at[dest].set(xt[tok_s])
    blk_e = jnp.minimum(jnp.searchsorted(pad_end, jnp.arange(n_blk) * MOE_BLK, side='right'), N_EXPERTS - 1)

    def expert_block(args):
        xb, e = args
        gu = xb @ w_gu[e] + b_gu[e]
        gate, up = gu[:, :EXPERT_FF], gu[:, EXPERT_FF:]
        gate = jnp.minimum(gate, SWIGLU_LIMIT)
        up = jnp.clip(up, -SWIGLU_LIMIT, SWIGLU_LIMIT)
        act = gate * jax.nn.sigmoid(SWIGLU_ALPHA * gate) * (up + 1.0)
        return act @ w_down[e] + b_down[e]

    ybuf = lax.map(expert_block, (buf.reshape(n_blk, MOE_BLK, D), blk_e)).reshape(n_rows, D)
    y = ybuf[dest] * g_s[:, None].astype(h.dtype)
    return jax.ops.segment_sum(y, tok_s, num_segments=T).reshape(B, S, D)


def kernel(x, c, positions, w_ada, b_ada, norm1_g, norm2_g, w_in, w_out, rwkv_mu, rwkv_w0, rwkv_w2, rwkv_a0, rwkv_a2, rwkv_g2, rwkv_kk, rwkv_ka, rwkv_rk, rwkv_gn_w, rwkv_gn_b, conv_w, dil_q_g, dil_k_g, nsa_q_g, nsa_kc_g, nsa_ks_g, nsa_kw_g, nsa_pe_k, nsa_pe_v, nsa_wk1, nsa_wk2, nsa_wv1, nsa_wv2, onorm_g, w_router, b_router, w_gu, b_gu, w_down, b_down):
    for l in range(DEPTH):
        mod = jax.nn.silu(c) @ w_ada[l] + b_ada[l]
        sh1, sc1, gt1, sh2, sc2, gt2 = [m[:, None, :] for m in jnp.split(mod, 6, axis=-1)]
        h = rmsnorm(x, norm1_g[l]) * (1.0 + sc1) + sh1
        y = hybrid_mixer(h, positions, w_in[l], w_out[l], rwkv_mu[l], rwkv_w0[l], rwkv_w2[l],
                         rwkv_a0[l], rwkv_a2[l], rwkv_g2[l], rwkv_kk[l], rwkv_ka[l], rwkv_rk[l],
                         rwkv_gn_w[l], rwkv_gn_b[l], conv_w[l], dil_q_g[l], dil_k_g[l],
                         nsa_q_g[l], nsa_kc_g[l], nsa_ks_g[l], nsa_kw_g[l], nsa_pe_k[l], nsa_pe_v[l],
                         nsa_wk1[l], nsa_wk2[l], nsa_wv1[l], nsa_wv2[l], onorm_g[l])
        x = x + gt1 * y
        h = rmsnorm(x, norm2_g[l]) * (1.0 + sc2) + sh2
        x = x + gt2 * moe_pallas(h, w_router[l], b_router[l], w_gu[l], b_gu[l], w_down[l], b_down[l])
    return x
```

```python
import functools

import numpy as np
import jax
import jax.numpy as jnp
from jax import lax
from jax.experimental import pallas as pl
from jax.experimental.pallas import tpu as pltpu

D_MODEL = 1024
DEPTH = 2

HEAD_DIM = 64
N_MIXERS = 4
GROUP_W = D_MODEL // N_MIXERS
N_HEADS = GROUP_W // HEAD_DIM
MIX_W = N_MIXERS * GROUP_W

RWKV_DECAY_RANK = 64
RWKV_AAA_RANK = 64
RWKV_GATE_RANK = 128
RWKV_DECAY_SCALE = 0.606531
RWKV_GN_EPS = 64e-5

CONV_W = 3

DIL_CFG = ((128, 1), (512, 4), (2048, 16))

CMP_LEN = 32
CMP_STRIDE = 16
CMP_HIDDEN = 256
SEL_BLK = 64
SEL_TOP = 16
SWA_WIN = 512
FORCE_SCORE = 1e4

N_EXPERTS = 32
TOP_K = 4
EXPERT_FF = D_MODEL
SWIGLU_LIMIT = 7.0
SWIGLU_ALPHA = 1.702

ROPE_THETA = 10000.0
NORM_EPS = 1e-6
NEG = -1e30

A_WIDTHS = (GROUP_W, GROUP_W, GROUP_W, RWKV_DECAY_RANK, RWKV_AAA_RANK, RWKV_GATE_RANK)
B_WIDTHS = (GROUP_W, GROUP_W, GROUP_W)
C_WIDTHS = (GROUP_W, GROUP_W, GROUP_W)
D_WIDTHS = (GROUP_W, HEAD_DIM, HEAD_DIM, HEAD_DIM, HEAD_DIM, HEAD_DIM, HEAD_DIM, 3 * N_HEADS)
SECTION_WIDTHS = (sum(A_WIDTHS), sum(B_WIDTHS), sum(C_WIDTHS), sum(D_WIDTHS))
PROJ_W = sum(SECTION_WIDTHS)

_F32 = jnp.float32
_BF16 = jnp.bfloat16
_NT = (((1,), (1,)), ((), ()))
_TN = (((0,), (0,)), ((), ()))


def _dot_hi(a, b, dims=None):
    if dims is None:
        return jnp.dot(a, b, precision=lax.Precision.HIGHEST, preferred_element_type=_F32)
    return lax.dot_general(a, b, dims, precision=lax.Precision.HIGHEST, preferred_element_type=_F32)


def _dot_lo(a, b, dims=None):
    a, b = a.astype(_BF16), b.astype(_BF16)
    if dims is None:
        return jnp.dot(a, b, preferred_element_type=_F32)
    return lax.dot_general(a, b, dims, preferred_element_type=_F32)


def _matmul_kernel(x_ref, w_ref, o_ref, *, exact):
    if exact:
        o_ref[...] = _dot_hi(x_ref[...], w_ref[...])
    else:
        o_ref[...] = jnp.dot(x_ref[...].astype(_BF16), w_ref[...], preferred_element_type=_F32)


def matmul(x, w, tm=512, tn=256, exact=False):
    M, K = x.shape
    N = w.shape[1]
    n_pad = -(-N // tn) * tn
    wb = jnp.pad(w if exact else w.astype(_BF16), ((0, 0), (0, n_pad - N)))
    out = pl.pallas_call(
        functools.partial(_matmul_kernel, exact=exact),
        grid=(M // tm, n_pad // tn),
        in_specs=[pl.BlockSpec((tm, K), lambda i, j: (i, 0)),
                  pl.BlockSpec((K, tn), lambda i, j: (0, j))],
        out_specs=pl.BlockSpec((tm, tn), lambda i, j: (i, j)),
        out_shape=jax.ShapeDtypeStruct((M, n_pad), _F32),
        name="matmul",
    )(x, wb)
    return out[:, :N]


RWKV_CHUNK = 64
RWKV_TILE = 128


def _group_indicator(n, group):
    r = lax.broadcasted_iota(jnp.int32, (n, n), 0) // group
    c = lax.broadcasted_iota(jnp.int32, (n, n), 1) // group
    return r == c


def _rwkv_kernel(za_ref, mu_ref, w0_ref, w2_ref, a0_ref, a2_ref, g2_ref, kk_ref, ka_ref, rk_ref,
                 gnw_ref, gnb_ref, o_ref, prev_sc, h_sc):
    C = RWKV_CHUNK
    TT = RWKV_TILE
    c = pl.program_id(1)

    @pl.when(c == 0)
    def _():
        prev_sc[...] = jnp.zeros_like(prev_sc)
        h_sc[...] = jnp.zeros_like(h_sc)

    z = za_ref[0]
    row = lax.broadcasted_iota(jnp.int32, z.shape, 0)
    zs = jnp.where(row == 0, prev_sc[...], pltpu.roll(z, 1, axis=0))
    prev_sc[...] = z[TT - 1:TT, :]
    z = z + (zs - z) * mu_ref[...]

    G = GROUP_W
    r, k, v = z[:, 0:G], z[:, G:2 * G], z[:, 2 * G:3 * G]
    o = 3 * G
    wd = z[:, o:o + RWKV_DECAY_RANK]
    ad = z[:, o + RWKV_DECAY_RANK:o + RWKV_DECAY_RANK + RWKV_AAA_RANK]
    gd = z[:, o + RWKV_DECAY_RANK + RWKV_AAA_RANK:]

    lw = -RWKV_DECAY_SCALE * jax.nn.sigmoid(w0_ref[...] + _dot_lo(jnp.tanh(wd), w2_ref[...]))
    a = jax.nn.sigmoid(a0_ref[...] + _dot_lo(ad, a2_ref[...]))
    g = _dot_lo(jax.nn.sigmoid(gd), g2_ref[...])

    head_sum = _group_indicator(G, HEAD_DIM).astype(_F32)
    kk = k * kk_ref[...]
    kk = kk * lax.rsqrt(_dot_hi(kk * kk, head_sum) + 1e-12)
    k = k * (1.0 + (a - 1.0) * ka_ref[...])
    b = kk * a

    ti = lax.broadcasted_iota(jnp.int32, (C, C), 0)
    tj = lax.broadcasted_iota(jnp.int32, (C, C), 1)
    incl = ti >= tj
    strict = ti > tj
    eye = ti == tj
    blk16 = (ti // 16) == (tj // 16)
    blk32 = (ti // 32) == (tj // 32)
    eye_f = eye.astype(_F32)

    ri = lax.broadcasted_iota(jnp.int32, (TT, TT), 0)
    rj = lax.broadcasted_iota(jnp.int32, (TT, TT), 1)
    chunk_tri = ((ri >= rj) & (ri // C == rj // C)).astype(_F32)
    cum = _dot_hi(chunk_tri, lw)
    g_in = jnp.exp(cum)
    A_all = -kk * jnp.exp(cum - lw)
    R_all = r * g_in
    g_inv = jnp.exp(-cum)
    B_all = b * g_inv
    K_all = k * g_inv

    units = [(ci, h) for ci in range(TT // C) for h in range(N_HEADS)]

    def part(t, u):
        ci, h = u
        return t[ci * C:(ci + 1) * C, h * HEAD_DIM:(h + 1) * HEAD_DIM]

    def g_end(u):
        ci, h = u
        return g_in[(ci + 1) * C - 1:(ci + 1) * C, h * HEAD_DIM:(h + 1) * HEAD_DIM]

    A = [part(A_all, u) for u in units]
    R = [part(R_all, u) for u in units]
    B = [part(B_all, u) for u in units]
    Kt = [part(K_all, u) for u in units]
    V = [part(v, u) for u in units]
    n = range(len(units))
    gram = [_dot_lo(jnp.concatenate([A[i], R[i]], axis=0), jnp.concatenate([B[i], Kt[i]], axis=0), _NT) for i in n]
    l_ab = [jnp.where(strict, gram[i][0:C, 0:C], 0.0) for i in n]
    l_ak = [jnp.where(strict, gram[i][0:C, C:2 * C], 0.0) for i in n]
    m_rb = [jnp.where(incl, gram[i][C:2 * C, 0:C], 0.0) for i in n]
    m_rk = [jnp.where(incl, gram[i][C:2 * C, C:2 * C], 0.0) for i in n]
    p = [jnp.where(blk16, l_ab[i], 0.0) for i in n]
    x = [eye_f + p[i] for i in n]
    for _ in range(3):
        p = [_dot_lo(p[i], p[i]) for i in n]
        x = [_dot_lo(x[i], eye_f + p[i]) for i in n]
    for lvl in (blk32 & ~blk16, ~blk32):
        xl = [_dot_lo(x[i], jnp.where(lvl, l_ab[i], 0.0)) for i in n]
        x = [x[i] + _dot_lo(xl[i], x[i]) for i in n]
    lv = [_dot_lo(l_ak[i], V[i]) for i in n]
    tap = [_dot_lo(x[i], jnp.concatenate([A[i], lv[i]], axis=1)) for i in n]
    m1 = [_dot_lo(m_rb[i], tap[i]) for i in n]
    mv = [_dot_lo(m_rk[i], V[i]) for i in n]
    bt = [_dot_lo(B[i] * g_end(units[i]), tap[i], _TN) for i in n]
    kv = [_dot_lo(Kt[i] * g_end(units[i]), V[i], _TN) for i in n]
    w_yh = [jnp.concatenate([R[i] + m1[i][:, 0:HEAD_DIM],
                             jnp.where(eye, g_end(units[i]), 0.0) + bt[i][:, 0:HEAD_DIM]], axis=0) for i in n]
    y0 = [m1[i][:, HEAD_DIM:] + mv[i] for i in n]
    h_add = [bt[i][:, HEAD_DIM:] + kv[i] for i in n]
    state = [h_sc[h] for h in range(N_HEADS)]
    y_rows = []
    for ci in range(TT // C):
        ys = []
        for h in range(N_HEADS):
            i = ci * N_HEADS + h
            nxt = _dot_hi(w_yh[i], state[h])
            ys.append(nxt[0:C] + y0[i])
            state[h] = nxt[C:2 * C] + h_add[i]
        y_rows.append(jnp.concatenate(ys, axis=1))
    for h in range(N_HEADS):
        h_sc[h] = state[h]
    y = jnp.concatenate(y_rows, axis=0)

    head_mean = head_sum * (1.0 / HEAD_DIM)
    mu = _dot_hi(y, head_mean)
    d = y - mu
    var = _dot_hi(d * d, head_mean)
    y = d * lax.rsqrt(var + RWKV_GN_EPS) * gnw_ref[...] + gnb_ref[...]
    y = y + _dot_hi(r * k * rk_ref[...], head_sum) * v
    o_ref[0] = y * g


def rwkv7_pallas(za, mu, w0, w2, a0, a2, g2, k_k, k_a, r_k, gn_w, gn_b):
    B, S, W = za.shape
    TT = RWKV_TILE
    row = lambda t: t.reshape(1, -1).astype(_F32)
    full = lambda t: pl.BlockSpec(t.shape, lambda b, c: (0,) * t.ndim)
    params = [row(mu), row(w0), w2.astype(_BF16), row(a0), a2.astype(_BF16), g2.astype(_BF16),
              row(k_k), row(k_a), row(r_k), row(gn_w), row(gn_b)]
    return pl.pallas_call(
        _rwkv_kernel,
        grid=(B, S // TT),
        in_specs=[pl.BlockSpec((1, TT, W), lambda b, c: (b, c, 0))] + [full(p) for p in params],
        out_specs=pl.BlockSpec((1, TT, GROUP_W), lambda b, c: (b, c, 0)),
        out_shape=jax.ShapeDtypeStruct((B, S, GROUP_W), _F32),
        scratch_shapes=[pltpu.VMEM((1, W), _F32), pltpu.VMEM((N_HEADS, HEAD_DIM, HEAD_DIM), _F32)],
        compiler_params=pltpu.CompilerParams(dimension_semantics=("parallel", "arbitrary")),
        name="rwkv7",
    )(za, *params)


MOE_TM = 512
MOE_FC = 512
MOE_CAST_ROWS = 128


def _moe_ffn_kernel(blk_e_ref, n_used_ref, x_ref, wgu_ref, bgu_ref, wd_ref, bd_ref, o_ref, wgu_sc, wd_sc):
    i = pl.program_id(0)
    F = EXPERT_FF
    new_expert = jnp.logical_or(i == 0, blk_e_ref[i] != blk_e_ref[jnp.maximum(i - 1, 0)])

    @pl.when(new_expert)
    def _():
        def cast(j, carry):
            rows = pl.ds(pl.multiple_of(j * MOE_CAST_ROWS, MOE_CAST_ROWS), MOE_CAST_ROWS)
            wgu_sc[rows, :] = wgu_ref[0, rows, :].astype(_BF16)
            wd_sc[rows, :] = wd_ref[0, rows, :].astype(_BF16)
            return carry
        lax.fori_loop(0, F // MOE_CAST_ROWS, cast, 0)

    @pl.when(i < n_used_ref[0])
    def _():
        x = x_ref[...]
        acc = None
        for c in range(F // MOE_FC):
            lo = c * MOE_FC
            gate = jnp.dot(x, wgu_sc[:, lo:lo + MOE_FC], preferred_element_type=_F32) + bgu_ref[0, :, lo:lo + MOE_FC]
            up = jnp.dot(x, wgu_sc[:, F + lo:F + lo + MOE_FC], preferred_element_type=_F32) + bgu_ref[0, :, F + lo:F + lo + MOE_FC]
            gate = jnp.minimum(gate, SWIGLU_LIMIT)
            up = jnp.clip(up, -SWIGLU_LIMIT, SWIGLU_LIMIT)
            act = gate * jax.nn.sigmoid(SWIGLU_ALPHA * gate) * (up + 1.0)
            part = jnp.dot(act.astype(_BF16), wd_sc[lo:lo + MOE_FC, :], preferred_element_type=_F32)
            acc = part if acc is None else acc + part
        o_ref[...] = acc + bd_ref[0]


def moe_ffn(xs, blk_e, n_used, w_gu, b_gu, w_down, b_down):
    n_rows, D = xs.shape
    E, _, F2 = w_gu.shape
    assert D == EXPERT_FF and F2 == 2 * EXPERT_FF
    n_blk = n_rows // MOE_TM
    grid_spec = pltpu.PrefetchScalarGridSpec(
        num_scalar_prefetch=2,
        grid=(n_blk,),
        in_specs=[pl.BlockSpec((MOE_TM, D), lambda i, e, n: (i, 0)),
                  pl.BlockSpec((1, D, F2), lambda i, e, n: (e[i], 0, 0)),
                  pl.BlockSpec((1, 1, F2), lambda i, e, n: (e[i], 0, 0)),
                  pl.BlockSpec((1, F2 // 2, D), lambda i, e, n: (e[i], 0, 0)),
                  pl.BlockSpec((1, 1, D), lambda i, e, n: (e[i], 0, 0))],
        out_specs=pl.BlockSpec((MOE_TM, D), lambda i, e, n: (i, 0)),
        scratch_shapes=[pltpu.VMEM((D, F2), _BF16), pltpu.VMEM((F2 // 2, D), _BF16)],
    )
    return pl.pallas_call(
        _moe_ffn_kernel,
        grid_spec=grid_spec,
        out_shape=jax.ShapeDtypeStruct((n_rows, D), _F32),
        compiler_params=pltpu.CompilerParams(dimension_semantics=("arbitrary",),
                                             vmem_limit_bytes=52 * 1024 * 1024),
        name="moe_ffn",
    )(blk_e, n_used, xs, w_gu, b_gu.reshape(E, 1, F2), w_down, b_down.reshape(E, 1, D))


def moe_pallas(h, w_router, b_router, w_gu, b_gu, w_down, b_down):
    B, S, D = h.shape
    T = B * S
    TK = T * TOP_K
    TM = MOE_TM
    xt = h.reshape(T, D)
    logits = matmul(xt, w_router, tn=128, exact=True) + b_router
    top_val, top_idx = lax.top_k(logits, TOP_K)
    gates = jax.nn.softmax(top_val, axis=-1)
    e_flat = top_idx.reshape(TK)
    order = jnp.argsort(e_flat)
    counts = jnp.bincount(e_flat, length=N_EXPERTS)
    padded = (counts + TM - 1) // TM * TM
    pad_end = jnp.cumsum(padded)
    pad_start = pad_end - padded
    start = jnp.cumsum(counts) - counts
    e_s = e_flat[order]
    dest = pad_start[e_s] + jnp.arange(TK) - start[e_s]
    row_of = dest[jnp.argsort(order)].astype(jnp.int32)
    n_rows = TK + N_EXPERTS * TM
    n_blk = n_rows // TM
    n_used = (pad_end[-1] // TM).astype(jnp.int32).reshape(1)
    blk_e = jnp.minimum(jnp.searchsorted(pad_end, jnp.arange(n_blk) * TM, side='right'), N_EXPERTS - 1)
    rows = jnp.arange(n_rows)
    e_row = blk_e[rows // TM]
    slot = rows - pad_start[e_row] + start[e_row]
    valid = slot < start[e_row] + counts[e_row]
    src_tok = jnp.where(valid, order[jnp.clip(slot, 0, TK - 1)] // TOP_K, 0).astype(jnp.int32)
    blk_e = jnp.where(jnp.arange(n_blk) < n_used[0], blk_e, blk_e[jnp.maximum(n_used[0] - 1, 0)]).astype(jnp.int32)
    xs = jnp.take(xt.astype(_BF16), src_tok, axis=0)
    ybuf = moe_ffn(xs, blk_e, n_used, w_gu, b_gu, w_down, b_down)
    y = jnp.take(ybuf, row_of, axis=0).reshape(T, TOP_K, D) * gates[:, :, None]
    return jnp.sum(y, axis=1).reshape(B, S, D)


ATT_T = 128
ATT_TK = 512
ATT_R = ATT_TK // ATT_T
NSA_NSEL = 64


def _window_table(window, weight_of_distance):
    T, TK = ATT_T, ATT_TK
    n = window // T + ATT_R + 1
    d = (np.arange(n)[:, None, None] * T + np.arange(T)[None, None, :] - np.arange(TK)[None, :, None])
    return jnp.asarray(weight_of_distance(d).astype(np.float32))


def _first_key_tile(qi, window):
    return jnp.maximum(qi * ATT_T - window, 0) // ATT_TK


def _swa_weight(d):
    return (d >= 0) & (d <= SWA_WIN - 1)


def _dil_weight(d):
    return sum(((d >= 0) & (d <= window) & (d % dil == 0)).astype(np.int32) for window, dil in DIL_CFG)


def _softmax_tile(s, w, m_ref, l_ref, weighted):
    s = jnp.where(w > 0.0, s, NEG)
    m_prev = m_ref[...]
    m_new = jnp.maximum(m_prev, jnp.max(s, axis=0, keepdims=True))
    alpha = jnp.exp(m_prev - m_new)
    p = jnp.exp(s - m_new)
    if weighted:
        p = p * w
    l_ref[...] = alpha * l_ref[...] + jnp.sum(p, axis=0, keepdims=True)
    m_ref[...] = m_new
    return alpha, p.astype(_BF16)


def _init_stats(m_sc, l_sc, acc_sc):
    m_sc[...] = jnp.full_like(m_sc, NEG)
    l_sc[...] = jnp.zeros_like(l_sc)
    acc_sc[...] = jnp.zeros_like(acc_sc)


def _heads_to_rows(ot):
    T = ATT_T
    pairs = []
    for h in range(0, N_HEADS, 2):
        two = jnp.concatenate([ot[:, h * T:(h + 1) * T], ot[:, (h + 1) * T:(h + 2) * T]], axis=0)
        pairs.append(two.T)
    return jnp.concatenate(pairs, axis=1)


def _nsa_attn_kernel(q_ref, ks_ref, vst_ref, kw_ref, vwt_ref, selt_ref, swa_ref, oslc_ref, oswa_ref,
                     m_sc, l_sc, acc_sc):
    T, TK = ATT_T, ATT_TK
    qi = pl.program_id(1)
    q = q_ref[0]
    q = jnp.concatenate([q[:, h * HEAD_DIM:(h + 1) * HEAD_DIM] for h in range(N_HEADS)], axis=0)
    selt = selt_ref[0]
    key_blk = lax.broadcasted_iota(jnp.int32, (TK, NSA_NSEL), 0) // SEL_BLK
    blk_id = lax.broadcasted_iota(jnp.int32, (TK, NSA_NSEL), 1)
    kr = lax.broadcasted_iota(jnp.int32, (TK, T), 0)
    qc = lax.broadcasted_iota(jnp.int32, (TK, T), 1)

    def step(k_ref, vt_ref, kt, w):
        base = pl.multiple_of(kt * TK, TK)
        s = lax.dot_general(k_ref[0, pl.ds(base, TK), :], q, _NT, preferred_element_type=_F32)
        alpha, p = _softmax_tile(s, jnp.concatenate([w] * N_HEADS, axis=1), m_sc, l_sc, False)
        acc_sc[...] = alpha * acc_sc[...] + jnp.dot(vt_ref[0, :, pl.ds(base, TK)], p, preferred_element_type=_F32)

    def finish(o_ref):
        o_ref[0] = _heads_to_rows(acc_sc[...] / l_sc[...])

    last = qi // ATT_R
    _init_stats(m_sc, l_sc, acc_sc)

    def slc_step(kt, carry):
        in_blk = (key_blk + kt * (TK // SEL_BLK) == blk_id).astype(_BF16)
        w = jnp.dot(in_blk, selt, preferred_element_type=_F32)
        w = jnp.where(kt * TK + kr <= qi * T + qc, w, 0.0)
        step(ks_ref, vst_ref, kt, w)
        return carry

    lax.fori_loop(0, last + 1, slc_step, 0)
    finish(oslc_ref)

    _init_stats(m_sc, l_sc, acc_sc)

    def swa_step(kt, carry):
        step(kw_ref, vwt_ref, kt, swa_ref[qi - kt * ATT_R])
        return carry

    lax.fori_loop(_first_key_tile(qi, SWA_WIN - 1), last + 1, swa_step, 0)
    finish(oswa_ref)


def nsa_attn_pallas(q_r, ks_r, vs, kw_r, vw, sel_mask):
    B, S, W = q_r.shape
    assert S // SEL_BLK == NSA_NSEL
    T = ATT_T
    R = N_HEADS * T
    bf = lambda t: t.astype(_BF16)
    tr = lambda t: jnp.swapaxes(bf(t), 1, 2)
    swa_tab = _window_table(SWA_WIN - 1, _swa_weight)
    k_spec = pl.BlockSpec((1, S, HEAD_DIM), lambda b, i: (b, 0, 0))
    vt_spec = pl.BlockSpec((1, HEAD_DIM, S), lambda b, i: (b, 0, 0))
    q_spec = pl.BlockSpec((1, T, W), lambda b, i: (b, i, 0))
    return pl.pallas_call(
        _nsa_attn_kernel,
        grid=(B, S // T),
        in_specs=[q_spec, k_spec, vt_spec, k_spec, vt_spec,
                  pl.BlockSpec((1, NSA_NSEL, T), lambda b, i: (b, 0, i)),
                  pl.BlockSpec(swa_tab.shape, lambda b, i: (0, 0, 0))],
        out_specs=[q_spec, q_spec],
        out_shape=[jax.ShapeDtypeStruct((B, S, W), _F32)] * 2,
        scratch_shapes=[pltpu.VMEM((1, R), _F32), pltpu.VMEM((1, R), _F32), pltpu.VMEM((HEAD_DIM, R), _F32)],
        compiler_params=pltpu.CompilerParams(dimension_semantics=("parallel", "arbitrary")),
        name="nsa_attn",
    )(bf(q_r * (HEAD_DIM ** -0.5)), bf(ks_r), tr(vs), bf(kw_r), tr(vw), tr(sel_mask), swa_tab)


def _dil_attn_kernel(q_ref, k_ref, vt_ref, tab_ref, o_ref, m_sc, l_sc, acc_sc):
    T, TK = ATT_T, ATT_TK
    qi = pl.program_id(1)
    _init_stats(m_sc, l_sc, acc_sc)
    qs = [q_ref[0, :, h * HEAD_DIM:(h + 1) * HEAD_DIM] for h in range(N_HEADS)]

    def step(kt, carry):
        base = pl.multiple_of(kt * TK, TK)
        w = tab_ref[qi - kt * ATT_R]
        s = jnp.concatenate([lax.dot_general(k_ref[0, h, pl.ds(base, TK), :], qs[h], _NT, preferred_element_type=_F32)
                             for h in range(N_HEADS)], axis=1)
        alpha, p = _softmax_tile(s, jnp.concatenate([w] * N_HEADS, axis=1), m_sc, l_sc, True)
        pv = jnp.concatenate([jnp.dot(vt_ref[0, h, :, pl.ds(base, TK)], p[:, h * T:(h + 1) * T],
                                      preferred_element_type=_F32) for h in range(N_HEADS)], axis=1)
        acc_sc[...] = alpha * acc_sc[...] + pv
        return carry

    lax.fori_loop(_first_key_tile(qi, DIL_CFG[-1][0]), qi // ATT_R + 1, step, 0)
    o_ref[0] = _heads_to_rows(acc_sc[...] / l_sc[...])


def dil_attn_pallas(q_r, k_r, v):
    B, S, H, hd = q_r.shape
    T = ATT_T
    R = H * T
    W = H * hd
    bf = lambda t: t.astype(_BF16)
    tab = _window_table(DIL_CFG[-1][0], _dil_weight)
    q_spec = pl.BlockSpec((1, T, W), lambda b, i: (b, i, 0))
    return pl.pallas_call(
        _dil_attn_kernel,
        grid=(B, S // T),
        in_specs=[q_spec,
                  pl.BlockSpec((1, H, S, hd), lambda b, i: (b, 0, 0, 0)),
                  pl.BlockSpec((1, H, hd, S), lambda b, i: (b, 0, 0, 0)),
                  pl.BlockSpec(tab.shape, lambda b, i: (0, 0, 0))],
        out_specs=q_spec,
        out_shape=jax.ShapeDtypeStruct((B, S, W), _F32),
        scratch_shapes=[pltpu.VMEM((1, R), _F32), pltpu.VMEM((1, R), _F32), pltpu.VMEM((hd, R), _F32)],
        compiler_params=pltpu.CompilerParams(dimension_semantics=("parallel", "arbitrary")),
        name="dil_attn",
    )(bf(q_r * (hd ** -0.5)).reshape(B, S, W), bf(k_r).transpose(0, 2, 1, 3), bf(v).transpose(0, 2, 3, 1), tab)


def split_cols(z, widths):
    offs = np.cumsum((0,) + tuple(widths))
    return [z[..., int(offs[i]):int(offs[i + 1])] for i in range(len(widths))]


def rmsnorm(x, g):
    xf = x.astype(jnp.float32)
    y = xf * lax.rsqrt(jnp.mean(xf * xf, axis=-1, keepdims=True) + NORM_EPS)
    return (y * g.astype(jnp.float32)).astype(x.dtype)


def rope(x, pos):
    half = x.shape[-1] // 2
    inv = ROPE_THETA ** (-jnp.arange(half, dtype=jnp.float32) / half)
    ang = pos.astype(jnp.float32)[:, :, None, None] * inv
    cos, sin = jnp.cos(ang), jnp.sin(ang)
    xf = x.astype(jnp.float32)
    x1, x2 = xf[..., :half], xf[..., half:]
    return jnp.concatenate([x1 * cos - x2 * sin, x2 * cos + x1 * sin], axis=-1).astype(x.dtype)


def short_conv_mix(bg, cg, xin, conv_w):
    u = cg * xin
    y = lax.conv_general_dilated(u, conv_w[:, None, :].astype(u.dtype), window_strides=(1,),
                                 padding=((CONV_W - 1, 0),), dimension_numbers=('NWC', 'WIO', 'NWC'),
                                 feature_group_count=u.shape[-1])
    return bg * y


def dilated_mix(q, k, v, q_g, k_g, pos):
    q = rope(rmsnorm(q, q_g), pos)
    k = rope(rmsnorm(k, k_g), pos)
    return dil_attn_pallas(q, k, v)


def selection_overlap(n_c, n_sel):
    r = SEL_BLK // CMP_STRIDE
    m = CMP_LEN // CMP_STRIDE
    diff = np.arange(n_c)[:, None] - r * np.arange(n_sel)[None, :]
    offs = (np.arange(r)[:, None] - np.arange(m)[None, :]).reshape(-1)
    return (diff[..., None] == offs).sum(-1).astype(np.float32)


def nsa_mix(q, kc, vc, ks, vs, kw, vw, gl, q_g, kc_g, ks_g, kw_g, pe_k, pe_v, wk1, wk2, wv1, wv2, pos):
    B, S, H, hd = q.shape
    f32 = jnp.float32
    scale = hd ** -0.5
    q_n = rmsnorm(q, q_g)
    q_r = rope(q_n, pos)
    qidx = np.arange(S)
    n_c = (S - CMP_LEN) // CMP_STRIDE + 1
    cidx = np.arange(n_c)[:, None] * CMP_STRIDE + np.arange(CMP_LEN)[None, :]

    def compress(t, pe, w1, w2):
        blk = (t[:, cidx] + pe).reshape(B, n_c, CMP_LEN * hd)
        return jax.nn.gelu(blk @ w1) @ w2

    k_cmp = rmsnorm(compress(kc, pe_k, wk1, wk2), kc_g)
    v_cmp = compress(vc, pe_v, wv1, wv2)
    s = jnp.einsum('bshd,bcd->bhsc', q_n, k_cmp).astype(f32) * scale
    valid = cidx[:, -1][None, :] <= qidx[:, None]
    s = jnp.where(valid, s, NEG)
    p_cmp = jax.nn.softmax(s, axis=-1) * valid.any(-1, keepdims=True).astype(np.float32)
    o_cmp = jnp.einsum('bhsc,bcd->bshd', p_cmp.astype(v_cmp.dtype), v_cmp)
    n_sel = S // SEL_BLK
    n_top = min(SEL_TOP, n_sel)
    imp = jnp.einsum('bhsc,cj->bsj', p_cmp, selection_overlap(n_c, n_sel))
    cur = qidx[:, None] // SEL_BLK
    jb = np.arange(n_sel)[None, :]
    future = jb > cur
    forced = (jb == 0) | (jb == cur) | (jb == cur - 1)
    score = jnp.where(future, -1.0, jnp.where(forced, FORCE_SCORE, imp))
    _, sel = lax.top_k(score, n_top)
    ks_r = rope(rmsnorm(ks[:, :, None, :], ks_g), pos)[:, :, 0]
    kw_r = rope(rmsnorm(kw[:, :, None, :], kw_g), pos)[:, :, 0]
    sel_mask = (sel[..., None] == jnp.arange(n_sel)).any(-2)
    o_slc, o_swa = nsa_attn_pallas(q_r.reshape(B, S, H * hd), ks_r, vs, kw_r, vw, sel_mask)
    o_slc = o_slc.reshape(B, S, H, hd)
    o_swa = o_swa.reshape(B, S, H, hd)
    gt = jax.nn.sigmoid(gl.astype(f32)).reshape(B, S, H, 3).astype(q.dtype)
    o = gt[..., 0:1] * o_cmp + gt[..., 1:2] * o_slc + gt[..., 2:3] * o_swa
    return o.reshape(B, S, H * hd)


def hybrid_mixer(h, positions, w_in, w_out, rwkv_mu, rwkv_w0, rwkv_w2, rwkv_a0, rwkv_a2, rwkv_g2,
                 rwkv_kk, rwkv_ka, rwkv_rk, rwkv_gn_w, rwkv_gn_b, conv_w, dil_q_g, dil_k_g,
                 nsa_q_g, nsa_kc_g, nsa_ks_g, nsa_kw_g, nsa_pe_k, nsa_pe_v, nsa_wk1, nsa_wk2,
                 nsa_wv1, nsa_wv2, onorm_g):
    B, S, D = h.shape
    proj = matmul(h.reshape(B * S, D), w_in).reshape(B, S, PROJ_W)
    za, zb, zc, zd = split_cols(proj, SECTION_WIDTHS)
    y_a = rwkv7_pallas(za, rwkv_mu, rwkv_w0, rwkv_w2, rwkv_a0, rwkv_a2, rwkv_g2,
                       rwkv_kk, rwkv_ka, rwkv_rk, rwkv_gn_w, rwkv_gn_b)
    bg, cg, xin = split_cols(zb, B_WIDTHS)
    y_b = short_conv_mix(bg, cg, xin, conv_w)
    cq, ck, cv = [t.reshape(B, S, N_HEADS, HEAD_DIM) for t in split_cols(zc, C_WIDTHS)]
    y_c = dilated_mix(cq, ck, cv, dil_q_g, dil_k_g, positions)
    dq, dkc, dvc, dks, dvs, dkw, dvw, dg = split_cols(zd, D_WIDTHS)
    y_d = nsa_mix(dq.reshape(B, S, N_HEADS, HEAD_DIM), dkc, dvc, dks, dvs, dkw, dvw, dg,
                  nsa_q_g, nsa_kc_g, nsa_ks_g, nsa_kw_g, nsa_pe_k, nsa_pe_v,
                  nsa_wk1, nsa_wk2, nsa_wv1, nsa_wv2, positions)
    y_bcd = jnp.concatenate([y_b, y_c, y_d], axis=-1).reshape(B, S, 3 * N_HEADS, HEAD_DIM)
    y_bcd = rmsnorm(y_bcd, onorm_g.reshape(3 * N_HEADS, HEAD_DIM)).reshape(B, S, 3 * GROUP_W)
    y = jnp.concatenate([y_a, y_bcd], axis=-1)
    return matmul(y.reshape(B * S, MIX_W), w_out).reshape(B, S, D)


def kernel(x, c, positions, w_ada, b_ada, norm1_g, norm2_g, w_in, w_out, rwkv_mu, rwkv_w0, rwkv_w2, rwkv_a0, rwkv_a2, rwkv_g2, rwkv_kk, rwkv_ka, rwkv_rk, rwkv_gn_w, rwkv_gn_b, conv_w, dil_q_g, dil_k_g, nsa_q_g, nsa_kc_g, nsa_ks_g, nsa_kw_g, nsa_pe_k, nsa_pe_v, nsa_wk1, nsa_wk2, nsa_wv1, nsa_wv2, onorm_g, w_router, b_router, w_gu, b_gu, w_down, b_down):
    for l in range(DEPTH):
        mod = jax.nn.silu(c) @ w_ada[l] + b_ada[l]
        sh1, sc1, gt1, sh2, sc2, gt2 = [m[:, None, :] for m in jnp.split(mod, 6, axis=-1)]
        h = rmsnorm(x, norm1_g[l]) * (1.0 + sc1) + sh1
        y = hybrid_mixer(h, positions, w_in[l], w_out[l], rwkv_mu[l], rwkv_w0[l], rwkv_w2[l],
                         rwkv_a0[l], rwkv_a2[l], rwkv_g2[l], rwkv_kk[l], rwkv_ka[l], rwkv_rk[l],
                         rwkv_gn_w[l], rwkv_gn_b[l], conv_w[l], dil_q_g[l], dil_k_g[l],
                         nsa_q_g[l], nsa_kc_g[l], nsa_ks_g[l], nsa_kw_g[l], nsa_pe_k[l], nsa_pe_v[l],
                         nsa_wk1[l], nsa_wk2[l], nsa_wv1[l], nsa_wv2[l], onorm_g[l])
        x = x + gt1 * y
        h = rmsnorm(x, norm2_g[l]) * (1.0 + sc2) + sh2
        x = x + gt2 * moe_pallas(h, w_router[l], b_router[l], w_gu[l], b_gu[l], w_down[l], b_down[l])
    return x
```

```python
import functools

import numpy as np
import jax
import jax.numpy as jnp
from jax import lax
from jax.experimental import pallas as pl
from jax.experimental.pallas import tpu as pltpu

D_MODEL = 1024
DEPTH = 2

HEAD_DIM = 64
N_MIXERS = 4
GROUP_W = D_MODEL // N_MIXERS
N_HEADS = GROUP_W // HEAD_DIM
MIX_W = N_MIXERS * GROUP_W

RWKV_DECAY_RANK = 64
RWKV_AAA_RANK = 64
RWKV_GATE_RANK = 128
RWKV_DECAY_SCALE = 0.606531
RWKV_GN_EPS = 64e-5

CONV_W = 3

DIL_CFG = ((128, 1), (512, 4), (2048, 16))

CMP_LEN = 32
CMP_STRIDE = 16
CMP_HIDDEN = 256
SEL_BLK = 64
SEL_TOP = 16
SWA_WIN = 512
FORCE_SCORE = 1e4

N_EXPERTS = 32
TOP_K = 4
EXPERT_FF = D_MODEL
SWIGLU_LIMIT = 7.0
SWIGLU_ALPHA = 1.702

ROPE_THETA = 10000.0
NORM_EPS = 1e-6
NEG = -1e30

A_WIDTHS = (GROUP_W, GROUP_W, GROUP_W, RWKV_DECAY_RANK, RWKV_AAA_RANK, RWKV_GATE_RANK)
B_WIDTHS = (GROUP_W, GROUP_W, GROUP_W)
C_WIDTHS = (GROUP_W, GROUP_W, GROUP_W)
D_WIDTHS = (GROUP_W, HEAD_DIM, HEAD_DIM, HEAD_DIM, HEAD_DIM, HEAD_DIM, HEAD_DIM, 3 * N_HEADS)
SECTION_WIDTHS = (sum(A_WIDTHS), sum(B_WIDTHS), sum(C_WIDTHS), sum(D_WIDTHS))
PROJ_W = sum(SECTION_WIDTHS)

_F32 = jnp.float32
_BF16 = jnp.bfloat16
_NT = (((1,), (1,)), ((), ()))
_TN = (((0,), (0,)), ((), ()))


def _dot_hi(a, b, dims=None):
    if dims is None:
        return jnp.dot(a, b, precision=lax.Precision.HIGHEST, preferred_element_type=_F32)
    return lax.dot_general(a, b, dims, precision=lax.Precision.HIGHEST, preferred_element_type=_F32)


def _dot_lo(a, b, dims=None):
    a, b = a.astype(_BF16), b.astype(_BF16)
    if dims is None:
        return jnp.dot(a, b, preferred_element_type=_F32)
    return lax.dot_general(a, b, dims, preferred_element_type=_F32)


def _matmul_kernel(x_ref, w_ref, o_ref, *, exact):
    if exact:
        o_ref[...] = _dot_hi(x_ref[...], w_ref[...])
    else:
        o_ref[...] = jnp.dot(x_ref[...].astype(_BF16), w_ref[...], preferred_element_type=_F32)


def matmul(x, w, tm=512, tn=256, exact=False):
    M, K = x.shape
    N = w.shape[1]
    n_pad = -(-N // tn) * tn
    wb = jnp.pad(w if exact else w.astype(_BF16), ((0, 0), (0, n_pad - N)))
    out = pl.pallas_call(
        functools.partial(_matmul_kernel, exact=exact),
        grid=(M // tm, n_pad // tn),
        in_specs=[pl.BlockSpec((tm, K), lambda i, j: (i, 0)),
                  pl.BlockSpec((K, tn), lambda i, j: (0, j))],
        out_specs=pl.BlockSpec((tm, tn), lambda i, j: (i, j)),
        out_shape=jax.ShapeDtypeStruct((M, n_pad), _F32),
        name="matmul",
    )(x, wb)
    return out[:, :N]


PROJ_PAD = -(-PROJ_W // 256) * 256
PROJ_TM = 256
PROJ_TN = 512
ROUTER_PAD = 128
GATE_COL_BLOCK = (PROJ_W - 3 * N_HEADS) // 128
assert GATE_COL_BLOCK * 128 == PROJ_W - 3 * N_HEADS


def _norm_mod(x, g, sc, sh):
    y = x * lax.rsqrt(jnp.mean(x * x, axis=-1, keepdims=True) + NORM_EPS) * g
    return y * (1.0 + sc) + sh


def _in_proj_kernel(x_ref, g_ref, sc_ref, sh_ref, w_ref, o_ref):
    h = _norm_mod(x_ref[0], g_ref[...], sc_ref[0], sh_ref[0]).astype(_BF16)
    for n0 in range(0, PROJ_PAD, PROJ_TN):
        o_ref[0, :, n0:n0 + PROJ_TN] = jnp.dot(h, w_ref[:, n0:n0 + PROJ_TN], preferred_element_type=_F32)


def in_proj_pallas(x, g, sc, sh, w_in):
    B, S, D = x.shape
    tm = PROJ_TM
    w = jnp.pad(w_in.astype(_BF16), ((0, 0), (0, PROJ_PAD - PROJ_W)))
    per_batch = pl.BlockSpec((1, 1, D), lambda b, i: (b, 0, 0))
    return pl.pallas_call(
        _in_proj_kernel,
        grid=(B, S // tm),
        in_specs=[pl.BlockSpec((1, tm, D), lambda b, i: (b, i, 0)),
                  pl.BlockSpec((1, D), lambda b, i: (0, 0)), per_batch, per_batch,
                  pl.BlockSpec((D, PROJ_PAD), lambda b, i: (0, 0))],
        out_specs=pl.BlockSpec((1, tm, PROJ_PAD), lambda b, i: (b, i, 0)),
        out_shape=jax.ShapeDtypeStruct((B, S, PROJ_PAD), _F32),
        compiler_params=pltpu.CompilerParams(dimension_semantics=("parallel", "parallel"),
                                             vmem_limit_bytes=48 * 1024 * 1024),
        name="in_proj",
    )(x, g.reshape(1, D), sc, sh, w)


def _mix_out_kernel(x_ref, ya_ref, bg_ref, cg_ref, xin_ref, cgh_ref, xinh_ref, yc_ref, ocmp_ref, oslc_ref, oswa_ref,
                    gl_ref, convw_ref, ong_ref, wout_ref, gt1_ref, g2_ref, sc2_ref, sh2_ref, wr_ref, br_ref,
                    xo_ref, hb_ref, lg_ref):
    i = pl.program_id(1)
    G = GROUP_W
    u = cg_ref[0] * xin_ref[0]
    halo = jnp.where(i == 0, 0.0, cgh_ref[0] * xinh_ref[0])
    row = lax.broadcasted_iota(jnp.int32, u.shape, 0)
    u1 = jnp.where(row == 0, halo[7:8], pltpu.roll(u, 1, axis=0))
    u2 = jnp.where(row == 0, halo[6:7], jnp.where(row == 1, halo[7:8], pltpu.roll(u, 2, axis=0)))
    cw = convw_ref[...]
    yb = bg_ref[0] * (cw[0:1] * u2 + cw[1:2] * u1 + cw[2:3] * u)

    sg = jax.nn.sigmoid(gl_ref[0])
    er = lax.broadcasted_iota(jnp.int32, (128, G), 0)
    ec = lax.broadcasted_iota(jnp.int32, (128, G), 1) // HEAD_DIM
    yd = None
    for j, o_ref in enumerate((ocmp_ref, oslc_ref, oswa_ref)):
        gate = _dot_hi(sg, (er == 3 * ec + j).astype(_F32))
        yd = gate * o_ref[0] if yd is None else yd + gate * o_ref[0]

    head_mean = _group_indicator(G, HEAD_DIM).astype(_F32) * (1.0 / HEAD_DIM)
    ong = ong_ref[...]
    parts = [ya_ref[0]]
    for n, y in enumerate((yb, yc_ref[0], yd)):
        parts.append(y * lax.rsqrt(_dot_hi(y * y, head_mean) + NORM_EPS) * ong[:, n * G:(n + 1) * G])
    mixed = jnp.dot(jnp.concatenate(parts, axis=1).astype(_BF16), wout_ref[...], preferred_element_type=_F32)
    x = x_ref[0] + gt1_ref[0] * mixed
    xo_ref[0] = x
    h = _norm_mod(x, g2_ref[...], sc2_ref[0], sh2_ref[0])
    hb_ref[0] = h.astype(_BF16)
    lg_ref[0] = _dot_hi(h, wr_ref[...]) + br_ref[...]


def mix_out_pallas(x, y_a, proj, y_c, o_cmp, o_slc, o_swa, conv_w, onorm_g, w_out, gt1, g2, sc2, sh2, w_router, b_router):
    B, S, D = x.shape
    tm = PROJ_TM
    G = GROUP_W
    b0 = SECTION_WIDTHS[0] // G
    tile = lambda w, col: pl.BlockSpec((1, tm, w), lambda b, i: (b, i, col))
    halo = lambda col: pl.BlockSpec((1, 8, G), lambda b, i: (b, jnp.maximum(i * (tm // 8) - 1, 0), col))
    const = lambda t: pl.BlockSpec(t.shape, lambda b, i: (0,) * t.ndim)
    per_batch = pl.BlockSpec((1, 1, D), lambda b, i: (b, 0, 0))
    wr = jnp.pad(w_router, ((0, 0), (0, ROUTER_PAD - N_EXPERTS)))
    br = jnp.pad(b_router, (0, ROUTER_PAD - N_EXPERTS)).reshape(1, ROUTER_PAD)
    consts = [conv_w, onorm_g.reshape(1, 3 * G), w_out.astype(_BF16)]
    return pl.pallas_call(
        _mix_out_kernel,
        grid=(B, S // tm),
        in_specs=[tile(D, 0), tile(G, 0), tile(G, b0), tile(G, b0 + 1), tile(G, b0 + 2), halo(b0 + 1), halo(b0 + 2),
                  tile(G, 0), tile(G, 0), tile(G, 0), tile(G, 0), tile(128, GATE_COL_BLOCK)]
                 + [const(t) for t in consts] + [per_batch, pl.BlockSpec((1, D), lambda b, i: (0, 0)), per_batch,
                                                 per_batch, const(wr), const(br)],
        out_specs=[tile(D, 0), tile(D, 0), tile(ROUTER_PAD, 0)],
        out_shape=[jax.ShapeDtypeStruct((B, S, D), _F32), jax.ShapeDtypeStruct((B, S, D), _BF16),
                   jax.ShapeDtypeStruct((B, S, ROUTER_PAD), _F32)],
        compiler_params=pltpu.CompilerParams(dimension_semantics=("parallel", "parallel")),
        name="mix_out",
    )(x, y_a, proj, proj, proj, proj, proj, y_c, o_cmp, o_slc, o_swa, proj, *consts, gt1, g2.reshape(1, D), sc2, sh2,
      wr, br)


RWKV_CHUNK = 64
RWKV_TILE = 128


def _group_indicator(n, group):
    r = lax.broadcasted_iota(jnp.int32, (n, n), 0) // group
    c = lax.broadcasted_iota(jnp.int32, (n, n), 1) // group
    return r == c


def _rwkv_kernel(za_ref, mu_ref, w0_ref, w2_ref, a0_ref, a2_ref, g2_ref, kk_ref, ka_ref, rk_ref,
                 gnw_ref, gnb_ref, o_ref, prev_sc, h_sc):
    C = RWKV_CHUNK
    TT = RWKV_TILE
    c = pl.program_id(1)

    @pl.when(c == 0)
    def _():
        prev_sc[...] = jnp.zeros_like(prev_sc)
        h_sc[...] = jnp.zeros_like(h_sc)

    z = za_ref[0]
    row = lax.broadcasted_iota(jnp.int32, z.shape, 0)
    zs = jnp.where(row == 0, prev_sc[...], pltpu.roll(z, 1, axis=0))
    prev_sc[...] = z[TT - 1:TT, :]
    z = z + (zs - z) * mu_ref[...]

    G = GROUP_W
    r, k, v = z[:, 0:G], z[:, G:2 * G], z[:, 2 * G:3 * G]
    o = 3 * G
    wd = z[:, o:o + RWKV_DECAY_RANK]
    ad = z[:, o + RWKV_DECAY_RANK:o + RWKV_DECAY_RANK + RWKV_AAA_RANK]
    gd = z[:, o + RWKV_DECAY_RANK + RWKV_AAA_RANK:]

    lw = -RWKV_DECAY_SCALE * jax.nn.sigmoid(w0_ref[...] + _dot_lo(jnp.tanh(wd), w2_ref[...]))
    a = jax.nn.sigmoid(a0_ref[...] + _dot_lo(ad, a2_ref[...]))
    g = _dot_lo(jax.nn.sigmoid(gd), g2_ref[...])

    head_sum = _group_indicator(G, HEAD_DIM).astype(_F32)
    kk = k * kk_ref[...]
    kk = kk * lax.rsqrt(_dot_hi(kk * kk, head_sum) + 1e-12)
    k = k * (1.0 + (a - 1.0) * ka_ref[...])
    b = kk * a

    ti = lax.broadcasted_iota(jnp.int32, (C, C), 0)
    tj = lax.broadcasted_iota(jnp.int32, (C, C), 1)
    incl = ti >= tj
    strict = ti > tj
    eye = ti == tj
    blk16 = (ti // 16) == (tj // 16)
    blk32 = (ti // 32) == (tj // 32)
    eye_f = eye.astype(_F32)

    ri = lax.broadcasted_iota(jnp.int32, (TT, TT), 0)
    rj = lax.broadcasted_iota(jnp.int32, (TT, TT), 1)
    chunk_tri = ((ri >= rj) & (ri // C == rj // C)).astype(_F32)
    cum = _dot_hi(chunk_tri, lw)
    g_in = jnp.exp(cum)
    A_all = -kk * jnp.exp(cum - lw)
    R_all = r * g_in
    g_inv = jnp.exp(-cum)
    B_all = b * g_inv
    K_all = k * g_inv

    units = [(ci, h) for ci in range(TT // C) for h in range(N_HEADS)]

    def part(t, u):
        ci, h = u
        return t[ci * C:(ci + 1) * C, h * HEAD_DIM:(h + 1) * HEAD_DIM]

    def g_end(u):
        ci, h = u
        return g_in[(ci + 1) * C - 1:(ci + 1) * C, h * HEAD_DIM:(h + 1) * HEAD_DIM]

    A = [part(A_all, u) for u in units]
    R = [part(R_all, u) for u in units]
    B = [part(B_all, u) for u in units]
    Kt = [part(K_all, u) for u in units]
    V = [part(v, u) for u in units]
    n = range(len(units))
    gram = [_dot_lo(jnp.concatenate([A[i], R[i]], axis=0), jnp.concatenate([B[i], Kt[i]], axis=0), _NT) for i in n]
    l_ab = [jnp.where(strict, gram[i][0:C, 0:C], 0.0) for i in n]
    l_ak = [jnp.where(strict, gram[i][0:C, C:2 * C], 0.0) for i in n]
    m_rb = [jnp.where(incl, gram[i][C:2 * C, 0:C], 0.0) for i in n]
    m_rk = [jnp.where(incl, gram[i][C:2 * C, C:2 * C], 0.0) for i in n]
    p = [jnp.where(blk16, l_ab[i], 0.0) for i in n]
    x = [eye_f + p[i] for i in n]
    for _ in range(3):
        p = [_dot_lo(p[i], p[i]) for i in n]
        x = [_dot_lo(x[i], eye_f + p[i]) for i in n]
    for lvl in (blk32 & ~blk16, ~blk32):
        xl = [_dot_lo(x[i], jnp.where(lvl, l_ab[i], 0.0)) for i in n]
        x = [x[i] + _dot_lo(xl[i], x[i]) for i in n]
    lv = [_dot_lo(l_ak[i], V[i]) for i in n]
    tap = [_dot_lo(x[i], jnp.concatenate([A[i], lv[i]], axis=1)) for i in n]
    m1 = [_dot_lo(m_rb[i], tap[i]) for i in n]
    mv = [_dot_lo(m_rk[i], V[i]) for i in n]
    bt = [_dot_lo(B[i] * g_end(units[i]), tap[i], _TN) for i in n]
    kv = [_dot_lo(Kt[i] * g_end(units[i]), V[i], _TN) for i in n]
    w_yh = [jnp.concatenate([R[i] + m1[i][:, 0:HEAD_DIM],
                             jnp.where(eye, g_end(units[i]), 0.0) + bt[i][:, 0:HEAD_DIM]], axis=0) for i in n]
    y0 = [m1[i][:, HEAD_DIM:] + mv[i] for i in n]
    h_add = [bt[i][:, HEAD_DIM:] + kv[i] for i in n]
    state = [h_sc[h] for h in range(N_HEADS)]
    y_rows = []
    for ci in range(TT // C):
        ys = []
        for h in range(N_HEADS):
            i = ci * N_HEADS + h
            nxt = _dot_hi(w_yh[i], state[h])
            ys.append(nxt[0:C] + y0[i])
            state[h] = nxt[C:2 * C] + h_add[i]
        y_rows.append(jnp.concatenate(ys, axis=1))
    for h in range(N_HEADS):
        h_sc[h] = state[h]
    y = jnp.concatenate(y_rows, axis=0)

    head_mean = head_sum * (1.0 / HEAD_DIM)
    mu = _dot_hi(y, head_mean)
    d = y - mu
    var = _dot_hi(d * d, head_mean)
    y = d * lax.rsqrt(var + RWKV_GN_EPS) * gnw_ref[...] + gnb_ref[...]
    y = y + _dot_hi(r * k * rk_ref[...], head_sum) * v
    o_ref[0] = y * g


def rwkv7_pallas(za, mu, w0, w2, a0, a2, g2, k_k, k_a, r_k, gn_w, gn_b):
    B, S, _ = za.shape
    W = SECTION_WIDTHS[0]
    TT = RWKV_TILE
    row = lambda t: t.reshape(1, -1).astype(_F32)
    full = lambda t: pl.BlockSpec(t.shape, lambda b, c: (0,) * t.ndim)
    params = [row(mu), row(w0), w2.astype(_BF16), row(a0), a2.astype(_BF16), g2.astype(_BF16),
              row(k_k), row(k_a), row(r_k), row(gn_w), row(gn_b)]
    return pl.pallas_call(
        _rwkv_kernel,
        grid=(B, S // TT),
        in_specs=[pl.BlockSpec((1, TT, W), lambda b, c: (b, c, 0))] + [full(p) for p in params],
        out_specs=pl.BlockSpec((1, TT, GROUP_W), lambda b, c: (b, c, 0)),
        out_shape=jax.ShapeDtypeStruct((B, S, GROUP_W), _F32),
        scratch_shapes=[pltpu.VMEM((1, W), _F32), pltpu.VMEM((N_HEADS, HEAD_DIM, HEAD_DIM), _F32)],
        compiler_params=pltpu.CompilerParams(dimension_semantics=("parallel", "arbitrary")),
        name="rwkv7",
    )(za, *params)


MOE_TM = 512
MOE_FC = 512
MOE_CAST_ROWS = 128


def _moe_ffn_kernel(blk_e_ref, n_used_ref, x_ref, wgu_ref, bgu_ref, wd_ref, bd_ref, o_ref, wgu_sc, wd_sc):
    i = pl.program_id(0)
    F = EXPERT_FF
    new_expert = jnp.logical_or(i == 0, blk_e_ref[i] != blk_e_ref[jnp.maximum(i - 1, 0)])

    @pl.when(new_expert)
    def _():
        def cast(j, carry):
            rows = pl.ds(pl.multiple_of(j * MOE_CAST_ROWS, MOE_CAST_ROWS), MOE_CAST_ROWS)
            wgu_sc[rows, :] = wgu_ref[0, rows, :].astype(_BF16)
            wd_sc[rows, :] = wd_ref[0, rows, :].astype(_BF16)
            return carry
        lax.fori_loop(0, F // MOE_CAST_ROWS, cast, 0)

    @pl.when(i < n_used_ref[0])
    def _():
        x = x_ref[...]
        acc = None
        for c in range(F // MOE_FC):
            lo = c * MOE_FC
            gate = jnp.dot(x, wgu_sc[:, lo:lo + MOE_FC], preferred_element_type=_F32) + bgu_ref[0, :, lo:lo + MOE_FC]
            up = jnp.dot(x, wgu_sc[:, F + lo:F + lo + MOE_FC], preferred_element_type=_F32) + bgu_ref[0, :, F + lo:F + lo + MOE_FC]
            gate = jnp.minimum(gate, SWIGLU_LIMIT)
            up = jnp.clip(up, -SWIGLU_LIMIT, SWIGLU_LIMIT)
            act = gate * jax.nn.sigmoid(SWIGLU_ALPHA * gate) * (up + 1.0)
            part = jnp.dot(act.astype(_BF16), wd_sc[lo:lo + MOE_FC, :], preferred_element_type=_F32)
            acc = part if acc is None else acc + part
        o_ref[...] = (acc + bd_ref[0]).astype(o_ref.dtype)


def moe_ffn(xs, blk_e, n_used, w_gu, b_gu, w_down, b_down):
    n_rows, D = xs.shape
    E, _, F2 = w_gu.shape
    assert D == EXPERT_FF and F2 == 2 * EXPERT_FF
    n_blk = n_rows // MOE_TM
    grid_spec = pltpu.PrefetchScalarGridSpec(
        num_scalar_prefetch=2,
        grid=(n_blk,),
        in_specs=[pl.BlockSpec((MOE_TM, D), lambda i, e, n: (i, 0)),
                  pl.BlockSpec((1, D, F2), lambda i, e, n: (e[i], 0, 0)),
                  pl.BlockSpec((1, 1, F2), lambda i, e, n: (e[i], 0, 0)),
                  pl.BlockSpec((1, F2 // 2, D), lambda i, e, n: (e[i], 0, 0)),
                  pl.BlockSpec((1, 1, D), lambda i, e, n: (e[i], 0, 0))],
        out_specs=pl.BlockSpec((MOE_TM, D), lambda i, e, n: (i, 0)),
        scratch_shapes=[pltpu.VMEM((D, F2), _BF16), pltpu.VMEM((F2 // 2, D), _BF16)],
    )
    return pl.pallas_call(
        _moe_ffn_kernel,
        grid_spec=grid_spec,
        out_shape=jax.ShapeDtypeStruct((n_rows, D), _BF16),
        compiler_params=pltpu.CompilerParams(dimension_semantics=("arbitrary",),
                                             vmem_limit_bytes=52 * 1024 * 1024),
        name="moe_ffn",
    )(blk_e, n_used, xs, w_gu, b_gu.reshape(E, 1, F2), w_down, b_down.reshape(E, 1, D))


def moe_pallas(hb, logits, w_gu, b_gu, w_down, b_down):
    B, S, D = hb.shape
    T = B * S
    TK = T * TOP_K
    TM = MOE_TM
    E = N_EXPERTS
    i32 = jnp.int32
    top_val, top_idx = lax.top_k(logits.reshape(T, -1)[:, :E], TOP_K)
    gates = jax.nn.softmax(top_val, axis=-1)
    e_flat = top_idx.reshape(TK).astype(i32)
    counts = jnp.sum((e_flat[:, None] == jnp.arange(E, dtype=i32)[None, :]).astype(i32), axis=0)
    need = jnp.repeat((-counts) % TM, TM)
    d_idx = jnp.arange(E * TM, dtype=i32)
    d_key = jnp.where(d_idx % TM < need, d_idx // TM, E)
    n_rows = TK + E * TM
    n_blk = n_rows // TM
    keys = jnp.concatenate([e_flat, d_key])
    toks = jnp.concatenate([jnp.arange(TK, dtype=i32) // TOP_K, jnp.zeros((E * TM,), i32)])
    rows = jnp.arange(n_rows, dtype=i32)
    s_keys, src_tok, s_slot = lax.sort((keys, toks, rows), num_keys=1)
    _, row_of_slot = lax.sort((s_slot, rows), num_keys=1)
    row_of = row_of_slot[:TK]
    blk_e = s_keys[::TM]
    n_used = jnp.sum((blk_e < E).astype(i32)).reshape(1)
    blk_e = jnp.where(blk_e < E, blk_e, blk_e[jnp.maximum(n_used[0] - 1, 0)])
    xs = jnp.take(hb.reshape(T, D), src_tok, axis=0)
    ybuf = moe_ffn(xs, blk_e, n_used, w_gu, b_gu, w_down, b_down)
    y = jnp.take(ybuf, row_of, axis=0).reshape(T, TOP_K, D).astype(_F32) * gates[:, :, None]
    return jnp.sum(y, axis=1).reshape(B, S, D)


ATT_T = 128
ATT_TK = 512
ATT_R = ATT_TK // ATT_T
NSA_NSEL = 64


def _window_table(window, weight_of_distance):
    T, TK = ATT_T, ATT_TK
    n = window // T + ATT_R + 1
    d = (np.arange(n)[:, None, None] * T + np.arange(T)[None, None, :] - np.arange(TK)[None, :, None])
    return jnp.asarray(weight_of_distance(d).astype(np.float32))


def _first_key_tile(qi, window):
    return jnp.maximum(qi * ATT_T - window, 0) // ATT_TK


def _swa_weight(d):
    return (d >= 0) & (d <= SWA_WIN - 1)


def _dil_weight(d):
    return sum(((d >= 0) & (d <= window) & (d % dil == 0)).astype(np.int32) for window, dil in DIL_CFG)


def _softmax_tile(s, w, m_ref, l_ref, weighted):
    s = jnp.where(w > 0.0, s, NEG)
    m_prev = m_ref[...]
    m_new = jnp.maximum(m_prev, jnp.max(s, axis=0, keepdims=True))
    alpha = jnp.exp(m_prev - m_new)
    p = jnp.exp(s - m_new)
    if weighted:
        p = p * w
    l_ref[...] = alpha * l_ref[...] + jnp.sum(p, axis=0, keepdims=True)
    m_ref[...] = m_new
    return alpha, p.astype(_BF16)


def _init_stats(m_sc, l_sc, acc_sc):
    m_sc[...] = jnp.full_like(m_sc, NEG)
    l_sc[...] = jnp.zeros_like(l_sc)
    acc_sc[...] = jnp.zeros_like(acc_sc)


def _heads_to_rows(ot):
    T = ATT_T
    pairs = []
    for h in range(0, N_HEADS, 2):
        two = jnp.concatenate([ot[:, h * T:(h + 1) * T], ot[:, (h + 1) * T:(h + 2) * T]], axis=0)
        pairs.append(two.T)
    return jnp.concatenate(pairs, axis=1)


def _nsa_attn_kernel(q_ref, ks_ref, vst_ref, kw_ref, vwt_ref, selt_ref, swa_ref, oslc_ref, oswa_ref,
                     m_sc, l_sc, acc_sc):
    T, TK = ATT_T, ATT_TK
    qi = pl.program_id(1)
    q = q_ref[0]
    q = jnp.concatenate([q[:, h * HEAD_DIM:(h + 1) * HEAD_DIM] for h in range(N_HEADS)], axis=0)
    selt = selt_ref[0]
    key_blk = lax.broadcasted_iota(jnp.int32, (TK, NSA_NSEL), 0) // SEL_BLK
    blk_id = lax.broadcasted_iota(jnp.int32, (TK, NSA_NSEL), 1)
    kr = lax.broadcasted_iota(jnp.int32, (TK, T), 0)
    qc = lax.broadcasted_iota(jnp.int32, (TK, T), 1)

    def step(k_ref, vt_ref, kt, w):
        base = pl.multiple_of(kt * TK, TK)
        s = lax.dot_general(k_ref[0, pl.ds(base, TK), :], q, _NT, preferred_element_type=_F32)
        alpha, p = _softmax_tile(s, jnp.concatenate([w] * N_HEADS, axis=1), m_sc, l_sc, False)
        acc_sc[...] = alpha * acc_sc[...] + jnp.dot(vt_ref[0, :, pl.ds(base, TK)], p, preferred_element_type=_F32)

    def finish(o_ref):
        o_ref[0] = _heads_to_rows(acc_sc[...] / l_sc[...])

    last = qi // ATT_R
    _init_stats(m_sc, l_sc, acc_sc)

    def slc_step(kt, carry):
        in_blk = (key_blk + kt * (TK // SEL_BLK) == blk_id).astype(_BF16)
        w = jnp.dot(in_blk, selt, preferred_element_type=_F32)
        w = jnp.where(kt * TK + kr <= qi * T + qc, w, 0.0)
        step(ks_ref, vst_ref, kt, w)
        return carry

    lax.fori_loop(0, last + 1, slc_step, 0)
    finish(oslc_ref)

    _init_stats(m_sc, l_sc, acc_sc)

    def swa_step(kt, carry):
        step(kw_ref, vwt_ref, kt, swa_ref[qi - kt * ATT_R])
        return carry

    lax.fori_loop(_first_key_tile(qi, SWA_WIN - 1), last + 1, swa_step, 0)
    finish(oswa_ref)


def nsa_attn_pallas(q_r, ks_r, vs, kw_r, vw, sel_mask):
    B, S, W = q_r.shape
    assert S // SEL_BLK == NSA_NSEL
    T = ATT_T
    R = N_HEADS * T
    bf = lambda t: t.astype(_BF16)
    tr = lambda t: jnp.swapaxes(bf(t), 1, 2)
    swa_tab = _window_table(SWA_WIN - 1, _swa_weight)
    k_spec = pl.BlockSpec((1, S, HEAD_DIM), lambda b, i: (b, 0, 0))
    vt_spec = pl.BlockSpec((1, HEAD_DIM, S), lambda b, i: (b, 0, 0))
    q_spec = pl.BlockSpec((1, T, W), lambda b, i: (b, i, 0))
    return pl.pallas_call(
        _nsa_attn_kernel,
        grid=(B, S // T),
        in_specs=[q_spec, k_spec, vt_spec, k_spec, vt_spec,
                  pl.BlockSpec((1, NSA_NSEL, T), lambda b, i: (b, 0, i)),
                  pl.BlockSpec(swa_tab.shape, lambda b, i: (0, 0, 0))],
        out_specs=[q_spec, q_spec],
        out_shape=[jax.ShapeDtypeStruct((B, S, W), _F32)] * 2,
        scratch_shapes=[pltpu.VMEM((1, R), _F32), pltpu.VMEM((1, R), _F32), pltpu.VMEM((HEAD_DIM, R), _F32)],
        compiler_params=pltpu.CompilerParams(dimension_semantics=("parallel", "arbitrary")),
        name="nsa_attn",
    )(bf(q_r * (HEAD_DIM ** -0.5)), bf(ks_r), tr(vs), bf(kw_r), tr(vw), tr(sel_mask), swa_tab)


def _dil_attn_kernel(q_ref, k_ref, vt_ref, tab_ref, o_ref, m_sc, l_sc, acc_sc):
    T, TK = ATT_T, ATT_TK
    qi = pl.program_id(1)
    _init_stats(m_sc, l_sc, acc_sc)
    qs = [q_ref[0, :, h * HEAD_DIM:(h + 1) * HEAD_DIM] for h in range(N_HEADS)]

    def step(kt, carry):
        base = pl.multiple_of(kt * TK, TK)
        w = tab_ref[qi - kt * ATT_R]
        s = jnp.concatenate([lax.dot_general(k_ref[0, h, pl.ds(base, TK), :], qs[h], _NT, preferred_element_type=_F32)
                             for h in range(N_HEADS)], axis=1)
        alpha, p = _softmax_tile(s, jnp.concatenate([w] * N_HEADS, axis=1), m_sc, l_sc, True)
        pv = jnp.concatenate([jnp.dot(vt_ref[0, h, :, pl.ds(base, TK)], p[:, h * T:(h + 1) * T],
                                      preferred_element_type=_F32) for h in range(N_HEADS)], axis=1)
        acc_sc[...] = alpha * acc_sc[...] + pv
        return carry

    lax.fori_loop(_first_key_tile(qi, DIL_CFG[-1][0]), qi // ATT_R + 1, step, 0)
    o_ref[0] = _heads_to_rows(acc_sc[...] / l_sc[...])


def dil_attn_pallas(q_r, k_r, v):
    B, S, H, hd = q_r.shape
    T = ATT_T
    R = H * T
    W = H * hd
    bf = lambda t: t.astype(_BF16)
    tab = _window_table(DIL_CFG[-1][0], _dil_weight)
    q_spec = pl.BlockSpec((1, T, W), lambda b, i: (b, i, 0))
    return pl.pallas_call(
        _dil_attn_kernel,
        grid=(B, S // T),
        in_specs=[q_spec,
                  pl.BlockSpec((1, H, S, hd), lambda b, i: (b, 0, 0, 0)),
                  pl.BlockSpec((1, H, hd, S), lambda b, i: (b, 0, 0, 0)),
                  pl.BlockSpec(tab.shape, lambda b, i: (0, 0, 0))],
        out_specs=q_spec,
        out_shape=jax.ShapeDtypeStruct((B, S, W), _F32),
        scratch_shapes=[pltpu.VMEM((1, R), _F32), pltpu.VMEM((1, R), _F32), pltpu.VMEM((hd, R), _F32)],
        compiler_params=pltpu.CompilerParams(dimension_semantics=("parallel", "arbitrary")),
        name="dil_attn",
    )(bf(q_r * (hd ** -0.5)).reshape(B, S, W), bf(k_r).transpose(0, 2, 1, 3), bf(v).transpose(0, 2, 3, 1), tab)


def split_cols(z, widths):
    offs = np.cumsum((0,) + tuple(widths))
    return [z[..., int(offs[i]):int(offs[i + 1])] for i in range(len(widths))]


def rmsnorm(x, g):
    xf = x.astype(jnp.float32)
    y = xf * lax.rsqrt(jnp.mean(xf * xf, axis=-1, keepdims=True) + NORM_EPS)
    return (y * g.astype(jnp.float32)).astype(x.dtype)


def rope(x, pos):
    half = x.shape[-1] // 2
    inv = ROPE_THETA ** (-jnp.arange(half, dtype=jnp.float32) / half)
    ang = pos.astype(jnp.float32)[:, :, None, None] * inv
    cos, sin = jnp.cos(ang), jnp.sin(ang)
    xf = x.astype(jnp.float32)
    x1, x2 = xf[..., :half], xf[..., half:]
    return jnp.concatenate([x1 * cos - x2 * sin, x2 * cos + x1 * sin], axis=-1).astype(x.dtype)


def dilated_mix(q, k, v, q_g, k_g, pos):
    q = rope(rmsnorm(q, q_g), pos)
    k = rope(rmsnorm(k, k_g), pos)
    return dil_attn_pallas(q, k, v)


def selection_overlap(n_c, n_sel):
    r = SEL_BLK // CMP_STRIDE
    m = CMP_LEN // CMP_STRIDE
    diff = np.arange(n_c)[:, None] - r * np.arange(n_sel)[None, :]
    offs = (np.arange(r)[:, None] - np.arange(m)[None, :]).reshape(-1)
    return (diff[..., None] == offs).sum(-1).astype(np.float32)


def nsa_mix(q, kc, vc, ks, vs, kw, vw, q_g, kc_g, ks_g, kw_g, pe_k, pe_v, wk1, wk2, wv1, wv2, pos):
    B, S, H, hd = q.shape
    f32 = jnp.float32
    scale = hd ** -0.5
    q_n = rmsnorm(q, q_g)
    q_r = rope(q_n, pos)
    qidx = np.arange(S)
    n_c = (S - CMP_LEN) // CMP_STRIDE + 1
    cidx = np.arange(n_c)[:, None] * CMP_STRIDE + np.arange(CMP_LEN)[None, :]

    def compress(t, pe, w1, w2):
        blk = (t[:, cidx] + pe).reshape(B, n_c, CMP_LEN * hd)
        return jax.nn.gelu(blk @ w1) @ w2

    k_cmp = rmsnorm(compress(kc, pe_k, wk1, wk2), kc_g)
    v_cmp = compress(vc, pe_v, wv1, wv2)
    s = jnp.einsum('bshd,bcd->bhsc', q_n, k_cmp).astype(f32) * scale
    valid = cidx[:, -1][None, :] <= qidx[:, None]
    s = jnp.where(valid, s, NEG)
    p_cmp = jax.nn.softmax(s, axis=-1) * valid.any(-1, keepdims=True).astype(np.float32)
    o_cmp = jnp.einsum('bhsc,bcd->bshd', p_cmp.astype(v_cmp.dtype), v_cmp)
    n_sel = S // SEL_BLK
    n_top = min(SEL_TOP, n_sel)
    imp = jnp.einsum('bhsc,cj->bsj', p_cmp, selection_overlap(n_c, n_sel))
    cur = qidx[:, None] // SEL_BLK
    jb = np.arange(n_sel)[None, :]
    future = jb > cur
    forced = (jb == 0) | (jb == cur) | (jb == cur - 1)
    score = jnp.where(future, -1.0, jnp.where(forced, FORCE_SCORE, imp))
    _, sel = lax.top_k(score, n_top)
    ks_r = rope(rmsnorm(ks[:, :, None, :], ks_g), pos)[:, :, 0]
    kw_r = rope(rmsnorm(kw[:, :, None, :], kw_g), pos)[:, :, 0]
    sel_mask = (sel[..., None] == jnp.arange(n_sel)).any(-2)
    o_slc, o_swa = nsa_attn_pallas(q_r.reshape(B, S, H * hd), ks_r, vs, kw_r, vw, sel_mask)
    return o_cmp.reshape(B, S, H * hd), o_slc, o_swa


def kernel(x, c, positions, w_ada, b_ada, norm1_g, norm2_g, w_in, w_out, rwkv_mu, rwkv_w0, rwkv_w2, rwkv_a0, rwkv_a2, rwkv_g2, rwkv_kk, rwkv_ka, rwkv_rk, rwkv_gn_w, rwkv_gn_b, conv_w, dil_q_g, dil_k_g, nsa_q_g, nsa_kc_g, nsa_ks_g, nsa_kw_g, nsa_pe_k, nsa_pe_v, nsa_wk1, nsa_wk2, nsa_wv1, nsa_wv2, onorm_g, w_router, b_router, w_gu, b_gu, w_down, b_down):
    B, S, D = x.shape
    heads = lambda t: t.reshape(B, S, N_HEADS, HEAD_DIM)
    for l in range(DEPTH):
        mod = matmul(jax.nn.silu(c), w_ada[l], tm=B, tn=512) + b_ada[l]
        sh1, sc1, gt1, sh2, sc2, gt2 = [m[:, None, :] for m in jnp.split(mod, 6, axis=-1)]
        proj = in_proj_pallas(x, norm1_g[l], sc1, sh1, w_in[l])
        y_a = rwkv7_pallas(proj, rwkv_mu[l], rwkv_w0[l], rwkv_w2[l], rwkv_a0[l], rwkv_a2[l], rwkv_g2[l],
                           rwkv_kk[l], rwkv_ka[l], rwkv_rk[l], rwkv_gn_w[l], rwkv_gn_b[l])
        _, _, zc, zd = split_cols(proj[..., :PROJ_W], SECTION_WIDTHS)
        cq, ck, cv = split_cols(zc, C_WIDTHS)
        y_c = dilated_mix(heads(cq), heads(ck), heads(cv), dil_q_g[l], dil_k_g[l], positions)
        dq, dkc, dvc, dks, dvs, dkw, dvw, _ = split_cols(zd, D_WIDTHS)
        o_cmp, o_slc, o_swa = nsa_mix(heads(dq), dkc, dvc, dks, dvs, dkw, dvw, nsa_q_g[l], nsa_kc_g[l], nsa_ks_g[l],
                                      nsa_kw_g[l], nsa_pe_k[l], nsa_pe_v[l], nsa_wk1[l], nsa_wk2[l], nsa_wv1[l],
                                      nsa_wv2[l], positions)
        x, hb, logits = mix_out_pallas(x, y_a, proj, y_c, o_cmp, o_slc, o_swa, conv_w[l], onorm_g[l], w_out[l], gt1,
                                       norm2_g[l], sc2, sh2, w_router[l], b_router[l])
        x = x + gt2 * moe_pallas(hb, logits, w_gu[l], b_gu[l], w_down[l], b_down[l])
    return x
```

```python
import functools

import numpy as np
import jax
import jax.numpy as jnp
from jax import lax
from jax.experimental import pallas as pl
from jax.experimental.pallas import tpu as pltpu

D_MODEL = 1024
DEPTH = 2

HEAD_DIM = 64
N_MIXERS = 4
GROUP_W = D_MODEL // N_MIXERS
N_HEADS = GROUP_W // HEAD_DIM
MIX_W = N_MIXERS * GROUP_W

RWKV_DECAY_RANK = 64
RWKV_AAA_RANK = 64
RWKV_GATE_RANK = 128
RWKV_DECAY_SCALE = 0.606531
RWKV_GN_EPS = 64e-5

CONV_W = 3

DIL_CFG = ((128, 1), (512, 4), (2048, 16))

CMP_LEN = 32
CMP_STRIDE = 16
CMP_HIDDEN = 256
SEL_BLK = 64
SEL_TOP = 16
SWA_WIN = 512
FORCE_SCORE = 1e4

N_EXPERTS = 32
TOP_K = 4
EXPERT_FF = D_MODEL
SWIGLU_LIMIT = 7.0
SWIGLU_ALPHA = 1.702

ROPE_THETA = 10000.0
NORM_EPS = 1e-6
NEG = -1e30

A_WIDTHS = (GROUP_W, GROUP_W, GROUP_W, RWKV_DECAY_RANK, RWKV_AAA_RANK, RWKV_GATE_RANK)
B_WIDTHS = (GROUP_W, GROUP_W, GROUP_W)
C_WIDTHS = (GROUP_W, GROUP_W, GROUP_W)
D_WIDTHS = (GROUP_W, HEAD_DIM, HEAD_DIM, HEAD_DIM, HEAD_DIM, HEAD_DIM, HEAD_DIM, 3 * N_HEADS)
SECTION_WIDTHS = (sum(A_WIDTHS), sum(B_WIDTHS), sum(C_WIDTHS), sum(D_WIDTHS))
PROJ_W = sum(SECTION_WIDTHS)

_F32 = jnp.float32
_BF16 = jnp.bfloat16
_NT = (((1,), (1,)), ((), ()))
_TN = (((0,), (0,)), ((), ()))


def _dot_hi(a, b, dims=None):
    if dims is None:
        return jnp.dot(a, b, precision=lax.Precision.HIGHEST, preferred_element_type=_F32)
    return lax.dot_general(a, b, dims, precision=lax.Precision.HIGHEST, preferred_element_type=_F32)


def _dot_lo(a, b, dims=None):
    a, b = a.astype(_BF16), b.astype(_BF16)
    if dims is None:
        return jnp.dot(a, b, preferred_element_type=_F32)
    return lax.dot_general(a, b, dims, preferred_element_type=_F32)


def _matmul_kernel(x_ref, w_ref, o_ref, *, exact):
    if exact:
        o_ref[...] = _dot_hi(x_ref[...], w_ref[...])
    else:
        o_ref[...] = jnp.dot(x_ref[...].astype(_BF16), w_ref[...], preferred_element_type=_F32)


def matmul(x, w, tm=512, tn=256, exact=False):
    M, K = x.shape
    N = w.shape[1]
    n_pad = -(-N // tn) * tn
    wb = jnp.pad(w if exact else w.astype(_BF16), ((0, 0), (0, n_pad - N)))
    out = pl.pallas_call(
        functools.partial(_matmul_kernel, exact=exact),
        grid=(M // tm, n_pad // tn),
        in_specs=[pl.BlockSpec((tm, K), lambda i, j: (i, 0)),
                  pl.BlockSpec((K, tn), lambda i, j: (0, j))],
        out_specs=pl.BlockSpec((tm, tn), lambda i, j: (i, j)),
        out_shape=jax.ShapeDtypeStruct((M, n_pad), _F32),
        name="matmul",
    )(x, wb)
    return out[:, :N]


PROJ_PAD = -(-PROJ_W // 256) * 256
PROJ_TM = 256
PROJ_TN = 512
ROUTER_PAD = 128
GATE_COL_BLOCK = (PROJ_W - 3 * N_HEADS) // 128
assert GATE_COL_BLOCK * 128 == PROJ_W - 3 * N_HEADS


def _norm_mod(x, g, sc, sh):
    y = x * lax.rsqrt(jnp.mean(x * x, axis=-1, keepdims=True) + NORM_EPS) * g
    return y * (1.0 + sc) + sh


def _in_proj_kernel(x_ref, g_ref, sc_ref, sh_ref, w_ref, o_ref):
    h = _norm_mod(x_ref[0], g_ref[...], sc_ref[0], sh_ref[0]).astype(_BF16)
    for n0 in range(0, PROJ_PAD, PROJ_TN):
        o_ref[0, :, n0:n0 + PROJ_TN] = jnp.dot(h, w_ref[:, n0:n0 + PROJ_TN], preferred_element_type=_F32)


def in_proj_pallas(x, g, sc, sh, w_in):
    B, S, D = x.shape
    tm = PROJ_TM
    w = jnp.pad(w_in.astype(_BF16), ((0, 0), (0, PROJ_PAD - PROJ_W)))
    per_batch = pl.BlockSpec((1, 1, D), lambda b, i: (b, 0, 0))
    return pl.pallas_call(
        _in_proj_kernel,
        grid=(B, S // tm),
        in_specs=[pl.BlockSpec((1, tm, D), lambda b, i: (b, i, 0)),
                  pl.BlockSpec((1, D), lambda b, i: (0, 0)), per_batch, per_batch,
                  pl.BlockSpec((D, PROJ_PAD), lambda b, i: (0, 0))],
        out_specs=pl.BlockSpec((1, tm, PROJ_PAD), lambda b, i: (b, i, 0)),
        out_shape=jax.ShapeDtypeStruct((B, S, PROJ_PAD), _F32),
        compiler_params=pltpu.CompilerParams(dimension_semantics=("parallel", "parallel"),
                                             vmem_limit_bytes=48 * 1024 * 1024),
        name="in_proj",
    )(x, g.reshape(1, D), sc, sh, w)


def _mix_out_kernel(x_ref, ya_ref, bg_ref, cg_ref, xin_ref, cgh_ref, xinh_ref, yc_ref, ocmp_ref, oslc_ref, oswa_ref,
                    gl_ref, convw_ref, ong_ref, wout_ref, gt1_ref, g2_ref, sc2_ref, sh2_ref, wr_ref, br_ref,
                    xo_ref, hb_ref, lg_ref):
    i = pl.program_id(1)
    G = GROUP_W
    u = cg_ref[0] * xin_ref[0]
    halo = jnp.where(i == 0, 0.0, cgh_ref[0] * xinh_ref[0])
    row = lax.broadcasted_iota(jnp.int32, u.shape, 0)
    u1 = jnp.where(row == 0, halo[7:8], pltpu.roll(u, 1, axis=0))
    u2 = jnp.where(row == 0, halo[6:7], jnp.where(row == 1, halo[7:8], pltpu.roll(u, 2, axis=0)))
    cw = convw_ref[...]
    yb = bg_ref[0] * (cw[0:1] * u2 + cw[1:2] * u1 + cw[2:3] * u)

    sg = jax.nn.sigmoid(gl_ref[0])
    er = lax.broadcasted_iota(jnp.int32, (128, G), 0)
    ec = lax.broadcasted_iota(jnp.int32, (128, G), 1) // HEAD_DIM
    yd = None
    for j, o_ref in enumerate((ocmp_ref, oslc_ref, oswa_ref)):
        gate = _dot_hi(sg, (er == 3 * ec + j).astype(_F32))
        yd = gate * o_ref[0] if yd is None else yd + gate * o_ref[0]

    head_mean = _group_indicator(G, HEAD_DIM).astype(_F32) * (1.0 / HEAD_DIM)
    ong = ong_ref[...]
    parts = [ya_ref[0]]
    for n, y in enumerate((yb, yc_ref[0], yd)):
        parts.append(y * lax.rsqrt(_dot_hi(y * y, head_mean) + NORM_EPS) * ong[:, n * G:(n + 1) * G])
    mixed = jnp.dot(jnp.concatenate(parts, axis=1).astype(_BF16), wout_ref[...], preferred_element_type=_F32)
    x = x_ref[0] + gt1_ref[0] * mixed
    xo_ref[0] = x
    h = _norm_mod(x, g2_ref[...], sc2_ref[0], sh2_ref[0])
    hb_ref[0] = h.astype(_BF16)
    lg_ref[0] = _dot_hi(h, wr_ref[...]) + br_ref[...]


def mix_out_pallas(x, y_a, proj, y_c, o_cmp, o_slc, o_swa, conv_w, onorm_g, w_out, gt1, g2, sc2, sh2, w_router, b_router):
    B, S, D = x.shape
    tm = PROJ_TM
    G = GROUP_W
    b0 = SECTION_WIDTHS[0] // G
    tile = lambda w, col: pl.BlockSpec((1, tm, w), lambda b, i: (b, i, col))
    halo = lambda col: pl.BlockSpec((1, 8, G), lambda b, i: (b, jnp.maximum(i * (tm // 8) - 1, 0), col))
    const = lambda t: pl.BlockSpec(t.shape, lambda b, i: (0,) * t.ndim)
    per_batch = pl.BlockSpec((1, 1, D), lambda b, i: (b, 0, 0))
    wr = jnp.pad(w_router, ((0, 0), (0, ROUTER_PAD - N_EXPERTS)))
    br = jnp.pad(b_router, (0, ROUTER_PAD - N_EXPERTS)).reshape(1, ROUTER_PAD)
    consts = [conv_w, onorm_g.reshape(1, 3 * G), w_out.astype(_BF16)]
    return pl.pallas_call(
        _mix_out_kernel,
        grid=(B, S // tm),
        in_specs=[tile(D, 0), tile(G, 0), tile(G, b0), tile(G, b0 + 1), tile(G, b0 + 2), halo(b0 + 1), halo(b0 + 2),
                  tile(G, 0), tile(G, 0), tile(G, 0), tile(G, 0), tile(128, GATE_COL_BLOCK)]
                 + [const(t) for t in consts] + [per_batch, pl.BlockSpec((1, D), lambda b, i: (0, 0)), per_batch,
                                                 per_batch, const(wr), const(br)],
        out_specs=[tile(D, 0), tile(D, 0), tile(ROUTER_PAD, 0)],
        out_shape=[jax.ShapeDtypeStruct((B, S, D), _F32), jax.ShapeDtypeStruct((B, S, D), _BF16),
                   jax.ShapeDtypeStruct((B, S, ROUTER_PAD), _F32)],
        compiler_params=pltpu.CompilerParams(dimension_semantics=("parallel", "parallel")),
        name="mix_out",
    )(x, y_a, proj, proj, proj, proj, proj, y_c, o_cmp, o_slc, o_swa, proj, *consts, gt1, g2.reshape(1, D), sc2, sh2,
      wr, br)


RWKV_CHUNK = 64
RWKV_TILE = 128


def _group_indicator(n, group):
    r = lax.broadcasted_iota(jnp.int32, (n, n), 0) // group
    c = lax.broadcasted_iota(jnp.int32, (n, n), 1) // group
    return r == c


def _rwkv_kernel(za_ref, mu_ref, w0_ref, w2_ref, a0_ref, a2_ref, g2_ref, kk_ref, ka_ref, rk_ref,
                 gnw_ref, gnb_ref, o_ref, prev_sc, h_sc):
    C = RWKV_CHUNK
    TT = RWKV_TILE
    c = pl.program_id(1)

    @pl.when(c == 0)
    def _():
        prev_sc[...] = jnp.zeros_like(prev_sc)
        h_sc[...] = jnp.zeros_like(h_sc)

    z = za_ref[0]
    row = lax.broadcasted_iota(jnp.int32, z.shape, 0)
    zs = jnp.where(row == 0, prev_sc[...], pltpu.roll(z, 1, axis=0))
    prev_sc[...] = z[TT - 1:TT, :]
    z = z + (zs - z) * mu_ref[...]

    G = GROUP_W
    r, k, v = z[:, 0:G], z[:, G:2 * G], z[:, 2 * G:3 * G]
    o = 3 * G
    wd = z[:, o:o + RWKV_DECAY_RANK]
    ad = z[:, o + RWKV_DECAY_RANK:o + RWKV_DECAY_RANK + RWKV_AAA_RANK]
    gd = z[:, o + RWKV_DECAY_RANK + RWKV_AAA_RANK:]

    lw = -RWKV_DECAY_SCALE * jax.nn.sigmoid(w0_ref[...] + _dot_lo(jnp.tanh(wd), w2_ref[...]))
    a = jax.nn.sigmoid(a0_ref[...] + _dot_lo(ad, a2_ref[...]))
    g = _dot_lo(jax.nn.sigmoid(gd), g2_ref[...])

    head_sum = _group_indicator(G, HEAD_DIM).astype(_F32)
    kk = k * kk_ref[...]
    kk = kk * lax.rsqrt(_dot_hi(kk * kk, head_sum) + 1e-12)
    k = k * (1.0 + (a - 1.0) * ka_ref[...])
    b = kk * a

    ti = lax.broadcasted_iota(jnp.int32, (C, C), 0)
    tj = lax.broadcasted_iota(jnp.int32, (C, C), 1)
    incl = ti >= tj
    strict = ti > tj
    eye = ti == tj
    blk16 = (ti // 16) == (tj // 16)
    blk32 = (ti // 32) == (tj // 32)
    eye_f = eye.astype(_F32)

    ri = lax.broadcasted_iota(jnp.int32, (TT, TT), 0)
    rj = lax.broadcasted_iota(jnp.int32, (TT, TT), 1)
    chunk_tri = ((ri >= rj) & (ri // C == rj // C)).astype(_F32)
    cum = _dot_hi(chunk_tri, lw)
    g_in = jnp.exp(cum)
    A_all = -kk * jnp.exp(cum - lw)
    R_all = r * g_in
    g_inv = jnp.exp(-cum)
    B_all = b * g_inv
    K_all = k * g_inv

    units = [(ci, h) for ci in range(TT // C) for h in range(N_HEADS)]

    def part(t, u):
        ci, h = u
        return t[ci * C:(ci + 1) * C, h * HEAD_DIM:(h + 1) * HEAD_DIM]

    def g_end(u):
        ci, h = u
        return g_in[(ci + 1) * C - 1:(ci + 1) * C, h * HEAD_DIM:(h + 1) * HEAD_DIM]

    A = [part(A_all, u) for u in units]
    R = [part(R_all, u) for u in units]
    B = [part(B_all, u) for u in units]
    Kt = [part(K_all, u) for u in units]
    V = [part(v, u) for u in units]
    n = range(len(units))
    gram = [_dot_lo(jnp.concatenate([A[i], R[i]], axis=0), jnp.concatenate([B[i], Kt[i]], axis=0), _NT) for i in n]
    l_ab = [jnp.where(strict, gram[i][0:C, 0:C], 0.0) for i in n]
    l_ak = [jnp.where(strict, gram[i][0:C, C:2 * C], 0.0) for i in n]
    m_rb = [jnp.where(incl, gram[i][C:2 * C, 0:C], 0.0) for i in n]
    m_rk = [jnp.where(incl, gram[i][C:2 * C, C:2 * C], 0.0) for i in n]
    p = [jnp.where(blk16, l_ab[i], 0.0) for i in n]
    x = [eye_f + p[i] for i in n]
    for _ in range(3):
        p = [_dot_lo(p[i], p[i]) for i in n]
        x = [_dot_lo(x[i], eye_f + p[i]) for i in n]
    for lvl in (blk32 & ~blk16, ~blk32):
        xl = [_dot_lo(x[i], jnp.where(lvl, l_ab[i], 0.0)) for i in n]
        x = [x[i] + _dot_lo(xl[i], x[i]) for i in n]
    lv = [_dot_lo(l_ak[i], V[i]) for i in n]
    tap = [_dot_lo(x[i], jnp.concatenate([A[i], lv[i]], axis=1)) for i in n]
    m1 = [_dot_lo(m_rb[i], tap[i]) for i in n]
    mv = [_dot_lo(m_rk[i], V[i]) for i in n]
    bt = [_dot_lo(B[i] * g_end(units[i]), tap[i], _TN) for i in n]
    kv = [_dot_lo(Kt[i] * g_end(units[i]), V[i], _TN) for i in n]
    w_yh = [jnp.concatenate([R[i] + m1[i][:, 0:HEAD_DIM],
                             jnp.where(eye, g_end(units[i]), 0.0) + bt[i][:, 0:HEAD_DIM]], axis=0) for i in n]
    y0 = [m1[i][:, HEAD_DIM:] + mv[i] for i in n]
    h_add = [bt[i][:, HEAD_DIM:] + kv[i] for i in n]
    state = [h_sc[h] for h in range(N_HEADS)]
    y_rows = []
    for ci in range(TT // C):
        ys = []
        for h in range(N_HEADS):
            i = ci * N_HEADS + h
            nxt = _dot_hi(w_yh[i], state[h])
            ys.append(nxt[0:C] + y0[i])
            state[h] = nxt[C:2 * C] + h_add[i]
        y_rows.append(jnp.concatenate(ys, axis=1))
    for h in range(N_HEADS):
        h_sc[h] = state[h]
    y = jnp.concatenate(y_rows, axis=0)

    head_mean = head_sum * (1.0 / HEAD_DIM)
    mu = _dot_hi(y, head_mean)
    d = y - mu
    var = _dot_hi(d * d, head_mean)
    y = d * lax.rsqrt(var + RWKV_GN_EPS) * gnw_ref[...] + gnb_ref[...]
    y = y + _dot_hi(r * k * rk_ref[...], head_sum) * v
    o_ref[0] = y * g


def rwkv7_pallas(za, mu, w0, w2, a0, a2, g2, k_k, k_a, r_k, gn_w, gn_b):
    B, S, _ = za.shape
    W = SECTION_WIDTHS[0]
    TT = RWKV_TILE
    row = lambda t: t.reshape(1, -1).astype(_F32)
    full = lambda t: pl.BlockSpec(t.shape, lambda b, c: (0,) * t.ndim)
    params = [row(mu), row(w0), w2.astype(_BF16), row(a0), a2.astype(_BF16), g2.astype(_BF16),
              row(k_k), row(k_a), row(r_k), row(gn_w), row(gn_b)]
    return pl.pallas_call(
        _rwkv_kernel,
        grid=(B, S // TT),
        in_specs=[pl.BlockSpec((1, TT, W), lambda b, c: (b, c, 0))] + [full(p) for p in params],
        out_specs=pl.BlockSpec((1, TT, GROUP_W), lambda b, c: (b, c, 0)),
        out_shape=jax.ShapeDtypeStruct((B, S, GROUP_W), _F32),
        scratch_shapes=[pltpu.VMEM((1, W), _F32), pltpu.VMEM((N_HEADS, HEAD_DIM, HEAD_DIM), _F32)],
        compiler_params=pltpu.CompilerParams(dimension_semantics=("parallel", "arbitrary")),
        name="rwkv7",
    )(za, *params)


MOE_TM = 512
MOE_FC = 512
MOE_CAST_ROWS = 128


def _moe_ffn_kernel(blk_e_ref, n_used_ref, x_ref, wgu_ref, bgu_ref, wd_ref, bd_ref, o_ref, wgu_sc, wd_sc):
    i = pl.program_id(0)
    F = EXPERT_FF
    new_expert = jnp.logical_or(i == 0, blk_e_ref[i] != blk_e_ref[jnp.maximum(i - 1, 0)])

    @pl.when(new_expert)
    def _():
        def cast(j, carry):
            rows = pl.ds(pl.multiple_of(j * MOE_CAST_ROWS, MOE_CAST_ROWS), MOE_CAST_ROWS)
            wgu_sc[rows, :] = wgu_ref[0, rows, :].astype(_BF16)
            wd_sc[rows, :] = wd_ref[0, rows, :].astype(_BF16)
            return carry
        lax.fori_loop(0, F // MOE_CAST_ROWS, cast, 0)

    @pl.when(i < n_used_ref[0])
    def _():
        x = x_ref[...]
        acc = None
        for c in range(F // MOE_FC):
            lo = c * MOE_FC
            gate = jnp.dot(x, wgu_sc[:, lo:lo + MOE_FC], preferred_element_type=_F32) + bgu_ref[0, :, lo:lo + MOE_FC]
            up = jnp.dot(x, wgu_sc[:, F + lo:F + lo + MOE_FC], preferred_element_type=_F32) + bgu_ref[0, :, F + lo:F + lo + MOE_FC]
            gate = jnp.minimum(gate, SWIGLU_LIMIT)
            up = jnp.clip(up, -SWIGLU_LIMIT, SWIGLU_LIMIT)
            act = gate * jax.nn.sigmoid(SWIGLU_ALPHA * gate) * (up + 1.0)
            part = jnp.dot(act.astype(_BF16), wd_sc[lo:lo + MOE_FC, :], preferred_element_type=_F32)
            acc = part if acc is None else acc + part
        o_ref[...] = (acc + bd_ref[0]).astype(o_ref.dtype)

    @pl.when(i >= n_used_ref[0])
    def _():
        o_ref[...] = jnp.zeros_like(o_ref)


def moe_ffn(xs, blk_e, n_used, w_gu, b_gu, w_down, b_down):
    n_rows, D = xs.shape
    E, _, F2 = w_gu.shape
    assert D == EXPERT_FF and F2 == 2 * EXPERT_FF
    n_blk = n_rows // MOE_TM
    grid_spec = pltpu.PrefetchScalarGridSpec(
        num_scalar_prefetch=2,
        grid=(n_blk,),
        in_specs=[pl.BlockSpec((MOE_TM, D), lambda i, e, n: (i, 0)),
                  pl.BlockSpec((1, D, F2), lambda i, e, n: (e[i], 0, 0)),
                  pl.BlockSpec((1, 1, F2), lambda i, e, n: (e[i], 0, 0)),
                  pl.BlockSpec((1, F2 // 2, D), lambda i, e, n: (e[i], 0, 0)),
                  pl.BlockSpec((1, 1, D), lambda i, e, n: (e[i], 0, 0))],
        out_specs=pl.BlockSpec((MOE_TM, D), lambda i, e, n: (i, 0)),
        scratch_shapes=[pltpu.VMEM((D, F2), _BF16), pltpu.VMEM((F2 // 2, D), _BF16)],
    )
    return pl.pallas_call(
        _moe_ffn_kernel,
        grid_spec=grid_spec,
        out_shape=jax.ShapeDtypeStruct((n_rows, D), _BF16),
        compiler_params=pltpu.CompilerParams(dimension_semantics=("arbitrary",),
                                             vmem_limit_bytes=52 * 1024 * 1024),
        name="moe_ffn",
    )(blk_e, n_used, xs, w_gu, b_gu.reshape(E, 1, F2), w_down, b_down.reshape(E, 1, D))


def moe_pallas(hb, logits, w_gu, b_gu, w_down, b_down):
    B, S, D = hb.shape
    T = B * S
    TK = T * TOP_K
    TM = MOE_TM
    E = N_EXPERTS
    i32 = jnp.int32
    top_val, top_idx = lax.top_k(logits.reshape(T, -1)[:, :E], TOP_K)
    gates = jax.nn.softmax(top_val, axis=-1)
    e_flat = top_idx.reshape(TK).astype(i32)
    counts = jnp.sum((e_flat[:, None] == jnp.arange(E, dtype=i32)[None, :]).astype(i32), axis=0)
    need = jnp.repeat((-counts) % TM, TM)
    d_idx = jnp.arange(E * TM, dtype=i32)
    d_key = jnp.where(d_idx % TM < need, d_idx // TM, E)
    n_rows = TK + E * TM
    n_blk = n_rows // TM
    keys = jnp.concatenate([e_flat, d_key])
    toks = jnp.concatenate([jnp.arange(TK, dtype=i32) // TOP_K, jnp.zeros((E * TM,), i32)])
    rows = jnp.arange(n_rows, dtype=i32)
    s_keys, src_tok, s_slot = lax.sort((keys, toks, rows), num_keys=1)
    _, row_of_slot = lax.sort((s_slot, rows), num_keys=1)
    row_of = row_of_slot[:TK]
    blk_e = s_keys[::TM]
    n_used = jnp.sum((blk_e < E).astype(i32)).reshape(1)
    blk_e = jnp.where(blk_e < E, blk_e, blk_e[jnp.maximum(n_used[0] - 1, 0)])
    xs = jnp.take(hb.reshape(T, D), src_tok, axis=0)
    ybuf = moe_ffn(xs, blk_e, n_used, w_gu, b_gu, w_down, b_down)
    y = jnp.take(ybuf, row_of, axis=0).reshape(T, TOP_K, D).astype(_F32) * gates[:, :, None]
    return jnp.sum(y, axis=1).reshape(B, S, D)


ATT_T = 128
ATT_TK = 512
ATT_R = ATT_TK // ATT_T
NSA_NSEL = 64


def _window_table(window, weight_of_distance):
    T, TK = ATT_T, ATT_TK
    n = window // T + ATT_R + 1
    d = (np.arange(n)[:, None, None] * T + np.arange(T)[None, None, :] - np.arange(TK)[None, :, None])
    return jnp.asarray(weight_of_distance(d).astype(np.float32))


def _first_key_tile(qi, window):
    return jnp.maximum(qi * ATT_T - window, 0) // ATT_TK


def _swa_weight(d):
    return (d >= 0) & (d <= SWA_WIN - 1)


def _dil_weight(d):
    return sum(((d >= 0) & (d <= window) & (d % dil == 0)).astype(np.int32) for window, dil in DIL_CFG)


def _softmax_tile(s, w, m_ref, l_ref, weighted):
    s = jnp.where(w > 0.0, s, NEG)
    m_prev = m_ref[...]
    m_new = jnp.maximum(m_prev, jnp.max(s, axis=0, keepdims=True))
    alpha = jnp.exp(m_prev - m_new)
    p = jnp.exp(s - m_new)
    if weighted:
        p = p * w
    l_ref[...] = alpha * l_ref[...] + jnp.sum(p, axis=0, keepdims=True)
    m_ref[...] = m_new
    return alpha, p.astype(_BF16)


def _init_stats(m_sc, l_sc, acc_sc):
    m_sc[...] = jnp.full_like(m_sc, NEG)
    l_sc[...] = jnp.zeros_like(l_sc)
    acc_sc[...] = jnp.zeros_like(acc_sc)


def _heads_to_rows(ot):
    T = ATT_T
    pairs = []
    for h in range(0, N_HEADS, 2):
        two = jnp.concatenate([ot[:, h * T:(h + 1) * T], ot[:, (h + 1) * T:(h + 2) * T]], axis=0)
        pairs.append(two.T)
    return jnp.concatenate(pairs, axis=1)


def _nsa_attn_kernel(q_ref, ks_ref, vst_ref, kw_ref, vwt_ref, selt_ref, swa_ref, oslc_ref, oswa_ref,
                     m_sc, l_sc, acc_sc):
    T, TK = ATT_T, ATT_TK
    qi = pl.program_id(1)
    q = q_ref[0]
    q = jnp.concatenate([q[:, h * HEAD_DIM:(h + 1) * HEAD_DIM] for h in range(N_HEADS)], axis=0)
    selt = selt_ref[0]
    key_blk = lax.broadcasted_iota(jnp.int32, (TK, NSA_NSEL), 0) // SEL_BLK
    blk_id = lax.broadcasted_iota(jnp.int32, (TK, NSA_NSEL), 1)
    kr = lax.broadcasted_iota(jnp.int32, (TK, T), 0)
    qc = lax.broadcasted_iota(jnp.int32, (TK, T), 1)

    def step(k_ref, vt_ref, kt, w):
        base = pl.multiple_of(kt * TK, TK)
        s = lax.dot_general(k_ref[0, pl.ds(base, TK), :], q, _NT, preferred_element_type=_F32)
        alpha, p = _softmax_tile(s, jnp.concatenate([w] * N_HEADS, axis=1), m_sc, l_sc, False)
        acc_sc[...] = alpha * acc_sc[...] + jnp.dot(vt_ref[0, :, pl.ds(base, TK)], p, preferred_element_type=_F32)

    def finish(o_ref):
        o_ref[0] = _heads_to_rows(acc_sc[...] / l_sc[...])

    last = qi // ATT_R
    _init_stats(m_sc, l_sc, acc_sc)

    def slc_step(kt, carry):
        in_blk = (key_blk + kt * (TK // SEL_BLK) == blk_id).astype(_BF16)
        w = jnp.dot(in_blk, selt, preferred_element_type=_F32)
        w = jnp.where(kt * TK + kr <= qi * T + qc, w, 0.0)
        step(ks_ref, vst_ref, kt, w)
        return carry

    lax.fori_loop(0, last + 1, slc_step, 0)
    finish(oslc_ref)

    _init_stats(m_sc, l_sc, acc_sc)

    def swa_step(kt, carry):
        step(kw_ref, vwt_ref, kt, swa_ref[qi - kt * ATT_R])
        return carry

    lax.fori_loop(_first_key_tile(qi, SWA_WIN - 1), last + 1, swa_step, 0)
    finish(oswa_ref)


def nsa_attn_pallas(q_r, ks_r, vs_t, kw_r, vw_t, sel_t):
    B, S, W = q_r.shape
    assert S // SEL_BLK == NSA_NSEL
    T = ATT_T
    R = N_HEADS * T
    swa_tab = _window_table(SWA_WIN - 1, _swa_weight)
    k_spec = pl.BlockSpec((1, S, HEAD_DIM), lambda b, i: (b, 0, 0))
    vt_spec = pl.BlockSpec((1, HEAD_DIM, S), lambda b, i: (b, 0, 0))
    q_spec = pl.BlockSpec((1, T, W), lambda b, i: (b, i, 0))
    return pl.pallas_call(
        _nsa_attn_kernel,
        grid=(B, S // T),
        in_specs=[q_spec, k_spec, vt_spec, k_spec, vt_spec,
                  pl.BlockSpec((1, NSA_NSEL, T), lambda b, i: (b, 0, i)),
                  pl.BlockSpec(swa_tab.shape, lambda b, i: (0, 0, 0))],
        out_specs=[q_spec, q_spec],
        out_shape=[jax.ShapeDtypeStruct((B, S, W), _F32)] * 2,
        scratch_shapes=[pltpu.VMEM((1, R), _F32), pltpu.VMEM((1, R), _F32), pltpu.VMEM((HEAD_DIM, R), _F32)],
        compiler_params=pltpu.CompilerParams(dimension_semantics=("parallel", "arbitrary")),
        name="nsa_attn",
    )(q_r, ks_r, vs_t, kw_r, vw_t, sel_t, swa_tab)


def _dil_attn_kernel(q_ref, k_ref, vt_ref, tab_ref, o_ref, m_sc, l_sc, acc_sc):
    T, TK = ATT_T, ATT_TK
    qi = pl.program_id(1)
    _init_stats(m_sc, l_sc, acc_sc)
    qs = [q_ref[0, :, h * HEAD_DIM:(h + 1) * HEAD_DIM] for h in range(N_HEADS)]

    def step(kt, carry):
        base = pl.multiple_of(kt * TK, TK)
        w = tab_ref[qi - kt * ATT_R]
        s = jnp.concatenate([lax.dot_general(k_ref[0, h, pl.ds(base, TK), :], qs[h], _NT, preferred_element_type=_F32)
                             for h in range(N_HEADS)], axis=1)
        alpha, p = _softmax_tile(s, jnp.concatenate([w] * N_HEADS, axis=1), m_sc, l_sc, True)
        pv = jnp.concatenate([jnp.dot(vt_ref[0, h, :, pl.ds(base, TK)], p[:, h * T:(h + 1) * T],
                                      preferred_element_type=_F32) for h in range(N_HEADS)], axis=1)
        acc_sc[...] = alpha * acc_sc[...] + pv
        return carry

    lax.fori_loop(_first_key_tile(qi, DIL_CFG[-1][0]), qi // ATT_R + 1, step, 0)
    o_ref[0] = _heads_to_rows(acc_sc[...] / l_sc[...])


def dil_attn_pallas(q_r, k_r, v_t):
    B, H, S, hd = k_r.shape
    T = ATT_T
    R = H * T
    W = H * hd
    tab = _window_table(DIL_CFG[-1][0], _dil_weight)
    q_spec = pl.BlockSpec((1, T, W), lambda b, i: (b, i, 0))
    return pl.pallas_call(
        _dil_attn_kernel,
        grid=(B, S // T),
        in_specs=[q_spec,
                  pl.BlockSpec((1, H, S, hd), lambda b, i: (b, 0, 0, 0)),
                  pl.BlockSpec((1, H, hd, S), lambda b, i: (b, 0, 0, 0)),
                  pl.BlockSpec(tab.shape, lambda b, i: (0, 0, 0))],
        out_specs=q_spec,
        out_shape=jax.ShapeDtypeStruct((B, S, W), _F32),
        scratch_shapes=[pltpu.VMEM((1, R), _F32), pltpu.VMEM((1, R), _F32), pltpu.VMEM((hd, R), _F32)],
        compiler_params=pltpu.CompilerParams(dimension_semantics=("parallel", "arbitrary")),
        name="dil_attn",
    )(q_r, k_r, v_t, tab)


ROPE_LANES = 128
PREP_TM = 256


def _dot_split(a, m):
    hi = a.astype(_BF16)
    lo = (a - hi.astype(_F32)).astype(_BF16)
    mb = m.astype(_BF16)
    return jnp.dot(hi, mb, preferred_element_type=_F32) + jnp.dot(lo, mb, preferred_element_type=_F32)


def _rope_table_kernel(pos_ref, inv_ref, cos_ref, slo_ref, shi_ref):
    ang = pos_ref[...].astype(_F32) * inv_ref[...]
    cos, sin = jnp.cos(ang), jnp.sin(ang)
    first_half = (lax.broadcasted_iota(jnp.int32, ang.shape, 1) % HEAD_DIM) < HEAD_DIM // 2
    cos_ref[...] = cos
    slo_ref[...] = jnp.where(first_half, -sin, 0.0)
    shi_ref[...] = jnp.where(first_half, 0.0, sin)


def rope_tables(positions):
    B, S = positions.shape
    T = B * S
    half = HEAD_DIM // 2
    inv = ROPE_THETA ** (-jnp.arange(half, dtype=_F32) / half)
    inv = jnp.tile(inv, ROPE_LANES // half).reshape(1, ROPE_LANES)
    pos = jnp.broadcast_to(positions.reshape(T, 1), (T, ROPE_LANES))
    tm = 1024
    spec = pl.BlockSpec((tm, ROPE_LANES), lambda i: (i, 0))
    return pl.pallas_call(
        _rope_table_kernel,
        grid=(T // tm,),
        in_specs=[spec, pl.BlockSpec((1, ROPE_LANES), lambda i: (0, 0))],
        out_specs=[spec] * 3,
        out_shape=[jax.ShapeDtypeStruct((T, ROPE_LANES), _F32)] * 3,
        name="rope_tables",
    )(pos, inv)


def _attn_prep_kernel(cq_ref, ck_ref, cv_ref, dq_ref, d1_ref, d2_ref, cos_ref, slo_ref, shi_ref, gains_ref,
                      cqo_ref, cko_ref, cvo_ref, qn_ref, qr_ref, kc_ref, vc_ref, ks_ref, vst_ref, kw_ref, vwt_ref):
    G, hd = GROUP_W, HEAD_DIM
    wide = lambda t: jnp.concatenate([t] * (G // ROPE_LANES), axis=1)
    cos, slo, shi = wide(cos_ref[...]), wide(slo_ref[...]), wide(shi_ref[...])
    head_sum = _group_indicator(G, hd)
    gains = gains_ref[...]
    scale = hd ** -0.5

    def norm(t, n):
        return t * lax.rsqrt(_dot_split(t * t, head_sum) * (1.0 / hd) + NORM_EPS) * gains[n:n + 1]

    def rope(t):
        return t * cos + pltpu.roll(t, G - hd // 2, axis=1) * slo + pltpu.roll(t, hd // 2, axis=1) * shi

    cqo_ref[0] = (rope(norm(cq_ref[0], 0)) * scale).astype(_BF16)
    ck = rope(norm(ck_ref[0], 1)).astype(_BF16)
    for h in range(N_HEADS):
        cko_ref[0, h] = ck[:, h * hd:(h + 1) * hd]
    cvo_ref[0] = cv_ref[0].T.astype(_BF16).reshape(N_HEADS, hd, cv_ref.shape[1])
    qn = norm(dq_ref[0], 2)
    qn_ref[0] = (qn * scale).astype(_BF16)
    qr_ref[0] = (rope(qn) * scale).astype(_BF16)
    d1 = d1_ref[0]
    d2 = d2_ref[0]
    kc_ref[0] = d1[:, 0:hd]
    vc_ref[0] = d1[:, hd:2 * hd]
    ks_ref[0] = rope(norm(d1, 3))[:, 2 * hd:3 * hd].astype(_BF16)
    kw_ref[0] = rope(norm(d2, 4))[:, 0:hd].astype(_BF16)
    vst_ref[0] = d1.T[3 * hd:4 * hd, :].astype(_BF16)
    vwt_ref[0] = d2.T[hd:2 * hd, :].astype(_BF16)


def attn_prep_pallas(proj, tabs, dil_q_g, dil_k_g, nsa_q_g, nsa_ks_g, nsa_kw_g):
    B, S, _ = proj.shape
    tm = PREP_TM
    G, hd, H = GROUP_W, HEAD_DIM, N_HEADS
    c0 = (SECTION_WIDTHS[0] + SECTION_WIDTHS[1]) // G
    z = jnp.zeros((hd,), _F32)
    gains = jnp.stack([jnp.tile(dil_q_g, H), jnp.tile(dil_k_g, H), jnp.tile(nsa_q_g, H),
                       jnp.concatenate([z, z, nsa_ks_g, z]), jnp.concatenate([nsa_kw_g, z, z, z])])
    col = lambda c: pl.BlockSpec((1, tm, G), lambda b, i: (b, i, c))
    nt = S // tm
    tab = pl.BlockSpec((tm, ROPE_LANES), lambda b, i: (b * nt + i, 0))
    tok = lambda w, dt: (pl.BlockSpec((1, tm, w), lambda b, i: (b, i, 0)), jax.ShapeDtypeStruct((B, S, w), dt))
    tr = (pl.BlockSpec((1, hd, tm), lambda b, i: (b, 0, i)), jax.ShapeDtypeStruct((B, hd, S), _BF16))
    outs = [tok(G, _BF16),
            (pl.BlockSpec((1, H, tm, hd), lambda b, i: (b, 0, i, 0)), jax.ShapeDtypeStruct((B, H, S, hd), _BF16)),
            (pl.BlockSpec((1, H, hd, tm), lambda b, i: (b, 0, 0, i)), jax.ShapeDtypeStruct((B, H, hd, S), _BF16)),
            tok(G, _BF16), tok(G, _BF16), tok(hd, _F32), tok(hd, _F32), tok(hd, _BF16), tr, tok(hd, _BF16), tr]
    return pl.pallas_call(
        _attn_prep_kernel,
        grid=(B, nt),
        in_specs=[col(c0), col(c0 + 1), col(c0 + 2), col(c0 + 3), col(c0 + 4), col(c0 + 5), tab, tab, tab,
                  pl.BlockSpec(gains.shape, lambda b, i: (0, 0))],
        out_specs=[o[0] for o in outs],
        out_shape=[o[1] for o in outs],
        compiler_params=pltpu.CompilerParams(dimension_semantics=("parallel", "parallel")),
        name="attn_prep",
    )(proj, proj, proj, proj, proj, proj, *tabs, gains)


NSA_NC = 256
CMP_CHUNK = CMP_STRIDE * HEAD_DIM


def _nsa_compress_kernel(kc_ref, vc_ref, pek_ref, pev_ref, wk1_ref, wk2_ref, wv1_ref, wv2_ref, g_ref, ko_ref, vo_ref):
    def compress(x_ref, pe_ref, w1_ref, w2_ref):
        x = x_ref[0]
        top = _dot_lo(x + pe_ref[0:1], w1_ref[0:CMP_CHUNK, :])
        bot = _dot_lo(x + pe_ref[1:2], w1_ref[CMP_CHUNK:, :])
        hid = top + pltpu.roll(bot, NSA_NC - 1, axis=0)
        return _dot_lo(jax.nn.gelu(hid), w2_ref[...])

    k = compress(kc_ref, pek_ref, wk1_ref, wk2_ref)
    ko_ref[0] = k * lax.rsqrt(jnp.mean(k * k, axis=-1, keepdims=True) + NORM_EPS) * g_ref[...]
    vo_ref[0] = compress(vc_ref, pev_ref, wv1_ref, wv2_ref)


def nsa_compress_pallas(kc, vc, pe_k, pe_v, wk1, wk2, wv1, wv2, kc_g):
    B, S, hd = kc.shape
    assert S // CMP_STRIDE == NSA_NC
    chunks = lambda t: t.reshape(B, NSA_NC, CMP_CHUNK)
    pe2 = lambda p: p.reshape(2, CMP_CHUNK)
    const = lambda t: pl.BlockSpec(t.shape, lambda b: (0,) * t.ndim)
    x_spec = pl.BlockSpec((1, NSA_NC, CMP_CHUNK), lambda b: (b, 0, 0))
    o_spec = pl.BlockSpec((1, NSA_NC, hd), lambda b: (b, 0, 0))
    params = [pe2(pe_k), pe2(pe_v), wk1.astype(_BF16), wk2.astype(_BF16), wv1.astype(_BF16), wv2.astype(_BF16),
              kc_g.reshape(1, hd)]
    return pl.pallas_call(
        _nsa_compress_kernel,
        grid=(B,),
        in_specs=[x_spec, x_spec] + [const(p) for p in params],
        out_specs=[o_spec, o_spec],
        out_shape=[jax.ShapeDtypeStruct((B, NSA_NC, hd), _F32)] * 2,
        name="nsa_compress",
    )(chunks(kc), chunks(vc), *params)


def selection_overlap(n_c, n_sel):
    r = SEL_BLK // CMP_STRIDE
    m = CMP_LEN // CMP_STRIDE
    diff = np.arange(n_c)[:, None] - r * np.arange(n_sel)[None, :]
    offs = (np.arange(r)[:, None] - np.arange(m)[None, :]).reshape(-1)
    return (diff[..., None] == offs).sum(-1).astype(np.float32)


def _nsa_select_kernel(q_ref, kc_ref, vc_ref, ovt_ref, o_ref, selt_ref):
    T = ATT_T
    qi = pl.program_id(1)
    kc = kc_ref[0].astype(_BF16)
    vc = vc_ref[0].astype(_BF16)
    cblk = lax.broadcasted_iota(jnp.int32, (NSA_NC, T), 0)
    tq = qi * T + lax.broadcasted_iota(jnp.int32, (NSA_NC, T), 1)
    valid = cblk * CMP_STRIDE + (CMP_LEN - 1) <= tq
    any_valid = (tq[0:1, :] >= CMP_LEN - 1).astype(_F32)
    p_sum = jnp.zeros((NSA_NC, T), _F32)
    outs = []
    for h in range(N_HEADS):
        s = lax.dot_general(kc, q_ref[0, :, h * HEAD_DIM:(h + 1) * HEAD_DIM], _NT, preferred_element_type=_F32)
        s = jnp.where(valid, s, NEG)
        e = jnp.exp(s - jnp.max(s, axis=0, keepdims=True))
        p = e * (any_valid / jnp.sum(e, axis=0, keepdims=True))
        p_sum = p_sum + p
        outs.append(lax.dot_general(vc, p.astype(_BF16), _TN, preferred_element_type=_F32))
    o_ref[0] = _heads_to_rows(jnp.concatenate(outs, axis=1))

    ovt = ovt_ref[...]
    hi = p_sum.astype(_BF16)
    lo = (p_sum - hi.astype(_F32)).astype(_BF16)
    imp = jnp.dot(ovt, hi, preferred_element_type=_F32) + jnp.dot(ovt, lo, preferred_element_type=_F32)
    jb = lax.broadcasted_iota(jnp.int32, (NSA_NSEL, T), 0)
    cur = (qi * T + lax.broadcasted_iota(jnp.int32, (NSA_NSEL, T), 1)) // SEL_BLK
    forced = (jb == 0) | (jb == cur) | (jb == cur - 1)
    score = jnp.where(jb > cur, -1.0, jnp.where(forced, FORCE_SCORE, imp))
    jbf = jb.astype(_F32)
    sel = jnp.zeros((NSA_NSEL, T), _F32)
    for _ in range(min(SEL_TOP, NSA_NSEL)):
        best = jnp.max(score, axis=0, keepdims=True)
        first = jnp.min(jnp.where(score == best, jbf, float(NSA_NSEL)), axis=0, keepdims=True)
        pick = jbf == first
        sel = jnp.where(pick, 1.0, sel)
        score = jnp.where(pick, NEG, score)
    selt_ref[0] = sel.astype(_BF16)


def nsa_select_pallas(q_n, k_cmp, v_cmp):
    B, S, W = q_n.shape
    T = ATT_T
    n_c = (S - CMP_LEN) // CMP_STRIDE + 1
    ovt = np.zeros((NSA_NSEL, NSA_NC), np.float32)
    ovt[:, :n_c] = selection_overlap(n_c, NSA_NSEL).T
    ovt = jnp.asarray(ovt, _BF16)
    c_spec = pl.BlockSpec((1, NSA_NC, HEAD_DIM), lambda b, i: (b, 0, 0))
    return pl.pallas_call(
        _nsa_select_kernel,
        grid=(B, S // T),
        in_specs=[pl.BlockSpec((1, T, W), lambda b, i: (b, i, 0)), c_spec, c_spec,
                  pl.BlockSpec(ovt.shape, lambda b, i: (0, 0))],
        out_specs=[pl.BlockSpec((1, T, W), lambda b, i: (b, i, 0)),
                   pl.BlockSpec((1, NSA_NSEL, T), lambda b, i: (b, 0, i))],
        out_shape=[jax.ShapeDtypeStruct((B, S, W), _F32), jax.ShapeDtypeStruct((B, NSA_NSEL, S), _BF16)],
        compiler_params=pltpu.CompilerParams(dimension_semantics=("parallel", "parallel")),
        name="nsa_select",
    )(q_n, k_cmp, v_cmp, ovt)


def kernel(x, c, positions, w_ada, b_ada, norm1_g, norm2_g, w_in, w_out, rwkv_mu, rwkv_w0, rwkv_w2, rwkv_a0, rwkv_a2, rwkv_g2, rwkv_kk, rwkv_ka, rwkv_rk, rwkv_gn_w, rwkv_gn_b, conv_w, dil_q_g, dil_k_g, nsa_q_g, nsa_kc_g, nsa_ks_g, nsa_kw_g, nsa_pe_k, nsa_pe_v, nsa_wk1, nsa_wk2, nsa_wv1, nsa_wv2, onorm_g, w_router, b_router, w_gu, b_gu, w_down, b_down):
    B, S, D = x.shape
    tabs = rope_tables(positions)
    for l in range(DEPTH):
        mod = matmul(jax.nn.silu(c), w_ada[l], tm=B, tn=512) + b_ada[l]
        sh1, sc1, gt1, sh2, sc2, gt2 = [m[:, None, :] for m in jnp.split(mod, 6, axis=-1)]
        proj = in_proj_pallas(x, norm1_g[l], sc1, sh1, w_in[l])
        y_a = rwkv7_pallas(proj, rwkv_mu[l], rwkv_w0[l], rwkv_w2[l], rwkv_a0[l], rwkv_a2[l], rwkv_g2[l],
                           rwkv_kk[l], rwkv_ka[l], rwkv_rk[l], rwkv_gn_w[l], rwkv_gn_b[l])
        cq, ck, cv_t, q_n, q_r, kc, vc, ks, vs_t, kw, vw_t = attn_prep_pallas(
            proj, tabs, dil_q_g[l], dil_k_g[l], nsa_q_g[l], nsa_ks_g[l], nsa_kw_g[l])
        y_c = dil_attn_pallas(cq, ck, cv_t)
        k_cmp, v_cmp = nsa_compress_pallas(kc, vc, nsa_pe_k[l], nsa_pe_v[l], nsa_wk1[l], nsa_wk2[l], nsa_wv1[l],
                                           nsa_wv2[l], nsa_kc_g[l])
        o_cmp, sel_t = nsa_select_pallas(q_n, k_cmp, v_cmp)
        o_slc, o_swa = nsa_attn_pallas(q_r, ks, vs_t, kw, vw_t, sel_t)
        x, hb, logits = mix_out_pallas(x, y_a, proj, y_c, o_cmp, o_slc, o_swa, conv_w[l], onorm_g[l], w_out[l], gt1,
                                       norm2_g[l], sc2, sh2, w_router[l], b_router[l])
        x = x + gt2 * moe_pallas(hb, logits, w_gu[l], b_gu[l], w_down[l], b_down[l])
    return x
```

```python
import functools

import numpy as np
import jax
import jax.numpy as jnp
from jax import lax
from jax.experimental import pallas as pl
from jax.experimental.pallas import tpu as pltpu

D_MODEL = 1024
DEPTH = 2

HEAD_DIM = 64
N_MIXERS = 4
GROUP_W = D_MODEL // N_MIXERS
N_HEADS = GROUP_W // HEAD_DIM
MIX_W = N_MIXERS * GROUP_W

RWKV_DECAY_RANK = 64
RWKV_AAA_RANK = 64
RWKV_GATE_RANK = 128
RWKV_DECAY_SCALE = 0.606531
RWKV_GN_EPS = 64e-5

CONV_W = 3

DIL_CFG = ((128, 1), (512, 4), (2048, 16))

CMP_LEN = 32
CMP_STRIDE = 16
CMP_HIDDEN = 256
SEL_BLK = 64
SEL_TOP = 16
SWA_WIN = 512
FORCE_SCORE = 1e4

N_EXPERTS = 32
TOP_K = 4
EXPERT_FF = D_MODEL
SWIGLU_LIMIT = 7.0
SWIGLU_ALPHA = 1.702

ROPE_THETA = 10000.0
NORM_EPS = 1e-6
NEG = -1e30

A_WIDTHS = (GROUP_W, GROUP_W, GROUP_W, RWKV_DECAY_RANK, RWKV_AAA_RANK, RWKV_GATE_RANK)
B_WIDTHS = (GROUP_W, GROUP_W, GROUP_W)
C_WIDTHS = (GROUP_W, GROUP_W, GROUP_W)
D_WIDTHS = (GROUP_W, HEAD_DIM, HEAD_DIM, HEAD_DIM, HEAD_DIM, HEAD_DIM, HEAD_DIM, 3 * N_HEADS)
SECTION_WIDTHS = (sum(A_WIDTHS), sum(B_WIDTHS), sum(C_WIDTHS), sum(D_WIDTHS))
PROJ_W = sum(SECTION_WIDTHS)

_F32 = jnp.float32
_BF16 = jnp.bfloat16
_NT = (((1,), (1,)), ((), ()))
_TN = (((0,), (0,)), ((), ()))


def _dot_hi(a, b, dims=None):
    if dims is None:
        return jnp.dot(a, b, precision=lax.Precision.HIGHEST, preferred_element_type=_F32)
    return lax.dot_general(a, b, dims, precision=lax.Precision.HIGHEST, preferred_element_type=_F32)


def _dot_lo(a, b, dims=None):
    a, b = a.astype(_BF16), b.astype(_BF16)
    if dims is None:
        return jnp.dot(a, b, preferred_element_type=_F32)
    return lax.dot_general(a, b, dims, preferred_element_type=_F32)


def _bf16_parts(a, parts):
    out = []
    for _ in range(parts):
        p = a.astype(_BF16)
        out.append(p)
        a = a - p.astype(_F32)
    return out


def _dot_split(a, m, parts=2):
    mb = m.astype(_BF16)
    return sum(jnp.dot(p, mb, preferred_element_type=_F32) for p in _bf16_parts(a, parts))


def _dot_split_left(m, a, parts=2):
    mb = m.astype(_BF16)
    return sum(jnp.dot(mb, p, preferred_element_type=_F32) for p in _bf16_parts(a, parts))


def _dot_x3(a, b):
    ah, al = _bf16_parts(a, 2)
    bh, bl = _bf16_parts(b, 2)
    d = lambda x, y: jnp.dot(x, y, preferred_element_type=_F32)
    return d(ah, bh) + (d(ah, bl) + d(al, bh))


def _matmul_kernel(x_ref, w_ref, o_ref, *, exact):
    if exact:
        o_ref[...] = _dot_hi(x_ref[...], w_ref[...])
    else:
        o_ref[...] = jnp.dot(x_ref[...].astype(_BF16), w_ref[...], preferred_element_type=_F32)


def matmul(x, w, tm=512, tn=256, exact=False):
    M, K = x.shape
    N = w.shape[1]
    n_pad = -(-N // tn) * tn
    wb = jnp.pad(w if exact else w.astype(_BF16), ((0, 0), (0, n_pad - N)))
    out = pl.pallas_call(
        functools.partial(_matmul_kernel, exact=exact),
        grid=(M // tm, n_pad // tn),
        in_specs=[pl.BlockSpec((tm, K), lambda i, j: (i, 0)),
                  pl.BlockSpec((K, tn), lambda i, j: (0, j))],
        out_specs=pl.BlockSpec((tm, tn), lambda i, j: (i, j)),
        out_shape=jax.ShapeDtypeStruct((M, n_pad), _F32),
        name="matmul",
    )(x, wb)
    return out[:, :N]


PROJ_PAD = -(-PROJ_W // 256) * 256
PROJ_TM = 256
PROJ_TN = 512
ROUTER_PAD = 128
GATE_COL_BLOCK = (PROJ_W - 3 * N_HEADS) // 128
assert GATE_COL_BLOCK * 128 == PROJ_W - 3 * N_HEADS


def _norm_mod(x, g, sc, sh):
    y = x * lax.rsqrt(jnp.mean(x * x, axis=-1, keepdims=True) + NORM_EPS) * g
    return y * (1.0 + sc) + sh


def _in_proj_kernel(x_ref, g_ref, sc_ref, sh_ref, w_ref, o_ref):
    h = _norm_mod(x_ref[0], g_ref[...], sc_ref[0], sh_ref[0]).astype(_BF16)
    for n0 in range(0, PROJ_PAD, PROJ_TN):
        o_ref[0, :, n0:n0 + PROJ_TN] = jnp.dot(h, w_ref[:, n0:n0 + PROJ_TN], preferred_element_type=_F32)


def in_proj_pallas(x, g, sc, sh, w_in):
    B, S, D = x.shape
    tm = PROJ_TM
    w = jnp.pad(w_in.astype(_BF16), ((0, 0), (0, PROJ_PAD - PROJ_W)))
    per_batch = pl.BlockSpec((1, 1, D), lambda b, i: (b, 0, 0))
    return pl.pallas_call(
        _in_proj_kernel,
        grid=(B, S // tm),
        in_specs=[pl.BlockSpec((1, tm, D), lambda b, i: (b, i, 0)),
                  pl.BlockSpec((1, D), lambda b, i: (0, 0)), per_batch, per_batch,
                  pl.BlockSpec((D, PROJ_PAD), lambda b, i: (0, 0))],
        out_specs=pl.BlockSpec((1, tm, PROJ_PAD), lambda b, i: (b, i, 0)),
        out_shape=jax.ShapeDtypeStruct((B, S, PROJ_PAD), _F32),
        compiler_params=pltpu.CompilerParams(dimension_semantics=("parallel", "parallel"),
                                             vmem_limit_bytes=48 * 1024 * 1024),
        name="in_proj",
    )(x, g.reshape(1, D), sc, sh, w)


def _mix_out_kernel(x_ref, ya_ref, bg_ref, cg_ref, xin_ref, cgh_ref, xinh_ref, yc_ref, ocmp_ref, oslc_ref, oswa_ref,
                    gl_ref, convw_ref, ong_ref, wout_ref, gt1_ref, g2_ref, sc2_ref, sh2_ref, wr_ref, br_ref,
                    xo_ref, hb_ref, lg_ref):
    i = pl.program_id(1)
    G = GROUP_W
    u = cg_ref[0] * xin_ref[0]
    halo = jnp.where(i == 0, 0.0, cgh_ref[0] * xinh_ref[0])
    row = lax.broadcasted_iota(jnp.int32, u.shape, 0)
    u1 = jnp.where(row == 0, halo[7:8], pltpu.roll(u, 1, axis=0))
    u2 = jnp.where(row == 0, halo[6:7], jnp.where(row == 1, halo[7:8], pltpu.roll(u, 2, axis=0)))
    cw = convw_ref[...]
    yb = bg_ref[0] * (cw[0:1] * u2 + cw[1:2] * u1 + cw[2:3] * u)

    sg = jax.nn.sigmoid(gl_ref[0])
    er = lax.broadcasted_iota(jnp.int32, (128, G), 0)
    ec = lax.broadcasted_iota(jnp.int32, (128, G), 1) // HEAD_DIM
    yd = None
    for j, o_ref in enumerate((ocmp_ref, oslc_ref, oswa_ref)):
        gate = _dot_split(sg, er == 3 * ec + j)
        yd = gate * o_ref[0] if yd is None else yd + gate * o_ref[0]

    head_sum = _group_indicator(G, HEAD_DIM)
    ong = ong_ref[...]
    parts = [ya_ref[0]]
    for n, y in enumerate((yb, yc_ref[0], yd)):
        ms = _dot_split(y * y, head_sum) * (1.0 / HEAD_DIM)
        parts.append(y * lax.rsqrt(ms + NORM_EPS) * ong[:, n * G:(n + 1) * G])
    mixed = jnp.dot(jnp.concatenate(parts, axis=1).astype(_BF16), wout_ref[...], preferred_element_type=_F32)
    x = x_ref[0] + gt1_ref[0] * mixed
    xo_ref[0] = x
    h = _norm_mod(x, g2_ref[...], sc2_ref[0], sh2_ref[0])
    hb_ref[0] = h.astype(_BF16)
    lg_ref[0] = _dot_hi(h, wr_ref[...]) + br_ref[...]


def mix_out_pallas(x, y_a, proj, y_c, o_cmp, o_slc, o_swa, conv_w, onorm_g, w_out, gt1, g2, sc2, sh2, w_router, b_router):
    B, S, D = x.shape
    tm = PROJ_TM
    G = GROUP_W
    b0 = SECTION_WIDTHS[0] // G
    tile = lambda w, col: pl.BlockSpec((1, tm, w), lambda b, i: (b, i, col))
    halo = lambda col: pl.BlockSpec((1, 8, G), lambda b, i: (b, jnp.maximum(i * (tm // 8) - 1, 0), col))
    const = lambda t: pl.BlockSpec(t.shape, lambda b, i: (0,) * t.ndim)
    per_batch = pl.BlockSpec((1, 1, D), lambda b, i: (b, 0, 0))
    wr = jnp.pad(w_router, ((0, 0), (0, ROUTER_PAD - N_EXPERTS)))
    br = jnp.pad(b_router, (0, ROUTER_PAD - N_EXPERTS)).reshape(1, ROUTER_PAD)
    consts = [conv_w, onorm_g.reshape(1, 3 * G), w_out.astype(_BF16)]
    return pl.pallas_call(
        _mix_out_kernel,
        grid=(B, S // tm),
        in_specs=[tile(D, 0), tile(G, 0), tile(G, b0), tile(G, b0 + 1), tile(G, b0 + 2), halo(b0 + 1), halo(b0 + 2),
                  tile(G, 0), tile(G, 0), tile(G, 0), tile(G, 0), tile(128, GATE_COL_BLOCK)]
                 + [const(t) for t in consts] + [per_batch, pl.BlockSpec((1, D), lambda b, i: (0, 0)), per_batch,
                                                 per_batch, const(wr), const(br)],
        out_specs=[tile(D, 0), tile(D, 0), tile(ROUTER_PAD, 0)],
        out_shape=[jax.ShapeDtypeStruct((B, S, D), _F32), jax.ShapeDtypeStruct((B, S, D), _BF16),
                   jax.ShapeDtypeStruct((B, S, ROUTER_PAD), _F32)],
        compiler_params=pltpu.CompilerParams(dimension_semantics=("parallel", "parallel")),
        name="mix_out",
    )(x, y_a, proj, proj, proj, proj, proj, y_c, o_cmp, o_slc, o_swa, proj, *consts, gt1, g2.reshape(1, D), sc2, sh2,
      wr, br)


RWKV_CHUNK = 64
RWKV_TILE = 256


def _group_indicator(n, group):
    r = lax.broadcasted_iota(jnp.int32, (n, n), 0) // group
    c = lax.broadcasted_iota(jnp.int32, (n, n), 1) // group
    return r == c


def _rwkv_kernel(za_ref, mu_ref, w0_ref, w2_ref, a0_ref, a2_ref, g2_ref, kk_ref, ka_ref, rk_ref,
                 gnw_ref, gnb_ref, o_ref, prev_sc, h_sc):
    C = RWKV_CHUNK
    TT = RWKV_TILE
    c = pl.program_id(1)

    @pl.when(c == 0)
    def _():
        prev_sc[...] = jnp.zeros_like(prev_sc)
        h_sc[...] = jnp.zeros_like(h_sc)

    z = za_ref[0]
    row = lax.broadcasted_iota(jnp.int32, z.shape, 0)
    zs = jnp.where(row == 0, prev_sc[...], pltpu.roll(z, 1, axis=0))
    prev_sc[...] = z[TT - 1:TT, :]
    z = z + (zs - z) * mu_ref[...]

    G = GROUP_W
    r, k, v = z[:, 0:G], z[:, G:2 * G], z[:, 2 * G:3 * G]
    o = 3 * G
    wd = z[:, o:o + RWKV_DECAY_RANK]
    ad = z[:, o + RWKV_DECAY_RANK:o + RWKV_DECAY_RANK + RWKV_AAA_RANK]
    gd = z[:, o + RWKV_DECAY_RANK + RWKV_AAA_RANK:]

    lw = -RWKV_DECAY_SCALE * jax.nn.sigmoid(w0_ref[...] + _dot_lo(jnp.tanh(wd), w2_ref[...]))
    a = jax.nn.sigmoid(a0_ref[...] + _dot_lo(ad, a2_ref[...]))
    g = _dot_lo(jax.nn.sigmoid(gd), g2_ref[...])

    head_sum = _group_indicator(G, HEAD_DIM)
    kk = k * kk_ref[...]
    kk = kk * lax.rsqrt(_dot_split(kk * kk, head_sum) + 1e-12)
    k = k * (1.0 + (a - 1.0) * ka_ref[...])
    b = kk * a

    ti = lax.broadcasted_iota(jnp.int32, (C, C), 0)
    tj = lax.broadcasted_iota(jnp.int32, (C, C), 1)
    incl = ti >= tj
    strict = ti > tj
    eye = ti == tj
    blk16 = (ti // 16) == (tj // 16)
    blk32 = (ti // 32) == (tj // 32)
    eye_f = eye.astype(_F32)

    ri = lax.broadcasted_iota(jnp.int32, (TT, TT), 0)
    rj = lax.broadcasted_iota(jnp.int32, (TT, TT), 1)
    chunk_tri = (ri >= rj) & (ri // C == rj // C)
    cum = _dot_split_left(chunk_tri, lw, parts=3)
    g_in = jnp.exp(cum)
    A_all = -kk * jnp.exp(cum - lw)
    R_all = r * g_in
    g_inv = jnp.exp(-cum)
    B_all = b * g_inv
    K_all = k * g_inv

    units = [(ci, h) for ci in range(TT // C) for h in range(N_HEADS)]

    def part(t, u):
        ci, h = u
        return t[ci * C:(ci + 1) * C, h * HEAD_DIM:(h + 1) * HEAD_DIM]

    def g_end(u):
        ci, h = u
        return g_in[(ci + 1) * C - 1:(ci + 1) * C, h * HEAD_DIM:(h + 1) * HEAD_DIM]

    A = [part(A_all, u) for u in units]
    R = [part(R_all, u) for u in units]
    B = [part(B_all, u) for u in units]
    Kt = [part(K_all, u) for u in units]
    V = [part(v, u) for u in units]
    n = range(len(units))
    gram = [_dot_lo(jnp.concatenate([A[i], R[i]], axis=0), jnp.concatenate([B[i], Kt[i]], axis=0), _NT) for i in n]
    l_ab = [jnp.where(strict, gram[i][0:C, 0:C], 0.0) for i in n]
    l_ak = [jnp.where(strict, gram[i][0:C, C:2 * C], 0.0) for i in n]
    m_rb = [jnp.where(incl, gram[i][C:2 * C, 0:C], 0.0) for i in n]
    m_rk = [jnp.where(incl, gram[i][C:2 * C, C:2 * C], 0.0) for i in n]
    p = [jnp.where(blk16, l_ab[i], 0.0) for i in n]
    x = [eye_f + p[i] for i in n]
    for _ in range(3):
        p = [_dot_lo(p[i], p[i]) for i in n]
        x = [_dot_lo(x[i], eye_f + p[i]) for i in n]
    for lvl in (blk32 & ~blk16, ~blk32):
        xl = [_dot_lo(x[i], jnp.where(lvl, l_ab[i], 0.0)) for i in n]
        x = [x[i] + _dot_lo(xl[i], x[i]) for i in n]
    lv = [_dot_lo(l_ak[i], V[i]) for i in n]
    tap = [_dot_lo(x[i], jnp.concatenate([A[i], lv[i]], axis=1)) for i in n]
    m1 = [_dot_lo(m_rb[i], tap[i]) for i in n]
    mv = [_dot_lo(m_rk[i], V[i]) for i in n]
    bt = [_dot_lo(B[i] * g_end(units[i]), tap[i], _TN) for i in n]
    kv = [_dot_lo(Kt[i] * g_end(units[i]), V[i], _TN) for i in n]
    w_yh = [jnp.concatenate([R[i] + m1[i][:, 0:HEAD_DIM],
                             jnp.where(eye, g_end(units[i]), 0.0) + bt[i][:, 0:HEAD_DIM]], axis=0) for i in n]
    y0 = [m1[i][:, HEAD_DIM:] + mv[i] for i in n]
    h_add = [bt[i][:, HEAD_DIM:] + kv[i] for i in n]
    state = [h_sc[h] for h in range(N_HEADS)]
    y_rows = []
    for ci in range(TT // C):
        ys = []
        for h in range(N_HEADS):
            i = ci * N_HEADS + h
            nxt = _dot_x3(w_yh[i], state[h])
            ys.append(nxt[0:C] + y0[i])
            state[h] = nxt[C:2 * C] + h_add[i]
        y_rows.append(jnp.concatenate(ys, axis=1))
    for h in range(N_HEADS):
        h_sc[h] = state[h]
    y = jnp.concatenate(y_rows, axis=0)

    mu = _dot_split(y, head_sum) * (1.0 / HEAD_DIM)
    d = y - mu
    var = _dot_split(d * d, head_sum) * (1.0 / HEAD_DIM)
    y = d * lax.rsqrt(var + RWKV_GN_EPS) * gnw_ref[...] + gnb_ref[...]
    y = y + _dot_split(r * k * rk_ref[...], head_sum) * v
    o_ref[0] = y * g


def rwkv7_pallas(za, mu, w0, w2, a0, a2, g2, k_k, k_a, r_k, gn_w, gn_b):
    B, S, _ = za.shape
    W = SECTION_WIDTHS[0]
    TT = RWKV_TILE
    row = lambda t: t.reshape(1, -1).astype(_F32)
    full = lambda t: pl.BlockSpec(t.shape, lambda b, c: (0,) * t.ndim)
    params = [row(mu), row(w0), w2.astype(_BF16), row(a0), a2.astype(_BF16), g2.astype(_BF16),
              row(k_k), row(k_a), row(r_k), row(gn_w), row(gn_b)]
    return pl.pallas_call(
        _rwkv_kernel,
        grid=(B, S // TT),
        in_specs=[pl.BlockSpec((1, TT, W), lambda b, c: (b, c, 0))] + [full(p) for p in params],
        out_specs=pl.BlockSpec((1, TT, GROUP_W), lambda b, c: (b, c, 0)),
        out_shape=jax.ShapeDtypeStruct((B, S, GROUP_W), _F32),
        scratch_shapes=[pltpu.VMEM((1, W), _F32), pltpu.VMEM((N_HEADS, HEAD_DIM, HEAD_DIM), _F32)],
        compiler_params=pltpu.CompilerParams(dimension_semantics=("parallel", "arbitrary")),
        name="rwkv7",
    )(za, *params)


MOE_TM = 512
MOE_FC = 512
MOE_CAST_ROWS = 128
MOE_GROUPS = 2


def _moe_ffn_kernel(blk_e_ref, n_used_ref, x_ref, wgu_ref, bgu_ref, wd_ref, bd_ref, o_ref, wgu_sc, wd_sc):
    i = pl.program_id(0)
    F = EXPERT_FF
    new_expert = jnp.logical_or(i == 0, blk_e_ref[i] != blk_e_ref[jnp.maximum(i - 1, 0)])

    @pl.when(new_expert)
    def _():
        def cast(j, carry):
            rows = pl.ds(pl.multiple_of(j * MOE_CAST_ROWS, MOE_CAST_ROWS), MOE_CAST_ROWS)
            wgu_sc[rows, :] = wgu_ref[0, rows, :].astype(_BF16)
            wd_sc[rows, :] = wd_ref[0, rows, :].astype(_BF16)
            return carry
        lax.fori_loop(0, F // MOE_CAST_ROWS, cast, 0)

    @pl.when(i < n_used_ref[0])
    def _():
        x = x_ref[...]
        acc = None
        for c in range(F // MOE_FC):
            lo = c * MOE_FC
            gate = jnp.dot(x, wgu_sc[:, lo:lo + MOE_FC], preferred_element_type=_F32) + bgu_ref[0, :, lo:lo + MOE_FC]
            up = jnp.dot(x, wgu_sc[:, F + lo:F + lo + MOE_FC], preferred_element_type=_F32) + bgu_ref[0, :, F + lo:F + lo + MOE_FC]
            gate = jnp.minimum(gate, SWIGLU_LIMIT)
            up = jnp.clip(up, -SWIGLU_LIMIT, SWIGLU_LIMIT)
            act = gate * jax.nn.sigmoid(SWIGLU_ALPHA * gate) * (up + 1.0)
            part = jnp.dot(act.astype(_BF16), wd_sc[lo:lo + MOE_FC, :], preferred_element_type=_F32)
            acc = part if acc is None else acc + part
        o_ref[...] = (acc + bd_ref[0]).astype(o_ref.dtype)

    @pl.when(i >= n_used_ref[0])
    def _():
        o_ref[...] = jnp.zeros_like(o_ref)


def moe_ffn(xs, blk_e, n_used, layer, w_gu, b_gu, w_down, b_down):
    n_rows, D = xs.shape
    L, E, _, F2 = w_gu.shape
    assert D == EXPERT_FF and F2 == 2 * EXPERT_FF
    n_blk = n_rows // MOE_TM
    grid_spec = pltpu.PrefetchScalarGridSpec(
        num_scalar_prefetch=2,
        grid=(n_blk,),
        in_specs=[pl.BlockSpec((MOE_TM, D), lambda i, e, n: (i, 0)),
                  pl.BlockSpec((None, 1, D, F2), lambda i, e, n: (layer, e[i], 0, 0)),
                  pl.BlockSpec((None, 1, 1, F2), lambda i, e, n: (layer, e[i], 0, 0)),
                  pl.BlockSpec((None, 1, F2 // 2, D), lambda i, e, n: (layer, e[i], 0, 0)),
                  pl.BlockSpec((None, 1, 1, D), lambda i, e, n: (layer, e[i], 0, 0))],
        out_specs=pl.BlockSpec((MOE_TM, D), lambda i, e, n: (i, 0)),
        scratch_shapes=[pltpu.VMEM((D, F2), _BF16), pltpu.VMEM((F2 // 2, D), _BF16)],
    )
    return pl.pallas_call(
        _moe_ffn_kernel,
        grid_spec=grid_spec,
        out_shape=jax.ShapeDtypeStruct((n_rows, D), _BF16),
        compiler_params=pltpu.CompilerParams(dimension_semantics=("arbitrary",),
                                             vmem_limit_bytes=52 * 1024 * 1024),
        name="moe_ffn",
    )(blk_e, n_used, xs, w_gu, b_gu.reshape(L, E, 1, F2), w_down, b_down.reshape(L, E, 1, D))


def moe_pallas(hb, logits, layer, w_gu, b_gu, w_down, b_down):
    T, D = hb.shape
    TK = T * TOP_K
    TM = MOE_TM
    E = N_EXPERTS
    i32 = jnp.int32
    top_val, top_idx = lax.top_k(logits[:, :E], TOP_K)
    gates = jax.nn.softmax(top_val, axis=-1)
    e_flat = top_idx.T.reshape(TK).astype(i32)
    counts = jnp.sum((e_flat[:, None] == jnp.arange(E, dtype=i32)[None, :]).astype(i32), axis=0)
    need = jnp.repeat((-counts) % TM, TM)
    d_idx = jnp.arange(E * TM, dtype=i32)
    d_key = jnp.where(d_idx % TM < need, d_idx // TM, E)
    n_rows = TK + E * TM
    keys = jnp.concatenate([e_flat, d_key])
    toks = jnp.concatenate([jnp.arange(TK, dtype=i32) % T, jnp.zeros((E * TM,), i32)])
    rows = jnp.arange(n_rows, dtype=i32)
    s_keys, src_tok, s_slot = lax.sort((keys, toks, rows), num_keys=1)
    _, row_of_slot = lax.sort((s_slot, rows), num_keys=1)
    row_of = row_of_slot[:TK]
    blk_e = s_keys[::TM]
    n_used = jnp.sum((blk_e < E).astype(i32)).reshape(1)
    blk_e = jnp.where(blk_e < E, blk_e, blk_e[jnp.maximum(n_used[0] - 1, 0)])
    xs = jnp.take(hb, src_tok, axis=0)
    ybuf = moe_ffn(xs, blk_e, n_used, layer, w_gu, b_gu, w_down, b_down)
    y = jnp.take(ybuf, row_of, axis=0).reshape(TOP_K, T, D).astype(_F32) * gates.T[:, :, None]
    return jnp.sum(y, axis=0)


ATT_T = 128
ATT_TK = 512
ATT_R = ATT_TK // ATT_T
NSA_NSEL = 64


def _window_table(window, weight_of_distance):
    T, TK = ATT_T, ATT_TK
    n = window // T + ATT_R + 1
    d = (np.arange(n)[:, None, None] * T + np.arange(T)[None, None, :] - np.arange(TK)[None, :, None])
    return jnp.asarray(weight_of_distance(d).astype(np.float32))


def _first_key_tile(qi, window):
    return jnp.maximum(qi * ATT_T - window, 0) // ATT_TK


def _swa_weight(d):
    return (d >= 0) & (d <= SWA_WIN - 1)


def _dil_weight(d):
    return sum(((d >= 0) & (d <= window) & (d % dil == 0)).astype(np.int32) for window, dil in DIL_CFG)


def _softmax_tile(s, w, m_ref, l_ref, weighted):
    s = jnp.where(w > 0.0, s, NEG)
    m_prev = m_ref[...]
    m_new = jnp.maximum(m_prev, jnp.max(s, axis=0, keepdims=True))
    alpha = jnp.exp(m_prev - m_new)
    p = jnp.exp(s - m_new)
    if weighted:
        p = p * w
    l_ref[...] = alpha * l_ref[...] + jnp.sum(p, axis=0, keepdims=True)
    m_ref[...] = m_new
    return alpha, p.astype(_BF16)


def _init_stats(m_sc, l_sc, acc_sc):
    m_sc[...] = jnp.full_like(m_sc, NEG)
    l_sc[...] = jnp.zeros_like(l_sc)
    acc_sc[...] = jnp.zeros_like(acc_sc)


def _heads_to_rows(ot):
    T = ATT_T
    pairs = []
    for h in range(0, N_HEADS, 2):
        two = jnp.concatenate([ot[:, h * T:(h + 1) * T], ot[:, (h + 1) * T:(h + 2) * T]], axis=0)
        pairs.append(two.T)
    return jnp.concatenate(pairs, axis=1)


def _nsa_attn_kernel(q_ref, ks_ref, vst_ref, kw_ref, vwt_ref, selt_ref, swa_ref, oslc_ref, oswa_ref,
                     m_sc, l_sc, acc_sc):
    T, TK = ATT_T, ATT_TK
    qi = pl.program_id(1)
    q = q_ref[0]
    q = jnp.concatenate([q[:, h * HEAD_DIM:(h + 1) * HEAD_DIM] for h in range(N_HEADS)], axis=0)
    selt = selt_ref[0]
    key_blk = lax.broadcasted_iota(jnp.int32, (TK, NSA_NSEL), 0) // SEL_BLK
    blk_id = lax.broadcasted_iota(jnp.int32, (TK, NSA_NSEL), 1)
    kr = lax.broadcasted_iota(jnp.int32, (TK, T), 0)
    qc = lax.broadcasted_iota(jnp.int32, (TK, T), 1)

    def step(k_ref, vt_ref, kt, w):
        base = pl.multiple_of(kt * TK, TK)
        s = lax.dot_general(k_ref[0, pl.ds(base, TK), :], q, _NT, preferred_element_type=_F32)
        alpha, p = _softmax_tile(s, jnp.concatenate([w] * N_HEADS, axis=1), m_sc, l_sc, False)
        acc_sc[...] = alpha * acc_sc[...] + jnp.dot(vt_ref[0, :, pl.ds(base, TK)], p, preferred_element_type=_F32)

    def finish(o_ref):
        o_ref[0] = _heads_to_rows(acc_sc[...] / l_sc[...])

    last = qi // ATT_R
    _init_stats(m_sc, l_sc, acc_sc)

    def slc_step(kt, carry):
        in_blk = (key_blk + kt * (TK // SEL_BLK) == blk_id).astype(_BF16)
        w = jnp.dot(in_blk, selt, preferred_element_type=_F32)
        w = jnp.where(kt * TK + kr <= qi * T + qc, w, 0.0)
        step(ks_ref, vst_ref, kt, w)
        return carry

    lax.fori_loop(0, last + 1, slc_step, 0)
    finish(oslc_ref)

    _init_stats(m_sc, l_sc, acc_sc)

    def swa_step(kt, carry):
        step(kw_ref, vwt_ref, kt, swa_ref[qi - kt * ATT_R])
        return carry

    lax.fori_loop(_first_key_tile(qi, SWA_WIN - 1), last + 1, swa_step, 0)
    finish(oswa_ref)


def nsa_attn_pallas(q_r, ks_r, vs_t, kw_r, vw_t, sel_t):
    B, S, W = q_r.shape
    assert S // SEL_BLK == NSA_NSEL
    T = ATT_T
    R = N_HEADS * T
    swa_tab = _window_table(SWA_WIN - 1, _swa_weight)
    k_spec = pl.BlockSpec((1, S, HEAD_DIM), lambda b, i: (b, 0, 0))
    vt_spec = pl.BlockSpec((1, HEAD_DIM, S), lambda b, i: (b, 0, 0))
    q_spec = pl.BlockSpec((1, T, W), lambda b, i: (b, i, 0))
    return pl.pallas_call(
        _nsa_attn_kernel,
        grid=(B, S // T),
        in_specs=[q_spec, k_spec, vt_spec, k_spec, vt_spec,
                  pl.BlockSpec((1, NSA_NSEL, T), lambda b, i: (b, 0, i)),
                  pl.BlockSpec(swa_tab.shape, lambda b, i: (0, 0, 0))],
        out_specs=[q_spec, q_spec],
        out_shape=[jax.ShapeDtypeStruct((B, S, W), _F32)] * 2,
        scratch_shapes=[pltpu.VMEM((1, R), _F32), pltpu.VMEM((1, R), _F32), pltpu.VMEM((HEAD_DIM, R), _F32)],
        compiler_params=pltpu.CompilerParams(dimension_semantics=("parallel", "arbitrary")),
        name="nsa_attn",
    )(q_r, ks_r, vs_t, kw_r, vw_t, sel_t, swa_tab)


def _dil_attn_kernel(q_ref, k_ref, vt_ref, tab_ref, o_ref, m_sc, l_sc, acc_sc):
    T, TK = ATT_T, ATT_TK
    qi = pl.program_id(1)
    _init_stats(m_sc, l_sc, acc_sc)
    qs = [q_ref[0, :, h * HEAD_DIM:(h + 1) * HEAD_DIM] for h in range(N_HEADS)]

    def step(kt, carry):
        base = pl.multiple_of(kt * TK, TK)
        w = tab_ref[qi - kt * ATT_R]
        s = jnp.concatenate([lax.dot_general(k_ref[0, h, pl.ds(base, TK), :], qs[h], _NT, preferred_element_type=_F32)
                             for h in range(N_HEADS)], axis=1)
        alpha, p = _softmax_tile(s, jnp.concatenate([w] * N_HEADS, axis=1), m_sc, l_sc, True)
        pv = jnp.concatenate([jnp.dot(vt_ref[0, h, :, pl.ds(base, TK)], p[:, h * T:(h + 1) * T],
                                      preferred_element_type=_F32) for h in range(N_HEADS)], axis=1)
        acc_sc[...] = alpha * acc_sc[...] + pv
        return carry

    lax.fori_loop(_first_key_tile(qi, DIL_CFG[-1][0]), qi // ATT_R + 1, step, 0)
    o_ref[0] = _heads_to_rows(acc_sc[...] / l_sc[...])


def dil_attn_pallas(q_r, k_r, v_t):
    B, H, S, hd = k_r.shape
    T = ATT_T
    R = H * T
    W = H * hd
    tab = _window_table(DIL_CFG[-1][0], _dil_weight)
    q_spec = pl.BlockSpec((1, T, W), lambda b, i: (b, i, 0))
    return pl.pallas_call(
        _dil_attn_kernel,
        grid=(B, S // T),
        in_specs=[q_spec,
                  pl.BlockSpec((1, H, S, hd), lambda b, i: (b, 0, 0, 0)),
                  pl.BlockSpec((1, H, hd, S), lambda b, i: (b, 0, 0, 0)),
                  pl.BlockSpec(tab.shape, lambda b, i: (0, 0, 0))],
        out_specs=q_spec,
        out_shape=jax.ShapeDtypeStruct((B, S, W), _F32),
        scratch_shapes=[pltpu.VMEM((1, R), _F32), pltpu.VMEM((1, R), _F32), pltpu.VMEM((hd, R), _F32)],
        compiler_params=pltpu.CompilerParams(dimension_semantics=("parallel", "arbitrary")),
        name="dil_attn",
    )(q_r, k_r, v_t, tab)


ROPE_LANES = 128
PREP_TM = 256


def _rope_table_kernel(pos_ref, inv_ref, cos_ref, slo_ref, shi_ref):
    ang = pos_ref[...].astype(_F32) * inv_ref[...]
    cos, sin = jnp.cos(ang), jnp.sin(ang)
    first_half = (lax.broadcasted_iota(jnp.int32, ang.shape, 1) % HEAD_DIM) < HEAD_DIM // 2
    cos_ref[...] = cos
    slo_ref[...] = jnp.where(first_half, -sin, 0.0)
    shi_ref[...] = jnp.where(first_half, 0.0, sin)


def rope_tables(positions):
    B, S = positions.shape
    T = B * S
    half = HEAD_DIM // 2
    inv = ROPE_THETA ** (-jnp.arange(half, dtype=_F32) / half)
    inv = jnp.tile(inv, ROPE_LANES // half).reshape(1, ROPE_LANES)
    pos = jnp.broadcast_to(positions.reshape(T, 1), (T, ROPE_LANES))
    tm = 1024
    spec = pl.BlockSpec((tm, ROPE_LANES), lambda i: (i, 0))
    return pl.pallas_call(
        _rope_table_kernel,
        grid=(T // tm,),
        in_specs=[spec, pl.BlockSpec((1, ROPE_LANES), lambda i: (0, 0))],
        out_specs=[spec] * 3,
        out_shape=[jax.ShapeDtypeStruct((T, ROPE_LANES), _F32)] * 3,
        name="rope_tables",
    )(pos, inv)


def _attn_prep_kernel(cq_ref, ck_ref, cv_ref, dq_ref, d1_ref, d2_ref, cos_ref, slo_ref, shi_ref, gains_ref,
                      cqo_ref, cko_ref, cvo_ref, qn_ref, qr_ref, kc_ref, vc_ref, ks_ref, vst_ref, kw_ref, vwt_ref):
    G, hd = GROUP_W, HEAD_DIM
    wide = lambda t: jnp.concatenate([t] * (G // ROPE_LANES), axis=1)
    cos, slo, shi = wide(cos_ref[...]), wide(slo_ref[...]), wide(shi_ref[...])
    head_sum = _group_indicator(G, hd)
    gains = gains_ref[...]
    scale = hd ** -0.5

    def norm(t, n):
        return t * lax.rsqrt(_dot_split(t * t, head_sum) * (1.0 / hd) + NORM_EPS) * gains[n:n + 1]

    def rope(t):
        return t * cos + pltpu.roll(t, G - hd // 2, axis=1) * slo + pltpu.roll(t, hd // 2, axis=1) * shi

    cqo_ref[0] = (rope(norm(cq_ref[0], 0)) * scale).astype(_BF16)
    ck = rope(norm(ck_ref[0], 1)).astype(_BF16)
    for h in range(N_HEADS):
        cko_ref[0, h] = ck[:, h * hd:(h + 1) * hd]
    cvo_ref[0] = cv_ref[0].T.astype(_BF16).reshape(N_HEADS, hd, cv_ref.shape[1])
    qn = norm(dq_ref[0], 2)
    qn_ref[0] = (qn * scale).astype(_BF16)
    qr_ref[0] = (rope(qn) * scale).astype(_BF16)
    d1 = d1_ref[0]
    d2 = d2_ref[0]
    kc_ref[0] = d1[:, 0:hd]
    vc_ref[0] = d1[:, hd:2 * hd]
    ks_ref[0] = rope(norm(d1, 3))[:, 2 * hd:3 * hd].astype(_BF16)
    kw_ref[0] = rope(norm(d2, 4))[:, 0:hd].astype(_BF16)
    vst_ref[0] = d1.T[3 * hd:4 * hd, :].astype(_BF16)
    vwt_ref[0] = d2.T[hd:2 * hd, :].astype(_BF16)


def attn_prep_pallas(proj, tabs, dil_q_g, dil_k_g, nsa_q_g, nsa_ks_g, nsa_kw_g):
    B, S, _ = proj.shape
    tm = PREP_TM
    G, hd, H = GROUP_W, HEAD_DIM, N_HEADS
    c0 = (SECTION_WIDTHS[0] + SECTION_WIDTHS[1]) // G
    z = jnp.zeros((hd,), _F32)
    gains = jnp.stack([jnp.tile(dil_q_g, H), jnp.tile(dil_k_g, H), jnp.tile(nsa_q_g, H),
                       jnp.concatenate([z, z, nsa_ks_g, z]), jnp.concatenate([nsa_kw_g, z, z, z])])
    col = lambda c: pl.BlockSpec((1, tm, G), lambda b, i: (b, i, c))
    nt = S // tm
    tab = pl.BlockSpec((tm, ROPE_LANES), lambda b, i: (b * nt + i, 0))
    tok = lambda w, dt: (pl.BlockSpec((1, tm, w), lambda b, i: (b, i, 0)), jax.ShapeDtypeStruct((B, S, w), dt))
    tr = (pl.BlockSpec((1, hd, tm), lambda b, i: (b, 0, i)), jax.ShapeDtypeStruct((B, hd, S), _BF16))
    outs = [tok(G, _BF16),
            (pl.BlockSpec((1, H, tm, hd), lambda b, i: (b, 0, i, 0)), jax.ShapeDtypeStruct((B, H, S, hd), _BF16)),
            (pl.BlockSpec((1, H, hd, tm), lambda b, i: (b, 0, 0, i)), jax.ShapeDtypeStruct((B, H, hd, S), _BF16)),
            tok(G, _BF16), tok(G, _BF16), tok(hd, _F32), tok(hd, _F32), tok(hd, _BF16), tr, tok(hd, _BF16), tr]
    return pl.pallas_call(
        _attn_prep_kernel,
        grid=(B, nt),
        in_specs=[col(c0), col(c0 + 1), col(c0 + 2), col(c0 + 3), col(c0 + 4), col(c0 + 5), tab, tab, tab,
                  pl.BlockSpec(gains.shape, lambda b, i: (0, 0))],
        out_specs=[o[0] for o in outs],
        out_shape=[o[1] for o in outs],
        compiler_params=pltpu.CompilerParams(dimension_semantics=("parallel", "parallel")),
        name="attn_prep",
    )(proj, proj, proj, proj, proj, proj, *tabs, gains)


NSA_NC = 256
CMP_CHUNK = CMP_STRIDE * HEAD_DIM


def _nsa_compress_kernel(kc_ref, vc_ref, pek_ref, pev_ref, wk1_ref, wk2_ref, wv1_ref, wv2_ref, g_ref, ko_ref, vo_ref):
    def compress(x_ref, pe_ref, w1_ref, w2_ref):
        x = x_ref[0]
        top = _dot_lo(x + pe_ref[0:1], w1_ref[0:CMP_CHUNK, :])
        bot = _dot_lo(x + pe_ref[1:2], w1_ref[CMP_CHUNK:, :])
        hid = top + pltpu.roll(bot, NSA_NC - 1, axis=0)
        return _dot_lo(jax.nn.gelu(hid), w2_ref[...])

    k = compress(kc_ref, pek_ref, wk1_ref, wk2_ref)
    ko_ref[0] = k * lax.rsqrt(jnp.mean(k * k, axis=-1, keepdims=True) + NORM_EPS) * g_ref[...]
    vo_ref[0] = compress(vc_ref, pev_ref, wv1_ref, wv2_ref)


def nsa_compress_pallas(kc, vc, pe_k, pe_v, wk1, wk2, wv1, wv2, kc_g):
    B, S, hd = kc.shape
    assert S // CMP_STRIDE == NSA_NC
    chunks = lambda t: t.reshape(B, NSA_NC, CMP_CHUNK)
    pe2 = lambda p: p.reshape(2, CMP_CHUNK)
    const = lambda t: pl.BlockSpec(t.shape, lambda b: (0,) * t.ndim)
    x_spec = pl.BlockSpec((1, NSA_NC, CMP_CHUNK), lambda b: (b, 0, 0))
    o_spec = pl.BlockSpec((1, NSA_NC, hd), lambda b: (b, 0, 0))
    params = [pe2(pe_k), pe2(pe_v), wk1.astype(_BF16), wk2.astype(_BF16), wv1.astype(_BF16), wv2.astype(_BF16),
              kc_g.reshape(1, hd)]
    return pl.pallas_call(
        _nsa_compress_kernel,
        grid=(B,),
        in_specs=[x_spec, x_spec] + [const(p) for p in params],
        out_specs=[o_spec, o_spec],
        out_shape=[jax.ShapeDtypeStruct((B, NSA_NC, hd), _F32)] * 2,
        name="nsa_compress",
    )(chunks(kc), chunks(vc), *params)


def selection_overlap(n_c, n_sel):
    r = SEL_BLK // CMP_STRIDE
    m = CMP_LEN // CMP_STRIDE
    diff = np.arange(n_c)[:, None] - r * np.arange(n_sel)[None, :]
    offs = (np.arange(r)[:, None] - np.arange(m)[None, :]).reshape(-1)
    return (diff[..., None] == offs).sum(-1).astype(np.float32)


def _nsa_select_kernel(q_ref, kc_ref, vc_ref, ovt_ref, o_ref, selt_ref):
    T = ATT_T
    qi = pl.program_id(1)
    kc = kc_ref[0].astype(_BF16)
    vc = vc_ref[0].astype(_BF16)
    cblk = lax.broadcasted_iota(jnp.int32, (NSA_NC, T), 0)
    tq = qi * T + lax.broadcasted_iota(jnp.int32, (NSA_NC, T), 1)
    valid = cblk * CMP_STRIDE + (CMP_LEN - 1) <= tq
    any_valid = (tq[0:1, :] >= CMP_LEN - 1).astype(_F32)
    p_sum = jnp.zeros((NSA_NC, T), _F32)
    outs = []
    for h in range(N_HEADS):
        s = lax.dot_general(kc, q_ref[0, :, h * HEAD_DIM:(h + 1) * HEAD_DIM], _NT, preferred_element_type=_F32)
        s = jnp.where(valid, s, NEG)
        e = jnp.exp(s - jnp.max(s, axis=0, keepdims=True))
        p = e * (any_valid / jnp.sum(e, axis=0, keepdims=True))
        p_sum = p_sum + p
        outs.append(lax.dot_general(vc, p.astype(_BF16), _TN, preferred_element_type=_F32))
    o_ref[0] = _heads_to_rows(jnp.concatenate(outs, axis=1))

    ovt = ovt_ref[...]
    hi = p_sum.astype(_BF16)
    lo = (p_sum - hi.astype(_F32)).astype(_BF16)
    imp = jnp.dot(ovt, hi, preferred_element_type=_F32) + jnp.dot(ovt, lo, preferred_element_type=_F32)
    jb = lax.broadcasted_iota(jnp.int32, (NSA_NSEL, T), 0)
    cur = (qi * T + lax.broadcasted_iota(jnp.int32, (NSA_NSEL, T), 1)) // SEL_BLK
    forced = (jb == 0) | (jb == cur) | (jb == cur - 1)
    score = jnp.where(jb > cur, -1.0, jnp.where(forced, FORCE_SCORE, imp))
    jbf = jb.astype(_F32)
    sel = jnp.zeros((NSA_NSEL, T), _F32)
    for _ in range(min(SEL_TOP, NSA_NSEL)):
        best = jnp.max(score, axis=0, keepdims=True)
        first = jnp.min(jnp.where(score == best, jbf, float(NSA_NSEL)), axis=0, keepdims=True)
        pick = jbf == first
        sel = jnp.where(pick, 1.0, sel)
        score = jnp.where(pick, NEG, score)
    selt_ref[0] = sel.astype(_BF16)


def nsa_select_pallas(q_n, k_cmp, v_cmp):
    B, S, W = q_n.shape
    T = ATT_T
    n_c = (S - CMP_LEN) // CMP_STRIDE + 1
    ovt = np.zeros((NSA_NSEL, NSA_NC), np.float32)
    ovt[:, :n_c] = selection_overlap(n_c, NSA_NSEL).T
    ovt = jnp.asarray(ovt, _BF16)
    c_spec = pl.BlockSpec((1, NSA_NC, HEAD_DIM), lambda b, i: (b, 0, 0))
    return pl.pallas_call(
        _nsa_select_kernel,
        grid=(B, S // T),
        in_specs=[pl.BlockSpec((1, T, W), lambda b, i: (b, i, 0)), c_spec, c_spec,
                  pl.BlockSpec(ovt.shape, lambda b, i: (0, 0))],
        out_specs=[pl.BlockSpec((1, T, W), lambda b, i: (b, i, 0)),
                   pl.BlockSpec((1, NSA_NSEL, T), lambda b, i: (b, 0, i))],
        out_shape=[jax.ShapeDtypeStruct((B, S, W), _F32), jax.ShapeDtypeStruct((B, NSA_NSEL, S), _BF16)],
        compiler_params=pltpu.CompilerParams(dimension_semantics=("parallel", "parallel")),
        name="nsa_select",
    )(q_n, k_cmp, v_cmp, ovt)


def kernel(x, c, positions, w_ada, b_ada, norm1_g, norm2_g, w_in, w_out, rwkv_mu, rwkv_w0, rwkv_w2, rwkv_a0, rwkv_a2, rwkv_g2, rwkv_kk, rwkv_ka, rwkv_rk, rwkv_gn_w, rwkv_gn_b, conv_w, dil_q_g, dil_k_g, nsa_q_g, nsa_kc_g, nsa_ks_g, nsa_kw_g, nsa_pe_k, nsa_pe_v, nsa_wk1, nsa_wk2, nsa_wv1, nsa_wv2, onorm_g, w_router, b_router, w_gu, b_gu, w_down, b_down):
    B, S, D = x.shape
    tabs = rope_tables(positions)
    for l in range(DEPTH):
        mod = matmul(jax.nn.silu(c), w_ada[l], tm=B, tn=512) + b_ada[l]
        sh1, sc1, gt1, sh2, sc2, gt2 = [m[:, None, :] for m in jnp.split(mod, 6, axis=-1)]
        proj = in_proj_pallas(x, norm1_g[l], sc1, sh1, w_in[l])
        y_a = rwkv7_pallas(proj, rwkv_mu[l], rwkv_w0[l], rwkv_w2[l], rwkv_a0[l], rwkv_a2[l], rwkv_g2[l],
                           rwkv_kk[l], rwkv_ka[l], rwkv_rk[l], rwkv_gn_w[l], rwkv_gn_b[l])
        cq, ck, cv_t, q_n, q_r, kc, vc, ks, vs_t, kw, vw_t = attn_prep_pallas(
            proj, tabs, dil_q_g[l], dil_k_g[l], nsa_q_g[l], nsa_ks_g[l], nsa_kw_g[l])
        y_c = dil_attn_pallas(cq, ck, cv_t)
        k_cmp, v_cmp = nsa_compress_pallas(kc, vc, nsa_pe_k[l], nsa_pe_v[l], nsa_wk1[l], nsa_wk2[l], nsa_wv1[l],
                                           nsa_wv2[l], nsa_kc_g[l])
        o_cmp, sel_t = nsa_select_pallas(q_n, k_cmp, v_cmp)
        o_slc, o_swa = nsa_attn_pallas(q_r, ks, vs_t, kw, vw_t, sel_t)
        x, hb, logits = mix_out_pallas(x, y_a, proj, y_c, o_cmp, o_slc, o_swa, conv_w[l], onorm_g[l], w_out[l], gt1,
                                       norm2_g[l], sc2, sh2, w_router[l], b_router[l])
        hb, logits = hb.reshape(MOE_GROUPS, -1, D), logits.reshape(MOE_GROUPS, -1, ROUTER_PAD)
        y = jnp.stack([moe_pallas(hb[g], logits[g], l, w_gu, b_gu, w_down, b_down) for g in range(MOE_GROUPS)])
        x = x + gt2 * y.reshape(B, S, D)
    return x
```

```python
import functools

import numpy as np
import jax
import jax.numpy as jnp
from jax import lax
from jax.experimental import pallas as pl
from jax.experimental.pallas import tpu as pltpu

D_MODEL = 1024
DEPTH = 2

HEAD_DIM = 64
N_MIXERS = 4
GROUP_W = D_MODEL // N_MIXERS
N_HEADS = GROUP_W // HEAD_DIM
MIX_W = N_MIXERS * GROUP_W

RWKV_DECAY_RANK = 64
RWKV_AAA_RANK = 64
RWKV_GATE_RANK = 128
RWKV_DECAY_SCALE = 0.606531
RWKV_GN_EPS = 64e-5

CONV_W = 3

DIL_CFG = ((128, 1), (512, 4), (2048, 16))

CMP_LEN = 32
CMP_STRIDE = 16
CMP_HIDDEN = 256
SEL_BLK = 64
SEL_TOP = 16
SWA_WIN = 512
FORCE_SCORE = 1e4

N_EXPERTS = 32
TOP_K = 4
EXPERT_FF = D_MODEL
SWIGLU_LIMIT = 7.0
SWIGLU_ALPHA = 1.702

ROPE_THETA = 10000.0
NORM_EPS = 1e-6
NEG = -1e30

A_WIDTHS = (GROUP_W, GROUP_W, GROUP_W, RWKV_DECAY_RANK, RWKV_AAA_RANK, RWKV_GATE_RANK)
B_WIDTHS = (GROUP_W, GROUP_W, GROUP_W)
C_WIDTHS = (GROUP_W, GROUP_W, GROUP_W)
D_WIDTHS = (GROUP_W, HEAD_DIM, HEAD_DIM, HEAD_DIM, HEAD_DIM, HEAD_DIM, HEAD_DIM, 3 * N_HEADS)
SECTION_WIDTHS = (sum(A_WIDTHS), sum(B_WIDTHS), sum(C_WIDTHS), sum(D_WIDTHS))
PROJ_W = sum(SECTION_WIDTHS)

_F32 = jnp.float32
_BF16 = jnp.bfloat16
_NT = (((1,), (1,)), ((), ()))
_TN = (((0,), (0,)), ((), ()))


def _dot_hi(a, b, dims=None):
    if dims is None:
        return jnp.dot(a, b, precision=lax.Precision.HIGHEST, preferred_element_type=_F32)
    return lax.dot_general(a, b, dims, precision=lax.Precision.HIGHEST, preferred_element_type=_F32)


def _dot_lo(a, b, dims=None):
    a, b = a.astype(_BF16), b.astype(_BF16)
    if dims is None:
        return jnp.dot(a, b, preferred_element_type=_F32)
    return lax.dot_general(a, b, dims, preferred_element_type=_F32)


def _bf16_parts(a, parts):
    out = []
    for _ in range(parts):
        p = a.astype(_BF16)
        out.append(p)
        a = a - p.astype(_F32)
    return out


def _dot_split(a, m, parts=2):
    mb = m.astype(_BF16)
    return sum(jnp.dot(p, mb, preferred_element_type=_F32) for p in _bf16_parts(a, parts))


def _dot_split_left(m, a, parts=2):
    mb = m.astype(_BF16)
    return sum(jnp.dot(mb, p, preferred_element_type=_F32) for p in _bf16_parts(a, parts))


def _dot_x3(a, b):
    ah, al = _bf16_parts(a, 2)
    bh, bl = _bf16_parts(b, 2)
    d = lambda x, y: jnp.dot(x, y, preferred_element_type=_F32)
    return d(ah, bh) + (d(ah, bl) + d(al, bh))


def _matmul_kernel(x_ref, w_ref, o_ref, *, exact):
    if exact:
        o_ref[...] = _dot_hi(x_ref[...], w_ref[...])
    else:
        o_ref[...] = jnp.dot(x_ref[...].astype(_BF16), w_ref[...], preferred_element_type=_F32)


def matmul(x, w, tm=512, tn=256, exact=False):
    M, K = x.shape
    N = w.shape[1]
    n_pad = -(-N // tn) * tn
    wb = jnp.pad(w if exact else w.astype(_BF16), ((0, 0), (0, n_pad - N)))
    out = pl.pallas_call(
        functools.partial(_matmul_kernel, exact=exact),
        grid=(M // tm, n_pad // tn),
        in_specs=[pl.BlockSpec((tm, K), lambda i, j: (i, 0)),
                  pl.BlockSpec((K, tn), lambda i, j: (0, j))],
        out_specs=pl.BlockSpec((tm, tn), lambda i, j: (i, j)),
        out_shape=jax.ShapeDtypeStruct((M, n_pad), _F32),
        name="matmul",
    )(x, wb)
    return out[:, :N]


PROJ_PAD = -(-PROJ_W // 256) * 256
PROJ_TM = 256
PROJ_TN = 512
ROUTER_PAD = 128
GATE_COL_BLOCK = (PROJ_W - 3 * N_HEADS) // 128
assert GATE_COL_BLOCK * 128 == PROJ_W - 3 * N_HEADS


def _norm_mod(x, g, sc, sh):
    y = x * lax.rsqrt(jnp.mean(x * x, axis=-1, keepdims=True) + NORM_EPS) * g
    return y * (1.0 + sc) + sh


def _in_proj_kernel(x_ref, g_ref, sc_ref, sh_ref, w_ref, o_ref):
    h = _norm_mod(x_ref[0], g_ref[...], sc_ref[0], sh_ref[0]).astype(_BF16)
    for n0 in range(0, PROJ_PAD, PROJ_TN):
        o_ref[0, :, n0:n0 + PROJ_TN] = jnp.dot(h, w_ref[:, n0:n0 + PROJ_TN], preferred_element_type=_F32)


def in_proj_pallas(x, g, sc, sh, w_in):
    B, S, D = x.shape
    tm = PROJ_TM
    w = jnp.pad(w_in.astype(_BF16), ((0, 0), (0, PROJ_PAD - PROJ_W)))
    per_batch = pl.BlockSpec((1, 1, D), lambda b, i: (b, 0, 0))
    return pl.pallas_call(
        _in_proj_kernel,
        grid=(B, S // tm),
        in_specs=[pl.BlockSpec((1, tm, D), lambda b, i: (b, i, 0)),
                  pl.BlockSpec((1, D), lambda b, i: (0, 0)), per_batch, per_batch,
                  pl.BlockSpec((D, PROJ_PAD), lambda b, i: (0, 0))],
        out_specs=pl.BlockSpec((1, tm, PROJ_PAD), lambda b, i: (b, i, 0)),
        out_shape=jax.ShapeDtypeStruct((B, S, PROJ_PAD), _F32),
        compiler_params=pltpu.CompilerParams(dimension_semantics=("parallel", "parallel"),
                                             vmem_limit_bytes=48 * 1024 * 1024),
        name="in_proj",
    )(x, g.reshape(1, D), sc, sh, w)


def _mix_out_kernel(x_ref, ya_ref, bg_ref, cg_ref, xin_ref, cgh_ref, xinh_ref, yc_ref, ocmp_ref, oslc_ref, oswa_ref,
                    gl_ref, convw_ref, ong_ref, wout_ref, gt1_ref, g2_ref, sc2_ref, sh2_ref, wr_ref, br_ref,
                    xo_ref, hb_ref, lg_ref):
    i = pl.program_id(1)
    G = GROUP_W
    u = cg_ref[0] * xin_ref[0]
    halo = jnp.where(i == 0, 0.0, cgh_ref[0] * xinh_ref[0])
    row = lax.broadcasted_iota(jnp.int32, u.shape, 0)
    u1 = jnp.where(row == 0, halo[7:8], pltpu.roll(u, 1, axis=0))
    u2 = jnp.where(row == 0, halo[6:7], jnp.where(row == 1, halo[7:8], pltpu.roll(u, 2, axis=0)))
    cw = convw_ref[...]
    yb = bg_ref[0] * (cw[0:1] * u2 + cw[1:2] * u1 + cw[2:3] * u)

    sg = jax.nn.sigmoid(gl_ref[0])
    er = lax.broadcasted_iota(jnp.int32, (128, G), 0)
    ec = lax.broadcasted_iota(jnp.int32, (128, G), 1) // HEAD_DIM
    yd = None
    for j, o_ref in enumerate((ocmp_ref, oslc_ref, oswa_ref)):
        gate = _dot_split(sg, er == 3 * ec + j)
        yd = gate * o_ref[0] if yd is None else yd + gate * o_ref[0]

    head_sum = _group_indicator(G, HEAD_DIM)
    ong = ong_ref[...]
    parts = [ya_ref[0]]
    for n, y in enumerate((yb, yc_ref[0], yd)):
        ms = _dot_split(y * y, head_sum) * (1.0 / HEAD_DIM)
        parts.append(y * lax.rsqrt(ms + NORM_EPS) * ong[:, n * G:(n + 1) * G])
    mixed = jnp.dot(jnp.concatenate(parts, axis=1).astype(_BF16), wout_ref[...], preferred_element_type=_F32)
    x = x_ref[0] + gt1_ref[0] * mixed
    xo_ref[0] = x
    h = _norm_mod(x, g2_ref[...], sc2_ref[0], sh2_ref[0])
    hb_ref[0] = h.astype(_BF16)
    lg_ref[0] = _dot_hi(h, wr_ref[...]) + br_ref[...]


def mix_out_pallas(x, y_a, proj, y_c, o_cmp, o_slc, o_swa, conv_w, onorm_g, w_out, gt1, g2, sc2, sh2, w_router, b_router):
    B, S, D = x.shape
    tm = PROJ_TM
    G = GROUP_W
    b0 = SECTION_WIDTHS[0] // G
    tile = lambda w, col: pl.BlockSpec((1, tm, w), lambda b, i: (b, i, col))
    halo = lambda col: pl.BlockSpec((1, 8, G), lambda b, i: (b, jnp.maximum(i * (tm // 8) - 1, 0), col))
    const = lambda t: pl.BlockSpec(t.shape, lambda b, i: (0,) * t.ndim)
    per_batch = pl.BlockSpec((1, 1, D), lambda b, i: (b, 0, 0))
    wr = jnp.pad(w_router, ((0, 0), (0, ROUTER_PAD - N_EXPERTS)))
    br = jnp.pad(b_router, (0, ROUTER_PAD - N_EXPERTS)).reshape(1, ROUTER_PAD)
    consts = [conv_w, onorm_g.reshape(1, 3 * G), w_out.astype(_BF16)]
    return pl.pallas_call(
        _mix_out_kernel,
        grid=(B, S // tm),
        in_specs=[tile(D, 0), tile(G, 0), tile(G, b0), tile(G, b0 + 1), tile(G, b0 + 2), halo(b0 + 1), halo(b0 + 2),
                  tile(G, 0), tile(G, 0), tile(G, 0), tile(G, 0), tile(128, GATE_COL_BLOCK)]
                 + [const(t) for t in consts] + [per_batch, pl.BlockSpec((1, D), lambda b, i: (0, 0)), per_batch,
                                                 per_batch, const(wr), const(br)],
        out_specs=[tile(D, 0), tile(D, 0), tile(ROUTER_PAD, 0)],
        out_shape=[jax.ShapeDtypeStruct((B, S, D), _F32), jax.ShapeDtypeStruct((B, S, D), _BF16),
                   jax.ShapeDtypeStruct((B, S, ROUTER_PAD), _F32)],
        compiler_params=pltpu.CompilerParams(dimension_semantics=("parallel", "parallel")),
        name="mix_out",
    )(x, y_a, proj, proj, proj, proj, proj, y_c, o_cmp, o_slc, o_swa, proj, *consts, gt1, g2.reshape(1, D), sc2, sh2,
      wr, br)


RWKV_CHUNK = 64
RWKV_TILE = 256


def _group_indicator(n, group):
    r = lax.broadcasted_iota(jnp.int32, (n, n), 0) // group
    c = lax.broadcasted_iota(jnp.int32, (n, n), 1) // group
    return r == c


def _rwkv_kernel(za_ref, mu_ref, w0_ref, w2_ref, a0_ref, a2_ref, g2_ref, kk_ref, ka_ref, rk_ref,
                 gnw_ref, gnb_ref, o_ref, prev_sc, h_sc):
    C = RWKV_CHUNK
    TT = RWKV_TILE
    c = pl.program_id(1)

    @pl.when(c == 0)
    def _():
        prev_sc[...] = jnp.zeros_like(prev_sc)
        h_sc[...] = jnp.zeros_like(h_sc)

    z = za_ref[0]
    row = lax.broadcasted_iota(jnp.int32, z.shape, 0)
    zs = jnp.where(row == 0, prev_sc[...], pltpu.roll(z, 1, axis=0))
    prev_sc[...] = z[TT - 1:TT, :]
    z = z + (zs - z) * mu_ref[...]

    G = GROUP_W
    r, k, v = z[:, 0:G], z[:, G:2 * G], z[:, 2 * G:3 * G]
    o = 3 * G
    wd = z[:, o:o + RWKV_DECAY_RANK]
    ad = z[:, o + RWKV_DECAY_RANK:o + RWKV_DECAY_RANK + RWKV_AAA_RANK]
    gd = z[:, o + RWKV_DECAY_RANK + RWKV_AAA_RANK:]

    lw = -RWKV_DECAY_SCALE * jax.nn.sigmoid(w0_ref[...] + _dot_lo(jnp.tanh(wd), w2_ref[...]))
    a = jax.nn.sigmoid(a0_ref[...] + _dot_lo(ad, a2_ref[...]))
    g = _dot_lo(jax.nn.sigmoid(gd), g2_ref[...])

    head_sum = _group_indicator(G, HEAD_DIM)
    kk = k * kk_ref[...]
    kk = kk * lax.rsqrt(_dot_split(kk * kk, head_sum) + 1e-12)
    k = k * (1.0 + (a - 1.0) * ka_ref[...])
    b = kk * a

    ti = lax.broadcasted_iota(jnp.int32, (C, C), 0)
    tj = lax.broadcasted_iota(jnp.int32, (C, C), 1)
    incl = ti >= tj
    strict = ti > tj
    eye = ti == tj
    blk16 = (ti // 16) == (tj // 16)
    blk32 = (ti // 32) == (tj // 32)
    eye_f = eye.astype(_F32)

    ri = lax.broadcasted_iota(jnp.int32, (TT, TT), 0)
    rj = lax.broadcasted_iota(jnp.int32, (TT, TT), 1)
    chunk_tri = (ri >= rj) & (ri // C == rj // C)
    cum = _dot_split_left(chunk_tri, lw, parts=3)
    g_in = jnp.exp(cum)
    A_all = -kk * jnp.exp(cum - lw)
    R_all = r * g_in
    g_inv = jnp.exp(-cum)
    B_all = b * g_inv
    K_all = k * g_inv

    units = [(ci, h) for ci in range(TT // C) for h in range(N_HEADS)]

    def part(t, u):
        ci, h = u
        return t[ci * C:(ci + 1) * C, h * HEAD_DIM:(h + 1) * HEAD_DIM]

    def g_end(u):
        ci, h = u
        return g_in[(ci + 1) * C - 1:(ci + 1) * C, h * HEAD_DIM:(h + 1) * HEAD_DIM]

    A = [part(A_all, u) for u in units]
    R = [part(R_all, u) for u in units]
    B = [part(B_all, u) for u in units]
    Kt = [part(K_all, u) for u in units]
    V = [part(v, u) for u in units]
    n = range(len(units))
    gram = [_dot_lo(jnp.concatenate([A[i], R[i]], axis=0), jnp.concatenate([B[i], Kt[i]], axis=0), _NT) for i in n]
    l_ab = [jnp.where(strict, gram[i][0:C, 0:C], 0.0) for i in n]
    l_ak = [jnp.where(strict, gram[i][0:C, C:2 * C], 0.0) for i in n]
    m_rb = [jnp.where(incl, gram[i][C:2 * C, 0:C], 0.0) for i in n]
    m_rk = [jnp.where(incl, gram[i][C:2 * C, C:2 * C], 0.0) for i in n]
    p = [jnp.where(blk16, l_ab[i], 0.0) for i in n]
    x = [eye_f + p[i] for i in n]
    for _ in range(3):
        p = [_dot_lo(p[i], p[i]) for i in n]
        x = [_dot_lo(x[i], eye_f + p[i]) for i in n]
    for lvl in (blk32 & ~blk16, ~blk32):
        xl = [_dot_lo(x[i], jnp.where(lvl, l_ab[i], 0.0)) for i in n]
        x = [x[i] + _dot_lo(xl[i], x[i]) for i in n]
    lv = [_dot_lo(l_ak[i], V[i]) for i in n]
    tap = [_dot_lo(x[i], jnp.concatenate([A[i], lv[i]], axis=1)) for i in n]
    m1 = [_dot_lo(m_rb[i], tap[i]) for i in n]
    mv = [_dot_lo(m_rk[i], V[i]) for i in n]
    bt = [_dot_lo(B[i] * g_end(units[i]), tap[i], _TN) for i in n]
    kv = [_dot_lo(Kt[i] * g_end(units[i]), V[i], _TN) for i in n]
    w_yh = [jnp.concatenate([R[i] + m1[i][:, 0:HEAD_DIM],
                             jnp.where(eye, g_end(units[i]), 0.0) + bt[i][:, 0:HEAD_DIM]], axis=0) for i in n]
    y0 = [m1[i][:, HEAD_DIM:] + mv[i] for i in n]
    h_add = [bt[i][:, HEAD_DIM:] + kv[i] for i in n]
    state = [h_sc[h] for h in range(N_HEADS)]
    y_rows = []
    for ci in range(TT // C):
        ys = []
        for h in range(N_HEADS):
            i = ci * N_HEADS + h
            nxt = _dot_x3(w_yh[i], state[h])
            ys.append(nxt[0:C] + y0[i])
            state[h] = nxt[C:2 * C] + h_add[i]
        y_rows.append(jnp.concatenate(ys, axis=1))
    for h in range(N_HEADS):
        h_sc[h] = state[h]
    y = jnp.concatenate(y_rows, axis=0)

    mu = _dot_split(y, head_sum) * (1.0 / HEAD_DIM)
    d = y - mu
    var = _dot_split(d * d, head_sum) * (1.0 / HEAD_DIM)
    y = d * lax.rsqrt(var + RWKV_GN_EPS) * gnw_ref[...] + gnb_ref[...]
    y = y + _dot_split(r * k * rk_ref[...], head_sum) * v
    o_ref[0] = y * g


def rwkv7_pallas(za, mu, w0, w2, a0, a2, g2, k_k, k_a, r_k, gn_w, gn_b):
    B, S, _ = za.shape
    W = SECTION_WIDTHS[0]
    TT = RWKV_TILE
    row = lambda t: t.reshape(1, -1).astype(_F32)
    full = lambda t: pl.BlockSpec(t.shape, lambda b, c: (0,) * t.ndim)
    params = [row(mu), row(w0), w2.astype(_BF16), row(a0), a2.astype(_BF16), g2.astype(_BF16),
              row(k_k), row(k_a), row(r_k), row(gn_w), row(gn_b)]
    return pl.pallas_call(
        _rwkv_kernel,
        grid=(B, S // TT),
        in_specs=[pl.BlockSpec((1, TT, W), lambda b, c: (b, c, 0))] + [full(p) for p in params],
        out_specs=pl.BlockSpec((1, TT, GROUP_W), lambda b, c: (b, c, 0)),
        out_shape=jax.ShapeDtypeStruct((B, S, GROUP_W), _F32),
        scratch_shapes=[pltpu.VMEM((1, W), _F32), pltpu.VMEM((N_HEADS, HEAD_DIM, HEAD_DIM), _F32)],
        compiler_params=pltpu.CompilerParams(dimension_semantics=("parallel", "arbitrary")),
        name="rwkv7",
    )(za, *params)


MOE_TM = 512
MOE_FC = 512
MOE_CAST_ROWS = 128
MOE_GROUPS = 2


def _moe_ffn_kernel(blk_e_ref, n_used_ref, x_ref, wgu_ref, bgu_ref, wd_ref, bd_ref, o_ref, wgu_sc, wd_sc):
    i = pl.program_id(0)
    F = EXPERT_FF
    new_expert = jnp.logical_or(i == 0, blk_e_ref[i] != blk_e_ref[jnp.maximum(i - 1, 0)])

    @pl.when(new_expert)
    def _():
        def cast(j, carry):
            rows = pl.ds(pl.multiple_of(j * MOE_CAST_ROWS, MOE_CAST_ROWS), MOE_CAST_ROWS)
            wgu_sc[rows, :] = wgu_ref[0, rows, :].astype(_BF16)
            wd_sc[rows, :] = wd_ref[0, rows, :].astype(_BF16)
            return carry
        lax.fori_loop(0, F // MOE_CAST_ROWS, cast, 0)

    @pl.when(i < n_used_ref[0])
    def _():
        x = x_ref[...]
        acc = None
        for c in range(F // MOE_FC):
            lo = c * MOE_FC
            gate = jnp.dot(x, wgu_sc[:, lo:lo + MOE_FC], preferred_element_type=_F32) + bgu_ref[0, :, lo:lo + MOE_FC]
            up = jnp.dot(x, wgu_sc[:, F + lo:F + lo + MOE_FC], preferred_element_type=_F32) + bgu_ref[0, :, F + lo:F + lo + MOE_FC]
            gate = jnp.minimum(gate, SWIGLU_LIMIT)
            up = jnp.clip(up, -SWIGLU_LIMIT, SWIGLU_LIMIT)
            act = gate * jax.nn.sigmoid(SWIGLU_ALPHA * gate) * (up + 1.0)
            part = jnp.dot(act.astype(_BF16), wd_sc[lo:lo + MOE_FC, :], preferred_element_type=_F32)
            acc = part if acc is None else acc + part
        o_ref[...] = (acc + bd_ref[0]).astype(o_ref.dtype)

    @pl.when(i >= n_used_ref[0])
    def _():
        o_ref[...] = jnp.zeros_like(o_ref)


def moe_ffn(xs, blk_e, n_used, layer, w_gu, b_gu, w_down, b_down):
    n_rows, D = xs.shape
    L, E, _, F2 = w_gu.shape
    assert D == EXPERT_FF and F2 == 2 * EXPERT_FF
    n_blk = n_rows // MOE_TM
    grid_spec = pltpu.PrefetchScalarGridSpec(
        num_scalar_prefetch=2,
        grid=(n_blk,),
        in_specs=[pl.BlockSpec((MOE_TM, D), lambda i, e, n: (i, 0)),
                  pl.BlockSpec((None, 1, D, F2), lambda i, e, n: (layer, e[i], 0, 0)),
                  pl.BlockSpec((None, 1, 1, F2), lambda i, e, n: (layer, e[i], 0, 0)),
                  pl.BlockSpec((None, 1, F2 // 2, D), lambda i, e, n: (layer, e[i], 0, 0)),
                  pl.BlockSpec((None, 1, 1, D), lambda i, e, n: (layer, e[i], 0, 0))],
        out_specs=pl.BlockSpec((MOE_TM, D), lambda i, e, n: (i, 0)),
        scratch_shapes=[pltpu.VMEM((D, F2), _BF16), pltpu.VMEM((F2 // 2, D), _BF16)],
    )
    return pl.pallas_call(
        _moe_ffn_kernel,
        grid_spec=grid_spec,
        out_shape=jax.ShapeDtypeStruct((n_rows, D), _BF16),
        compiler_params=pltpu.CompilerParams(dimension_semantics=("arbitrary",),
                                             vmem_limit_bytes=52 * 1024 * 1024),
        name="moe_ffn",
    )(blk_e, n_used, xs, w_gu, b_gu.reshape(L, E, 1, F2), w_down, b_down.reshape(L, E, 1, D))


def moe_pallas(hb, logits, layer, w_gu, b_gu, w_down, b_down):
    T, D = hb.shape
    TK = T * TOP_K
    TM = MOE_TM
    E = N_EXPERTS
    i32 = jnp.int32
    top_val, top_idx = lax.top_k(logits[:, :E], TOP_K)
    gates = jax.nn.softmax(top_val, axis=-1)
    e_flat = top_idx.T.reshape(TK).astype(i32)
    counts = jnp.sum((e_flat[:, None] == jnp.arange(E, dtype=i32)[None, :]).astype(i32), axis=0)
    need = jnp.repeat((-counts) % TM, TM)
    d_idx = jnp.arange(E * TM, dtype=i32)
    d_key = jnp.where(d_idx % TM < need, d_idx // TM, E)
    n_rows = TK + E * TM
    keys = jnp.concatenate([e_flat, d_key])
    toks = jnp.concatenate([jnp.arange(TK, dtype=i32) % T, jnp.zeros((E * TM,), i32)])
    rows = jnp.arange(n_rows, dtype=i32)
    s_keys, src_tok, s_slot = lax.sort((keys, toks, rows), num_keys=1)
    _, row_of_slot = lax.sort((s_slot, rows), num_keys=1)
    row_of = row_of_slot[:TK]
    blk_e = s_keys[::TM]
    n_used = jnp.sum((blk_e < E).astype(i32)).reshape(1)
    blk_e = jnp.where(blk_e < E, blk_e, blk_e[jnp.maximum(n_used[0] - 1, 0)])
    xs = jnp.take(hb, src_tok, axis=0)
    ybuf = moe_ffn(xs, blk_e, n_used, layer, w_gu, b_gu, w_down, b_down)
    y = lax.optimization_barrier(jnp.take(ybuf, row_of, axis=0))
    y = y.reshape(TOP_K, T, D).astype(_F32) * gates.T[:, :, None]
    return jnp.sum(y, axis=0)


ATT_T = 128
ATT_TK = 512
ATT_R = ATT_TK // ATT_T
NSA_NSEL = 64


def _window_table(window, weight_of_distance):
    T, TK = ATT_T, ATT_TK
    n = window // T + ATT_R + 1
    d = (np.arange(n)[:, None, None] * T + np.arange(T)[None, None, :] - np.arange(TK)[None, :, None])
    return jnp.asarray(weight_of_distance(d).astype(np.float32))


def _first_key_tile(qi, window):
    return jnp.maximum(qi * ATT_T - window, 0) // ATT_TK


def _swa_weight(d):
    return (d >= 0) & (d <= SWA_WIN - 1)


def _dil_weight(d):
    return sum(((d >= 0) & (d <= window) & (d % dil == 0)).astype(np.int32) for window, dil in DIL_CFG)


def _softmax_tile(s, w, m_ref, l_ref, weighted):
    s = jnp.where(w > 0.0, s, NEG)
    m_prev = m_ref[...]
    m_new = jnp.maximum(m_prev, jnp.max(s, axis=0, keepdims=True))
    alpha = jnp.exp(m_prev - m_new)
    p = jnp.exp(s - m_new)
    if weighted:
        p = p * w
    l_ref[...] = alpha * l_ref[...] + jnp.sum(p, axis=0, keepdims=True)
    m_ref[...] = m_new
    return alpha, p.astype(_BF16)


def _init_stats(m_sc, l_sc, acc_sc):
    m_sc[...] = jnp.full_like(m_sc, NEG)
    l_sc[...] = jnp.zeros_like(l_sc)
    acc_sc[...] = jnp.zeros_like(acc_sc)


def _heads_to_rows(ot):
    T = ATT_T
    pairs = []
    for h in range(0, N_HEADS, 2):
        two = jnp.concatenate([ot[:, h * T:(h + 1) * T], ot[:, (h + 1) * T:(h + 2) * T]], axis=0)
        pairs.append(two.T)
    return jnp.concatenate(pairs, axis=1)


def _nsa_attn_kernel(q_ref, ks_ref, vst_ref, kw_ref, vwt_ref, selt_ref, swa_ref, oslc_ref, oswa_ref,
                     m_sc, l_sc, acc_sc):
    T, TK = ATT_T, ATT_TK
    qi = pl.program_id(1)
    q = q_ref[0]
    q = jnp.concatenate([q[:, h * HEAD_DIM:(h + 1) * HEAD_DIM] for h in range(N_HEADS)], axis=0)
    selt = selt_ref[0]
    key_blk = lax.broadcasted_iota(jnp.int32, (TK, NSA_NSEL), 0) // SEL_BLK
    blk_id = lax.broadcasted_iota(jnp.int32, (TK, NSA_NSEL), 1)
    kr = lax.broadcasted_iota(jnp.int32, (TK, T), 0)
    qc = lax.broadcasted_iota(jnp.int32, (TK, T), 1)

    def step(k_ref, vt_ref, kt, w):
        base = pl.multiple_of(kt * TK, TK)
        s = lax.dot_general(k_ref[0, pl.ds(base, TK), :], q, _NT, preferred_element_type=_F32)
        alpha, p = _softmax_tile(s, jnp.concatenate([w] * N_HEADS, axis=1), m_sc, l_sc, False)
        acc_sc[...] = alpha * acc_sc[...] + jnp.dot(vt_ref[0, :, pl.ds(base, TK)], p, preferred_element_type=_F32)

    def finish(o_ref):
        o_ref[0] = _heads_to_rows(acc_sc[...] / l_sc[...])

    last = qi // ATT_R
    _init_stats(m_sc, l_sc, acc_sc)

    def slc_step(kt, carry):
        in_blk = (key_blk + kt * (TK // SEL_BLK) == blk_id).astype(_BF16)
        w = jnp.dot(in_blk, selt, preferred_element_type=_F32)
        w = jnp.where(kt * TK + kr <= qi * T + qc, w, 0.0)
        step(ks_ref, vst_ref, kt, w)
        return carry

    lax.fori_loop(0, last + 1, slc_step, 0)
    finish(oslc_ref)

    _init_stats(m_sc, l_sc, acc_sc)

    def swa_step(kt, carry):
        step(kw_ref, vwt_ref, kt, swa_ref[qi - kt * ATT_R])
        return carry

    lax.fori_loop(_first_key_tile(qi, SWA_WIN - 1), last + 1, swa_step, 0)
    finish(oswa_ref)


def nsa_attn_pallas(q_r, ks_r, vs_t, kw_r, vw_t, sel_t):
    B, S, W = q_r.shape
    assert S // SEL_BLK == NSA_NSEL
    T = ATT_T
    R = N_HEADS * T
    swa_tab = _window_table(SWA_WIN - 1, _swa_weight)
    k_spec = pl.BlockSpec((1, S, HEAD_DIM), lambda b, i: (b, 0, 0))
    vt_spec = pl.BlockSpec((1, HEAD_DIM, S), lambda b, i: (b, 0, 0))
    q_spec = pl.BlockSpec((1, T, W), lambda b, i: (b, i, 0))
    return pl.pallas_call(
        _nsa_attn_kernel,
        grid=(B, S // T),
        in_specs=[q_spec, k_spec, vt_spec, k_spec, vt_spec,
                  pl.BlockSpec((1, NSA_NSEL, T), lambda b, i: (b, 0, i)),
                  pl.BlockSpec(swa_tab.shape, lambda b, i: (0, 0, 0))],
        out_specs=[q_spec, q_spec],
        out_shape=[jax.ShapeDtypeStruct((B, S, W), _F32)] * 2,
        scratch_shapes=[pltpu.VMEM((1, R), _F32), pltpu.VMEM((1, R), _F32), pltpu.VMEM((HEAD_DIM, R), _F32)],
        compiler_params=pltpu.CompilerParams(dimension_semantics=("parallel", "arbitrary")),
        name="nsa_attn",
    )(q_r, ks_r, vs_t, kw_r, vw_t, sel_t, swa_tab)


def _dil_attn_kernel(q_ref, k_ref, vt_ref, tab_ref, o_ref, m_sc, l_sc, acc_sc):
    T, TK = ATT_T, ATT_TK
    qi = pl.program_id(1)
    _init_stats(m_sc, l_sc, acc_sc)
    qs = [q_ref[0, :, h * HEAD_DIM:(h + 1) * HEAD_DIM] for h in range(N_HEADS)]

    def step(kt, carry):
        base = pl.multiple_of(kt * TK, TK)
        w = tab_ref[qi - kt * ATT_R]
        s = jnp.concatenate([lax.dot_general(k_ref[0, h, pl.ds(base, TK), :], qs[h], _NT, preferred_element_type=_F32)
                             for h in range(N_HEADS)], axis=1)
        alpha, p = _softmax_tile(s, jnp.concatenate([w] * N_HEADS, axis=1), m_sc, l_sc, True)
        pv = jnp.concatenate([jnp.dot(vt_ref[0, h, :, pl.ds(base, TK)], p[:, h * T:(h + 1) * T],
                                      preferred_element_type=_F32) for h in range(N_HEADS)], axis=1)
        acc_sc[...] = alpha * acc_sc[...] + pv
        return carry

    lax.fori_loop(_first_key_tile(qi, DIL_CFG[-1][0]), qi // ATT_R + 1, step, 0)
    o_ref[0] = _heads_to_rows(acc_sc[...] / l_sc[...])


def dil_attn_pallas(q_r, k_r, v_t):
    B, H, S, hd = k_r.shape
    T = ATT_T
    R = H * T
    W = H * hd
    tab = _window_table(DIL_CFG[-1][0], _dil_weight)
    q_spec = pl.BlockSpec((1, T, W), lambda b, i: (b, i, 0))
    return pl.pallas_call(
        _dil_attn_kernel,
        grid=(B, S // T),
        in_specs=[q_spec,
                  pl.BlockSpec((1, H, S, hd), lambda b, i: (b, 0, 0, 0)),
                  pl.BlockSpec((1, H, hd, S), lambda b, i: (b, 0, 0, 0)),
                  pl.BlockSpec(tab.shape, lambda b, i: (0, 0, 0))],
        out_specs=q_spec,
        out_shape=jax.ShapeDtypeStruct((B, S, W), _F32),
        scratch_shapes=[pltpu.VMEM((1, R), _F32), pltpu.VMEM((1, R), _F32), pltpu.VMEM((hd, R), _F32)],
        compiler_params=pltpu.CompilerParams(dimension_semantics=("parallel", "arbitrary")),
        name="dil_attn",
    )(q_r, k_r, v_t, tab)


ROPE_LANES = 128
PREP_TM = 256


def _rope_table_kernel(pos_ref, inv_ref, cos_ref, slo_ref, shi_ref):
    ang = pos_ref[...].astype(_F32) * inv_ref[...]
    cos, sin = jnp.cos(ang), jnp.sin(ang)
    first_half = (lax.broadcasted_iota(jnp.int32, ang.shape, 1) % HEAD_DIM) < HEAD_DIM // 2
    cos_ref[...] = cos
    slo_ref[...] = jnp.where(first_half, -sin, 0.0)
    shi_ref[...] = jnp.where(first_half, 0.0, sin)


def rope_tables(positions):
    B, S = positions.shape
    T = B * S
    half = HEAD_DIM // 2
    inv = ROPE_THETA ** (-jnp.arange(half, dtype=_F32) / half)
    inv = jnp.tile(inv, ROPE_LANES // half).reshape(1, ROPE_LANES)
    pos = jnp.broadcast_to(positions.reshape(T, 1), (T, ROPE_LANES))
    tm = 1024
    spec = pl.BlockSpec((tm, ROPE_LANES), lambda i: (i, 0))
    return pl.pallas_call(
        _rope_table_kernel,
        grid=(T // tm,),
        in_specs=[spec, pl.BlockSpec((1, ROPE_LANES), lambda i: (0, 0))],
        out_specs=[spec] * 3,
        out_shape=[jax.ShapeDtypeStruct((T, ROPE_LANES), _F32)] * 3,
        name="rope_tables",
    )(pos, inv)


def _attn_prep_kernel(cq_ref, ck_ref, cv_ref, dq_ref, d1_ref, d2_ref, cos_ref, slo_ref, shi_ref, gains_ref,
                      cqo_ref, cko_ref, cvo_ref, qn_ref, qr_ref, kc_ref, vc_ref, ks_ref, vst_ref, kw_ref, vwt_ref):
    G, hd = GROUP_W, HEAD_DIM
    wide = lambda t: jnp.concatenate([t] * (G // ROPE_LANES), axis=1)
    cos, slo, shi = wide(cos_ref[...]), wide(slo_ref[...]), wide(shi_ref[...])
    head_sum = _group_indicator(G, hd)
    gains = gains_ref[...]
    scale = hd ** -0.5

    def norm(t, n):
        return t * lax.rsqrt(_dot_split(t * t, head_sum) * (1.0 / hd) + NORM_EPS) * gains[n:n + 1]

    def rope(t):
        return t * cos + pltpu.roll(t, G - hd // 2, axis=1) * slo + pltpu.roll(t, hd // 2, axis=1) * shi

    cqo_ref[0] = (rope(norm(cq_ref[0], 0)) * scale).astype(_BF16)
    ck = rope(norm(ck_ref[0], 1)).astype(_BF16)
    for h in range(N_HEADS):
        cko_ref[0, h] = ck[:, h * hd:(h + 1) * hd]
    cvo_ref[0] = cv_ref[0].T.astype(_BF16).reshape(N_HEADS, hd, cv_ref.shape[1])
    qn = norm(dq_ref[0], 2)
    qn_ref[0] = (qn * scale).astype(_BF16)
    qr_ref[0] = (rope(qn) * scale).astype(_BF16)
    d1 = d1_ref[0]
    d2 = d2_ref[0]
    kc_ref[0] = d1[:, 0:hd]
    vc_ref[0] = d1[:, hd:2 * hd]
    ks_ref[0] = rope(norm(d1, 3))[:, 2 * hd:3 * hd].astype(_BF16)
    kw_ref[0] = rope(norm(d2, 4))[:, 0:hd].astype(_BF16)
    vst_ref[0] = d1.T[3 * hd:4 * hd, :].astype(_BF16)
    vwt_ref[0] = d2.T[hd:2 * hd, :].astype(_BF16)


def attn_prep_pallas(proj, tabs, dil_q_g, dil_k_g, nsa_q_g, nsa_ks_g, nsa_kw_g):
    B, S, _ = proj.shape
    tm = PREP_TM
    G, hd, H = GROUP_W, HEAD_DIM, N_HEADS
    c0 = (SECTION_WIDTHS[0] + SECTION_WIDTHS[1]) // G
    z = jnp.zeros((hd,), _F32)
    gains = jnp.stack([jnp.tile(dil_q_g, H), jnp.tile(dil_k_g, H), jnp.tile(nsa_q_g, H),
                       jnp.concatenate([z, z, nsa_ks_g, z]), jnp.concatenate([nsa_kw_g, z, z, z])])
    col = lambda c: pl.BlockSpec((1, tm, G), lambda b, i: (b, i, c))
    nt = S // tm
    tab = pl.BlockSpec((tm, ROPE_LANES), lambda b, i: (b * nt + i, 0))
    tok = lambda w, dt: (pl.BlockSpec((1, tm, w), lambda b, i: (b, i, 0)), jax.ShapeDtypeStruct((B, S, w), dt))
    tr = (pl.BlockSpec((1, hd, tm), lambda b, i: (b, 0, i)), jax.ShapeDtypeStruct((B, hd, S), _BF16))
    outs = [tok(G, _BF16),
            (pl.BlockSpec((1, H, tm, hd), lambda b, i: (b, 0, i, 0)), jax.ShapeDtypeStruct((B, H, S, hd), _BF16)),
            (pl.BlockSpec((1, H, hd, tm), lambda b, i: (b, 0, 0, i)), jax.ShapeDtypeStruct((B, H, hd, S), _BF16)),
            tok(G, _BF16), tok(G, _BF16), tok(hd, _F32), tok(hd, _F32), tok(hd, _BF16), tr, tok(hd, _BF16), tr]
    return pl.pallas_call(
        _attn_prep_kernel,
        grid=(B, nt),
        in_specs=[col(c0), col(c0 + 1), col(c0 + 2), col(c0 + 3), col(c0 + 4), col(c0 + 5), tab, tab, tab,
                  pl.BlockSpec(gains.shape, lambda b, i: (0, 0))],
        out_specs=[o[0] for o in outs],
        out_shape=[o[1] for o in outs],
        compiler_params=pltpu.CompilerParams(dimension_semantics=("parallel", "parallel")),
        name="attn_prep",
    )(proj, proj, proj, proj, proj, proj, *tabs, gains)


NSA_NC = 256
CMP_CHUNK = CMP_STRIDE * HEAD_DIM


def _nsa_compress_kernel(kc_ref, vc_ref, pek_ref, pev_ref, wk1_ref, wk2_ref, wv1_ref, wv2_ref, g_ref, ko_ref, vo_ref):
    def compress(x_ref, pe_ref, w1_ref, w2_ref):
        x = x_ref[0]
        top = _dot_lo(x + pe_ref[0:1], w1_ref[0:CMP_CHUNK, :])
        bot = _dot_lo(x + pe_ref[1:2], w1_ref[CMP_CHUNK:, :])
        hid = top + pltpu.roll(bot, NSA_NC - 1, axis=0)
        return _dot_lo(jax.nn.gelu(hid), w2_ref[...])

    k = compress(kc_ref, pek_ref, wk1_ref, wk2_ref)
    ko_ref[0] = k * lax.rsqrt(jnp.mean(k * k, axis=-1, keepdims=True) + NORM_EPS) * g_ref[...]
    vo_ref[0] = compress(vc_ref, pev_ref, wv1_ref, wv2_ref)


def nsa_compress_pallas(kc, vc, pe_k, pe_v, wk1, wk2, wv1, wv2, kc_g):
    B, S, hd = kc.shape
    assert S // CMP_STRIDE == NSA_NC
    chunks = lambda t: t.reshape(B, NSA_NC, CMP_CHUNK)
    pe2 = lambda p: p.reshape(2, CMP_CHUNK)
    const = lambda t: pl.BlockSpec(t.shape, lambda b: (0,) * t.ndim)
    x_spec = pl.BlockSpec((1, NSA_NC, CMP_CHUNK), lambda b: (b, 0, 0))
    o_spec = pl.BlockSpec((1, NSA_NC, hd), lambda b: (b, 0, 0))
    params = [pe2(pe_k), pe2(pe_v), wk1.astype(_BF16), wk2.astype(_BF16), wv1.astype(_BF16), wv2.astype(_BF16),
              kc_g.reshape(1, hd)]
    return pl.pallas_call(
        _nsa_compress_kernel,
        grid=(B,),
        in_specs=[x_spec, x_spec] + [const(p) for p in params],
        out_specs=[o_spec, o_spec],
        out_shape=[jax.ShapeDtypeStruct((B, NSA_NC, hd), _F32)] * 2,
        name="nsa_compress",
    )(chunks(kc), chunks(vc), *params)


def selection_overlap(n_c, n_sel):
    r = SEL_BLK // CMP_STRIDE
    m = CMP_LEN // CMP_STRIDE
    diff = np.arange(n_c)[:, None] - r * np.arange(n_sel)[None, :]
    offs = (np.arange(r)[:, None] - np.arange(m)[None, :]).reshape(-1)
    return (diff[..., None] == offs).sum(-1).astype(np.float32)


def _nsa_select_kernel(q_ref, kc_ref, vc_ref, ovt_ref, o_ref, selt_ref):
    T = ATT_T
    qi = pl.program_id(1)
    kc = kc_ref[0].astype(_BF16)
    vc = vc_ref[0].astype(_BF16)
    cblk = lax.broadcasted_iota(jnp.int32, (NSA_NC, T), 0)
    tq = qi * T + lax.broadcasted_iota(jnp.int32, (NSA_NC, T), 1)
    valid = cblk * CMP_STRIDE + (CMP_LEN - 1) <= tq
    any_valid = (tq[0:1, :] >= CMP_LEN - 1).astype(_F32)
    p_sum = jnp.zeros((NSA_NC, T), _F32)
    outs = []
    for h in range(N_HEADS):
        s = lax.dot_general(kc, q_ref[0, :, h * HEAD_DIM:(h + 1) * HEAD_DIM], _NT, preferred_element_type=_F32)
        s = jnp.where(valid, s, NEG)
        e = jnp.exp(s - jnp.max(s, axis=0, keepdims=True))
        p = e * (any_valid / jnp.sum(e, axis=0, keepdims=True))
        p_sum = p_sum + p
        outs.append(lax.dot_general(vc, p.astype(_BF16), _TN, preferred_element_type=_F32))
    o_ref[0] = _heads_to_rows(jnp.concatenate(outs, axis=1))

    ovt = ovt_ref[...]
    hi = p_sum.astype(_BF16)
    lo = (p_sum - hi.astype(_F32)).astype(_BF16)
    imp = jnp.dot(ovt, hi, preferred_element_type=_F32) + jnp.dot(ovt, lo, preferred_element_type=_F32)
    jb = lax.broadcasted_iota(jnp.int32, (NSA_NSEL, T), 0)
    cur = (qi * T + lax.broadcasted_iota(jnp.int32, (NSA_NSEL, T), 1)) // SEL_BLK
    forced = (jb == 0) | (jb == cur) | (jb == cur - 1)
    score = jnp.where(jb > cur, -1.0, jnp.where(forced, FORCE_SCORE, imp))
    jbf = jb.astype(_F32)
    sel = jnp.zeros((NSA_NSEL, T), _F32)
    for _ in range(min(SEL_TOP, NSA_NSEL)):
        best = jnp.max(score, axis=0, keepdims=True)
        first = jnp.min(jnp.where(score == best, jbf, float(NSA_NSEL)), axis=0, keepdims=True)
        pick = jbf == first
        sel = jnp.where(pick, 1.0, sel)
        score = jnp.where(pick, NEG, score)
    selt_ref[0] = sel.astype(_BF16)


def nsa_select_pallas(q_n, k_cmp, v_cmp):
    B, S, W = q_n.shape
    T = ATT_T
    n_c = (S - CMP_LEN) // CMP_STRIDE + 1
    ovt = np.zeros((NSA_NSEL, NSA_NC), np.float32)
    ovt[:, :n_c] = selection_overlap(n_c, NSA_NSEL).T
    ovt = jnp.asarray(ovt, _BF16)
    c_spec = pl.BlockSpec((1, NSA_NC, HEAD_DIM), lambda b, i: (b, 0, 0))
    return pl.pallas_call(
        _nsa_select_kernel,
        grid=(B, S // T),
        in_specs=[pl.BlockSpec((1, T, W), lambda b, i: (b, i, 0)), c_spec, c_spec,
                  pl.BlockSpec(ovt.shape, lambda b, i: (0, 0))],
        out_specs=[pl.BlockSpec((1, T, W), lambda b, i: (b, i, 0)),
                   pl.BlockSpec((1, NSA_NSEL, T), lambda b, i: (b, 0, i))],
        out_shape=[jax.ShapeDtypeStruct((B, S, W), _F32), jax.ShapeDtypeStruct((B, NSA_NSEL, S), _BF16)],
        compiler_params=pltpu.CompilerParams(dimension_semantics=("parallel", "parallel")),
        name="nsa_select",
    )(q_n, k_cmp, v_cmp, ovt)


def kernel(x, c, positions, w_ada, b_ada, norm1_g, norm2_g, w_in, w_out, rwkv_mu, rwkv_w0, rwkv_w2, rwkv_a0, rwkv_a2, rwkv_g2, rwkv_kk, rwkv_ka, rwkv_rk, rwkv_gn_w, rwkv_gn_b, conv_w, dil_q_g, dil_k_g, nsa_q_g, nsa_kc_g, nsa_ks_g, nsa_kw_g, nsa_pe_k, nsa_pe_v, nsa_wk1, nsa_wk2, nsa_wv1, nsa_wv2, onorm_g, w_router, b_router, w_gu, b_gu, w_down, b_down):
    B, S, D = x.shape
    tabs = rope_tables(positions)
    for l in range(DEPTH):
        mod = matmul(jax.nn.silu(c), w_ada[l], tm=B, tn=512) + b_ada[l]
        sh1, sc1, gt1, sh2, sc2, gt2 = [m[:, None, :] for m in jnp.split(mod, 6, axis=-1)]
        proj = in_proj_pallas(x, norm1_g[l], sc1, sh1, w_in[l])
        y_a = rwkv7_pallas(proj, rwkv_mu[l], rwkv_w0[l], rwkv_w2[l], rwkv_a0[l], rwkv_a2[l], rwkv_g2[l],
                           rwkv_kk[l], rwkv_ka[l], rwkv_rk[l], rwkv_gn_w[l], rwkv_gn_b[l])
        cq, ck, cv_t, q_n, q_r, kc, vc, ks, vs_t, kw, vw_t = attn_prep_pallas(
            proj, tabs, dil_q_g[l], dil_k_g[l], nsa_q_g[l], nsa_ks_g[l], nsa_kw_g[l])
        y_c = dil_attn_pallas(cq, ck, cv_t)
        k_cmp, v_cmp = nsa_compress_pallas(kc, vc, nsa_pe_k[l], nsa_pe_v[l], nsa_wk1[l], nsa_wk2[l], nsa_wv1[l],
                                           nsa_wv2[l], nsa_kc_g[l])
        o_cmp, sel_t = nsa_select_pallas(q_n, k_cmp, v_cmp)
        o_slc, o_swa = nsa_attn_pallas(q_r, ks, vs_t, kw, vw_t, sel_t)
        x, hb, logits = mix_out_pallas(x, y_a, proj, y_c, o_cmp, o_slc, o_swa, conv_w[l], onorm_g[l], w_out[l], gt1,
                                       norm2_g[l], sc2, sh2, w_router[l], b_router[l])
        hb, logits = hb.reshape(MOE_GROUPS, -1, D), logits.reshape(MOE_GROUPS, -1, ROUTER_PAD)
        y = jnp.stack([moe_pallas(hb[g], logits[g], l, w_gu, b_gu, w_down, b_down) for g in range(MOE_GROUPS)])
        x = x + gt2 * y.reshape(B, S, D)
    return x
```

```python
import functools

import numpy as np
import jax
import jax.numpy as jnp
from jax import lax
from jax.experimental import pallas as pl
from jax.experimental.pallas import tpu as pltpu

D_MODEL = 1024
DEPTH = 2

HEAD_DIM = 64
N_MIXERS = 4
GROUP_W = D_MODEL // N_MIXERS
N_HEADS = GROUP_W // HEAD_DIM
MIX_W = N_MIXERS * GROUP_W

RWKV_DECAY_RANK = 64
RWKV_AAA_RANK = 64
RWKV_GATE_RANK = 128
RWKV_DECAY_SCALE = 0.606531
RWKV_GN_EPS = 64e-5

CONV_W = 3

DIL_CFG = ((128, 1), (512, 4), (2048, 16))

CMP_LEN = 32
CMP_STRIDE = 16
CMP_HIDDEN = 256
SEL_BLK = 64
SEL_TOP = 16
SWA_WIN = 512
FORCE_SCORE = 1e4

N_EXPERTS = 32
TOP_K = 4
EXPERT_FF = D_MODEL
SWIGLU_LIMIT = 7.0
SWIGLU_ALPHA = 1.702

ROPE_THETA = 10000.0
NORM_EPS = 1e-6
NEG = -1e30

A_WIDTHS = (GROUP_W, GROUP_W, GROUP_W, RWKV_DECAY_RANK, RWKV_AAA_RANK, RWKV_GATE_RANK)
B_WIDTHS = (GROUP_W, GROUP_W, GROUP_W)
C_WIDTHS = (GROUP_W, GROUP_W, GROUP_W)
D_WIDTHS = (GROUP_W, HEAD_DIM, HEAD_DIM, HEAD_DIM, HEAD_DIM, HEAD_DIM, HEAD_DIM, 3 * N_HEADS)
SECTION_WIDTHS = (sum(A_WIDTHS), sum(B_WIDTHS), sum(C_WIDTHS), sum(D_WIDTHS))
PROJ_W = sum(SECTION_WIDTHS)

_F32 = jnp.float32
_BF16 = jnp.bfloat16
_NT = (((1,), (1,)), ((), ()))
_TN = (((0,), (0,)), ((), ()))


def _dot_hi(a, b, dims=None):
    if dims is None:
        return jnp.dot(a, b, precision=lax.Precision.HIGHEST, preferred_element_type=_F32)
    return lax.dot_general(a, b, dims, precision=lax.Precision.HIGHEST, preferred_element_type=_F32)


def _dot_lo(a, b, dims=None):
    a, b = a.astype(_BF16), b.astype(_BF16)
    if dims is None:
        return jnp.dot(a, b, preferred_element_type=_F32)
    return lax.dot_general(a, b, dims, preferred_element_type=_F32)


def _bf16_parts(a, parts):
    out = []
    for _ in range(parts):
        p = a.astype(_BF16)
        out.append(p)
        a = a - p.astype(_F32)
    return out


def _dot_split(a, m, parts=2):
    mb = m.astype(_BF16)
    return sum(jnp.dot(p, mb, preferred_element_type=_F32) for p in _bf16_parts(a, parts))


def _dot_split_left(m, a, parts=2):
    mb = m.astype(_BF16)
    return sum(jnp.dot(mb, p, preferred_element_type=_F32) for p in _bf16_parts(a, parts))


def _dot_x3(a, b):
    ah, al = _bf16_parts(a, 2)
    bh, bl = _bf16_parts(b, 2)
    d = lambda x, y: jnp.dot(x, y, preferred_element_type=_F32)
    return d(ah, bh) + (d(ah, bl) + d(al, bh))


def _matmul_kernel(x_ref, w_ref, o_ref, *, exact):
    if exact:
        o_ref[...] = _dot_hi(x_ref[...], w_ref[...])
    else:
        o_ref[...] = jnp.dot(x_ref[...].astype(_BF16), w_ref[...], preferred_element_type=_F32)


def matmul(x, w, tm=512, tn=256, exact=False):
    M, K = x.shape
    N = w.shape[1]
    n_pad = -(-N // tn) * tn
    wb = jnp.pad(w if exact else w.astype(_BF16), ((0, 0), (0, n_pad - N)))
    out = pl.pallas_call(
        functools.partial(_matmul_kernel, exact=exact),
        grid=(M // tm, n_pad // tn),
        in_specs=[pl.BlockSpec((tm, K), lambda i, j: (i, 0)),
                  pl.BlockSpec((K, tn), lambda i, j: (0, j))],
        out_specs=pl.BlockSpec((tm, tn), lambda i, j: (i, j)),
        out_shape=jax.ShapeDtypeStruct((M, n_pad), _F32),
        name="matmul",
    )(x, wb)
    return out[:, :N]


PROJ_PAD = -(-PROJ_W // 256) * 256
PROJ_TM = 256
PROJ_TN = 512
ROUTER_PAD = 128
GATE_COL_BLOCK = (PROJ_W - 3 * N_HEADS) // 128
assert GATE_COL_BLOCK * 128 == PROJ_W - 3 * N_HEADS


def _norm_mod(x, g, sc, sh):
    y = x * lax.rsqrt(jnp.mean(x * x, axis=-1, keepdims=True) + NORM_EPS) * g
    return y * (1.0 + sc) + sh


def _in_proj_kernel(x_ref, g_ref, sc_ref, sh_ref, w_ref, o_ref):
    h = _norm_mod(x_ref[0], g_ref[...], sc_ref[0], sh_ref[0]).astype(_BF16)
    for n0 in range(0, PROJ_PAD, PROJ_TN):
        o_ref[0, :, n0:n0 + PROJ_TN] = jnp.dot(h, w_ref[:, n0:n0 + PROJ_TN], preferred_element_type=_F32)


def in_proj_pallas(x, g, sc, sh, w_in):
    B, S, D = x.shape
    tm = PROJ_TM
    w = jnp.pad(w_in.astype(_BF16), ((0, 0), (0, PROJ_PAD - PROJ_W)))
    per_batch = pl.BlockSpec((1, 1, D), lambda b, i: (b, 0, 0))
    return pl.pallas_call(
        _in_proj_kernel,
        grid=(B, S // tm),
        in_specs=[pl.BlockSpec((1, tm, D), lambda b, i: (b, i, 0)),
                  pl.BlockSpec((1, D), lambda b, i: (0, 0)), per_batch, per_batch,
                  pl.BlockSpec((D, PROJ_PAD), lambda b, i: (0, 0))],
        out_specs=pl.BlockSpec((1, tm, PROJ_PAD), lambda b, i: (b, i, 0)),
        out_shape=jax.ShapeDtypeStruct((B, S, PROJ_PAD), _F32),
        compiler_params=pltpu.CompilerParams(dimension_semantics=("parallel", "parallel"),
                                             vmem_limit_bytes=48 * 1024 * 1024),
        name="in_proj",
    )(x, g.reshape(1, D), sc, sh, w)


def _mix_out_kernel(x_ref, ya_ref, bg_ref, cg_ref, xin_ref, cgh_ref, xinh_ref, yc_ref, ocmp_ref, oslc_ref, oswa_ref,
                    gl_ref, convw_ref, ong_ref, wout_ref, gt1_ref, g2_ref, sc2_ref, sh2_ref, wr_ref, br_ref,
                    xo_ref, hb_ref, lg_ref):
    i = pl.program_id(1)
    G = GROUP_W
    u = cg_ref[0] * xin_ref[0]
    halo = jnp.where(i == 0, 0.0, cgh_ref[0] * xinh_ref[0])
    row = lax.broadcasted_iota(jnp.int32, u.shape, 0)
    u1 = jnp.where(row == 0, halo[7:8], pltpu.roll(u, 1, axis=0))
    u2 = jnp.where(row == 0, halo[6:7], jnp.where(row == 1, halo[7:8], pltpu.roll(u, 2, axis=0)))
    cw = convw_ref[...]
    yb = bg_ref[0] * (cw[0:1] * u2 + cw[1:2] * u1 + cw[2:3] * u)

    sg = jax.nn.sigmoid(gl_ref[0])
    er = lax.broadcasted_iota(jnp.int32, (128, G), 0)
    ec = lax.broadcasted_iota(jnp.int32, (128, G), 1) // HEAD_DIM
    yd = None
    for j, o_ref in enumerate((ocmp_ref, oslc_ref, oswa_ref)):
        gate = _dot_split(sg, er == 3 * ec + j)
        yd = gate * o_ref[0] if yd is None else yd + gate * o_ref[0]

    head_sum = _group_indicator(G, HEAD_DIM)
    ong = ong_ref[...]
    parts = [ya_ref[0]]
    for n, y in enumerate((yb, yc_ref[0], yd)):
        ms = _dot_split(y * y, head_sum) * (1.0 / HEAD_DIM)
        parts.append(y * lax.rsqrt(ms + NORM_EPS) * ong[:, n * G:(n + 1) * G])
    mixed = jnp.dot(jnp.concatenate(parts, axis=1).astype(_BF16), wout_ref[...], preferred_element_type=_F32)
    x = x_ref[0] + gt1_ref[0] * mixed
    xo_ref[0] = x
    h = _norm_mod(x, g2_ref[...], sc2_ref[0], sh2_ref[0])
    hb_ref[0] = h.astype(_BF16)
    lg_ref[0] = _dot_x3(h, wr_ref[...]) + br_ref[...]


def mix_out_pallas(x, y_a, proj, y_c, o_cmp, o_slc, o_swa, conv_w, onorm_g, w_out, gt1, g2, sc2, sh2, w_router, b_router):
    B, S, D = x.shape
    tm = PROJ_TM
    G = GROUP_W
    b0 = SECTION_WIDTHS[0] // G
    tile = lambda w, col: pl.BlockSpec((1, tm, w), lambda b, i: (b, i, col))
    halo = lambda col: pl.BlockSpec((1, 8, G), lambda b, i: (b, jnp.maximum(i * (tm // 8) - 1, 0), col))
    const = lambda t: pl.BlockSpec(t.shape, lambda b, i: (0,) * t.ndim)
    per_batch = pl.BlockSpec((1, 1, D), lambda b, i: (b, 0, 0))
    wr = jnp.pad(w_router, ((0, 0), (0, ROUTER_PAD - N_EXPERTS)))
    br = jnp.pad(b_router, (0, ROUTER_PAD - N_EXPERTS)).reshape(1, ROUTER_PAD)
    consts = [conv_w, onorm_g.reshape(1, 3 * G), w_out.astype(_BF16)]
    return pl.pallas_call(
        _mix_out_kernel,
        grid=(B, S // tm),
        in_specs=[tile(D, 0), tile(G, 0), tile(G, b0), tile(G, b0 + 1), tile(G, b0 + 2), halo(b0 + 1), halo(b0 + 2),
                  tile(G, 0), tile(G, 0), tile(G, 0), tile(G, 0), tile(128, GATE_COL_BLOCK)]
                 + [const(t) for t in consts] + [per_batch, pl.BlockSpec((1, D), lambda b, i: (0, 0)), per_batch,
                                                 per_batch, const(wr), const(br)],
        out_specs=[tile(D, 0), tile(D, 0), tile(ROUTER_PAD, 0)],
        out_shape=[jax.ShapeDtypeStruct((B, S, D), _F32), jax.ShapeDtypeStruct((B, S, D), _BF16),
                   jax.ShapeDtypeStruct((B, S, ROUTER_PAD), _F32)],
        compiler_params=pltpu.CompilerParams(dimension_semantics=("parallel", "parallel")),
        name="mix_out",
    )(x, y_a, proj, proj, proj, proj, proj, y_c, o_cmp, o_slc, o_swa, proj, *consts, gt1, g2.reshape(1, D), sc2, sh2,
      wr, br)


RWKV_CHUNK = 64
RWKV_TILE = 256


def _group_indicator(n, group):
    r = lax.broadcasted_iota(jnp.int32, (n, n), 0) // group
    c = lax.broadcasted_iota(jnp.int32, (n, n), 1) // group
    return r == c


def _rwkv_kernel(za_ref, mu_ref, w0_ref, w2_ref, a0_ref, a2_ref, g2_ref, kk_ref, ka_ref, rk_ref,
                 gnw_ref, gnb_ref, o_ref, prev_sc, h_sc):
    C = RWKV_CHUNK
    TT = RWKV_TILE
    c = pl.program_id(1)

    @pl.when(c == 0)
    def _():
        prev_sc[...] = jnp.zeros_like(prev_sc)
        h_sc[...] = jnp.zeros_like(h_sc)

    z = za_ref[0]
    row = lax.broadcasted_iota(jnp.int32, z.shape, 0)
    zs = jnp.where(row == 0, prev_sc[...], pltpu.roll(z, 1, axis=0))
    prev_sc[...] = z[TT - 1:TT, :]
    z = z + (zs - z) * mu_ref[...]

    G = GROUP_W
    r, k, v = z[:, 0:G], z[:, G:2 * G], z[:, 2 * G:3 * G]
    o = 3 * G
    wd = z[:, o:o + RWKV_DECAY_RANK]
    ad = z[:, o + RWKV_DECAY_RANK:o + RWKV_DECAY_RANK + RWKV_AAA_RANK]
    gd = z[:, o + RWKV_DECAY_RANK + RWKV_AAA_RANK:]

    lw = -RWKV_DECAY_SCALE * jax.nn.sigmoid(w0_ref[...] + _dot_lo(jnp.tanh(wd), w2_ref[...]))
    a = jax.nn.sigmoid(a0_ref[...] + _dot_lo(ad, a2_ref[...]))
    g = _dot_lo(jax.nn.sigmoid(gd), g2_ref[...])

    head_sum = _group_indicator(G, HEAD_DIM)
    kk = k * kk_ref[...]
    kk = kk * lax.rsqrt(_dot_split(kk * kk, head_sum) + 1e-12)
    k = k * (1.0 + (a - 1.0) * ka_ref[...])
    b = kk * a

    ti = lax.broadcasted_iota(jnp.int32, (C, C), 0)
    tj = lax.broadcasted_iota(jnp.int32, (C, C), 1)
    incl = ti >= tj
    strict = ti > tj
    eye = ti == tj
    blk16 = (ti // 16) == (tj // 16)
    blk32 = (ti // 32) == (tj // 32)
    eye_f = eye.astype(_F32)

    ri = lax.broadcasted_iota(jnp.int32, (TT, TT), 0)
    rj = lax.broadcasted_iota(jnp.int32, (TT, TT), 1)
    chunk_tri = (ri >= rj) & (ri // C == rj // C)
    cum = _dot_split_left(chunk_tri, lw, parts=3)
    g_in = jnp.exp(cum)
    A_all = -kk * jnp.exp(cum - lw)
    R_all = r * g_in
    g_inv = jnp.exp(-cum)
    B_all = b * g_inv
    K_all = k * g_inv

    units = [(ci, h) for ci in range(TT // C) for h in range(N_HEADS)]

    def part(t, u):
        ci, h = u
        return t[ci * C:(ci + 1) * C, h * HEAD_DIM:(h + 1) * HEAD_DIM]

    def g_end(u):
        ci, h = u
        return g_in[(ci + 1) * C - 1:(ci + 1) * C, h * HEAD_DIM:(h + 1) * HEAD_DIM]

    A = [part(A_all, u) for u in units]
    R = [part(R_all, u) for u in units]
    B = [part(B_all, u) for u in units]
    Kt = [part(K_all, u) for u in units]
    V = [part(v, u) for u in units]
    n = range(len(units))
    gram = [_dot_lo(jnp.concatenate([A[i], R[i]], axis=0), jnp.concatenate([B[i], Kt[i]], axis=0), _NT) for i in n]
    l_ab = [jnp.where(strict, gram[i][0:C, 0:C], 0.0) for i in n]
    l_ak = [jnp.where(strict, gram[i][0:C, C:2 * C], 0.0) for i in n]
    m_rb = [jnp.where(incl, gram[i][C:2 * C, 0:C], 0.0) for i in n]
    m_rk = [jnp.where(incl, gram[i][C:2 * C, C:2 * C], 0.0) for i in n]
    p = [jnp.where(blk16, l_ab[i], 0.0) for i in n]
    x = [eye_f + p[i] for i in n]
    for _ in range(3):
        p = [_dot_lo(p[i], p[i]) for i in n]
        x = [_dot_lo(x[i], eye_f + p[i]) for i in n]
    for lvl in (blk32 & ~blk16, ~blk32):
        xl = [_dot_lo(x[i], jnp.where(lvl, l_ab[i], 0.0)) for i in n]
        x = [x[i] + _dot_lo(xl[i], x[i]) for i in n]
    lv = [_dot_lo(l_ak[i], V[i]) for i in n]
    tap = [_dot_lo(x[i], jnp.concatenate([A[i], lv[i]], axis=1)) for i in n]
    m1 = [_dot_lo(m_rb[i], tap[i]) for i in n]
    mv = [_dot_lo(m_rk[i], V[i]) for i in n]
    bt = [_dot_lo(B[i] * g_end(units[i]), tap[i], _TN) for i in n]
    kv = [_dot_lo(Kt[i] * g_end(units[i]), V[i], _TN) for i in n]
    w_yh = [jnp.concatenate([R[i] + m1[i][:, 0:HEAD_DIM],
                             jnp.where(eye, g_end(units[i]), 0.0) + bt[i][:, 0:HEAD_DIM]], axis=0) for i in n]
    y0 = [m1[i][:, HEAD_DIM:] + mv[i] for i in n]
    h_add = [bt[i][:, HEAD_DIM:] + kv[i] for i in n]
    state = [h_sc[h] for h in range(N_HEADS)]
    y_rows = []
    for ci in range(TT // C):
        ys = []
        for h in range(N_HEADS):
            i = ci * N_HEADS + h
            nxt = _dot_x3(w_yh[i], state[h])
            ys.append(nxt[0:C] + y0[i])
            state[h] = nxt[C:2 * C] + h_add[i]
        y_rows.append(jnp.concatenate(ys, axis=1))
    for h in range(N_HEADS):
        h_sc[h] = state[h]
    y = jnp.concatenate(y_rows, axis=0)

    mu = _dot_split(y, head_sum) * (1.0 / HEAD_DIM)
    d = y - mu
    var = _dot_split(d * d, head_sum) * (1.0 / HEAD_DIM)
    y = d * lax.rsqrt(var + RWKV_GN_EPS) * gnw_ref[...] + gnb_ref[...]
    y = y + _dot_split(r * k * rk_ref[...], head_sum) * v
    o_ref[0] = y * g


def rwkv7_pallas(za, mu, w0, w2, a0, a2, g2, k_k, k_a, r_k, gn_w, gn_b):
    B, S, _ = za.shape
    W = SECTION_WIDTHS[0]
    TT = RWKV_TILE
    row = lambda t: t.reshape(1, -1).astype(_F32)
    full = lambda t: pl.BlockSpec(t.shape, lambda b, c: (0,) * t.ndim)
    params = [row(mu), row(w0), w2.astype(_BF16), row(a0), a2.astype(_BF16), g2.astype(_BF16),
              row(k_k), row(k_a), row(r_k), row(gn_w), row(gn_b)]
    return pl.pallas_call(
        _rwkv_kernel,
        grid=(B, S // TT),
        in_specs=[pl.BlockSpec((1, TT, W), lambda b, c: (b, c, 0))] + [full(p) for p in params],
        out_specs=pl.BlockSpec((1, TT, GROUP_W), lambda b, c: (b, c, 0)),
        out_shape=jax.ShapeDtypeStruct((B, S, GROUP_W), _F32),
        scratch_shapes=[pltpu.VMEM((1, W), _F32), pltpu.VMEM((N_HEADS, HEAD_DIM, HEAD_DIM), _F32)],
        compiler_params=pltpu.CompilerParams(dimension_semantics=("parallel", "arbitrary")),
        name="rwkv7",
    )(za, *params)


MOE_TM = 512
MOE_FC = 512
MOE_CAST_ROWS = 128
MOE_GROUPS = 1


def _moe_ffn_kernel(blk_e_ref, n_used_ref, x_ref, wgu_ref, bgu_ref, wd_ref, bd_ref, o_ref, wgu_sc, wd_sc):
    i = pl.program_id(0)
    F = EXPERT_FF
    new_expert = jnp.logical_or(i == 0, blk_e_ref[i] != blk_e_ref[jnp.maximum(i - 1, 0)])

    @pl.when(new_expert)
    def _():
        def cast(j, carry):
            rows = pl.ds(pl.multiple_of(j * MOE_CAST_ROWS, MOE_CAST_ROWS), MOE_CAST_ROWS)
            wgu_sc[rows, :] = wgu_ref[0, rows, :].astype(_BF16)
            wd_sc[rows, :] = wd_ref[0, rows, :].astype(_BF16)
            return carry
        lax.fori_loop(0, F // MOE_CAST_ROWS, cast, 0)

    @pl.when(i < n_used_ref[0])
    def _():
        x = x_ref[...]
        acc = None
        for c in range(F // MOE_FC):
            lo = c * MOE_FC
            gate = jnp.dot(x, wgu_sc[:, lo:lo + MOE_FC], preferred_element_type=_F32) + bgu_ref[0, :, lo:lo + MOE_FC]
            up = jnp.dot(x, wgu_sc[:, F + lo:F + lo + MOE_FC], preferred_element_type=_F32) + bgu_ref[0, :, F + lo:F + lo + MOE_FC]
            gate = jnp.minimum(gate, SWIGLU_LIMIT)
            up = jnp.clip(up, -SWIGLU_LIMIT, SWIGLU_LIMIT)
            act = gate * jax.nn.sigmoid(SWIGLU_ALPHA * gate) * (up + 1.0)
            part = jnp.dot(act.astype(_BF16), wd_sc[lo:lo + MOE_FC, :], preferred_element_type=_F32)
            acc = part if acc is None else acc + part
        o_ref[...] = (acc + bd_ref[0]).astype(o_ref.dtype)

    @pl.when(i >= n_used_ref[0])
    def _():
        o_ref[...] = jnp.zeros_like(o_ref)


def moe_ffn(xs, blk_e, n_used, layer, w_gu, b_gu, w_down, b_down):
    n_rows, D = xs.shape
    L, E, _, F2 = w_gu.shape
    assert D == EXPERT_FF and F2 == 2 * EXPERT_FF
    n_blk = n_rows // MOE_TM
    grid_spec = pltpu.PrefetchScalarGridSpec(
        num_scalar_prefetch=2,
        grid=(n_blk,),
        in_specs=[pl.BlockSpec((MOE_TM, D), lambda i, e, n: (i, 0)),
                  pl.BlockSpec((None, 1, D, F2), lambda i, e, n: (layer, e[i], 0, 0)),
                  pl.BlockSpec((None, 1, 1, F2), lambda i, e, n: (layer, e[i], 0, 0)),
                  pl.BlockSpec((None, 1, F2 // 2, D), lambda i, e, n: (layer, e[i], 0, 0)),
                  pl.BlockSpec((None, 1, 1, D), lambda i, e, n: (layer, e[i], 0, 0))],
        out_specs=pl.BlockSpec((MOE_TM, D), lambda i, e, n: (i, 0)),
        scratch_shapes=[pltpu.VMEM((D, F2), _BF16), pltpu.VMEM((F2 // 2, D), _BF16)],
    )
    return pl.pallas_call(
        _moe_ffn_kernel,
        grid_spec=grid_spec,
        out_shape=jax.ShapeDtypeStruct((n_rows, D), _BF16),
        compiler_params=pltpu.CompilerParams(dimension_semantics=("arbitrary",),
                                             vmem_limit_bytes=52 * 1024 * 1024),
        name="moe_ffn",
    )(blk_e, n_used, xs, w_gu, b_gu.reshape(L, E, 1, F2), w_down, b_down.reshape(L, E, 1, D))


def moe_pallas(hb, logits, layer, w_gu, b_gu, w_down, b_down):
    T, D = hb.shape
    TK = T * TOP_K
    TM = MOE_TM
    E = N_EXPERTS
    i32 = jnp.int32
    top_val, top_idx = lax.top_k(logits[:, :E], TOP_K)
    gates = jax.nn.softmax(top_val, axis=-1)
    e_flat = top_idx.T.reshape(TK).astype(i32)
    counts = jnp.sum((e_flat[:, None] == jnp.arange(E, dtype=i32)[None, :]).astype(i32), axis=0)
    need = jnp.repeat((-counts) % TM, TM)
    d_idx = jnp.arange(E * TM, dtype=i32)
    d_key = jnp.where(d_idx % TM < need, d_idx // TM, E)
    n_rows = TK + E * TM
    keys = jnp.concatenate([e_flat, d_key])
    toks = jnp.concatenate([jnp.arange(TK, dtype=i32) % T, jnp.zeros((E * TM,), i32)])
    rows = jnp.arange(n_rows, dtype=i32)
    s_keys, src_tok, s_slot = lax.sort((keys, toks, rows), num_keys=1)
    _, row_of_slot = lax.sort((s_slot, rows), num_keys=1)
    row_of = row_of_slot[:TK]
    blk_e = s_keys[::TM]
    n_used = jnp.sum((blk_e < E).astype(i32)).reshape(1)
    blk_e = jnp.where(blk_e < E, blk_e, blk_e[jnp.maximum(n_used[0] - 1, 0)])
    xs = jnp.take(hb, src_tok, axis=0)
    ybuf = moe_ffn(xs, blk_e, n_used, layer, w_gu, b_gu, w_down, b_down)
    y = lax.optimization_barrier(jnp.take(ybuf, row_of, axis=0))
    y = y.reshape(TOP_K, T, D).astype(_F32) * gates.T[:, :, None]
    return jnp.sum(y, axis=0)


ATT_T = 128
ATT_TK = 512
ATT_R = ATT_TK // ATT_T
NSA_NSEL = 64


def _window_table(window, weight_of_distance):
    T, TK = ATT_T, ATT_TK
    n = window // T + ATT_R + 1
    d = (np.arange(n)[:, None, None] * T + np.arange(T)[None, None, :] - np.arange(TK)[None, :, None])
    return jnp.asarray(weight_of_distance(d).astype(np.float32))


def _first_key_tile(qi, window):
    return jnp.maximum(qi * ATT_T - window, 0) // ATT_TK


def _swa_weight(d):
    return (d >= 0) & (d <= SWA_WIN - 1)


def _dil_weight(d):
    return sum(((d >= 0) & (d <= window) & (d % dil == 0)).astype(np.int32) for window, dil in DIL_CFG)


def _softmax_tile(s, w, m_ref, l_ref, weighted):
    s = jnp.where(w > 0.0, s, NEG)
    m_prev = m_ref[...]
    m_new = jnp.maximum(m_prev, jnp.max(s, axis=0, keepdims=True))
    alpha = jnp.exp2(m_prev - m_new)
    p = jnp.exp2(s - m_new)
    if weighted:
        p = p * w
    l_ref[...] = alpha * l_ref[...] + jnp.sum(p, axis=0, keepdims=True)
    m_ref[...] = m_new
    return alpha, p.astype(_BF16)


def _init_stats(m_sc, l_sc, acc_sc):
    m_sc[...] = jnp.full_like(m_sc, NEG)
    l_sc[...] = jnp.zeros_like(l_sc)
    acc_sc[...] = jnp.zeros_like(acc_sc)


def _heads_to_rows(ot):
    T = ATT_T
    pairs = []
    for h in range(0, N_HEADS, 2):
        two = jnp.concatenate([ot[:, h * T:(h + 1) * T], ot[:, (h + 1) * T:(h + 2) * T]], axis=0)
        pairs.append(two.T)
    return jnp.concatenate(pairs, axis=1)


def _nsa_attn_kernel(q_ref, ks_ref, vst_ref, kw_ref, vwt_ref, selt_ref, swa_ref, oslc_ref, oswa_ref,
                     m_sc, l_sc, acc_sc):
    T, TK = ATT_T, ATT_TK
    qi = pl.program_id(1)
    q = q_ref[0]
    q = jnp.concatenate([q[:, h * HEAD_DIM:(h + 1) * HEAD_DIM] for h in range(N_HEADS)], axis=0)
    selt = selt_ref[0]
    key_blk = lax.broadcasted_iota(jnp.int32, (TK, NSA_NSEL), 0) // SEL_BLK
    blk_id = lax.broadcasted_iota(jnp.int32, (TK, NSA_NSEL), 1)
    kr = lax.broadcasted_iota(jnp.int32, (TK, T), 0)
    qc = lax.broadcasted_iota(jnp.int32, (TK, T), 1)

    def step(k_ref, vt_ref, kt, w):
        base = pl.multiple_of(kt * TK, TK)
        s = lax.dot_general(k_ref[0, pl.ds(base, TK), :], q, _NT, preferred_element_type=_F32)
        alpha, p = _softmax_tile(s, jnp.concatenate([w] * N_HEADS, axis=1), m_sc, l_sc, False)
        acc_sc[...] = alpha * acc_sc[...] + jnp.dot(vt_ref[0, :, pl.ds(base, TK)], p, preferred_element_type=_F32)

    def finish(o_ref):
        o_ref[0] = _heads_to_rows(acc_sc[...] / l_sc[...])

    last = qi // ATT_R
    _init_stats(m_sc, l_sc, acc_sc)

    def slc_step(kt, carry):
        in_blk = (key_blk + kt * (TK // SEL_BLK) == blk_id).astype(_BF16)
        w = jnp.dot(in_blk, selt, preferred_element_type=_F32)
        w = jnp.where(kt * TK + kr <= qi * T + qc, w, 0.0)
        step(ks_ref, vst_ref, kt, w)
        return carry

    lax.fori_loop(0, last + 1, slc_step, 0)
    finish(oslc_ref)

    _init_stats(m_sc, l_sc, acc_sc)

    def swa_step(kt, carry):
        step(kw_ref, vwt_ref, kt, swa_ref[qi - kt * ATT_R])
        return carry

    lax.fori_loop(_first_key_tile(qi, SWA_WIN - 1), last + 1, swa_step, 0)
    finish(oswa_ref)


def nsa_attn_pallas(q_r, ks_r, vs_t, kw_r, vw_t, sel_t):
    B, S, W = q_r.shape
    assert S // SEL_BLK == NSA_NSEL
    T = ATT_T
    R = N_HEADS * T
    swa_tab = _window_table(SWA_WIN - 1, _swa_weight)
    k_spec = pl.BlockSpec((1, S, HEAD_DIM), lambda b, i: (b, 0, 0))
    vt_spec = pl.BlockSpec((1, HEAD_DIM, S), lambda b, i: (b, 0, 0))
    q_spec = pl.BlockSpec((1, T, W), lambda b, i: (b, i, 0))
    return pl.pallas_call(
        _nsa_attn_kernel,
        grid=(B, S // T),
        in_specs=[q_spec, k_spec, vt_spec, k_spec, vt_spec,
                  pl.BlockSpec((1, NSA_NSEL, T), lambda b, i: (b, 0, i)),
                  pl.BlockSpec(swa_tab.shape, lambda b, i: (0, 0, 0))],
        out_specs=[q_spec, q_spec],
        out_shape=[jax.ShapeDtypeStruct((B, S, W), _F32)] * 2,
        scratch_shapes=[pltpu.VMEM((1, R), _F32), pltpu.VMEM((1, R), _F32), pltpu.VMEM((HEAD_DIM, R), _F32)],
        compiler_params=pltpu.CompilerParams(dimension_semantics=("parallel", "arbitrary")),
        name="nsa_attn",
    )(q_r, ks_r, vs_t, kw_r, vw_t, sel_t, swa_tab)


def _dil_attn_kernel(q_ref, k_ref, vt_ref, tab_ref, o_ref, m_sc, l_sc, acc_sc):
    T, TK = ATT_T, ATT_TK
    qi = pl.program_id(1)
    _init_stats(m_sc, l_sc, acc_sc)
    qs = [q_ref[0, :, h * HEAD_DIM:(h + 1) * HEAD_DIM] for h in range(N_HEADS)]

    def step(kt, carry):
        base = pl.multiple_of(kt * TK, TK)
        w = tab_ref[qi - kt * ATT_R]
        s = jnp.concatenate([lax.dot_general(k_ref[0, h, pl.ds(base, TK), :], qs[h], _NT, preferred_element_type=_F32)
                             for h in range(N_HEADS)], axis=1)
        alpha, p = _softmax_tile(s, jnp.concatenate([w] * N_HEADS, axis=1), m_sc, l_sc, True)
        pv = jnp.concatenate([jnp.dot(vt_ref[0, h, :, pl.ds(base, TK)], p[:, h * T:(h + 1) * T],
                                      preferred_element_type=_F32) for h in range(N_HEADS)], axis=1)
        acc_sc[...] = alpha * acc_sc[...] + pv
        return carry

    lax.fori_loop(_first_key_tile(qi, DIL_CFG[-1][0]), qi // ATT_R + 1, step, 0)
    o_ref[0] = _heads_to_rows(acc_sc[...] / l_sc[...])


def dil_attn_pallas(q_r, k_r, v_t):
    B, H, S, hd = k_r.shape
    T = ATT_T
    R = H * T
    W = H * hd
    tab = _window_table(DIL_CFG[-1][0], _dil_weight)
    q_spec = pl.BlockSpec((1, T, W), lambda b, i: (b, i, 0))
    return pl.pallas_call(
        _dil_attn_kernel,
        grid=(B, S // T),
        in_specs=[q_spec,
                  pl.BlockSpec((1, H, S, hd), lambda b, i: (b, 0, 0, 0)),
                  pl.BlockSpec((1, H, hd, S), lambda b, i: (b, 0, 0, 0)),
                  pl.BlockSpec(tab.shape, lambda b, i: (0, 0, 0))],
        out_specs=q_spec,
        out_shape=jax.ShapeDtypeStruct((B, S, W), _F32),
        scratch_shapes=[pltpu.VMEM((1, R), _F32), pltpu.VMEM((1, R), _F32), pltpu.VMEM((hd, R), _F32)],
        compiler_params=pltpu.CompilerParams(dimension_semantics=("parallel", "arbitrary")),
        name="dil_attn",
    )(q_r, k_r, v_t, tab)


LOG2_E = 1.4426950408889634
ROPE_LANES = 128
PREP_TM = 256


def _rope_table_kernel(pos_ref, inv_ref, cos_ref, slo_ref, shi_ref):
    ang = pos_ref[...].astype(_F32) * inv_ref[...]
    cos, sin = jnp.cos(ang), jnp.sin(ang)
    first_half = (lax.broadcasted_iota(jnp.int32, ang.shape, 1) % HEAD_DIM) < HEAD_DIM // 2
    cos_ref[...] = cos
    slo_ref[...] = jnp.where(first_half, -sin, 0.0)
    shi_ref[...] = jnp.where(first_half, 0.0, sin)


def rope_tables(positions):
    B, S = positions.shape
    T = B * S
    half = HEAD_DIM // 2
    inv = ROPE_THETA ** (-jnp.arange(half, dtype=_F32) / half)
    inv = jnp.tile(inv, ROPE_LANES // half).reshape(1, ROPE_LANES)
    pos = jnp.broadcast_to(positions.reshape(T, 1), (T, ROPE_LANES))
    tm = 1024
    spec = pl.BlockSpec((tm, ROPE_LANES), lambda i: (i, 0))
    return pl.pallas_call(
        _rope_table_kernel,
        grid=(T // tm,),
        in_specs=[spec, pl.BlockSpec((1, ROPE_LANES), lambda i: (0, 0))],
        out_specs=[spec] * 3,
        out_shape=[jax.ShapeDtypeStruct((T, ROPE_LANES), _F32)] * 3,
        name="rope_tables",
    )(pos, inv)


def _attn_prep_kernel(cq_ref, ck_ref, cv_ref, dq_ref, d1_ref, d2_ref, cos_ref, slo_ref, shi_ref, gains_ref,
                      cqo_ref, cko_ref, cvo_ref, qn_ref, qr_ref, kc_ref, vc_ref, ks_ref, vst_ref, kw_ref, vwt_ref):
    G, hd = GROUP_W, HEAD_DIM
    wide = lambda t: jnp.concatenate([t] * (G // ROPE_LANES), axis=1)
    cos, slo, shi = wide(cos_ref[...]), wide(slo_ref[...]), wide(shi_ref[...])
    head_sum = _group_indicator(G, hd)
    gains = gains_ref[...]
    scale = hd ** -0.5

    def norm(t, n):
        return t * lax.rsqrt(_dot_split(t * t, head_sum) * (1.0 / hd) + NORM_EPS) * gains[n:n + 1]

    def rope(t):
        return t * cos + pltpu.roll(t, G - hd // 2, axis=1) * slo + pltpu.roll(t, hd // 2, axis=1) * shi

    cqo_ref[0] = (rope(norm(cq_ref[0], 0)) * (scale * LOG2_E)).astype(_BF16)
    ck = rope(norm(ck_ref[0], 1)).astype(_BF16)
    for h in range(N_HEADS):
        cko_ref[0, h] = ck[:, h * hd:(h + 1) * hd]
    cvo_ref[0] = cv_ref[0].T.astype(_BF16).reshape(N_HEADS, hd, cv_ref.shape[1])
    qn = norm(dq_ref[0], 2)
    qn_ref[0] = (qn * scale).astype(_BF16)
    qr_ref[0] = (rope(qn) * (scale * LOG2_E)).astype(_BF16)
    d1 = d1_ref[0]
    d2 = d2_ref[0]
    kc_ref[0] = d1[:, 0:hd]
    vc_ref[0] = d1[:, hd:2 * hd]
    ks_ref[0] = rope(norm(d1, 3))[:, 2 * hd:3 * hd].astype(_BF16)
    kw_ref[0] = rope(norm(d2, 4))[:, 0:hd].astype(_BF16)
    vst_ref[0] = d1.T[3 * hd:4 * hd, :].astype(_BF16)
    vwt_ref[0] = d2.T[hd:2 * hd, :].astype(_BF16)


def attn_prep_pallas(proj, tabs, dil_q_g, dil_k_g, nsa_q_g, nsa_ks_g, nsa_kw_g):
    B, S, _ = proj.shape
    tm = PREP_TM
    G, hd, H = GROUP_W, HEAD_DIM, N_HEADS
    c0 = (SECTION_WIDTHS[0] + SECTION_WIDTHS[1]) // G
    z = jnp.zeros((hd,), _F32)
    gains = jnp.stack([jnp.tile(dil_q_g, H), jnp.tile(dil_k_g, H), jnp.tile(nsa_q_g, H),
                       jnp.concatenate([z, z, nsa_ks_g, z]), jnp.concatenate([nsa_kw_g, z, z, z])])
    col = lambda c: pl.BlockSpec((1, tm, G), lambda b, i: (b, i, c))
    nt = S // tm
    tab = pl.BlockSpec((tm, ROPE_LANES), lambda b, i: (b * nt + i, 0))
    tok = lambda w, dt: (pl.BlockSpec((1, tm, w), lambda b, i: (b, i, 0)), jax.ShapeDtypeStruct((B, S, w), dt))
    tr = (pl.BlockSpec((1, hd, tm), lambda b, i: (b, 0, i)), jax.ShapeDtypeStruct((B, hd, S), _BF16))
    outs = [tok(G, _BF16),
            (pl.BlockSpec((1, H, tm, hd), lambda b, i: (b, 0, i, 0)), jax.ShapeDtypeStruct((B, H, S, hd), _BF16)),
            (pl.BlockSpec((1, H, hd, tm), lambda b, i: (b, 0, 0, i)), jax.ShapeDtypeStruct((B, H, hd, S), _BF16)),
            tok(G, _BF16), tok(G, _BF16), tok(hd, _F32), tok(hd, _F32), tok(hd, _BF16), tr, tok(hd, _BF16), tr]
    return pl.pallas_call(
        _attn_prep_kernel,
        grid=(B, nt),
        in_specs=[col(c0), col(c0 + 1), col(c0 + 2), col(c0 + 3), col(c0 + 4), col(c0 + 5), tab, tab, tab,
                  pl.BlockSpec(gains.shape, lambda b, i: (0, 0))],
        out_specs=[o[0] for o in outs],
        out_shape=[o[1] for o in outs],
        compiler_params=pltpu.CompilerParams(dimension_semantics=("parallel", "parallel")),
        name="attn_prep",
    )(proj, proj, proj, proj, proj, proj, *tabs, gains)


NSA_NC = 256
CMP_CHUNK = CMP_STRIDE * HEAD_DIM


def _nsa_compress_kernel(kc_ref, vc_ref, pek_ref, pev_ref, wk1_ref, wk2_ref, wv1_ref, wv2_ref, g_ref, ko_ref, vo_ref):
    def compress(x_ref, pe_ref, w1_ref, w2_ref):
        x = x_ref[0]
        top = _dot_lo(x + pe_ref[0:1], w1_ref[0:CMP_CHUNK, :])
        bot = _dot_lo(x + pe_ref[1:2], w1_ref[CMP_CHUNK:, :])
        hid = top + pltpu.roll(bot, NSA_NC - 1, axis=0)
        return _dot_lo(jax.nn.gelu(hid), w2_ref[...])

    k = compress(kc_ref, pek_ref, wk1_ref, wk2_ref)
    ko_ref[0] = k * lax.rsqrt(jnp.mean(k * k, axis=-1, keepdims=True) + NORM_EPS) * g_ref[...]
    vo_ref[0] = compress(vc_ref, pev_ref, wv1_ref, wv2_ref)


def nsa_compress_pallas(kc, vc, pe_k, pe_v, wk1, wk2, wv1, wv2, kc_g):
    B, S, hd = kc.shape
    assert S // CMP_STRIDE == NSA_NC
    chunks = lambda t: t.reshape(B, NSA_NC, CMP_CHUNK)
    pe2 = lambda p: p.reshape(2, CMP_CHUNK)
    const = lambda t: pl.BlockSpec(t.shape, lambda b: (0,) * t.ndim)
    x_spec = pl.BlockSpec((1, NSA_NC, CMP_CHUNK), lambda b: (b, 0, 0))
    o_spec = pl.BlockSpec((1, NSA_NC, hd), lambda b: (b, 0, 0))
    params = [pe2(pe_k), pe2(pe_v), wk1.astype(_BF16), wk2.astype(_BF16), wv1.astype(_BF16), wv2.astype(_BF16),
              kc_g.reshape(1, hd)]
    return pl.pallas_call(
        _nsa_compress_kernel,
        grid=(B,),
        in_specs=[x_spec, x_spec] + [const(p) for p in params],
        out_specs=[o_spec, o_spec],
        out_shape=[jax.ShapeDtypeStruct((B, NSA_NC, hd), _F32)] * 2,
        name="nsa_compress",
    )(chunks(kc), chunks(vc), *params)


def selection_overlap(n_c, n_sel):
    r = SEL_BLK // CMP_STRIDE
    m = CMP_LEN // CMP_STRIDE
    diff = np.arange(n_c)[:, None] - r * np.arange(n_sel)[None, :]
    offs = (np.arange(r)[:, None] - np.arange(m)[None, :]).reshape(-1)
    return (diff[..., None] == offs).sum(-1).astype(np.float32)


def _nsa_select_kernel(q_ref, kc_ref, vc_ref, ovt_ref, o_ref, selt_ref):
    T = ATT_T
    qi = pl.program_id(1)
    kc = kc_ref[0].astype(_BF16)
    vc = vc_ref[0].astype(_BF16)
    cblk = lax.broadcasted_iota(jnp.int32, (NSA_NC, T), 0)
    tq = qi * T + lax.broadcasted_iota(jnp.int32, (NSA_NC, T), 1)
    valid = cblk * CMP_STRIDE + (CMP_LEN - 1) <= tq
    any_valid = (tq[0:1, :] >= CMP_LEN - 1).astype(_F32)
    heads = range(N_HEADS)
    ss = [lax.dot_general(kc, q_ref[0, :, h * HEAD_DIM:(h + 1) * HEAD_DIM], _NT, preferred_element_type=_F32)
          for h in heads]
    ss = [jnp.where(valid, s, NEG) for s in ss]
    es = [jnp.exp(s - jnp.max(s, axis=0, keepdims=True)) for s in ss]
    ps = [e * (any_valid / jnp.sum(e, axis=0, keepdims=True)) for e in es]
    outs = [lax.dot_general(vc, p.astype(_BF16), _TN, preferred_element_type=_F32) for p in ps]
    p_sum = sum(ps[1:], ps[0])
    o_ref[0] = _heads_to_rows(jnp.concatenate(outs, axis=1))

    ovt = ovt_ref[...]
    hi = p_sum.astype(_BF16)
    lo = (p_sum - hi.astype(_F32)).astype(_BF16)
    imp = jnp.dot(ovt, hi, preferred_element_type=_F32) + jnp.dot(ovt, lo, preferred_element_type=_F32)
    jb = lax.broadcasted_iota(jnp.int32, (NSA_NSEL, T), 0)
    cur = (qi * T + lax.broadcasted_iota(jnp.int32, (NSA_NSEL, T), 1)) // SEL_BLK
    forced = (jb == 0) | (jb == cur) | (jb == cur - 1)
    score = jnp.where(jb > cur, -1.0, jnp.where(forced, FORCE_SCORE, imp))
    jbf = jb.astype(_F32)
    sel = jnp.zeros((NSA_NSEL, T), _F32)
    for _ in range(min(SEL_TOP, NSA_NSEL)):
        best = jnp.max(score, axis=0, keepdims=True)
        first = jnp.min(jnp.where(score == best, jbf, float(NSA_NSEL)), axis=0, keepdims=True)
        pick = jbf == first
        sel = jnp.where(pick, 1.0, sel)
        score = jnp.where(pick, NEG, score)
    selt_ref[0] = sel.astype(_BF16)


def nsa_select_pallas(q_n, k_cmp, v_cmp):
    B, S, W = q_n.shape
    T = ATT_T
    n_c = (S - CMP_LEN) // CMP_STRIDE + 1
    ovt = np.zeros((NSA_NSEL, NSA_NC), np.float32)
    ovt[:, :n_c] = selection_overlap(n_c, NSA_NSEL).T
    ovt = jnp.asarray(ovt, _BF16)
    c_spec = pl.BlockSpec((1, NSA_NC, HEAD_DIM), lambda b, i: (b, 0, 0))
    return pl.pallas_call(
        _nsa_select_kernel,
        grid=(B, S // T),
        in_specs=[pl.BlockSpec((1, T, W), lambda b, i: (b, i, 0)), c_spec, c_spec,
                  pl.BlockSpec(ovt.shape, lambda b, i: (0, 0))],
        out_specs=[pl.BlockSpec((1, T, W), lambda b, i: (b, i, 0)),
                   pl.BlockSpec((1, NSA_NSEL, T), lambda b, i: (b, 0, i))],
        out_shape=[jax.ShapeDtypeStruct((B, S, W), _F32), jax.ShapeDtypeStruct((B, NSA_NSEL, S), _BF16)],
        compiler_params=pltpu.CompilerParams(dimension_semantics=("parallel", "parallel")),
        name="nsa_select",
    )(q_n, k_cmp, v_cmp, ovt)


def kernel(x, c, positions, w_ada, b_ada, norm1_g, norm2_g, w_in, w_out, rwkv_mu, rwkv_w0, rwkv_w2, rwkv_a0, rwkv_a2, rwkv_g2, rwkv_kk, rwkv_ka, rwkv_rk, rwkv_gn_w, rwkv_gn_b, conv_w, dil_q_g, dil_k_g, nsa_q_g, nsa_kc_g, nsa_ks_g, nsa_kw_g, nsa_pe_k, nsa_pe_v, nsa_wk1, nsa_wk2, nsa_wv1, nsa_wv2, onorm_g, w_router, b_router, w_gu, b_gu, w_down, b_down):
    B, S, D = x.shape
    tabs = rope_tables(positions)
    for l in range(DEPTH):
        mod = matmul(jax.nn.silu(c), w_ada[l], tm=B, tn=512) + b_ada[l]
        sh1, sc1, gt1, sh2, sc2, gt2 = [m[:, None, :] for m in jnp.split(mod, 6, axis=-1)]
        proj = in_proj_pallas(x, norm1_g[l], sc1, sh1, w_in[l])
        y_a = rwkv7_pallas(proj, rwkv_mu[l], rwkv_w0[l], rwkv_w2[l], rwkv_a0[l], rwkv_a2[l], rwkv_g2[l],
                           rwkv_kk[l], rwkv_ka[l], rwkv_rk[l], rwkv_gn_w[l], rwkv_gn_b[l])
        cq, ck, cv_t, q_n, q_r, kc, vc, ks, vs_t, kw, vw_t = attn_prep_pallas(
            proj, tabs, dil_q_g[l], dil_k_g[l], nsa_q_g[l], nsa_ks_g[l], nsa_kw_g[l])
        y_c = dil_attn_pallas(cq, ck, cv_t)
        k_cmp, v_cmp = nsa_compress_pallas(kc, vc, nsa_pe_k[l], nsa_pe_v[l], nsa_wk1[l], nsa_wk2[l], nsa_wv1[l],
                                           nsa_wv2[l], nsa_kc_g[l])
        o_cmp, sel_t = nsa_select_pallas(q_n, k_cmp, v_cmp)
        o_slc, o_swa = nsa_attn_pallas(q_r, ks, vs_t, kw, vw_t, sel_t)
        x, hb, logits = mix_out_pallas(x, y_a, proj, y_c, o_cmp, o_slc, o_swa, conv_w[l], onorm_g[l], w_out[l], gt1,
                                       norm2_g[l], sc2, sh2, w_router[l], b_router[l])
        hb, logits = hb.reshape(MOE_GROUPS, -1, D), logits.reshape(MOE_GROUPS, -1, ROUTER_PAD)
        y = jnp.stack([moe_pallas(hb[g], logits[g], l, w_gu, b_gu, w_down, b_down) for g in range(MOE_GROUPS)])
        x = x + gt2 * y.reshape(B, S, D)
    return x
```

```python
import functools

import numpy as np
import jax
import jax.numpy as jnp
from jax import lax
from jax.experimental import pallas as pl
from jax.experimental.pallas import tpu as pltpu

D_MODEL = 1024
DEPTH = 2

HEAD_DIM = 64
N_MIXERS = 4
GROUP_W = D_MODEL // N_MIXERS
N_HEADS = GROUP_W // HEAD_DIM
MIX_W = N_MIXERS * GROUP_W

RWKV_DECAY_RANK = 64
RWKV_AAA_RANK = 64
RWKV_GATE_RANK = 128
RWKV_DECAY_SCALE = 0.606531
RWKV_GN_EPS = 64e-5

CONV_W = 3

DIL_CFG = ((128, 1), (512, 4), (2048, 16))

CMP_LEN = 32
CMP_STRIDE = 16
CMP_HIDDEN = 256
SEL_BLK = 64
SEL_TOP = 16
SWA_WIN = 512
FORCE_SCORE = 1e4

N_EXPERTS = 32
TOP_K = 4
EXPERT_FF = D_MODEL
SWIGLU_LIMIT = 7.0
SWIGLU_ALPHA = 1.702

ROPE_THETA = 10000.0
NORM_EPS = 1e-6
NEG = -1e30

A_WIDTHS = (GROUP_W, GROUP_W, GROUP_W, RWKV_DECAY_RANK, RWKV_AAA_RANK, RWKV_GATE_RANK)
B_WIDTHS = (GROUP_W, GROUP_W, GROUP_W)
C_WIDTHS = (GROUP_W, GROUP_W, GROUP_W)
D_WIDTHS = (GROUP_W, HEAD_DIM, HEAD_DIM, HEAD_DIM, HEAD_DIM, HEAD_DIM, HEAD_DIM, 3 * N_HEADS)
SECTION_WIDTHS = (sum(A_WIDTHS), sum(B_WIDTHS), sum(C_WIDTHS), sum(D_WIDTHS))
PROJ_W = sum(SECTION_WIDTHS)

_F32 = jnp.float32
_BF16 = jnp.bfloat16
_NT = (((1,), (1,)), ((), ()))
_TN = (((0,), (0,)), ((), ()))


def _dot_hi(a, b, dims=None):
    if dims is None:
        return jnp.dot(a, b, precision=lax.Precision.HIGHEST, preferred_element_type=_F32)
    return lax.dot_general(a, b, dims, precision=lax.Precision.HIGHEST, preferred_element_type=_F32)


def _dot_lo(a, b, dims=None):
    a, b = a.astype(_BF16), b.astype(_BF16)
    if dims is None:
        return jnp.dot(a, b, preferred_element_type=_F32)
    return lax.dot_general(a, b, dims, preferred_element_type=_F32)


def _bf16_parts(a, parts):
    out = []
    for _ in range(parts):
        p = a.astype(_BF16)
        out.append(p)
        a = a - p.astype(_F32)
    return out


def _dot_split(a, m, parts=2):
    mb = m.astype(_BF16)
    return sum(jnp.dot(p, mb, preferred_element_type=_F32) for p in _bf16_parts(a, parts))


def _dot_split_left(m, a, parts=2):
    mb = m.astype(_BF16)
    return sum(jnp.dot(mb, p, preferred_element_type=_F32) for p in _bf16_parts(a, parts))


def _dot_x3(a, b):
    ah, al = _bf16_parts(a, 2)
    bh, bl = _bf16_parts(b, 2)
    d = lambda x, y: jnp.dot(x, y, preferred_element_type=_F32)
    return d(ah, bh) + (d(ah, bl) + d(al, bh))


def _matmul_kernel(x_ref, w_ref, o_ref, *, exact):
    if exact:
        o_ref[...] = _dot_hi(x_ref[...], w_ref[...])
    else:
        o_ref[...] = jnp.dot(x_ref[...].astype(_BF16), w_ref[...], preferred_element_type=_F32)


def matmul(x, w, tm=512, tn=256, exact=False):
    M, K = x.shape
    N = w.shape[1]
    n_pad = -(-N // tn) * tn
    wb = jnp.pad(w if exact else w.astype(_BF16), ((0, 0), (0, n_pad - N)))
    out = pl.pallas_call(
        functools.partial(_matmul_kernel, exact=exact),
        grid=(M // tm, n_pad // tn),
        in_specs=[pl.BlockSpec((tm, K), lambda i, j: (i, 0)),
                  pl.BlockSpec((K, tn), lambda i, j: (0, j))],
        out_specs=pl.BlockSpec((tm, tn), lambda i, j: (i, j)),
        out_shape=jax.ShapeDtypeStruct((M, n_pad), _F32),
        name="matmul",
    )(x, wb)
    return out[:, :N]


PROJ_PAD = -(-PROJ_W // 256) * 256
PROJ_TM = 256
PROJ_TN = 512
ROUTER_PAD = 128
GATE_COL_BLOCK = (PROJ_W - 3 * N_HEADS) // 128
assert GATE_COL_BLOCK * 128 == PROJ_W - 3 * N_HEADS


def _norm_mod(x, g, sc, sh):
    y = x * lax.rsqrt(jnp.mean(x * x, axis=-1, keepdims=True) + NORM_EPS) * g
    return y * (1.0 + sc) + sh


def _in_proj_kernel(x_ref, g_ref, sc_ref, sh_ref, w_ref, o_ref):
    h = _norm_mod(x_ref[0], g_ref[...], sc_ref[0], sh_ref[0]).astype(_BF16)
    for n0 in range(0, PROJ_PAD, PROJ_TN):
        o_ref[0, :, n0:n0 + PROJ_TN] = jnp.dot(h, w_ref[:, n0:n0 + PROJ_TN], preferred_element_type=_F32)


def in_proj_pallas(x, g, sc, sh, w_in):
    B, S, D = x.shape
    tm = PROJ_TM
    w = jnp.pad(w_in.astype(_BF16), ((0, 0), (0, PROJ_PAD - PROJ_W)))
    per_batch = pl.BlockSpec((1, 1, D), lambda b, i: (b, 0, 0))
    return pl.pallas_call(
        _in_proj_kernel,
        grid=(B, S // tm),
        in_specs=[pl.BlockSpec((1, tm, D), lambda b, i: (b, i, 0)),
                  pl.BlockSpec((1, D), lambda b, i: (0, 0)), per_batch, per_batch,
                  pl.BlockSpec((D, PROJ_PAD), lambda b, i: (0, 0))],
        out_specs=pl.BlockSpec((1, tm, PROJ_PAD), lambda b, i: (b, i, 0)),
        out_shape=jax.ShapeDtypeStruct((B, S, PROJ_PAD), _F32),
        compiler_params=pltpu.CompilerParams(dimension_semantics=("parallel", "parallel"),
                                             vmem_limit_bytes=48 * 1024 * 1024),
        name="in_proj",
    )(x, g.reshape(1, D), sc, sh, w)


def _mix_out_kernel(x_ref, ya_ref, bg_ref, cg_ref, xin_ref, cgh_ref, xinh_ref, do0_ref, dl0_ref, do1_ref, dl1_ref,
                    do2_ref, dl2_ref, ocmp_ref, oslc_ref, oswa_ref,
                    gl_ref, convw_ref, ong_ref, wout_ref, gt1_ref, g2_ref, sc2_ref, sh2_ref, wr_ref, br_ref,
                    xo_ref, hb_ref, lg_ref):
    i = pl.program_id(1)
    G = GROUP_W
    u = cg_ref[0] * xin_ref[0]
    halo = jnp.where(i == 0, 0.0, cgh_ref[0] * xinh_ref[0])
    row = lax.broadcasted_iota(jnp.int32, u.shape, 0)
    u1 = jnp.where(row == 0, halo[7:8], pltpu.roll(u, 1, axis=0))
    u2 = jnp.where(row == 0, halo[6:7], jnp.where(row == 1, halo[7:8], pltpu.roll(u, 2, axis=0)))
    cw = convw_ref[...]
    yb = bg_ref[0] * (cw[0:1] * u2 + cw[1:2] * u1 + cw[2:3] * u)

    sg = jax.nn.sigmoid(gl_ref[0])
    er = lax.broadcasted_iota(jnp.int32, (128, G), 0)
    ec = lax.broadcasted_iota(jnp.int32, (128, G), 1) // HEAD_DIM
    yd = None
    for j, o_ref in enumerate((ocmp_ref, oslc_ref, oswa_ref)):
        gate = _dot_split(sg, er == 3 * ec + j)
        yd = gate * o_ref[0] if yd is None else yd + gate * o_ref[0]

    head_sum = _group_indicator(G, HEAD_DIM)
    ong = ong_ref[...]
    parts = [ya_ref[0]]
    dil = ((do0_ref, dl0_ref), (do1_ref, dl1_ref), (do2_ref, dl2_ref))
    top = functools.reduce(jnp.maximum, [l_ref[0] for _, l_ref in dil])
    wts = [jnp.exp2(l_ref[0] - top) for _, l_ref in dil]
    yc = sum(w * o_ref[0] for w, (o_ref, _) in zip(wts, dil)) / sum(wts)
    for n, y in enumerate((yb, yc, yd)):
        ms = _dot_split(y * y, head_sum) * (1.0 / HEAD_DIM)
        parts.append(y * lax.rsqrt(ms + NORM_EPS) * ong[:, n * G:(n + 1) * G])
    mixed = jnp.dot(jnp.concatenate(parts, axis=1).astype(_BF16), wout_ref[...], preferred_element_type=_F32)
    x = x_ref[0] + gt1_ref[0] * mixed
    xo_ref[0] = x
    h = _norm_mod(x, g2_ref[...], sc2_ref[0], sh2_ref[0])
    hb_ref[0] = h.astype(_BF16)
    lg_ref[0] = _dot_x3(h, wr_ref[...]) + br_ref[...]


def mix_out_pallas(x, y_a, proj, dil_outs, o_cmp, o_slc, o_swa, conv_w, onorm_g, w_out, gt1, g2, sc2, sh2, w_router, b_router):
    B, S, D = x.shape
    tm = PROJ_TM
    G = GROUP_W
    b0 = SECTION_WIDTHS[0] // G
    tile = lambda w, col: pl.BlockSpec((1, tm, w), lambda b, i: (b, i, col))
    halo = lambda col: pl.BlockSpec((1, 8, G), lambda b, i: (b, jnp.maximum(i * (tm // 8) - 1, 0), col))
    const = lambda t: pl.BlockSpec(t.shape, lambda b, i: (0,) * t.ndim)
    per_batch = pl.BlockSpec((1, 1, D), lambda b, i: (b, 0, 0))
    wr = jnp.pad(w_router, ((0, 0), (0, ROUTER_PAD - N_EXPERTS)))
    br = jnp.pad(b_router, (0, ROUTER_PAD - N_EXPERTS)).reshape(1, ROUTER_PAD)
    consts = [conv_w, onorm_g.reshape(1, 3 * G), w_out.astype(_BF16)]
    return pl.pallas_call(
        _mix_out_kernel,
        grid=(B, S // tm),
        in_specs=[tile(D, 0), tile(G, 0), tile(G, b0), tile(G, b0 + 1), tile(G, b0 + 2), halo(b0 + 1), halo(b0 + 2),
                  *([tile(G, 0)] * (2 * len(dil_outs) + 3)), tile(128, GATE_COL_BLOCK)]
                 + [const(t) for t in consts] + [per_batch, pl.BlockSpec((1, D), lambda b, i: (0, 0)), per_batch,
                                                 per_batch, const(wr), const(br)],
        out_specs=[tile(D, 0), tile(D, 0), tile(ROUTER_PAD, 0)],
        out_shape=[jax.ShapeDtypeStruct((B, S, D), _F32), jax.ShapeDtypeStruct((B, S, D), _BF16),
                   jax.ShapeDtypeStruct((B, S, ROUTER_PAD), _F32)],
        compiler_params=pltpu.CompilerParams(dimension_semantics=("parallel", "parallel")),
        name="mix_out",
    )(x, y_a, proj, proj, proj, proj, proj, *[t for pair in dil_outs for t in pair], o_cmp, o_slc, o_swa, proj, *consts, gt1, g2.reshape(1, D), sc2, sh2,
      wr, br)


RWKV_CHUNK = 64
RWKV_TILE = 256


def _group_indicator(n, group):
    r = lax.broadcasted_iota(jnp.int32, (n, n), 0) // group
    c = lax.broadcasted_iota(jnp.int32, (n, n), 1) // group
    return r == c


def _rwkv_kernel(za_ref, mu_ref, w0_ref, w2_ref, a0_ref, a2_ref, g2_ref, kk_ref, ka_ref, rk_ref,
                 gnw_ref, gnb_ref, o_ref, prev_sc, h_sc):
    C = RWKV_CHUNK
    TT = RWKV_TILE
    c = pl.program_id(1)

    @pl.when(c == 0)
    def _():
        prev_sc[...] = jnp.zeros_like(prev_sc)
        h_sc[...] = jnp.zeros_like(h_sc)

    z = za_ref[0]
    row = lax.broadcasted_iota(jnp.int32, z.shape, 0)
    zs = jnp.where(row == 0, prev_sc[...], pltpu.roll(z, 1, axis=0))
    prev_sc[...] = z[TT - 1:TT, :]
    z = z + (zs - z) * mu_ref[...]

    G = GROUP_W
    r, k, v = z[:, 0:G], z[:, G:2 * G], z[:, 2 * G:3 * G]
    o = 3 * G
    wd = z[:, o:o + RWKV_DECAY_RANK]
    ad = z[:, o + RWKV_DECAY_RANK:o + RWKV_DECAY_RANK + RWKV_AAA_RANK]
    gd = z[:, o + RWKV_DECAY_RANK + RWKV_AAA_RANK:]

    lw = -RWKV_DECAY_SCALE * jax.nn.sigmoid(w0_ref[...] + _dot_lo(jnp.tanh(wd), w2_ref[...]))
    a = jax.nn.sigmoid(a0_ref[...] + _dot_lo(ad, a2_ref[...]))
    g = _dot_lo(jax.nn.sigmoid(gd), g2_ref[...])

    head_sum = _group_indicator(G, HEAD_DIM)
    kk = k * kk_ref[...]
    kk = kk * lax.rsqrt(_dot_split(kk * kk, head_sum) + 1e-12)
    k = k * (1.0 + (a - 1.0) * ka_ref[...])
    b = kk * a

    ti = lax.broadcasted_iota(jnp.int32, (C, C), 0)
    tj = lax.broadcasted_iota(jnp.int32, (C, C), 1)
    incl = ti >= tj
    strict = ti > tj
    eye = ti == tj
    blk16 = (ti // 16) == (tj // 16)
    blk32 = (ti // 32) == (tj // 32)
    eye_f = eye.astype(_F32)

    ri = lax.broadcasted_iota(jnp.int32, (TT, TT), 0)
    rj = lax.broadcasted_iota(jnp.int32, (TT, TT), 1)
    chunk_tri = (ri >= rj) & (ri // C == rj // C)
    cum = _dot_split_left(chunk_tri, lw, parts=3)
    g_in = jnp.exp(cum)
    A_all = -kk * jnp.exp(cum - lw)
    R_all = r * g_in
    g_inv = jnp.exp(-cum)
    B_all = b * g_inv
    K_all = k * g_inv

    units = [(ci, h) for ci in range(TT // C) for h in range(N_HEADS)]

    def part(t, u):
        ci, h = u
        return t[ci * C:(ci + 1) * C, h * HEAD_DIM:(h + 1) * HEAD_DIM]

    def g_end(u):
        ci, h = u
        return g_in[(ci + 1) * C - 1:(ci + 1) * C, h * HEAD_DIM:(h + 1) * HEAD_DIM]

    A = [part(A_all, u) for u in units]
    R = [part(R_all, u) for u in units]
    B = [part(B_all, u) for u in units]
    Kt = [part(K_all, u) for u in units]
    V = [part(v, u) for u in units]
    n = range(len(units))
    gram = [_dot_lo(jnp.concatenate([A[i], R[i]], axis=0), jnp.concatenate([B[i], Kt[i]], axis=0), _NT) for i in n]
    l_ab = [jnp.where(strict, gram[i][0:C, 0:C], 0.0) for i in n]
    l_ak = [jnp.where(strict, gram[i][0:C, C:2 * C], 0.0) for i in n]
    m_rb = [jnp.where(incl, gram[i][C:2 * C, 0:C], 0.0) for i in n]
    m_rk = [jnp.where(incl, gram[i][C:2 * C, C:2 * C], 0.0) for i in n]
    p = [jnp.where(blk16, l_ab[i], 0.0) for i in n]
    x = [eye_f + p[i] for i in n]
    for _ in range(3):
        p = [_dot_lo(p[i], p[i]) for i in n]
        x = [_dot_lo(x[i], eye_f + p[i]) for i in n]
    for lvl in (blk32 & ~blk16, ~blk32):
        xl = [_dot_lo(x[i], jnp.where(lvl, l_ab[i], 0.0)) for i in n]
        x = [x[i] + _dot_lo(xl[i], x[i]) for i in n]
    lv = [_dot_lo(l_ak[i], V[i]) for i in n]
    tap = [_dot_lo(x[i], jnp.concatenate([A[i], lv[i]], axis=1)) for i in n]
    m1 = [_dot_lo(m_rb[i], tap[i]) for i in n]
    mv = [_dot_lo(m_rk[i], V[i]) for i in n]
    bt = [_dot_lo(B[i] * g_end(units[i]), tap[i], _TN) for i in n]
    kv = [_dot_lo(Kt[i] * g_end(units[i]), V[i], _TN) for i in n]
    w_yh = [jnp.concatenate([R[i] + m1[i][:, 0:HEAD_DIM],
                             jnp.where(eye, g_end(units[i]), 0.0) + bt[i][:, 0:HEAD_DIM]], axis=0) for i in n]
    y0 = [m1[i][:, HEAD_DIM:] + mv[i] for i in n]
    h_add = [bt[i][:, HEAD_DIM:] + kv[i] for i in n]
    state = [h_sc[h] for h in range(N_HEADS)]
    y_rows = []
    for ci in range(TT // C):
        ys = []
        for h in range(N_HEADS):
            i = ci * N_HEADS + h
            nxt = _dot_x3(w_yh[i], state[h])
            ys.append(nxt[0:C] + y0[i])
            state[h] = nxt[C:2 * C] + h_add[i]
        y_rows.append(jnp.concatenate(ys, axis=1))
    for h in range(N_HEADS):
        h_sc[h] = state[h]
    y = jnp.concatenate(y_rows, axis=0)

    mu = _dot_split(y, head_sum) * (1.0 / HEAD_DIM)
    d = y - mu
    var = _dot_split(d * d, head_sum) * (1.0 / HEAD_DIM)
    y = d * lax.rsqrt(var + RWKV_GN_EPS) * gnw_ref[...] + gnb_ref[...]
    y = y + _dot_split(r * k * rk_ref[...], head_sum) * v
    o_ref[0] = y * g


def rwkv7_pallas(za, mu, w0, w2, a0, a2, g2, k_k, k_a, r_k, gn_w, gn_b):
    B, S, _ = za.shape
    W = SECTION_WIDTHS[0]
    TT = RWKV_TILE
    row = lambda t: t.reshape(1, -1).astype(_F32)
    full = lambda t: pl.BlockSpec(t.shape, lambda b, c: (0,) * t.ndim)
    params = [row(mu), row(w0), w2.astype(_BF16), row(a0), a2.astype(_BF16), g2.astype(_BF16),
              row(k_k), row(k_a), row(r_k), row(gn_w), row(gn_b)]
    return pl.pallas_call(
        _rwkv_kernel,
        grid=(B, S // TT),
        in_specs=[pl.BlockSpec((1, TT, W), lambda b, c: (b, c, 0))] + [full(p) for p in params],
        out_specs=pl.BlockSpec((1, TT, GROUP_W), lambda b, c: (b, c, 0)),
        out_shape=jax.ShapeDtypeStruct((B, S, GROUP_W), _F32),
        scratch_shapes=[pltpu.VMEM((1, W), _F32), pltpu.VMEM((N_HEADS, HEAD_DIM, HEAD_DIM), _F32)],
        compiler_params=pltpu.CompilerParams(dimension_semantics=("parallel", "arbitrary")),
        name="rwkv7",
    )(za, *params)


MOE_TM = 512
MOE_FC = 512
MOE_CAST_ROWS = 128


def _moe_ffn_kernel(blk_e_ref, n_used_ref, x_ref, wgu_ref, bgu_ref, wd_ref, bd_ref, o_ref, wgu_sc, wd_sc):
    i = pl.program_id(0)
    F = EXPERT_FF
    new_expert = jnp.logical_or(i == 0, blk_e_ref[i] != blk_e_ref[jnp.maximum(i - 1, 0)])

    @pl.when(new_expert)
    def _():
        def cast(j, carry):
            rows = pl.ds(pl.multiple_of(j * MOE_CAST_ROWS, MOE_CAST_ROWS), MOE_CAST_ROWS)
            wgu_sc[rows, :] = wgu_ref[0, rows, :].astype(_BF16)
            wd_sc[rows, :] = wd_ref[0, rows, :].astype(_BF16)
            return carry
        lax.fori_loop(0, F // MOE_CAST_ROWS, cast, 0)

    @pl.when(i < n_used_ref[0])
    def _():
        x = x_ref[...]
        acc = None
        for c in range(F // MOE_FC):
            lo = c * MOE_FC
            gate = jnp.dot(x, wgu_sc[:, lo:lo + MOE_FC], preferred_element_type=_F32) + bgu_ref[0, :, lo:lo + MOE_FC]
            up = jnp.dot(x, wgu_sc[:, F + lo:F + lo + MOE_FC], preferred_element_type=_F32) + bgu_ref[0, :, F + lo:F + lo + MOE_FC]
            gate = jnp.minimum(gate, SWIGLU_LIMIT)
            up = jnp.clip(up, -SWIGLU_LIMIT, SWIGLU_LIMIT)
            act = gate * jax.nn.sigmoid(SWIGLU_ALPHA * gate) * (up + 1.0)
            part = jnp.dot(act.astype(_BF16), wd_sc[lo:lo + MOE_FC, :], preferred_element_type=_F32)
            acc = part if acc is None else acc + part
        o_ref[...] = (acc + bd_ref[0]).astype(o_ref.dtype)

    @pl.when(i >= n_used_ref[0])
    def _():
        o_ref[...] = jnp.zeros_like(o_ref)


def moe_ffn(xs, blk_e, n_used, layer, w_gu, b_gu, w_down, b_down):
    n_rows, D = xs.shape
    L, E, _, F2 = w_gu.shape
    assert D == EXPERT_FF and F2 == 2 * EXPERT_FF
    n_blk = n_rows // MOE_TM
    grid_spec = pltpu.PrefetchScalarGridSpec(
        num_scalar_prefetch=2,
        grid=(n_blk,),
        in_specs=[pl.BlockSpec((MOE_TM, D), lambda i, e, n: (i, 0)),
                  pl.BlockSpec((None, 1, D, F2), lambda i, e, n: (layer, e[i], 0, 0)),
                  pl.BlockSpec((None, 1, 1, F2), lambda i, e, n: (layer, e[i], 0, 0)),
                  pl.BlockSpec((None, 1, F2 // 2, D), lambda i, e, n: (layer, e[i], 0, 0)),
                  pl.BlockSpec((None, 1, 1, D), lambda i, e, n: (layer, e[i], 0, 0))],
        out_specs=pl.BlockSpec((MOE_TM, D), lambda i, e, n: (i, 0)),
        scratch_shapes=[pltpu.VMEM((D, F2), _BF16), pltpu.VMEM((F2 // 2, D), _BF16)],
    )
    return pl.pallas_call(
        _moe_ffn_kernel,
        grid_spec=grid_spec,
        out_shape=jax.ShapeDtypeStruct((n_rows, D), _BF16),
        compiler_params=pltpu.CompilerParams(dimension_semantics=("arbitrary",),
                                             vmem_limit_bytes=52 * 1024 * 1024),
        name="moe_ffn",
    )(blk_e, n_used, xs, w_gu, b_gu.reshape(L, E, 1, F2), w_down, b_down.reshape(L, E, 1, D))


def moe_pallas(x, gt2, hb, logits, layer, w_gu, b_gu, w_down, b_down):
    D = x.shape[-1]
    hb = hb.reshape(-1, D)
    logits = logits.reshape(hb.shape[0], -1)
    T = hb.shape[0]
    TK = T * TOP_K
    TM = MOE_TM
    E = N_EXPERTS
    i32 = jnp.int32
    top_val, top_idx = lax.top_k(logits[:, :E], TOP_K)
    gates = jax.nn.softmax(top_val, axis=-1)
    e_flat = top_idx.T.reshape(TK).astype(i32)
    counts = jnp.sum((e_flat[:, None] == jnp.arange(E, dtype=i32)[None, :]).astype(i32), axis=0)
    need = jnp.repeat((-counts) % TM, TM)
    d_idx = jnp.arange(E * TM, dtype=i32)
    d_key = jnp.where(d_idx % TM < need, d_idx // TM, E)
    n_rows = TK + E * TM
    keys = jnp.concatenate([e_flat, d_key])
    toks = jnp.concatenate([jnp.arange(TK, dtype=i32) % T, jnp.zeros((E * TM,), i32)])
    rows = jnp.arange(n_rows, dtype=i32)
    s_keys, src_tok, s_slot = lax.sort((keys, toks, rows), num_keys=1)
    _, row_of_slot = lax.sort((s_slot, rows), num_keys=1)
    row_of = row_of_slot[:TK]
    blk_e = s_keys[::TM]
    n_used = jnp.sum((blk_e < E).astype(i32)).reshape(1)
    blk_e = jnp.where(blk_e < E, blk_e, blk_e[jnp.maximum(n_used[0] - 1, 0)])
    xs = jnp.take(hb, src_tok, axis=0)
    ybuf = moe_ffn(xs, blk_e, n_used, layer, w_gu, b_gu, w_down, b_down)
    y = jnp.take(ybuf, row_of, axis=0).reshape(TOP_K, T, D)
    return moe_combine(x, y, gates, gt2)


def _moe_combine_kernel(y_ref, g_ref, x_ref, gt2_ref, o_ref):
    g = g_ref[...]
    acc = None
    for k in range(TOP_K):
        term = y_ref[k].astype(_F32) * g[:, k:k + 1]
        acc = term if acc is None else acc + term
    o_ref[0] = x_ref[0] + gt2_ref[0] * acc


def moe_combine(x, y, gates, gt2):
    B, S, D = x.shape
    tm = PROJ_TM
    nt = S // tm
    g = jnp.pad(gates, ((0, 0), (0, 128 - TOP_K)))
    return pl.pallas_call(
        _moe_combine_kernel,
        grid=(B, nt),
        in_specs=[pl.BlockSpec((TOP_K, tm, D), lambda b, i: (0, b * nt + i, 0)),
                  pl.BlockSpec((tm, 128), lambda b, i: (b * nt + i, 0)),
                  pl.BlockSpec((1, tm, D), lambda b, i: (b, i, 0)),
                  pl.BlockSpec((1, 1, D), lambda b, i: (b, 0, 0))],
        out_specs=pl.BlockSpec((1, tm, D), lambda b, i: (b, i, 0)),
        out_shape=jax.ShapeDtypeStruct((B, S, D), _F32),
        compiler_params=pltpu.CompilerParams(dimension_semantics=("parallel", "parallel")),
        name="moe_combine",
    )(y, g, x, gt2)


ATT_T = 128
ATT_TK = 512
ATT_R = ATT_TK // ATT_T
NSA_NSEL = 64


def _window_table(window, weight_of_distance):
    T, TK = ATT_T, ATT_TK
    n = window // T + ATT_R + 1
    d = (np.arange(n)[:, None, None] * T + np.arange(T)[None, None, :] - np.arange(TK)[None, :, None])
    return jnp.asarray(weight_of_distance(d).astype(np.float32))


def _first_key_tile(qi, window):
    return jnp.maximum(qi * ATT_T - window, 0) // ATT_TK


def _swa_weight(d):
    return (d >= 0) & (d <= SWA_WIN - 1)


def _softmax_tile(s, w, m_ref, l_ref):
    s = jnp.where(w > 0.0, s, NEG)
    m_prev = m_ref[...]
    m_new = jnp.maximum(m_prev, jnp.max(s, axis=0, keepdims=True))
    alpha = jnp.exp2(m_prev - m_new)
    p = jnp.exp2(s - m_new)
    l_ref[...] = alpha * l_ref[...] + jnp.sum(p, axis=0, keepdims=True)
    m_ref[...] = m_new
    return alpha, p.astype(_BF16)


def _init_stats(m_sc, l_sc, acc_sc):
    m_sc[...] = jnp.full_like(m_sc, NEG)
    l_sc[...] = jnp.zeros_like(l_sc)
    acc_sc[...] = jnp.zeros_like(acc_sc)


def _heads_to_rows(ot):
    T = ATT_T
    pairs = []
    for h in range(0, N_HEADS, 2):
        two = jnp.concatenate([ot[:, h * T:(h + 1) * T], ot[:, (h + 1) * T:(h + 2) * T]], axis=0)
        pairs.append(two.T)
    return jnp.concatenate(pairs, axis=1)


def _nsa_attn_kernel(q_ref, ks_ref, vst_ref, kw_ref, vwt_ref, selt_ref, swa_ref, oslc_ref, oswa_ref,
                     m_sc, l_sc, acc_sc):
    T, TK = ATT_T, ATT_TK
    qi = pl.program_id(1)
    q = q_ref[0]
    q = jnp.concatenate([q[:, h * HEAD_DIM:(h + 1) * HEAD_DIM] for h in range(N_HEADS)], axis=0)
    selt = selt_ref[0]
    key_blk = lax.broadcasted_iota(jnp.int32, (TK, NSA_NSEL), 0) // SEL_BLK
    blk_id = lax.broadcasted_iota(jnp.int32, (TK, NSA_NSEL), 1)
    kr = lax.broadcasted_iota(jnp.int32, (TK, T), 0)
    qc = lax.broadcasted_iota(jnp.int32, (TK, T), 1)

    def step(k_ref, vt_ref, kt, w):
        base = pl.multiple_of(kt * TK, TK)
        s = lax.dot_general(k_ref[0, pl.ds(base, TK), :], q, _NT, preferred_element_type=_F32)
        alpha, p = _softmax_tile(s, jnp.concatenate([w] * N_HEADS, axis=1), m_sc, l_sc)
        acc_sc[...] = alpha * acc_sc[...] + jnp.dot(vt_ref[0, :, pl.ds(base, TK)], p, preferred_element_type=_F32)

    def finish(o_ref):
        o_ref[0] = _heads_to_rows(acc_sc[...] / l_sc[...])

    last = qi // ATT_R
    _init_stats(m_sc, l_sc, acc_sc)

    def slc_step(kt, carry):
        in_blk = (key_blk + kt * (TK // SEL_BLK) == blk_id).astype(_BF16)
        w = jnp.dot(in_blk, selt, preferred_element_type=_F32)
        w = jnp.where(kt * TK + kr <= qi * T + qc, w, 0.0)
        step(ks_ref, vst_ref, kt, w)
        return carry

    lax.fori_loop(0, last + 1, slc_step, 0)
    finish(oslc_ref)

    _init_stats(m_sc, l_sc, acc_sc)

    def swa_step(kt, carry):
        step(kw_ref, vwt_ref, kt, swa_ref[qi - kt * ATT_R])
        return carry

    lax.fori_loop(_first_key_tile(qi, SWA_WIN - 1), last + 1, swa_step, 0)
    finish(oswa_ref)


def nsa_attn_pallas(q_r, ks_r, vs_t, kw_r, vw_t, sel_t):
    B, S, W = q_r.shape
    assert S // SEL_BLK == NSA_NSEL
    T = ATT_T
    R = N_HEADS * T
    swa_tab = _window_table(SWA_WIN - 1, _swa_weight)
    k_spec = pl.BlockSpec((1, S, HEAD_DIM), lambda b, i: (b, 0, 0))
    vt_spec = pl.BlockSpec((1, HEAD_DIM, S), lambda b, i: (b, 0, 0))
    q_spec = pl.BlockSpec((1, T, W), lambda b, i: (b, i, 0))
    return pl.pallas_call(
        _nsa_attn_kernel,
        grid=(B, S // T),
        in_specs=[q_spec, k_spec, vt_spec, k_spec, vt_spec,
                  pl.BlockSpec((1, NSA_NSEL, T), lambda b, i: (b, 0, i)),
                  pl.BlockSpec(swa_tab.shape, lambda b, i: (0, 0, 0))],
        out_specs=[q_spec, q_spec],
        out_shape=[jax.ShapeDtypeStruct((B, S, W), _F32)] * 2,
        scratch_shapes=[pltpu.VMEM((1, R), _F32), pltpu.VMEM((1, R), _F32), pltpu.VMEM((HEAD_DIM, R), _F32)],
        compiler_params=pltpu.CompilerParams(dimension_semantics=("parallel", "arbitrary")),
        name="nsa_attn",
    )(q_r, ks_r, vs_t, kw_r, vw_t, sel_t, swa_tab)


DIL_TQ = 256
DIL_WIN = DIL_CFG[0][0] // DIL_CFG[0][1]
assert all(w // d == DIL_WIN for w, d in DIL_CFG) and DIL_WIN == ATT_T and DIL_TQ == 2 * ATT_T


def _dil_attn_kernel(q_ref, kp_ref, k0_ref, k1_ref, vp_ref, v0_ref, v1_ref, o_ref, lse_ref):
    TQ, W = DIL_TQ, DIL_WIN
    i = pl.program_id(1)
    k = jnp.concatenate([kp_ref[0], k0_ref[0], k1_ref[0]], axis=0)
    v = jnp.concatenate([vp_ref[0], v0_ref[0], v1_ref[0]], axis=0)
    row = lax.broadcasted_iota(jnp.int32, (W + TQ, TQ), 0)
    col = lax.broadcasted_iota(jnp.int32, (W + TQ, TQ), 1)
    dist = W + col - row
    valid = (dist >= 0) & (dist <= W) & ((i > 0) | (row >= W))
    heads = [slice(h * HEAD_DIM, (h + 1) * HEAD_DIM) for h in range(N_HEADS)]
    ss = [lax.dot_general(k[:, hs], q_ref[0, :, hs], _NT, preferred_element_type=_F32) for hs in heads]
    ss = [jnp.where(valid, s, NEG) for s in ss]
    ms = [jnp.max(s, axis=0, keepdims=True) for s in ss]
    ps = [jnp.exp2(s - m) for s, m in zip(ss, ms)]
    ls = [jnp.sum(p, axis=0, keepdims=True) for p in ps]
    os_ = [lax.dot_general(v[:, hs], p.astype(_BF16), _TN, preferred_element_type=_F32) for hs, p in zip(heads, ps)]
    outs = [o / l for o, l in zip(os_, ls)]
    lses = [jnp.broadcast_to(m + jnp.log2(l), (HEAD_DIM, TQ)) for m, l in zip(ms, ls)]
    for t, ref in ((outs, o_ref), (lses, lse_ref)):
        for half in range(TQ // ATT_T):
            cols = slice(half * ATT_T, (half + 1) * ATT_T)
            ref[0, cols, :] = _heads_to_rows(jnp.concatenate([x[:, cols] for x in t], axis=1))


def dil_attn_pallas(q_r, k_r, v, dil):
    B, S, W = q_r.shape
    Sd = S // dil
    TQ, T = DIL_TQ, ATT_T
    R = TQ // T
    view = lambda t: t.reshape(B, Sd, dil * W)
    q_spec = pl.BlockSpec((1, TQ, W), lambda b, i, r: (b, i, r))
    prev = pl.BlockSpec((1, T, W), lambda b, i, r: (b, jnp.maximum(i * R - 1, 0), r))
    cur = [pl.BlockSpec((1, T, W), functools.partial(lambda b, i, r, j: (b, i * R + j, r), j=j)) for j in range(R)]
    o, lse = pl.pallas_call(
        _dil_attn_kernel,
        grid=(B, Sd // TQ, dil),
        in_specs=[q_spec, prev, *cur, prev, *cur],
        out_specs=[q_spec, q_spec],
        out_shape=[jax.ShapeDtypeStruct((B, Sd, dil * W), _F32)] * 2,
        compiler_params=pltpu.CompilerParams(dimension_semantics=("parallel", "parallel", "parallel")),
        name="dil_attn",
    )(view(q_r), *([view(k_r)] * (R + 1)), *([view(v)] * (R + 1)))
    return o.reshape(B, S, W), lse.reshape(B, S, W)


LOG2_E = 1.4426950408889634
ROPE_LANES = 128
PREP_TM = 256


def _rope_table_kernel(pos_ref, inv_ref, cos_ref, slo_ref, shi_ref):
    ang = pos_ref[...].astype(_F32) * inv_ref[...]
    cos, sin = jnp.cos(ang), jnp.sin(ang)
    first_half = (lax.broadcasted_iota(jnp.int32, ang.shape, 1) % HEAD_DIM) < HEAD_DIM // 2
    cos_ref[...] = cos
    slo_ref[...] = jnp.where(first_half, -sin, 0.0)
    shi_ref[...] = jnp.where(first_half, 0.0, sin)


def rope_tables(positions):
    B, S = positions.shape
    T = B * S
    half = HEAD_DIM // 2
    inv = ROPE_THETA ** (-jnp.arange(half, dtype=_F32) / half)
    inv = jnp.tile(inv, ROPE_LANES // half).reshape(1, ROPE_LANES)
    pos = jnp.broadcast_to(positions.reshape(T, 1), (T, ROPE_LANES))
    tm = 1024
    spec = pl.BlockSpec((tm, ROPE_LANES), lambda i: (i, 0))
    return pl.pallas_call(
        _rope_table_kernel,
        grid=(T // tm,),
        in_specs=[spec, pl.BlockSpec((1, ROPE_LANES), lambda i: (0, 0))],
        out_specs=[spec] * 3,
        out_shape=[jax.ShapeDtypeStruct((T, ROPE_LANES), _F32)] * 3,
        name="rope_tables",
    )(pos, inv)


def _attn_prep_kernel(cq_ref, ck_ref, cv_ref, dq_ref, d1_ref, d2_ref, cos_ref, slo_ref, shi_ref, gains_ref,
                      cqo_ref, cko_ref, cvo_ref, qn_ref, qr_ref, kc_ref, vc_ref, ks_ref, vst_ref, kw_ref, vwt_ref):
    G, hd = GROUP_W, HEAD_DIM
    wide = lambda t: jnp.concatenate([t] * (G // ROPE_LANES), axis=1)
    cos, slo, shi = wide(cos_ref[...]), wide(slo_ref[...]), wide(shi_ref[...])
    head_sum = _group_indicator(G, hd)
    gains = gains_ref[...]
    scale = hd ** -0.5

    def norm(t, n):
        return t * lax.rsqrt(_dot_split(t * t, head_sum) * (1.0 / hd) + NORM_EPS) * gains[n:n + 1]

    def rope(t):
        return t * cos + pltpu.roll(t, G - hd // 2, axis=1) * slo + pltpu.roll(t, hd // 2, axis=1) * shi

    cqo_ref[0] = (rope(norm(cq_ref[0], 0)) * (scale * LOG2_E)).astype(_BF16)
    cko_ref[0] = rope(norm(ck_ref[0], 1)).astype(_BF16)
    cvo_ref[0] = cv_ref[0].astype(_BF16)
    qn = norm(dq_ref[0], 2)
    qn_ref[0] = (qn * scale).astype(_BF16)
    qr_ref[0] = (rope(qn) * (scale * LOG2_E)).astype(_BF16)
    d1 = d1_ref[0]
    d2 = d2_ref[0]
    kc_ref[0] = d1[:, 0:hd]
    vc_ref[0] = d1[:, hd:2 * hd]
    ks_ref[0] = rope(norm(d1, 3))[:, 2 * hd:3 * hd].astype(_BF16)
    kw_ref[0] = rope(norm(d2, 4))[:, 0:hd].astype(_BF16)
    vst_ref[0] = d1.T[3 * hd:4 * hd, :].astype(_BF16)
    vwt_ref[0] = d2.T[hd:2 * hd, :].astype(_BF16)


def attn_prep_pallas(proj, tabs, dil_q_g, dil_k_g, nsa_q_g, nsa_ks_g, nsa_kw_g):
    B, S, _ = proj.shape
    tm = PREP_TM
    G, hd, H = GROUP_W, HEAD_DIM, N_HEADS
    c0 = (SECTION_WIDTHS[0] + SECTION_WIDTHS[1]) // G
    z = jnp.zeros((hd,), _F32)
    gains = jnp.stack([jnp.tile(dil_q_g, H), jnp.tile(dil_k_g, H), jnp.tile(nsa_q_g, H),
                       jnp.concatenate([z, z, nsa_ks_g, z]), jnp.concatenate([nsa_kw_g, z, z, z])])
    col = lambda c: pl.BlockSpec((1, tm, G), lambda b, i: (b, i, c))
    nt = S // tm
    tab = pl.BlockSpec((tm, ROPE_LANES), lambda b, i: (b * nt + i, 0))
    tok = lambda w, dt: (pl.BlockSpec((1, tm, w), lambda b, i: (b, i, 0)), jax.ShapeDtypeStruct((B, S, w), dt))
    tr = (pl.BlockSpec((1, hd, tm), lambda b, i: (b, 0, i)), jax.ShapeDtypeStruct((B, hd, S), _BF16))
    outs = [tok(G, _BF16), tok(G, _BF16), tok(G, _BF16), tok(G, _BF16), tok(G, _BF16), tok(hd, _F32), tok(hd, _F32), tok(hd, _BF16), tr, tok(hd, _BF16), tr]
    return pl.pallas_call(
        _attn_prep_kernel,
        grid=(B, nt),
        in_specs=[col(c0), col(c0 + 1), col(c0 + 2), col(c0 + 3), col(c0 + 4), col(c0 + 5), tab, tab, tab,
                  pl.BlockSpec(gains.shape, lambda b, i: (0, 0))],
        out_specs=[o[0] for o in outs],
        out_shape=[o[1] for o in outs],
        compiler_params=pltpu.CompilerParams(dimension_semantics=("parallel", "parallel")),
        name="attn_prep",
    )(proj, proj, proj, proj, proj, proj, *tabs, gains)


NSA_NC = 256
CMP_CHUNK = CMP_STRIDE * HEAD_DIM


def _nsa_compress_kernel(kc_ref, vc_ref, pek_ref, pev_ref, wk1_ref, wk2_ref, wv1_ref, wv2_ref, g_ref, ko_ref, vo_ref):
    def compress(x_ref, pe_ref, w1_ref, w2_ref):
        x = x_ref[0]
        top = _dot_lo(x + pe_ref[0:1], w1_ref[0:CMP_CHUNK, :])
        bot = _dot_lo(x + pe_ref[1:2], w1_ref[CMP_CHUNK:, :])
        hid = top + pltpu.roll(bot, NSA_NC - 1, axis=0)
        return _dot_lo(jax.nn.gelu(hid), w2_ref[...])

    k = compress(kc_ref, pek_ref, wk1_ref, wk2_ref)
    ko_ref[0] = k * lax.rsqrt(jnp.mean(k * k, axis=-1, keepdims=True) + NORM_EPS) * g_ref[...]
    vo_ref[0] = compress(vc_ref, pev_ref, wv1_ref, wv2_ref)


def nsa_compress_pallas(kc, vc, pe_k, pe_v, wk1, wk2, wv1, wv2, kc_g):
    B, S, hd = kc.shape
    assert S // CMP_STRIDE == NSA_NC
    chunks = lambda t: t.reshape(B, NSA_NC, CMP_CHUNK)
    pe2 = lambda p: p.reshape(2, CMP_CHUNK)
    const = lambda t: pl.BlockSpec(t.shape, lambda b: (0,) * t.ndim)
    x_spec = pl.BlockSpec((1, NSA_NC, CMP_CHUNK), lambda b: (b, 0, 0))
    o_spec = pl.BlockSpec((1, NSA_NC, hd), lambda b: (b, 0, 0))
    params = [pe2(pe_k), pe2(pe_v), wk1.astype(_BF16), wk2.astype(_BF16), wv1.astype(_BF16), wv2.astype(_BF16),
              kc_g.reshape(1, hd)]
    return pl.pallas_call(
        _nsa_compress_kernel,
        grid=(B,),
        in_specs=[x_spec, x_spec] + [const(p) for p in params],
        out_specs=[o_spec, o_spec],
        out_shape=[jax.ShapeDtypeStruct((B, NSA_NC, hd), _F32)] * 2,
        name="nsa_compress",
    )(chunks(kc), chunks(vc), *params)


def selection_overlap(n_c, n_sel):
    r = SEL_BLK // CMP_STRIDE
    m = CMP_LEN // CMP_STRIDE
    diff = np.arange(n_c)[:, None] - r * np.arange(n_sel)[None, :]
    offs = (np.arange(r)[:, None] - np.arange(m)[None, :]).reshape(-1)
    return (diff[..., None] == offs).sum(-1).astype(np.float32)


def _nsa_select_kernel(q_ref, kc_ref, vc_ref, ovt_ref, o_ref, selt_ref):
    T = ATT_T
    qi = pl.program_id(1)
    kc = kc_ref[0].astype(_BF16)
    vc = vc_ref[0].astype(_BF16)
    cblk = lax.broadcasted_iota(jnp.int32, (NSA_NC, T), 0)
    tq = qi * T + lax.broadcasted_iota(jnp.int32, (NSA_NC, T), 1)
    valid = cblk * CMP_STRIDE + (CMP_LEN - 1) <= tq
    any_valid = (tq[0:1, :] >= CMP_LEN - 1).astype(_F32)
    heads = range(N_HEADS)
    ss = [lax.dot_general(kc, q_ref[0, :, h * HEAD_DIM:(h + 1) * HEAD_DIM], _NT, preferred_element_type=_F32)
          for h in heads]
    ss = [jnp.where(valid, s, NEG) for s in ss]
    es = [jnp.exp(s - jnp.max(s, axis=0, keepdims=True)) for s in ss]
    ps = [e * (any_valid / jnp.sum(e, axis=0, keepdims=True)) for e in es]
    outs = [lax.dot_general(vc, p.astype(_BF16), _TN, preferred_element_type=_F32) for p in ps]
    p_sum = sum(ps[1:], ps[0])
    o_ref[0] = _heads_to_rows(jnp.concatenate(outs, axis=1))

    ovt = ovt_ref[...]
    hi = p_sum.astype(_BF16)
    lo = (p_sum - hi.astype(_F32)).astype(_BF16)
    imp = jnp.dot(ovt, hi, preferred_element_type=_F32) + jnp.dot(ovt, lo, preferred_element_type=_F32)
    jb = lax.broadcasted_iota(jnp.int32, (NSA_NSEL, T), 0)
    cur = (qi * T + lax.broadcasted_iota(jnp.int32, (NSA_NSEL, T), 1)) // SEL_BLK
    forced = (jb == 0) | (jb == cur) | (jb == cur - 1)
    score = jnp.where(jb > cur, -1.0, jnp.where(forced, FORCE_SCORE, imp))
    jbf = jb.astype(_F32)
    sel = jnp.zeros((NSA_NSEL, T), _F32)
    for _ in range(min(SEL_TOP, NSA_NSEL)):
        best = jnp.max(score, axis=0, keepdims=True)
        first = jnp.min(jnp.where(score == best, jbf, float(NSA_NSEL)), axis=0, keepdims=True)
        pick = jbf == first
        sel = jnp.where(pick, 1.0, sel)
        score = jnp.where(pick, NEG, score)
    selt_ref[0] = sel.astype(_BF16)


def nsa_select_pallas(q_n, k_cmp, v_cmp):
    B, S, W = q_n.shape
    T = ATT_T
    n_c = (S - CMP_LEN) // CMP_STRIDE + 1
    ovt = np.zeros((NSA_NSEL, NSA_NC), np.float32)
    ovt[:, :n_c] = selection_overlap(n_c, NSA_NSEL).T
    ovt = jnp.asarray(ovt, _BF16)
    c_spec = pl.BlockSpec((1, NSA_NC, HEAD_DIM), lambda b, i: (b, 0, 0))
    return pl.pallas_call(
        _nsa_select_kernel,
        grid=(B, S // T),
        in_specs=[pl.BlockSpec((1, T, W), lambda b, i: (b, i, 0)), c_spec, c_spec,
                  pl.BlockSpec(ovt.shape, lambda b, i: (0, 0))],
        out_specs=[pl.BlockSpec((1, T, W), lambda b, i: (b, i, 0)),
                   pl.BlockSpec((1, NSA_NSEL, T), lambda b, i: (b, 0, i))],
        out_shape=[jax.ShapeDtypeStruct((B, S, W), _F32), jax.ShapeDtypeStruct((B, NSA_NSEL, S), _BF16)],
        compiler_params=pltpu.CompilerParams(dimension_semantics=("parallel", "parallel")),
        name="nsa_select",
    )(q_n, k_cmp, v_cmp, ovt)


def kernel(x, c, positions, w_ada, b_ada, norm1_g, norm2_g, w_in, w_out, rwkv_mu, rwkv_w0, rwkv_w2, rwkv_a0, rwkv_a2, rwkv_g2, rwkv_kk, rwkv_ka, rwkv_rk, rwkv_gn_w, rwkv_gn_b, conv_w, dil_q_g, dil_k_g, nsa_q_g, nsa_kc_g, nsa_ks_g, nsa_kw_g, nsa_pe_k, nsa_pe_v, nsa_wk1, nsa_wk2, nsa_wv1, nsa_wv2, onorm_g, w_router, b_router, w_gu, b_gu, w_down, b_down):
    B, S, D = x.shape
    tabs = rope_tables(positions)
    for l in range(DEPTH):
        mod = matmul(jax.nn.silu(c), w_ada[l], tm=B, tn=512) + b_ada[l]
        sh1, sc1, gt1, sh2, sc2, gt2 = [m[:, None, :] for m in jnp.split(mod, 6, axis=-1)]
        proj = in_proj_pallas(x, norm1_g[l], sc1, sh1, w_in[l])
        y_a = rwkv7_pallas(proj, rwkv_mu[l], rwkv_w0[l], rwkv_w2[l], rwkv_a0[l], rwkv_a2[l], rwkv_g2[l],
                           rwkv_kk[l], rwkv_ka[l], rwkv_rk[l], rwkv_gn_w[l], rwkv_gn_b[l])
        cq, ck, cv, q_n, q_r, kc, vc, ks, vs_t, kw, vw_t = attn_prep_pallas(
            proj, tabs, dil_q_g[l], dil_k_g[l], nsa_q_g[l], nsa_ks_g[l], nsa_kw_g[l])
        dil_outs = [dil_attn_pallas(cq, ck, cv, dil) for _, dil in DIL_CFG]
        k_cmp, v_cmp = nsa_compress_pallas(kc, vc, nsa_pe_k[l], nsa_pe_v[l], nsa_wk1[l], nsa_wk2[l], nsa_wv1[l],
                                           nsa_wv2[l], nsa_kc_g[l])
        o_cmp, sel_t = nsa_select_pallas(q_n, k_cmp, v_cmp)
        o_slc, o_swa = nsa_attn_pallas(q_r, ks, vs_t, kw, vw_t, sel_t)
        x, hb, logits = mix_out_pallas(x, y_a, proj, dil_outs, o_cmp, o_slc, o_swa, conv_w[l], onorm_g[l], w_out[l], gt1,
                                       norm2_g[l], sc2, sh2, w_router[l], b_router[l])
        x = moe_pallas(x, gt2, hb, logits, l, w_gu, b_gu, w_down, b_down)
    return x
```

```python
import functools

import numpy as np
import jax
import jax.numpy as jnp
from jax import lax
from jax.experimental import pallas as pl
from jax.experimental.pallas import tpu as pltpu

D_MODEL = 1024
DEPTH = 2

HEAD_DIM = 64
N_MIXERS = 4
GROUP_W = D_MODEL // N_MIXERS
N_HEADS = GROUP_W // HEAD_DIM
MIX_W = N_MIXERS * GROUP_W

RWKV_DECAY_RANK = 64
RWKV_AAA_RANK = 64
RWKV_GATE_RANK = 128
RWKV_DECAY_SCALE = 0.606531
RWKV_GN_EPS = 64e-5

CONV_W = 3

DIL_CFG = ((128, 1), (512, 4), (2048, 16))

CMP_LEN = 32
CMP_STRIDE = 16
CMP_HIDDEN = 256
SEL_BLK = 64
SEL_TOP = 16
SWA_WIN = 512
FORCE_SCORE = 1e4

N_EXPERTS = 32
TOP_K = 4
EXPERT_FF = D_MODEL
SWIGLU_LIMIT = 7.0
SWIGLU_ALPHA = 1.702

ROPE_THETA = 10000.0
NORM_EPS = 1e-6
NEG = -1e30

A_WIDTHS = (GROUP_W, GROUP_W, GROUP_W, RWKV_DECAY_RANK, RWKV_AAA_RANK, RWKV_GATE_RANK)
B_WIDTHS = (GROUP_W, GROUP_W, GROUP_W)
C_WIDTHS = (GROUP_W, GROUP_W, GROUP_W)
D_WIDTHS = (GROUP_W, HEAD_DIM, HEAD_DIM, HEAD_DIM, HEAD_DIM, HEAD_DIM, HEAD_DIM, 3 * N_HEADS)
SECTION_WIDTHS = (sum(A_WIDTHS), sum(B_WIDTHS), sum(C_WIDTHS), sum(D_WIDTHS))
PROJ_W = sum(SECTION_WIDTHS)

_F32 = jnp.float32
_BF16 = jnp.bfloat16
_NT = (((1,), (1,)), ((), ()))
_TN = (((0,), (0,)), ((), ()))


def _dot_hi(a, b, dims=None):
    if dims is None:
        return jnp.dot(a, b, precision=lax.Precision.HIGHEST, preferred_element_type=_F32)
    return lax.dot_general(a, b, dims, precision=lax.Precision.HIGHEST, preferred_element_type=_F32)


def _dot_lo(a, b, dims=None):
    a, b = a.astype(_BF16), b.astype(_BF16)
    if dims is None:
        return jnp.dot(a, b, preferred_element_type=_F32)
    return lax.dot_general(a, b, dims, preferred_element_type=_F32)


def _bf16_parts(a, parts):
    out = []
    for _ in range(parts):
        p = a.astype(_BF16)
        out.append(p)
        a = a - p.astype(_F32)
    return out


def _dot_split(a, m, parts=2):
    mb = m.astype(_BF16)
    return sum(jnp.dot(p, mb, preferred_element_type=_F32) for p in _bf16_parts(a, parts))


def _dot_split_left(m, a, parts=2):
    mb = m.astype(_BF16)
    return sum(jnp.dot(mb, p, preferred_element_type=_F32) for p in _bf16_parts(a, parts))


def _dot_x3(a, b):
    ah, al = _bf16_parts(a, 2)
    bh, bl = _bf16_parts(b, 2)
    d = lambda x, y: jnp.dot(x, y, preferred_element_type=_F32)
    return d(ah, bh) + (d(ah, bl) + d(al, bh))


def _matmul_kernel(x_ref, w_ref, o_ref, *, exact):
    if exact:
        o_ref[...] = _dot_hi(x_ref[...], w_ref[...])
    else:
        o_ref[...] = jnp.dot(x_ref[...].astype(_BF16), w_ref[...], preferred_element_type=_F32)


def matmul(x, w, tm=512, tn=256, exact=False):
    M, K = x.shape
    N = w.shape[1]
    n_pad = -(-N // tn) * tn
    wb = jnp.pad(w if exact else w.astype(_BF16), ((0, 0), (0, n_pad - N)))
    out = pl.pallas_call(
        functools.partial(_matmul_kernel, exact=exact),
        grid=(M // tm, n_pad // tn),
        in_specs=[pl.BlockSpec((tm, K), lambda i, j: (i, 0)),
                  pl.BlockSpec((K, tn), lambda i, j: (0, j))],
        out_specs=pl.BlockSpec((tm, tn), lambda i, j: (i, j)),
        out_shape=jax.ShapeDtypeStruct((M, n_pad), _F32),
        name="matmul",
    )(x, wb)
    return out[:, :N]


PROJ_PAD = -(-PROJ_W // 256) * 256
PROJ_TM = 256
PROJ_TN = 512
ROUTER_PAD = 128
GATE_COL_BLOCK = (PROJ_W - 3 * N_HEADS) // 128
assert GATE_COL_BLOCK * 128 == PROJ_W - 3 * N_HEADS


def _norm_mod(x, g, sc, sh):
    y = x * lax.rsqrt(jnp.mean(x * x, axis=-1, keepdims=True) + NORM_EPS) * g
    return y * (1.0 + sc) + sh


def _in_proj_kernel(x_ref, g_ref, sc_ref, sh_ref, w_ref, o_ref):
    h = _norm_mod(x_ref[0], g_ref[...], sc_ref[0], sh_ref[0]).astype(_BF16)
    for n0 in range(0, PROJ_PAD, PROJ_TN):
        o_ref[0, :, n0:n0 + PROJ_TN] = jnp.dot(h, w_ref[:, n0:n0 + PROJ_TN], preferred_element_type=_F32)


def in_proj_pallas(x, g, sc, sh, w_in):
    B, S, D = x.shape
    tm = PROJ_TM
    w = jnp.pad(w_in.astype(_BF16), ((0, 0), (0, PROJ_PAD - PROJ_W)))
    per_batch = pl.BlockSpec((1, 1, D), lambda b, i: (b, 0, 0))
    return pl.pallas_call(
        _in_proj_kernel,
        grid=(B, S // tm),
        in_specs=[pl.BlockSpec((1, tm, D), lambda b, i: (b, i, 0)),
                  pl.BlockSpec((1, D), lambda b, i: (0, 0)), per_batch, per_batch,
                  pl.BlockSpec((D, PROJ_PAD), lambda b, i: (0, 0))],
        out_specs=pl.BlockSpec((1, tm, PROJ_PAD), lambda b, i: (b, i, 0)),
        out_shape=jax.ShapeDtypeStruct((B, S, PROJ_PAD), _F32),
        compiler_params=pltpu.CompilerParams(dimension_semantics=("parallel", "parallel"),
                                             vmem_limit_bytes=48 * 1024 * 1024),
        name="in_proj",
    )(x, g.reshape(1, D), sc, sh, w)


def _mix_out_kernel(x_ref, ya_ref, bg_ref, cg_ref, xin_ref, cgh_ref, xinh_ref, do0_ref, dl0_ref, do1_ref, dl1_ref,
                    do2_ref, dl2_ref, ocmp_ref, oslc_ref, oswa_ref,
                    gl_ref, convw_ref, ong_ref, wout_ref, gt1_ref, g2_ref, sc2_ref, sh2_ref, wr_ref, br_ref,
                    xo_ref, hb_ref, lg_ref):
    i = pl.program_id(1)
    G = GROUP_W
    u = cg_ref[0] * xin_ref[0]
    halo = jnp.where(i == 0, 0.0, cgh_ref[0] * xinh_ref[0])
    row = lax.broadcasted_iota(jnp.int32, u.shape, 0)
    u1 = jnp.where(row == 0, halo[7:8], pltpu.roll(u, 1, axis=0))
    u2 = jnp.where(row == 0, halo[6:7], jnp.where(row == 1, halo[7:8], pltpu.roll(u, 2, axis=0)))
    cw = convw_ref[...]
    yb = bg_ref[0] * (cw[0:1] * u2 + cw[1:2] * u1 + cw[2:3] * u)

    sg = jax.nn.sigmoid(gl_ref[0])
    er = lax.broadcasted_iota(jnp.int32, (128, G), 0)
    ec = lax.broadcasted_iota(jnp.int32, (128, G), 1) // HEAD_DIM
    yd = None
    for j, o_ref in enumerate((ocmp_ref, oslc_ref, oswa_ref)):
        gate = _dot_split(sg, er == 3 * ec + j)
        yd = gate * o_ref[0] if yd is None else yd + gate * o_ref[0]

    head_sum = _group_indicator(G, HEAD_DIM)
    ong = ong_ref[...]
    parts = [ya_ref[0]]
    dil = ((do0_ref, dl0_ref), (do1_ref, dl1_ref), (do2_ref, dl2_ref))
    top = functools.reduce(jnp.maximum, [l_ref[0] for _, l_ref in dil])
    wts = [jnp.exp2(l_ref[0] - top) for _, l_ref in dil]
    yc = sum(w * o_ref[0] for w, (o_ref, _) in zip(wts, dil)) / sum(wts)
    for n, y in enumerate((yb, yc, yd)):
        ms = _dot_split(y * y, head_sum) * (1.0 / HEAD_DIM)
        parts.append(y * lax.rsqrt(ms + NORM_EPS) * ong[:, n * G:(n + 1) * G])
    mixed = jnp.dot(jnp.concatenate(parts, axis=1).astype(_BF16), wout_ref[...], preferred_element_type=_F32)
    x = x_ref[0] + gt1_ref[0] * mixed
    xo_ref[0] = x
    h = _norm_mod(x, g2_ref[...], sc2_ref[0], sh2_ref[0])
    hb_ref[0] = h.astype(_BF16)
    lg_ref[0] = _dot_x3(h, wr_ref[...]) + br_ref[...]


def mix_out_pallas(x, y_a, proj, dil_outs, o_cmp, o_slc, o_swa, conv_w, onorm_g, w_out, gt1, g2, sc2, sh2, w_router, b_router):
    B, S, D = x.shape
    tm = PROJ_TM
    G = GROUP_W
    b0 = SECTION_WIDTHS[0] // G
    tile = lambda w, col: pl.BlockSpec((1, tm, w), lambda b, i: (b, i, col))
    halo = lambda col: pl.BlockSpec((1, 8, G), lambda b, i: (b, jnp.maximum(i * (tm // 8) - 1, 0), col))
    const = lambda t: pl.BlockSpec(t.shape, lambda b, i: (0,) * t.ndim)
    per_batch = pl.BlockSpec((1, 1, D), lambda b, i: (b, 0, 0))
    wr = jnp.pad(w_router, ((0, 0), (0, ROUTER_PAD - N_EXPERTS)))
    br = jnp.pad(b_router, (0, ROUTER_PAD - N_EXPERTS)).reshape(1, ROUTER_PAD)
    consts = [conv_w, onorm_g.reshape(1, 3 * G), w_out.astype(_BF16)]
    return pl.pallas_call(
        _mix_out_kernel,
        grid=(B, S // tm),
        in_specs=[tile(D, 0), tile(G, 0), tile(G, b0), tile(G, b0 + 1), tile(G, b0 + 2), halo(b0 + 1), halo(b0 + 2),
                  *([tile(G, 0)] * (2 * len(dil_outs) + 3)), tile(128, GATE_COL_BLOCK)]
                 + [const(t) for t in consts] + [per_batch, pl.BlockSpec((1, D), lambda b, i: (0, 0)), per_batch,
                                                 per_batch, const(wr), const(br)],
        out_specs=[tile(D, 0), tile(D, 0), tile(ROUTER_PAD, 0)],
        out_shape=[jax.ShapeDtypeStruct((B, S, D), _F32), jax.ShapeDtypeStruct((B, S, D), _BF16),
                   jax.ShapeDtypeStruct((B, S, ROUTER_PAD), _F32)],
        compiler_params=pltpu.CompilerParams(dimension_semantics=("parallel", "parallel")),
        name="mix_out",
    )(x, y_a, proj, proj, proj, proj, proj, *[t for pair in dil_outs for t in pair], o_cmp, o_slc, o_swa, proj, *consts, gt1, g2.reshape(1, D), sc2, sh2,
      wr, br)


RWKV_CHUNK = 64
RWKV_TILE = 512


def _group_indicator(n, group):
    r = lax.broadcasted_iota(jnp.int32, (n, n), 0) // group
    c = lax.broadcasted_iota(jnp.int32, (n, n), 1) // group
    return r == c


def _rwkv_kernel(za_ref, mu_ref, w0_ref, w2_ref, a0_ref, a2_ref, g2_ref, kk_ref, ka_ref, rk_ref,
                 gnw_ref, gnb_ref, o_ref, prev_sc, h_sc):
    C = RWKV_CHUNK
    TT = RWKV_TILE
    c = pl.program_id(1)

    @pl.when(c == 0)
    def _():
        prev_sc[...] = jnp.zeros_like(prev_sc)
        h_sc[...] = jnp.zeros_like(h_sc)

    z = za_ref[0]
    row = lax.broadcasted_iota(jnp.int32, z.shape, 0)
    zs = jnp.where(row == 0, prev_sc[...], pltpu.roll(z, 1, axis=0))
    prev_sc[...] = z[TT - 1:TT, :]
    z = z + (zs - z) * mu_ref[...]

    G = GROUP_W
    r, k, v = z[:, 0:G], z[:, G:2 * G], z[:, 2 * G:3 * G]
    o = 3 * G
    wd = z[:, o:o + RWKV_DECAY_RANK]
    ad = z[:, o + RWKV_DECAY_RANK:o + RWKV_DECAY_RANK + RWKV_AAA_RANK]
    gd = z[:, o + RWKV_DECAY_RANK + RWKV_AAA_RANK:]

    lw = -RWKV_DECAY_SCALE * jax.nn.sigmoid(w0_ref[...] + _dot_lo(jnp.tanh(wd), w2_ref[...]))
    a = jax.nn.sigmoid(a0_ref[...] + _dot_lo(ad, a2_ref[...]))
    g = _dot_lo(jax.nn.sigmoid(gd), g2_ref[...])

    head_sum = _group_indicator(G, HEAD_DIM)
    kk = k * kk_ref[...]
    kk = kk * lax.rsqrt(_dot_split(kk * kk, head_sum) + 1e-12)
    k = k * (1.0 + (a - 1.0) * ka_ref[...])
    b = kk * a

    ti = lax.broadcasted_iota(jnp.int32, (C, C), 0)
    tj = lax.broadcasted_iota(jnp.int32, (C, C), 1)
    incl = ti >= tj
    strict = ti > tj
    eye = ti == tj
    blk16 = (ti // 16) == (tj // 16)
    blk32 = (ti // 32) == (tj // 32)
    eye_f = eye.astype(_F32)

    ri = lax.broadcasted_iota(jnp.int32, (TT, TT), 0)
    rj = lax.broadcasted_iota(jnp.int32, (TT, TT), 1)
    chunk_tri = (ri >= rj) & (ri // C == rj // C)
    cum = _dot_split_left(chunk_tri, lw, parts=3)
    g_in = jnp.exp(cum)
    A_all = -kk * jnp.exp(cum - lw)
    R_all = r * g_in
    g_inv = jnp.exp(-cum)
    B_all = b * g_inv
    K_all = k * g_inv

    units = [(ci, h) for ci in range(TT // C) for h in range(N_HEADS)]

    def part(t, u):
        ci, h = u
        return t[ci * C:(ci + 1) * C, h * HEAD_DIM:(h + 1) * HEAD_DIM]

    def g_end(u):
        ci, h = u
        return g_in[(ci + 1) * C - 1:(ci + 1) * C, h * HEAD_DIM:(h + 1) * HEAD_DIM]

    A = [part(A_all, u) for u in units]
    R = [part(R_all, u) for u in units]
    B = [part(B_all, u) for u in units]
    Kt = [part(K_all, u) for u in units]
    V = [part(v, u) for u in units]
    n = range(len(units))
    gram = [_dot_lo(jnp.concatenate([A[i], R[i]], axis=0), jnp.concatenate([B[i], Kt[i]], axis=0), _NT) for i in n]
    l_ab = [jnp.where(strict, gram[i][0:C, 0:C], 0.0) for i in n]
    l_ak = [jnp.where(strict, gram[i][0:C, C:2 * C], 0.0) for i in n]
    m_rb = [jnp.where(incl, gram[i][C:2 * C, 0:C], 0.0) for i in n]
    m_rk = [jnp.where(incl, gram[i][C:2 * C, C:2 * C], 0.0) for i in n]
    p = [jnp.where(blk16, l_ab[i], 0.0) for i in n]
    x = [eye_f + p[i] for i in n]
    for _ in range(3):
        p = [_dot_lo(p[i], p[i]) for i in n]
        x = [_dot_lo(x[i], eye_f + p[i]) for i in n]
    for lvl in (blk32 & ~blk16, ~blk32):
        xl = [_dot_lo(x[i], jnp.where(lvl, l_ab[i], 0.0)) for i in n]
        x = [x[i] + _dot_lo(xl[i], x[i]) for i in n]
    lv = [_dot_lo(l_ak[i], V[i]) for i in n]
    tap = [_dot_lo(x[i], jnp.concatenate([A[i], lv[i]], axis=1)) for i in n]
    m1 = [_dot_lo(m_rb[i], tap[i]) for i in n]
    mv = [_dot_lo(m_rk[i], V[i]) for i in n]
    bt = [_dot_lo(B[i] * g_end(units[i]), tap[i], _TN) for i in n]
    kv = [_dot_lo(Kt[i] * g_end(units[i]), V[i], _TN) for i in n]
    w_yh = [jnp.concatenate([R[i] + m1[i][:, 0:HEAD_DIM],
                             jnp.where(eye, g_end(units[i]), 0.0) + bt[i][:, 0:HEAD_DIM]], axis=0) for i in n]
    y0 = [m1[i][:, HEAD_DIM:] + mv[i] for i in n]
    h_add = [bt[i][:, HEAD_DIM:] + kv[i] for i in n]
    state = [h_sc[h] for h in range(N_HEADS)]
    y_rows = []
    for ci in range(TT // C):
        ys = []
        for h in range(N_HEADS):
            i = ci * N_HEADS + h
            nxt = _dot_x3(w_yh[i], state[h])
            ys.append(nxt[0:C] + y0[i])
            state[h] = nxt[C:2 * C] + h_add[i]
        y_rows.append(jnp.concatenate(ys, axis=1))
    for h in range(N_HEADS):
        h_sc[h] = state[h]
    y = jnp.concatenate(y_rows, axis=0)

    mu = _dot_split(y, head_sum) * (1.0 / HEAD_DIM)
    d = y - mu
    var = _dot_split(d * d, head_sum) * (1.0 / HEAD_DIM)
    y = d * lax.rsqrt(var + RWKV_GN_EPS) * gnw_ref[...] + gnb_ref[...]
    y = y + _dot_split(r * k * rk_ref[...], head_sum) * v
    o_ref[0] = y * g


def rwkv7_pallas(za, mu, w0, w2, a0, a2, g2, k_k, k_a, r_k, gn_w, gn_b):
    B, S, _ = za.shape
    W = SECTION_WIDTHS[0]
    TT = RWKV_TILE
    row = lambda t: t.reshape(1, -1).astype(_F32)
    full = lambda t: pl.BlockSpec(t.shape, lambda b, c: (0,) * t.ndim)
    params = [row(mu), row(w0), w2.astype(_BF16), row(a0), a2.astype(_BF16), g2.astype(_BF16),
              row(k_k), row(k_a), row(r_k), row(gn_w), row(gn_b)]
    return pl.pallas_call(
        _rwkv_kernel,
        grid=(B, S // TT),
        in_specs=[pl.BlockSpec((1, TT, W), lambda b, c: (b, c, 0))] + [full(p) for p in params],
        out_specs=pl.BlockSpec((1, TT, GROUP_W), lambda b, c: (b, c, 0)),
        out_shape=jax.ShapeDtypeStruct((B, S, GROUP_W), _F32),
        scratch_shapes=[pltpu.VMEM((1, W), _F32), pltpu.VMEM((N_HEADS, HEAD_DIM, HEAD_DIM), _F32)],
        compiler_params=pltpu.CompilerParams(dimension_semantics=("parallel", "arbitrary")),
        name="rwkv7",
    )(za, *params)


MOE_TM = 512
MOE_FC = 512
MOE_CAST_ROWS = 128


def _moe_ffn_kernel(blk_e_ref, n_used_ref, x_ref, wgu_ref, bgu_ref, wd_ref, bd_ref, o_ref, wgu_sc, wd_sc):
    i = pl.program_id(0)
    F = EXPERT_FF
    new_expert = jnp.logical_or(i == 0, blk_e_ref[i] != blk_e_ref[jnp.maximum(i - 1, 0)])

    @pl.when(new_expert)
    def _():
        def cast(j, carry):
            rows = pl.ds(pl.multiple_of(j * MOE_CAST_ROWS, MOE_CAST_ROWS), MOE_CAST_ROWS)
            wgu_sc[rows, :] = wgu_ref[0, rows, :].astype(_BF16)
            wd_sc[rows, :] = wd_ref[0, rows, :].astype(_BF16)
            return carry
        lax.fori_loop(0, F // MOE_CAST_ROWS, cast, 0)

    @pl.when(i < n_used_ref[0])
    def _():
        x = x_ref[...]
        acc = None
        for c in range(F // MOE_FC):
            lo = c * MOE_FC
            gate = jnp.dot(x, wgu_sc[:, lo:lo + MOE_FC], preferred_element_type=_F32) + bgu_ref[0, :, lo:lo + MOE_FC]
            up = jnp.dot(x, wgu_sc[:, F + lo:F + lo + MOE_FC], preferred_element_type=_F32) + bgu_ref[0, :, F + lo:F + lo + MOE_FC]
            gate = jnp.minimum(gate, SWIGLU_LIMIT)
            up = jnp.clip(up, -SWIGLU_LIMIT, SWIGLU_LIMIT)
            act = gate * jax.nn.sigmoid(SWIGLU_ALPHA * gate) * (up + 1.0)
            part = jnp.dot(act.astype(_BF16), wd_sc[lo:lo + MOE_FC, :], preferred_element_type=_F32)
            acc = part if acc is None else acc + part
        o_ref[...] = (acc + bd_ref[0]).astype(o_ref.dtype)

    @pl.when(i >= n_used_ref[0])
    def _():
        o_ref[...] = jnp.zeros_like(o_ref)


def moe_ffn(xs, blk_e, n_used, layer, w_gu, b_gu, w_down, b_down):
    n_rows, D = xs.shape
    L, E, _, F2 = w_gu.shape
    assert D == EXPERT_FF and F2 == 2 * EXPERT_FF
    n_blk = n_rows // MOE_TM
    grid_spec = pltpu.PrefetchScalarGridSpec(
        num_scalar_prefetch=2,
        grid=(n_blk,),
        in_specs=[pl.BlockSpec((MOE_TM, D), lambda i, e, n: (i, 0)),
                  pl.BlockSpec((None, 1, D, F2), lambda i, e, n: (layer, e[i], 0, 0)),
                  pl.BlockSpec((None, 1, 1, F2), lambda i, e, n: (layer, e[i], 0, 0)),
                  pl.BlockSpec((None, 1, F2 // 2, D), lambda i, e, n: (layer, e[i], 0, 0)),
                  pl.BlockSpec((None, 1, 1, D), lambda i, e, n: (layer, e[i], 0, 0))],
        out_specs=pl.BlockSpec((MOE_TM, D), lambda i, e, n: (i, 0)),
        scratch_shapes=[pltpu.VMEM((D, F2), _BF16), pltpu.VMEM((F2 // 2, D), _BF16)],
    )
    return pl.pallas_call(
        _moe_ffn_kernel,
        grid_spec=grid_spec,
        out_shape=jax.ShapeDtypeStruct((n_rows, D), _BF16),
        compiler_params=pltpu.CompilerParams(dimension_semantics=("arbitrary",),
                                             vmem_limit_bytes=52 * 1024 * 1024),
        name="moe_ffn",
    )(blk_e, n_used, xs, w_gu, b_gu.reshape(L, E, 1, F2), w_down, b_down.reshape(L, E, 1, D))


def moe_pallas(x, gt2, hb, logits, layer, w_gu, b_gu, w_down, b_down):
    D = x.shape[-1]
    hb = hb.reshape(-1, D)
    logits = logits.reshape(hb.shape[0], -1)
    T = hb.shape[0]
    TK = T * TOP_K
    TM = MOE_TM
    E = N_EXPERTS
    i32 = jnp.int32
    top_val, top_idx = lax.top_k(logits[:, :E], TOP_K)
    gates = jax.nn.softmax(top_val, axis=-1)
    e_flat = top_idx.T.reshape(TK).astype(i32)
    counts = jnp.sum((jnp.arange(E, dtype=i32)[:, None] == e_flat[None, :]).astype(i32), axis=1)
    need = jnp.repeat((-counts) % TM, TM)
    d_idx = jnp.arange(E * TM, dtype=i32)
    d_key = jnp.where(d_idx % TM < need, d_idx // TM, E)
    n_rows = TK + E * TM
    keys = jnp.concatenate([e_flat, d_key])
    toks = jnp.concatenate([jnp.arange(TK, dtype=i32) % T, jnp.zeros((E * TM,), i32)])
    rows = jnp.arange(n_rows, dtype=i32)
    s_keys, src_tok, s_slot = lax.sort((keys, toks, rows), num_keys=1)
    _, row_of_slot = lax.sort((s_slot, rows), num_keys=1)
    row_of = row_of_slot[:TK]
    blk_e = s_keys[::TM]
    n_used = jnp.sum((blk_e < E).astype(i32)).reshape(1)
    blk_e = jnp.where(blk_e < E, blk_e, blk_e[jnp.maximum(n_used[0] - 1, 0)])
    xs = jnp.take(hb, src_tok, axis=0)
    ybuf = moe_ffn(xs, blk_e, n_used, layer, w_gu, b_gu, w_down, b_down)
    y = jnp.take(ybuf, row_of, axis=0).reshape(TOP_K, T, D)
    return moe_combine(x, y, gates, gt2)


def _moe_combine_kernel(y_ref, g_ref, x_ref, gt2_ref, o_ref):
    g = g_ref[...]
    acc = None
    for k in range(TOP_K):
        term = y_ref[k].astype(_F32) * g[:, k:k + 1]
        acc = term if acc is None else acc + term
    o_ref[0] = x_ref[0] + gt2_ref[0] * acc


def moe_combine(x, y, gates, gt2):
    B, S, D = x.shape
    tm = PROJ_TM
    nt = S // tm
    g = jnp.pad(gates, ((0, 0), (0, 128 - TOP_K)))
    return pl.pallas_call(
        _moe_combine_kernel,
        grid=(B, nt),
        in_specs=[pl.BlockSpec((TOP_K, tm, D), lambda b, i: (0, b * nt + i, 0)),
                  pl.BlockSpec((tm, 128), lambda b, i: (b * nt + i, 0)),
                  pl.BlockSpec((1, tm, D), lambda b, i: (b, i, 0)),
                  pl.BlockSpec((1, 1, D), lambda b, i: (b, 0, 0))],
        out_specs=pl.BlockSpec((1, tm, D), lambda b, i: (b, i, 0)),
        out_shape=jax.ShapeDtypeStruct((B, S, D), _F32),
        compiler_params=pltpu.CompilerParams(dimension_semantics=("parallel", "parallel")),
        name="moe_combine",
    )(y, g, x, gt2)


ATT_T = 128
ATT_TK = 512
ATT_R = ATT_TK // ATT_T
NSA_NSEL = 64


def _window_table(window, weight_of_distance):
    T, TK = ATT_T, ATT_TK
    n = window // T + ATT_R + 1
    d = (np.arange(n)[:, None, None] * T + np.arange(T)[None, None, :] - np.arange(TK)[None, :, None])
    return jnp.asarray(weight_of_distance(d).astype(np.float32))


def _first_key_tile(qi, window):
    return jnp.maximum(qi * ATT_T - window, 0) // ATT_TK


def _swa_weight(d):
    return (d >= 0) & (d <= SWA_WIN - 1)


def _softmax_tile(s, w, m_ref, l_ref):
    s = jnp.where(w > 0.0, s, NEG)
    m_prev = m_ref[...]
    m_new = jnp.maximum(m_prev, jnp.max(s, axis=0, keepdims=True))
    alpha = jnp.exp2(m_prev - m_new)
    p = jnp.exp2(s - m_new)
    l_ref[...] = alpha * l_ref[...] + jnp.sum(p, axis=0, keepdims=True)
    m_ref[...] = m_new
    return alpha, p.astype(_BF16)


def _init_stats(m_sc, l_sc, acc_sc):
    m_sc[...] = jnp.full_like(m_sc, NEG)
    l_sc[...] = jnp.zeros_like(l_sc)
    acc_sc[...] = jnp.zeros_like(acc_sc)


def _heads_to_rows(ot):
    T = ATT_T
    pairs = []
    for h in range(0, N_HEADS, 2):
        two = jnp.concatenate([ot[:, h * T:(h + 1) * T], ot[:, (h + 1) * T:(h + 2) * T]], axis=0)
        pairs.append(two.T)
    return jnp.concatenate(pairs, axis=1)


def _nsa_attn_kernel(q_ref, ks_ref, vst_ref, kw_ref, vwt_ref, selt_ref, swa_ref, oslc_ref, oswa_ref,
                     m_sc, l_sc, acc_sc):
    T, TK = ATT_T, ATT_TK
    qi = pl.program_id(1)
    q = q_ref[0]
    q = jnp.concatenate([q[:, h * HEAD_DIM:(h + 1) * HEAD_DIM] for h in range(N_HEADS)], axis=0)
    selt = selt_ref[0]
    key_blk = lax.broadcasted_iota(jnp.int32, (TK, NSA_NSEL), 0) // SEL_BLK
    blk_id = lax.broadcasted_iota(jnp.int32, (TK, NSA_NSEL), 1)
    kr = lax.broadcasted_iota(jnp.int32, (TK, T), 0)
    qc = lax.broadcasted_iota(jnp.int32, (TK, T), 1)

    def step(k_ref, vt_ref, kt, w):
        base = pl.multiple_of(kt * TK, TK)
        s = lax.dot_general(k_ref[0, pl.ds(base, TK), :], q, _NT, preferred_element_type=_F32)
        alpha, p = _softmax_tile(s, jnp.concatenate([w] * N_HEADS, axis=1), m_sc, l_sc)
        acc_sc[...] = alpha * acc_sc[...] + jnp.dot(vt_ref[0, :, pl.ds(base, TK)], p, preferred_element_type=_F32)

    def finish(o_ref):
        o_ref[0] = _heads_to_rows(acc_sc[...] / l_sc[...])

    last = qi // ATT_R
    _init_stats(m_sc, l_sc, acc_sc)

    def slc_step(kt, carry):
        in_blk = (key_blk + kt * (TK // SEL_BLK) == blk_id).astype(_BF16)
        w = jnp.dot(in_blk, selt, preferred_element_type=_F32)
        w = jnp.where(kt * TK + kr <= qi * T + qc, w, 0.0)
        step(ks_ref, vst_ref, kt, w)
        return carry

    lax.fori_loop(0, last + 1, slc_step, 0)
    finish(oslc_ref)

    _init_stats(m_sc, l_sc, acc_sc)

    def swa_step(kt, carry):
        step(kw_ref, vwt_ref, kt, swa_ref[qi - kt * ATT_R])
        return carry

    lax.fori_loop(_first_key_tile(qi, SWA_WIN - 1), last + 1, swa_step, 0)
    finish(oswa_ref)


def nsa_attn_pallas(q_r, ks_r, vs_t, kw_r, vw_t, sel_t):
    B, S, W = q_r.shape
    assert S // SEL_BLK == NSA_NSEL
    T = ATT_T
    R = N_HEADS * T
    swa_tab = _window_table(SWA_WIN - 1, _swa_weight)
    k_spec = pl.BlockSpec((1, S, HEAD_DIM), lambda b, i: (b, 0, 0))
    vt_spec = pl.BlockSpec((1, HEAD_DIM, S), lambda b, i: (b, 0, 0))
    q_spec = pl.BlockSpec((1, T, W), lambda b, i: (b, i, 0))
    return pl.pallas_call(
        _nsa_attn_kernel,
        grid=(B, S // T),
        in_specs=[q_spec, k_spec, vt_spec, k_spec, vt_spec,
                  pl.BlockSpec((1, NSA_NSEL, T), lambda b, i: (b, 0, i)),
                  pl.BlockSpec(swa_tab.shape, lambda b, i: (0, 0, 0))],
        out_specs=[q_spec, q_spec],
        out_shape=[jax.ShapeDtypeStruct((B, S, W), _F32)] * 2,
        scratch_shapes=[pltpu.VMEM((1, R), _F32), pltpu.VMEM((1, R), _F32), pltpu.VMEM((HEAD_DIM, R), _F32)],
        compiler_params=pltpu.CompilerParams(dimension_semantics=("parallel", "arbitrary")),
        name="nsa_attn",
    )(q_r, ks_r, vs_t, kw_r, vw_t, sel_t, swa_tab)


DIL_TQ = 256
DIL_WIN = DIL_CFG[0][0] // DIL_CFG[0][1]
assert all(w // d == DIL_WIN for w, d in DIL_CFG) and DIL_WIN == ATT_T and DIL_TQ == 2 * ATT_T


def _dil_attn_kernel(q_ref, kp_ref, k0_ref, k1_ref, vp_ref, v0_ref, v1_ref, o_ref, lse_ref):
    TQ, W = DIL_TQ, DIL_WIN
    i = pl.program_id(1)
    k = jnp.concatenate([kp_ref[0], k0_ref[0], k1_ref[0]], axis=0)
    v = jnp.concatenate([vp_ref[0], v0_ref[0], v1_ref[0]], axis=0)
    row = lax.broadcasted_iota(jnp.int32, (W + TQ, TQ), 0)
    col = lax.broadcasted_iota(jnp.int32, (W + TQ, TQ), 1)
    dist = W + col - row
    valid = (dist >= 0) & (dist <= W) & ((i > 0) | (row >= W))
    heads = [slice(h * HEAD_DIM, (h + 1) * HEAD_DIM) for h in range(N_HEADS)]
    ss = [lax.dot_general(k[:, hs], q_ref[0, :, hs], _NT, preferred_element_type=_F32) for hs in heads]
    ss = [jnp.where(valid, s, NEG) for s in ss]
    ms = [jnp.max(s, axis=0, keepdims=True) for s in ss]
    ps = [jnp.exp2(s - m) for s, m in zip(ss, ms)]
    ls = [jnp.sum(p, axis=0, keepdims=True) for p in ps]
    os_ = [lax.dot_general(v[:, hs], p.astype(_BF16), _TN, preferred_element_type=_F32) for hs, p in zip(heads, ps)]
    outs = [o / l for o, l in zip(os_, ls)]
    lses = [jnp.broadcast_to(m + jnp.log2(l), (HEAD_DIM, TQ)) for m, l in zip(ms, ls)]
    for t, ref in ((outs, o_ref), (lses, lse_ref)):
        for half in range(TQ // ATT_T):
            cols = slice(half * ATT_T, (half + 1) * ATT_T)
            ref[0, cols, :] = _heads_to_rows(jnp.concatenate([x[:, cols] for x in t], axis=1))


def dil_attn_pallas(q_r, k_r, v, dil):
    B, S, W = q_r.shape
    Sd = S // dil
    TQ, T = DIL_TQ, ATT_T
    R = TQ // T
    view = lambda t: t.reshape(B, Sd, dil * W)
    q_spec = pl.BlockSpec((1, TQ, W), lambda b, i, r: (b, i, r))
    prev = pl.BlockSpec((1, T, W), lambda b, i, r: (b, jnp.maximum(i * R - 1, 0), r))
    cur = [pl.BlockSpec((1, T, W), functools.partial(lambda b, i, r, j: (b, i * R + j, r), j=j)) for j in range(R)]
    o, lse = pl.pallas_call(
        _dil_attn_kernel,
        grid=(B, Sd // TQ, dil),
        in_specs=[q_spec, prev, *cur, prev, *cur],
        out_specs=[q_spec, q_spec],
        out_shape=[jax.ShapeDtypeStruct((B, Sd, dil * W), _F32)] * 2,
        compiler_params=pltpu.CompilerParams(dimension_semantics=("parallel", "parallel", "parallel")),
        name="dil_attn",
    )(view(q_r), *([view(k_r)] * (R + 1)), *([view(v)] * (R + 1)))
    return o.reshape(B, S, W), lse.reshape(B, S, W)


LOG2_E = 1.4426950408889634
ROPE_LANES = 128
PREP_TM = 256


def _rope_table_kernel(pos_ref, inv_ref, cos_ref, slo_ref, shi_ref):
    ang = pos_ref[...].astype(_F32) * inv_ref[...]
    cos, sin = jnp.cos(ang), jnp.sin(ang)
    first_half = (lax.broadcasted_iota(jnp.int32, ang.shape, 1) % HEAD_DIM) < HEAD_DIM // 2
    cos_ref[...] = cos
    slo_ref[...] = jnp.where(first_half, -sin, 0.0)
    shi_ref[...] = jnp.where(first_half, 0.0, sin)


def rope_tables(positions):
    B, S = positions.shape
    T = B * S
    half = HEAD_DIM // 2
    inv = ROPE_THETA ** (-jnp.arange(half, dtype=_F32) / half)
    inv = jnp.tile(inv, ROPE_LANES // half).reshape(1, ROPE_LANES)
    pos = jnp.broadcast_to(positions.reshape(T, 1), (T, ROPE_LANES))
    tm = 1024
    spec = pl.BlockSpec((tm, ROPE_LANES), lambda i: (i, 0))
    return pl.pallas_call(
        _rope_table_kernel,
        grid=(T // tm,),
        in_specs=[spec, pl.BlockSpec((1, ROPE_LANES), lambda i: (0, 0))],
        out_specs=[spec] * 3,
        out_shape=[jax.ShapeDtypeStruct((T, ROPE_LANES), _F32)] * 3,
        name="rope_tables",
    )(pos, inv)


def _attn_prep_kernel(cq_ref, ck_ref, cv_ref, dq_ref, d1_ref, d2_ref, cos_ref, slo_ref, shi_ref, gains_ref,
                      cqo_ref, cko_ref, cvo_ref, qn_ref, qr_ref, kc_ref, vc_ref, ks_ref, vst_ref, kw_ref, vwt_ref):
    G, hd = GROUP_W, HEAD_DIM
    wide = lambda t: jnp.concatenate([t] * (G // ROPE_LANES), axis=1)
    cos, slo, shi = wide(cos_ref[...]), wide(slo_ref[...]), wide(shi_ref[...])
    head_sum = _group_indicator(G, hd)
    gains = gains_ref[...]
    scale = hd ** -0.5

    def norm(t, n):
        return t * lax.rsqrt(_dot_split(t * t, head_sum) * (1.0 / hd) + NORM_EPS) * gains[n:n + 1]

    def rope(t):
        return t * cos + pltpu.roll(t, G - hd // 2, axis=1) * slo + pltpu.roll(t, hd // 2, axis=1) * shi

    cqo_ref[0] = (rope(norm(cq_ref[0], 0)) * (scale * LOG2_E)).astype(_BF16)
    cko_ref[0] = rope(norm(ck_ref[0], 1)).astype(_BF16)
    cvo_ref[0] = cv_ref[0].astype(_BF16)
    qn = norm(dq_ref[0], 2)
    qn_ref[0] = (qn * scale).astype(_BF16)
    qr_ref[0] = (rope(qn) * (scale * LOG2_E)).astype(_BF16)
    d1 = d1_ref[0]
    d2 = d2_ref[0]
    kc_ref[0] = d1[:, 0:hd]
    vc_ref[0] = d1[:, hd:2 * hd]
    ks_ref[0] = rope(norm(d1, 3))[:, 2 * hd:3 * hd].astype(_BF16)
    kw_ref[0] = rope(norm(d2, 4))[:, 0:hd].astype(_BF16)
    vst_ref[0] = d1.T[3 * hd:4 * hd, :].astype(_BF16)
    vwt_ref[0] = d2.T[hd:2 * hd, :].astype(_BF16)


def attn_prep_pallas(proj, tabs, dil_q_g, dil_k_g, nsa_q_g, nsa_ks_g, nsa_kw_g):
    B, S, _ = proj.shape
    tm = PREP_TM
    G, hd, H = GROUP_W, HEAD_DIM, N_HEADS
    c0 = (SECTION_WIDTHS[0] + SECTION_WIDTHS[1]) // G
    z = jnp.zeros((hd,), _F32)
    gains = jnp.stack([jnp.tile(dil_q_g, H), jnp.tile(dil_k_g, H), jnp.tile(nsa_q_g, H),
                       jnp.concatenate([z, z, nsa_ks_g, z]), jnp.concatenate([nsa_kw_g, z, z, z])])
    col = lambda c: pl.BlockSpec((1, tm, G), lambda b, i: (b, i, c))
    nt = S // tm
    tab = pl.BlockSpec((tm, ROPE_LANES), lambda b, i: (b * nt + i, 0))
    tok = lambda w, dt: (pl.BlockSpec((1, tm, w), lambda b, i: (b, i, 0)), jax.ShapeDtypeStruct((B, S, w), dt))
    tr = (pl.BlockSpec((1, hd, tm), lambda b, i: (b, 0, i)), jax.ShapeDtypeStruct((B, hd, S), _BF16))
    outs = [tok(G, _BF16), tok(G, _BF16), tok(G, _BF16), tok(G, _BF16), tok(G, _BF16), tok(hd, _F32), tok(hd, _F32), tok(hd, _BF16), tr, tok(hd, _BF16), tr]
    return pl.pallas_call(
        _attn_prep_kernel,
        grid=(B, nt),
        in_specs=[col(c0), col(c0 + 1), col(c0 + 2), col(c0 + 3), col(c0 + 4), col(c0 + 5), tab, tab, tab,
                  pl.BlockSpec(gains.shape, lambda b, i: (0, 0))],
        out_specs=[o[0] for o in outs],
        out_shape=[o[1] for o in outs],
        compiler_params=pltpu.CompilerParams(dimension_semantics=("parallel", "parallel")),
        name="attn_prep",
    )(proj, proj, proj, proj, proj, proj, *tabs, gains)


NSA_NC = 256
CMP_CHUNK = CMP_STRIDE * HEAD_DIM


def _nsa_compress_kernel(kc_ref, vc_ref, pek_ref, pev_ref, wk1_ref, wk2_ref, wv1_ref, wv2_ref, g_ref, ko_ref, vo_ref):
    def compress(x_ref, pe_ref, w1_ref, w2_ref):
        x = x_ref[0]
        top = _dot_lo(x + pe_ref[0:1], w1_ref[0:CMP_CHUNK, :])
        bot = _dot_lo(x + pe_ref[1:2], w1_ref[CMP_CHUNK:, :])
        hid = top + pltpu.roll(bot, NSA_NC - 1, axis=0)
        return _dot_lo(jax.nn.gelu(hid), w2_ref[...])

    k = compress(kc_ref, pek_ref, wk1_ref, wk2_ref)
    ko_ref[0] = k * lax.rsqrt(jnp.mean(k * k, axis=-1, keepdims=True) + NORM_EPS) * g_ref[...]
    vo_ref[0] = compress(vc_ref, pev_ref, wv1_ref, wv2_ref)


def nsa_compress_pallas(kc, vc, pe_k, pe_v, wk1, wk2, wv1, wv2, kc_g):
    B, S, hd = kc.shape
    assert S // CMP_STRIDE == NSA_NC
    chunks = lambda t: t.reshape(B, NSA_NC, CMP_CHUNK)
    pe2 = lambda p: p.reshape(2, CMP_CHUNK)
    const = lambda t: pl.BlockSpec(t.shape, lambda b: (0,) * t.ndim)
    x_spec = pl.BlockSpec((1, NSA_NC, CMP_CHUNK), lambda b: (b, 0, 0))
    o_spec = pl.BlockSpec((1, NSA_NC, hd), lambda b: (b, 0, 0))
    params = [pe2(pe_k), pe2(pe_v), wk1.astype(_BF16), wk2.astype(_BF16), wv1.astype(_BF16), wv2.astype(_BF16),
              kc_g.reshape(1, hd)]
    return pl.pallas_call(
        _nsa_compress_kernel,
        grid=(B,),
        in_specs=[x_spec, x_spec] + [const(p) for p in params],
        out_specs=[o_spec, o_spec],
        out_shape=[jax.ShapeDtypeStruct((B, NSA_NC, hd), _F32)] * 2,
        name="nsa_compress",
    )(chunks(kc), chunks(vc), *params)


def selection_overlap(n_c, n_sel):
    r = SEL_BLK // CMP_STRIDE
    m = CMP_LEN // CMP_STRIDE
    diff = np.arange(n_c)[:, None] - r * np.arange(n_sel)[None, :]
    offs = (np.arange(r)[:, None] - np.arange(m)[None, :]).reshape(-1)
    return (diff[..., None] == offs).sum(-1).astype(np.float32)


def _nsa_select_kernel(q_ref, kc_ref, vc_ref, ovt_ref, o_ref, selt_ref):
    T = ATT_T
    qi = pl.program_id(1)
    kc = kc_ref[0].astype(_BF16)
    vc = vc_ref[0].astype(_BF16)
    cblk = lax.broadcasted_iota(jnp.int32, (NSA_NC, T), 0)
    tq = qi * T + lax.broadcasted_iota(jnp.int32, (NSA_NC, T), 1)
    valid = cblk * CMP_STRIDE + (CMP_LEN - 1) <= tq
    any_valid = (tq[0:1, :] >= CMP_LEN - 1).astype(_F32)
    heads = range(N_HEADS)
    ss = [lax.dot_general(kc, q_ref[0, :, h * HEAD_DIM:(h + 1) * HEAD_DIM], _NT, preferred_element_type=_F32)
          for h in heads]
    ss = [jnp.where(valid, s, NEG) for s in ss]
    es = [jnp.exp(s - jnp.max(s, axis=0, keepdims=True)) for s in ss]
    ps = [e * (any_valid / jnp.sum(e, axis=0, keepdims=True)) for e in es]
    outs = [lax.dot_general(vc, p.astype(_BF16), _TN, preferred_element_type=_F32) for p in ps]
    p_sum = sum(ps[1:], ps[0])
    o_ref[0] = _heads_to_rows(jnp.concatenate(outs, axis=1))

    ovt = ovt_ref[...]
    hi = p_sum.astype(_BF16)
    lo = (p_sum - hi.astype(_F32)).astype(_BF16)
    imp = jnp.dot(ovt, hi, preferred_element_type=_F32) + jnp.dot(ovt, lo, preferred_element_type=_F32)
    jb = lax.broadcasted_iota(jnp.int32, (NSA_NSEL, T), 0)
    cur = (qi * T + lax.broadcasted_iota(jnp.int32, (NSA_NSEL, T), 1)) // SEL_BLK
    forced = (jb == 0) | (jb == cur) | (jb == cur - 1)
    score = jnp.where(jb > cur, -1.0, jnp.where(forced, FORCE_SCORE, imp))
    jbf = jb.astype(_F32)
    sel = jnp.zeros((NSA_NSEL, T), _F32)
    for _ in range(min(SEL_TOP, NSA_NSEL)):
        best = jnp.max(score, axis=0, keepdims=True)
        first = jnp.min(jnp.where(score == best, jbf, float(NSA_NSEL)), axis=0, keepdims=True)
        pick = jbf == first
        sel = jnp.where(pick, 1.0, sel)
        score = jnp.where(pick, NEG, score)
    selt_ref[0] = sel.astype(_BF16)


def nsa_select_pallas(q_n, k_cmp, v_cmp):
    B, S, W = q_n.shape
    T = ATT_T
    n_c = (S - CMP_LEN) // CMP_STRIDE + 1
    ovt = np.zeros((NSA_NSEL, NSA_NC), np.float32)
    ovt[:, :n_c] = selection_overlap(n_c, NSA_NSEL).T
    ovt = jnp.asarray(ovt, _BF16)
    c_spec = pl.BlockSpec((1, NSA_NC, HEAD_DIM), lambda b, i: (b, 0, 0))
    return pl.pallas_call(
        _nsa_select_kernel,
        grid=(B, S // T),
        in_specs=[pl.BlockSpec((1, T, W), lambda b, i: (b, i, 0)), c_spec, c_spec,
                  pl.BlockSpec(ovt.shape, lambda b, i: (0, 0))],
        out_specs=[pl.BlockSpec((1, T, W), lambda b, i: (b, i, 0)),
                   pl.BlockSpec((1, NSA_NSEL, T), lambda b, i: (b, 0, i))],
        out_shape=[jax.ShapeDtypeStruct((B, S, W), _F32), jax.ShapeDtypeStruct((B, NSA_NSEL, S), _BF16)],
        compiler_params=pltpu.CompilerParams(dimension_semantics=("parallel", "parallel")),
        name="nsa_select",
    )(q_n, k_cmp, v_cmp, ovt)


def kernel(x, c, positions, w_ada, b_ada, norm1_g, norm2_g, w_in, w_out, rwkv_mu, rwkv_w0, rwkv_w2, rwkv_a0, rwkv_a2, rwkv_g2, rwkv_kk, rwkv_ka, rwkv_rk, rwkv_gn_w, rwkv_gn_b, conv_w, dil_q_g, dil_k_g, nsa_q_g, nsa_kc_g, nsa_ks_g, nsa_kw_g, nsa_pe_k, nsa_pe_v, nsa_wk1, nsa_wk2, nsa_wv1, nsa_wv2, onorm_g, w_router, b_router, w_gu, b_gu, w_down, b_down):
    B, S, D = x.shape
    tabs = rope_tables(positions)
    for l in range(DEPTH):
        mod = matmul(jax.nn.silu(c), w_ada[l], tm=B, tn=512) + b_ada[l]
        sh1, sc1, gt1, sh2, sc2, gt2 = [m[:, None, :] for m in jnp.split(mod, 6, axis=-1)]
        proj = in_proj_pallas(x, norm1_g[l], sc1, sh1, w_in[l])
        y_a = rwkv7_pallas(proj, rwkv_mu[l], rwkv_w0[l], rwkv_w2[l], rwkv_a0[l], rwkv_a2[l], rwkv_g2[l],
                           rwkv_kk[l], rwkv_ka[l], rwkv_rk[l], rwkv_gn_w[l], rwkv_gn_b[l])
        cq, ck, cv, q_n, q_r, kc, vc, ks, vs_t, kw, vw_t = attn_prep_pallas(
            proj, tabs, dil_q_g[l], dil_k_g[l], nsa_q_g[l], nsa_ks_g[l], nsa_kw_g[l])
        dil_outs = [dil_attn_pallas(cq, ck, cv, dil) for _, dil in DIL_CFG]
        k_cmp, v_cmp = nsa_compress_pallas(kc, vc, nsa_pe_k[l], nsa_pe_v[l], nsa_wk1[l], nsa_wk2[l], nsa_wv1[l],
                                           nsa_wv2[l], nsa_kc_g[l])
        o_cmp, sel_t = nsa_select_pallas(q_n, k_cmp, v_cmp)
        o_slc, o_swa = nsa_attn_pallas(q_r, ks, vs_t, kw, vw_t, sel_t)
        x, hb, logits = mix_out_pallas(x, y_a, proj, dil_outs, o_cmp, o_slc, o_swa, conv_w[l], onorm_g[l], w_out[l], gt1,
                                       norm2_g[l], sc2, sh2, w_router[l], b_router[l])
        x = moe_pallas(x, gt2, hb, logits, l, w_gu, b_gu, w_down, b_down)
    return x
```

```python
import functools

import numpy as np
import jax
import jax.numpy as jnp
from jax import lax
from jax.experimental import pallas as pl
from jax.experimental.pallas import tpu as pltpu

D_MODEL = 1024
DEPTH = 2

HEAD_DIM = 64
N_MIXERS = 4
GROUP_W = D_MODEL // N_MIXERS
N_HEADS = GROUP_W // HEAD_DIM
MIX_W = N_MIXERS * GROUP_W

RWKV_DECAY_RANK = 64
RWKV_AAA_RANK = 64
RWKV_GATE_RANK = 128
RWKV_DECAY_SCALE = 0.606531
RWKV_GN_EPS = 64e-5

CONV_W = 3

DIL_CFG = ((128, 1), (512, 4), (2048, 16))

CMP_LEN = 32
CMP_STRIDE = 16
CMP_HIDDEN = 256
SEL_BLK = 64
SEL_TOP = 16
SWA_WIN = 512
FORCE_SCORE = 1e4

N_EXPERTS = 32
TOP_K = 4
EXPERT_FF = D_MODEL
SWIGLU_LIMIT = 7.0
SWIGLU_ALPHA = 1.702

ROPE_THETA = 10000.0
NORM_EPS = 1e-6
NEG = -1e30

A_WIDTHS = (GROUP_W, GROUP_W, GROUP_W, RWKV_DECAY_RANK, RWKV_AAA_RANK, RWKV_GATE_RANK)
B_WIDTHS = (GROUP_W, GROUP_W, GROUP_W)
C_WIDTHS = (GROUP_W, GROUP_W, GROUP_W)
D_WIDTHS = (GROUP_W, HEAD_DIM, HEAD_DIM, HEAD_DIM, HEAD_DIM, HEAD_DIM, HEAD_DIM, 3 * N_HEADS)
SECTION_WIDTHS = (sum(A_WIDTHS), sum(B_WIDTHS), sum(C_WIDTHS), sum(D_WIDTHS))
PROJ_W = sum(SECTION_WIDTHS)

_F32 = jnp.float32
_BF16 = jnp.bfloat16
_NT = (((1,), (1,)), ((), ()))
_TN = (((0,), (0,)), ((), ()))


def _dot_hi(a, b, dims=None):
    if dims is None:
        return jnp.dot(a, b, precision=lax.Precision.HIGHEST, preferred_element_type=_F32)
    return lax.dot_general(a, b, dims, precision=lax.Precision.HIGHEST, preferred_element_type=_F32)


def _dot_lo(a, b, dims=None):
    a, b = a.astype(_BF16), b.astype(_BF16)
    if dims is None:
        return jnp.dot(a, b, preferred_element_type=_F32)
    return lax.dot_general(a, b, dims, preferred_element_type=_F32)


def _bf16_parts(a, parts):
    out = []
    for _ in range(parts):
        p = a.astype(_BF16)
        out.append(p)
        a = a - p.astype(_F32)
    return out


def _dot_split(a, m, parts=2):
    mb = m.astype(_BF16)
    return sum(jnp.dot(p, mb, preferred_element_type=_F32) for p in _bf16_parts(a, parts))


def _dot_split_left(m, a, parts=2):
    mb = m.astype(_BF16)
    return sum(jnp.dot(mb, p, preferred_element_type=_F32) for p in _bf16_parts(a, parts))


def _dot_x3(a, b):
    ah, al = _bf16_parts(a, 2)
    bh, bl = _bf16_parts(b, 2)
    d = lambda x, y: jnp.dot(x, y, preferred_element_type=_F32)
    return d(ah, bh) + (d(ah, bl) + d(al, bh))


def _matmul_kernel(x_ref, w_ref, o_ref, *, exact):
    if exact:
        o_ref[...] = _dot_hi(x_ref[...], w_ref[...])
    else:
        o_ref[...] = jnp.dot(x_ref[...].astype(_BF16), w_ref[...], preferred_element_type=_F32)


def matmul(x, w, tm=512, tn=256, exact=False):
    M, K = x.shape
    N = w.shape[1]
    n_pad = -(-N // tn) * tn
    wb = jnp.pad(w if exact else w.astype(_BF16), ((0, 0), (0, n_pad - N)))
    out = pl.pallas_call(
        functools.partial(_matmul_kernel, exact=exact),
        grid=(M // tm, n_pad // tn),
        in_specs=[pl.BlockSpec((tm, K), lambda i, j: (i, 0)),
                  pl.BlockSpec((K, tn), lambda i, j: (0, j))],
        out_specs=pl.BlockSpec((tm, tn), lambda i, j: (i, j)),
        out_shape=jax.ShapeDtypeStruct((M, n_pad), _F32),
        name="matmul",
    )(x, wb)
    return out[:, :N]


PROJ_PAD = -(-PROJ_W // 256) * 256
PROJ_TM = 256
PROJ_TN = 512
ROUTER_PAD = 128
GATE_COL_BLOCK = (PROJ_W - 3 * N_HEADS) // 128
assert GATE_COL_BLOCK * 128 == PROJ_W - 3 * N_HEADS


def _norm_mod(x, g, sc, sh):
    y = x * lax.rsqrt(jnp.mean(x * x, axis=-1, keepdims=True) + NORM_EPS) * g
    return y * (1.0 + sc) + sh


def _in_proj_kernel(x_ref, g_ref, sc_ref, sh_ref, w_ref, o_ref):
    h = _norm_mod(x_ref[0], g_ref[...], sc_ref[0], sh_ref[0]).astype(_BF16)
    for n0 in range(0, PROJ_PAD, PROJ_TN):
        o_ref[0, :, n0:n0 + PROJ_TN] = jnp.dot(h, w_ref[:, n0:n0 + PROJ_TN], preferred_element_type=_F32)


def in_proj_pallas(x, g, sc, sh, w_in):
    B, S, D = x.shape
    tm = PROJ_TM
    w = jnp.pad(w_in.astype(_BF16), ((0, 0), (0, PROJ_PAD - PROJ_W)))
    per_batch = pl.BlockSpec((1, 1, D), lambda b, i: (b, 0, 0))
    return pl.pallas_call(
        _in_proj_kernel,
        grid=(B, S // tm),
        in_specs=[pl.BlockSpec((1, tm, D), lambda b, i: (b, i, 0)),
                  pl.BlockSpec((1, D), lambda b, i: (0, 0)), per_batch, per_batch,
                  pl.BlockSpec((D, PROJ_PAD), lambda b, i: (0, 0))],
        out_specs=pl.BlockSpec((1, tm, PROJ_PAD), lambda b, i: (b, i, 0)),
        out_shape=jax.ShapeDtypeStruct((B, S, PROJ_PAD), _F32),
        compiler_params=pltpu.CompilerParams(dimension_semantics=("parallel", "parallel"),
                                             vmem_limit_bytes=48 * 1024 * 1024),
        name="in_proj",
    )(x, g.reshape(1, D), sc, sh, w)


def _mix_out_kernel(x_ref, ya_ref, bg_ref, cg_ref, xin_ref, cgh_ref, xinh_ref, do0_ref, dl0_ref, do1_ref, dl1_ref,
                    do2_ref, dl2_ref, ocmp_ref, oslc_ref, oswa_ref,
                    gl_ref, convw_ref, ong_ref, wout_ref, gt1_ref, g2_ref, sc2_ref, sh2_ref, wr_ref, br_ref,
                    xo_ref, hb_ref, lg_ref):
    i = pl.program_id(1)
    G = GROUP_W
    u = cg_ref[0] * xin_ref[0]
    halo = jnp.where(i == 0, 0.0, cgh_ref[0] * xinh_ref[0])
    row = lax.broadcasted_iota(jnp.int32, u.shape, 0)
    u1 = jnp.where(row == 0, halo[7:8], pltpu.roll(u, 1, axis=0))
    u2 = jnp.where(row == 0, halo[6:7], jnp.where(row == 1, halo[7:8], pltpu.roll(u, 2, axis=0)))
    cw = convw_ref[...]
    yb = bg_ref[0] * (cw[0:1] * u2 + cw[1:2] * u1 + cw[2:3] * u)

    sg = jax.nn.sigmoid(gl_ref[0])
    er = lax.broadcasted_iota(jnp.int32, (128, G), 0)
    ec = lax.broadcasted_iota(jnp.int32, (128, G), 1) // HEAD_DIM
    yd = None
    for j, o_ref in enumerate((ocmp_ref, oslc_ref, oswa_ref)):
        gate = _dot_split(sg, er == 3 * ec + j)
        yd = gate * o_ref[0] if yd is None else yd + gate * o_ref[0]

    head_sum = _group_indicator(G, HEAD_DIM)
    ong = ong_ref[...]
    parts = [ya_ref[0]]
    dil = ((do0_ref, dl0_ref), (do1_ref, dl1_ref), (do2_ref, dl2_ref))
    top = functools.reduce(jnp.maximum, [l_ref[0] for _, l_ref in dil])
    wts = [jnp.exp2(l_ref[0] - top) for _, l_ref in dil]
    yc = sum(w * o_ref[0] for w, (o_ref, _) in zip(wts, dil)) / sum(wts)
    for n, y in enumerate((yb, yc, yd)):
        ms = _dot_split(y * y, head_sum) * (1.0 / HEAD_DIM)
        parts.append(y * lax.rsqrt(ms + NORM_EPS) * ong[:, n * G:(n + 1) * G])
    mixed = jnp.dot(jnp.concatenate(parts, axis=1).astype(_BF16), wout_ref[...], preferred_element_type=_F32)
    x = x_ref[0] + gt1_ref[0] * mixed
    xo_ref[0] = x
    h = _norm_mod(x, g2_ref[...], sc2_ref[0], sh2_ref[0])
    hb_ref[0] = h.astype(_BF16)
    lg_ref[0] = _dot_x3(h, wr_ref[...]) + br_ref[...]


def mix_out_pallas(x, y_a, proj, dil_outs, o_cmp, o_slc, o_swa, conv_w, onorm_g, w_out, gt1, g2, sc2, sh2, w_router, b_router):
    B, S, D = x.shape
    tm = PROJ_TM
    G = GROUP_W
    b0 = SECTION_WIDTHS[0] // G
    tile = lambda w, col: pl.BlockSpec((1, tm, w), lambda b, i: (b, i, col))
    halo = lambda col: pl.BlockSpec((1, 8, G), lambda b, i: (b, jnp.maximum(i * (tm // 8) - 1, 0), col))
    const = lambda t: pl.BlockSpec(t.shape, lambda b, i: (0,) * t.ndim)
    per_batch = pl.BlockSpec((1, 1, D), lambda b, i: (b, 0, 0))
    wr = jnp.pad(w_router, ((0, 0), (0, ROUTER_PAD - N_EXPERTS)))
    br = jnp.pad(b_router, (0, ROUTER_PAD - N_EXPERTS)).reshape(1, ROUTER_PAD)
    consts = [conv_w, onorm_g.reshape(1, 3 * G), w_out.astype(_BF16)]
    return pl.pallas_call(
        _mix_out_kernel,
        grid=(B, S // tm),
        in_specs=[tile(D, 0), tile(G, 0), tile(G, b0), tile(G, b0 + 1), tile(G, b0 + 2), halo(b0 + 1), halo(b0 + 2),
                  *([tile(G, 0)] * (2 * len(dil_outs) + 3)), tile(128, GATE_COL_BLOCK)]
                 + [const(t) for t in consts] + [per_batch, pl.BlockSpec((1, D), lambda b, i: (0, 0)), per_batch,
                                                 per_batch, const(wr), const(br)],
        out_specs=[tile(D, 0), tile(D, 0), tile(ROUTER_PAD, 0)],
        out_shape=[jax.ShapeDtypeStruct((B, S, D), _F32), jax.ShapeDtypeStruct((B, S, D), _BF16),
                   jax.ShapeDtypeStruct((B, S, ROUTER_PAD), _F32)],
        compiler_params=pltpu.CompilerParams(dimension_semantics=("parallel", "parallel")),
        name="mix_out",
    )(x, y_a, proj, proj, proj, proj, proj, *[t for pair in dil_outs for t in pair], o_cmp, o_slc, o_swa, proj, *consts, gt1, g2.reshape(1, D), sc2, sh2,
      wr, br)


RWKV_CHUNK = 64
RWKV_TILE = 512


def _group_indicator(n, group):
    r = lax.broadcasted_iota(jnp.int32, (n, n), 0) // group
    c = lax.broadcasted_iota(jnp.int32, (n, n), 1) // group
    return r == c


def _rwkv_kernel(za_ref, mu_ref, w0_ref, w2_ref, a0_ref, a2_ref, g2_ref, kk_ref, ka_ref, rk_ref,
                 gnw_ref, gnb_ref, o_ref, prev_sc, h_sc):
    C = RWKV_CHUNK
    TT = RWKV_TILE
    c = pl.program_id(1)

    @pl.when(c == 0)
    def _():
        prev_sc[...] = jnp.zeros_like(prev_sc)
        h_sc[...] = jnp.zeros_like(h_sc)

    z = za_ref[0]
    row = lax.broadcasted_iota(jnp.int32, z.shape, 0)
    zs = jnp.where(row == 0, prev_sc[...], pltpu.roll(z, 1, axis=0))
    prev_sc[...] = z[TT - 1:TT, :]
    z = z + (zs - z) * mu_ref[...]

    G = GROUP_W
    r, k, v = z[:, 0:G], z[:, G:2 * G], z[:, 2 * G:3 * G]
    o = 3 * G
    wd = z[:, o:o + RWKV_DECAY_RANK]
    ad = z[:, o + RWKV_DECAY_RANK:o + RWKV_DECAY_RANK + RWKV_AAA_RANK]
    gd = z[:, o + RWKV_DECAY_RANK + RWKV_AAA_RANK:]

    lw = -RWKV_DECAY_SCALE * jax.nn.sigmoid(w0_ref[...] + _dot_lo(jnp.tanh(wd), w2_ref[...]))
    a = jax.nn.sigmoid(a0_ref[...] + _dot_lo(ad, a2_ref[...]))
    g = _dot_lo(jax.nn.sigmoid(gd), g2_ref[...])

    head_sum = _group_indicator(G, HEAD_DIM)
    kk = k * kk_ref[...]
    kk = kk * lax.rsqrt(_dot_split(kk * kk, head_sum) + 1e-12)
    k = k * (1.0 + (a - 1.0) * ka_ref[...])
    b = kk * a

    ti = lax.broadcasted_iota(jnp.int32, (C, C), 0)
    tj = lax.broadcasted_iota(jnp.int32, (C, C), 1)
    incl = ti >= tj
    strict = ti > tj
    eye = ti == tj
    blk16 = (ti // 16) == (tj // 16)
    blk32 = (ti // 32) == (tj // 32)
    eye_f = eye.astype(_F32)

    ri = lax.broadcasted_iota(jnp.int32, (TT, TT), 0)
    rj = lax.broadcasted_iota(jnp.int32, (TT, TT), 1)
    chunk_tri = (ri >= rj) & (ri // C == rj // C)
    cum = _dot_split_left(chunk_tri, lw, parts=3)
    g_in = jnp.exp(cum)
    A_all = -kk * jnp.exp(cum - lw)
    R_all = r * g_in
    g_inv = jnp.exp(-cum)
    B_all = b * g_inv
    K_all = k * g_inv

    units = [(ci, h) for ci in range(TT // C) for h in range(N_HEADS)]

    def part(t, u):
        ci, h = u
        return t[ci * C:(ci + 1) * C, h * HEAD_DIM:(h + 1) * HEAD_DIM]

    def g_end(u):
        ci, h = u
        return g_in[(ci + 1) * C - 1:(ci + 1) * C, h * HEAD_DIM:(h + 1) * HEAD_DIM]

    A = [part(A_all, u) for u in units]
    R = [part(R_all, u) for u in units]
    B = [part(B_all, u) for u in units]
    Kt = [part(K_all, u) for u in units]
    V = [part(v, u) for u in units]
    n = range(len(units))
    gram = [_dot_lo(jnp.concatenate([A[i], R[i]], axis=0), jnp.concatenate([B[i], Kt[i]], axis=0), _NT) for i in n]
    l_ab = [jnp.where(strict, gram[i][0:C, 0:C], 0.0) for i in n]
    l_ak = [jnp.where(strict, gram[i][0:C, C:2 * C], 0.0) for i in n]
    m_rb = [jnp.where(incl, gram[i][C:2 * C, 0:C], 0.0) for i in n]
    m_rk = [jnp.where(incl, gram[i][C:2 * C, C:2 * C], 0.0) for i in n]
    p = [jnp.where(blk16, l_ab[i], 0.0) for i in n]
    x = [eye_f + p[i] for i in n]
    for _ in range(3):
        p = [_dot_lo(p[i], p[i]) for i in n]
        x = [_dot_lo(x[i], eye_f + p[i]) for i in n]
    for lvl in (blk32 & ~blk16, ~blk32):
        xl = [_dot_lo(x[i], jnp.where(lvl, l_ab[i], 0.0)) for i in n]
        x = [x[i] + _dot_lo(xl[i], x[i]) for i in n]
    lv = [_dot_lo(l_ak[i], V[i]) for i in n]
    tap = [_dot_lo(x[i], jnp.concatenate([A[i], lv[i]], axis=1)) for i in n]
    m1 = [_dot_lo(m_rb[i], tap[i]) for i in n]
    mv = [_dot_lo(m_rk[i], V[i]) for i in n]
    bt = [_dot_lo(B[i] * g_end(units[i]), tap[i], _TN) for i in n]
    kv = [_dot_lo(Kt[i] * g_end(units[i]), V[i], _TN) for i in n]
    w_yh = [jnp.concatenate([R[i] + m1[i][:, 0:HEAD_DIM],
                             jnp.where(eye, g_end(units[i]), 0.0) + bt[i][:, 0:HEAD_DIM]], axis=0) for i in n]
    y0 = [m1[i][:, HEAD_DIM:] + mv[i] for i in n]
    h_add = [bt[i][:, HEAD_DIM:] + kv[i] for i in n]
    state = [h_sc[h] for h in range(N_HEADS)]
    y_rows = []
    for ci in range(TT // C):
        ys = []
        for h in range(N_HEADS):
            i = ci * N_HEADS + h
            nxt = _dot_x3(w_yh[i], state[h])
            ys.append(nxt[0:C] + y0[i])
            state[h] = nxt[C:2 * C] + h_add[i]
        y_rows.append(jnp.concatenate(ys, axis=1))
    for h in range(N_HEADS):
        h_sc[h] = state[h]
    y = jnp.concatenate(y_rows, axis=0)

    mu = _dot_split(y, head_sum) * (1.0 / HEAD_DIM)
    d = y - mu
    var = _dot_split(d * d, head_sum) * (1.0 / HEAD_DIM)
    y = d * lax.rsqrt(var + RWKV_GN_EPS) * gnw_ref[...] + gnb_ref[...]
    y = y + _dot_split(r * k * rk_ref[...], head_sum) * v
    o_ref[0] = y * g


def rwkv7_pallas(za, mu, w0, w2, a0, a2, g2, k_k, k_a, r_k, gn_w, gn_b):
    B, S, _ = za.shape
    W = SECTION_WIDTHS[0]
    TT = RWKV_TILE
    row = lambda t: t.reshape(1, -1).astype(_F32)
    full = lambda t: pl.BlockSpec(t.shape, lambda b, c: (0,) * t.ndim)
    params = [row(mu), row(w0), w2.astype(_BF16), row(a0), a2.astype(_BF16), g2.astype(_BF16),
              row(k_k), row(k_a), row(r_k), row(gn_w), row(gn_b)]
    return pl.pallas_call(
        _rwkv_kernel,
        grid=(B, S // TT),
        in_specs=[pl.BlockSpec((1, TT, W), lambda b, c: (b, c, 0))] + [full(p) for p in params],
        out_specs=pl.BlockSpec((1, TT, GROUP_W), lambda b, c: (b, c, 0)),
        out_shape=jax.ShapeDtypeStruct((B, S, GROUP_W), _F32),
        scratch_shapes=[pltpu.VMEM((1, W), _F32), pltpu.VMEM((N_HEADS, HEAD_DIM, HEAD_DIM), _F32)],
        compiler_params=pltpu.CompilerParams(dimension_semantics=("parallel", "arbitrary")),
        name="rwkv7",
    )(za, *params)


MOE_TM = 512
MOE_FC = 512
MOE_CAST_ROWS = 128


def _moe_ffn_kernel(blk_e_ref, n_used_ref, x_ref, wgu_ref, bgu_ref, wd_ref, bd_ref, o_ref, wgu_sc, wd_sc):
    i = pl.program_id(0)
    F = EXPERT_FF
    new_expert = jnp.logical_or(i == 0, blk_e_ref[i] != blk_e_ref[jnp.maximum(i - 1, 0)])

    @pl.when(new_expert)
    def _():
        def cast(j, carry):
            rows = pl.ds(pl.multiple_of(j * MOE_CAST_ROWS, MOE_CAST_ROWS), MOE_CAST_ROWS)
            wgu_sc[rows, :] = wgu_ref[0, rows, :].astype(_BF16)
            wd_sc[rows, :] = wd_ref[0, rows, :].astype(_BF16)
            return carry
        lax.fori_loop(0, F // MOE_CAST_ROWS, cast, 0)

    @pl.when(i < n_used_ref[0])
    def _():
        x = x_ref[...]
        acc = None
        for c in range(F // MOE_FC):
            lo = c * MOE_FC
            gate = jnp.dot(x, wgu_sc[:, lo:lo + MOE_FC], preferred_element_type=_F32) + bgu_ref[0, :, lo:lo + MOE_FC]
            up = jnp.dot(x, wgu_sc[:, F + lo:F + lo + MOE_FC], preferred_element_type=_F32) + bgu_ref[0, :, F + lo:F + lo + MOE_FC]
            gate = jnp.minimum(gate, SWIGLU_LIMIT)
            up = jnp.clip(up, -SWIGLU_LIMIT, SWIGLU_LIMIT)
            act = gate * jax.nn.sigmoid(SWIGLU_ALPHA * gate) * (up + 1.0)
            part = jnp.dot(act.astype(_BF16), wd_sc[lo:lo + MOE_FC, :], preferred_element_type=_F32)
            acc = part if acc is None else acc + part
        o_ref[...] = (acc + bd_ref[0]).astype(o_ref.dtype)

    @pl.when(i >= n_used_ref[0])
    def _():
        o_ref[...] = jnp.zeros_like(o_ref)


def moe_ffn(xs, blk_e, n_used, layer, w_gu, b_gu, w_down, b_down):
    n_rows, D = xs.shape
    L, E, _, F2 = w_gu.shape
    assert D == EXPERT_FF and F2 == 2 * EXPERT_FF
    n_blk = n_rows // MOE_TM
    grid_spec = pltpu.PrefetchScalarGridSpec(
        num_scalar_prefetch=2,
        grid=(n_blk,),
        in_specs=[pl.BlockSpec((MOE_TM, D), lambda i, e, n: (i, 0)),
                  pl.BlockSpec((None, 1, D, F2), lambda i, e, n: (layer, e[i], 0, 0)),
                  pl.BlockSpec((None, 1, 1, F2), lambda i, e, n: (layer, e[i], 0, 0)),
                  pl.BlockSpec((None, 1, F2 // 2, D), lambda i, e, n: (layer, e[i], 0, 0)),
                  pl.BlockSpec((None, 1, 1, D), lambda i, e, n: (layer, e[i], 0, 0))],
        out_specs=pl.BlockSpec((MOE_TM, D), lambda i, e, n: (i, 0)),
        scratch_shapes=[pltpu.VMEM((D, F2), _BF16), pltpu.VMEM((F2 // 2, D), _BF16)],
    )
    return pl.pallas_call(
        _moe_ffn_kernel,
        grid_spec=grid_spec,
        out_shape=jax.ShapeDtypeStruct((n_rows, D), _BF16),
        compiler_params=pltpu.CompilerParams(dimension_semantics=("arbitrary",),
                                             vmem_limit_bytes=52 * 1024 * 1024),
        name="moe_ffn",
    )(blk_e, n_used, xs, w_gu, b_gu.reshape(L, E, 1, F2), w_down, b_down.reshape(L, E, 1, D))


def moe_pallas(x, gt2, hb, logits, layer, w_gu, b_gu, w_down, b_down):
    D = x.shape[-1]
    hb = hb.reshape(-1, D)
    logits = logits.reshape(hb.shape[0], -1)
    T = hb.shape[0]
    TK = T * TOP_K
    TM = MOE_TM
    E = N_EXPERTS
    i32 = jnp.int32
    top_val, top_idx = lax.top_k(logits[:, :E], TOP_K)
    gates = jax.nn.softmax(top_val, axis=-1)
    e_flat = top_idx.T.reshape(TK).astype(i32)
    counts = jnp.sum((jnp.arange(E, dtype=i32)[:, None] == e_flat[None, :]).astype(i32), axis=1)
    need = jnp.repeat((-counts) % TM, TM)
    d_idx = jnp.arange(E * TM, dtype=i32)
    d_key = jnp.where(d_idx % TM < need, d_idx // TM, E)
    n_rows = TK + E * TM
    keys = jnp.concatenate([e_flat, d_key])
    toks = jnp.concatenate([jnp.arange(TK, dtype=i32) % T, jnp.zeros((E * TM,), i32)])
    rows = jnp.arange(n_rows, dtype=i32)
    s_keys, src_tok, s_slot = lax.sort((keys, toks, rows), num_keys=1)
    _, row_of_slot = lax.sort((s_slot, rows), num_keys=1)
    row_of = row_of_slot[:TK]
    blk_e = s_keys[::TM]
    n_used = jnp.sum((blk_e < E).astype(i32)).reshape(1)
    blk_e = jnp.where(blk_e < E, blk_e, blk_e[jnp.maximum(n_used[0] - 1, 0)])
    xs = jnp.take(hb, src_tok, axis=0, mode="clip")
    ybuf = moe_ffn(xs, blk_e, n_used, layer, w_gu, b_gu, w_down, b_down)
    y = jnp.take(ybuf, row_of, axis=0, mode="clip").reshape(TOP_K, T, D)
    return moe_combine(x, y, gates, gt2)


def _moe_combine_kernel(y_ref, g_ref, x_ref, gt2_ref, o_ref):
    g = g_ref[...]
    acc = None
    for k in range(TOP_K):
        term = y_ref[k].astype(_F32) * g[:, k:k + 1]
        acc = term if acc is None else acc + term
    o_ref[0] = x_ref[0] + gt2_ref[0] * acc


def moe_combine(x, y, gates, gt2):
    B, S, D = x.shape
    tm = PROJ_TM
    nt = S // tm
    g = jnp.pad(gates, ((0, 0), (0, 128 - TOP_K)))
    return pl.pallas_call(
        _moe_combine_kernel,
        grid=(B, nt),
        in_specs=[pl.BlockSpec((TOP_K, tm, D), lambda b, i: (0, b * nt + i, 0)),
                  pl.BlockSpec((tm, 128), lambda b, i: (b * nt + i, 0)),
                  pl.BlockSpec((1, tm, D), lambda b, i: (b, i, 0)),
                  pl.BlockSpec((1, 1, D), lambda b, i: (b, 0, 0))],
        out_specs=pl.BlockSpec((1, tm, D), lambda b, i: (b, i, 0)),
        out_shape=jax.ShapeDtypeStruct((B, S, D), _F32),
        compiler_params=pltpu.CompilerParams(dimension_semantics=("parallel", "parallel")),
        name="moe_combine",
    )(y, g, x, gt2)


ATT_T = 128
ATT_TK = 512
ATT_R = ATT_TK // ATT_T
NSA_NSEL = 64


def _window_table(window, weight_of_distance):
    T, TK = ATT_T, ATT_TK
    n = window // T + ATT_R + 1
    d = (np.arange(n)[:, None, None] * T + np.arange(T)[None, None, :] - np.arange(TK)[None, :, None])
    return jnp.asarray(weight_of_distance(d).astype(np.float32))


def _first_key_tile(qi, window):
    return jnp.maximum(qi * ATT_T - window, 0) // ATT_TK


def _swa_weight(d):
    return (d >= 0) & (d <= SWA_WIN - 1)


def _softmax_tile(s, w, m_ref, l_ref):
    s = jnp.where(w > 0.0, s, NEG)
    m_prev = m_ref[...]
    m_new = jnp.maximum(m_prev, jnp.max(s, axis=0, keepdims=True))
    alpha = jnp.exp2(m_prev - m_new)
    p = jnp.exp2(s - m_new)
    l_ref[...] = alpha * l_ref[...] + jnp.sum(p, axis=0, keepdims=True)
    m_ref[...] = m_new
    return alpha, p.astype(_BF16)


def _init_stats(m_sc, l_sc, acc_sc):
    m_sc[...] = jnp.full_like(m_sc, NEG)
    l_sc[...] = jnp.zeros_like(l_sc)
    acc_sc[...] = jnp.zeros_like(acc_sc)


def _heads_to_rows(ot):
    T = ATT_T
    pairs = []
    for h in range(0, N_HEADS, 2):
        two = jnp.concatenate([ot[:, h * T:(h + 1) * T], ot[:, (h + 1) * T:(h + 2) * T]], axis=0)
        pairs.append(two.T)
    return jnp.concatenate(pairs, axis=1)


def _nsa_attn_kernel(q_ref, ks_ref, vst_ref, kw_ref, vwt_ref, selt_ref, swa_ref, oslc_ref, oswa_ref,
                     m_sc, l_sc, acc_sc):
    T, TK = ATT_T, ATT_TK
    qi = pl.program_id(1)
    q = q_ref[0]
    q = jnp.concatenate([q[:, h * HEAD_DIM:(h + 1) * HEAD_DIM] for h in range(N_HEADS)], axis=0)
    selt = selt_ref[0]
    key_blk = lax.broadcasted_iota(jnp.int32, (TK, NSA_NSEL), 0) // SEL_BLK
    blk_id = lax.broadcasted_iota(jnp.int32, (TK, NSA_NSEL), 1)
    kr = lax.broadcasted_iota(jnp.int32, (TK, T), 0)
    qc = lax.broadcasted_iota(jnp.int32, (TK, T), 1)

    def step(k_ref, vt_ref, kt, w):
        base = pl.multiple_of(kt * TK, TK)
        s = lax.dot_general(k_ref[0, pl.ds(base, TK), :], q, _NT, preferred_element_type=_F32)
        alpha, p = _softmax_tile(s, jnp.concatenate([w] * N_HEADS, axis=1), m_sc, l_sc)
        acc_sc[...] = alpha * acc_sc[...] + jnp.dot(vt_ref[0, :, pl.ds(base, TK)], p, preferred_element_type=_F32)

    def finish(o_ref):
        o_ref[0] = _heads_to_rows(acc_sc[...] / l_sc[...])

    last = qi // ATT_R
    _init_stats(m_sc, l_sc, acc_sc)

    def slc_step(kt, carry):
        in_blk = (key_blk + kt * (TK // SEL_BLK) == blk_id).astype(_BF16)
        w = jnp.dot(in_blk, selt, preferred_element_type=_F32)
        w = jnp.where(kt * TK + kr <= qi * T + qc, w, 0.0)
        step(ks_ref, vst_ref, kt, w)
        return carry

    lax.fori_loop(0, last + 1, slc_step, 0)
    finish(oslc_ref)

    _init_stats(m_sc, l_sc, acc_sc)

    def swa_step(kt, carry):
        step(kw_ref, vwt_ref, kt, swa_ref[qi - kt * ATT_R])
        return carry

    lax.fori_loop(_first_key_tile(qi, SWA_WIN - 1), last + 1, swa_step, 0)
    finish(oswa_ref)


def nsa_attn_pallas(q_r, ks_r, vs_t, kw_r, vw_t, sel_t):
    B, S, W = q_r.shape
    assert S // SEL_BLK == NSA_NSEL
    T = ATT_T
    R = N_HEADS * T
    swa_tab = _window_table(SWA_WIN - 1, _swa_weight)
    k_spec = pl.BlockSpec((1, S, HEAD_DIM), lambda b, i: (b, 0, 0))
    vt_spec = pl.BlockSpec((1, HEAD_DIM, S), lambda b, i: (b, 0, 0))
    q_spec = pl.BlockSpec((1, T, W), lambda b, i: (b, i, 0))
    return pl.pallas_call(
        _nsa_attn_kernel,
        grid=(B, S // T),
        in_specs=[q_spec, k_spec, vt_spec, k_spec, vt_spec,
                  pl.BlockSpec((1, NSA_NSEL, T), lambda b, i: (b, 0, i)),
                  pl.BlockSpec(swa_tab.shape, lambda b, i: (0, 0, 0))],
        out_specs=[q_spec, q_spec],
        out_shape=[jax.ShapeDtypeStruct((B, S, W), _F32)] * 2,
        scratch_shapes=[pltpu.VMEM((1, R), _F32), pltpu.VMEM((1, R), _F32), pltpu.VMEM((HEAD_DIM, R), _F32)],
        compiler_params=pltpu.CompilerParams(dimension_semantics=("parallel", "arbitrary")),
        name="nsa_attn",
    )(q_r, ks_r, vs_t, kw_r, vw_t, sel_t, swa_tab)


DIL_TQ = 256
DIL_WIN = DIL_CFG[0][0] // DIL_CFG[0][1]
assert all(w // d == DIL_WIN for w, d in DIL_CFG) and DIL_WIN == ATT_T and DIL_TQ == 2 * ATT_T


def _dil_attn_kernel(q_ref, kp_ref, k0_ref, k1_ref, vp_ref, v0_ref, v1_ref, o_ref, lse_ref):
    TQ, W = DIL_TQ, DIL_WIN
    i = pl.program_id(1)
    k = jnp.concatenate([kp_ref[0], k0_ref[0], k1_ref[0]], axis=0)
    v = jnp.concatenate([vp_ref[0], v0_ref[0], v1_ref[0]], axis=0)
    row = lax.broadcasted_iota(jnp.int32, (W + TQ, TQ), 0)
    col = lax.broadcasted_iota(jnp.int32, (W + TQ, TQ), 1)
    dist = W + col - row
    valid = (dist >= 0) & (dist <= W) & ((i > 0) | (row >= W))
    heads = [slice(h * HEAD_DIM, (h + 1) * HEAD_DIM) for h in range(N_HEADS)]
    ss = [lax.dot_general(k[:, hs], q_ref[0, :, hs], _NT, preferred_element_type=_F32) for hs in heads]
    ss = [jnp.where(valid, s, NEG) for s in ss]
    ms = [jnp.max(s, axis=0, keepdims=True) for s in ss]
    ps = [jnp.exp2(s - m) for s, m in zip(ss, ms)]
    ls = [jnp.sum(p, axis=0, keepdims=True) for p in ps]
    os_ = [lax.dot_general(v[:, hs], p.astype(_BF16), _TN, preferred_element_type=_F32) for hs, p in zip(heads, ps)]
    outs = [o / l for o, l in zip(os_, ls)]
    lses = [jnp.broadcast_to(m + jnp.log2(l), (HEAD_DIM, TQ)) for m, l in zip(ms, ls)]
    for t, ref in ((outs, o_ref), (lses, lse_ref)):
        for half in range(TQ // ATT_T):
            cols = slice(half * ATT_T, (half + 1) * ATT_T)
            ref[0, cols, :] = _heads_to_rows(jnp.concatenate([x[:, cols] for x in t], axis=1))


def dil_attn_pallas(q_r, k_r, v, dil):
    B, S, W = q_r.shape
    Sd = S // dil
    TQ, T = DIL_TQ, ATT_T
    R = TQ // T
    view = lambda t: t.reshape(B, Sd, dil * W)
    q_spec = pl.BlockSpec((1, TQ, W), lambda b, i, r: (b, i, r))
    prev = pl.BlockSpec((1, T, W), lambda b, i, r: (b, jnp.maximum(i * R - 1, 0), r))
    cur = [pl.BlockSpec((1, T, W), functools.partial(lambda b, i, r, j: (b, i * R + j, r), j=j)) for j in range(R)]
    o, lse = pl.pallas_call(
        _dil_attn_kernel,
        grid=(B, Sd // TQ, dil),
        in_specs=[q_spec, prev, *cur, prev, *cur],
        out_specs=[q_spec, q_spec],
        out_shape=[jax.ShapeDtypeStruct((B, Sd, dil * W), _F32)] * 2,
        compiler_params=pltpu.CompilerParams(dimension_semantics=("parallel", "parallel", "parallel")),
        name="dil_attn",
    )(view(q_r), *([view(k_r)] * (R + 1)), *([view(v)] * (R + 1)))
    return o.reshape(B, S, W), lse.reshape(B, S, W)


LOG2_E = 1.4426950408889634
ROPE_LANES = 128
PREP_TM = 256


def _rope_table_kernel(pos_ref, inv_ref, cos_ref, slo_ref, shi_ref):
    ang = pos_ref[...].astype(_F32) * inv_ref[...]
    cos, sin = jnp.cos(ang), jnp.sin(ang)
    first_half = (lax.broadcasted_iota(jnp.int32, ang.shape, 1) % HEAD_DIM) < HEAD_DIM // 2
    cos_ref[...] = cos
    slo_ref[...] = jnp.where(first_half, -sin, 0.0)
    shi_ref[...] = jnp.where(first_half, 0.0, sin)


def rope_tables(positions):
    B, S = positions.shape
    T = B * S
    half = HEAD_DIM // 2
    inv = ROPE_THETA ** (-jnp.arange(half, dtype=_F32) / half)
    inv = jnp.tile(inv, ROPE_LANES // half).reshape(1, ROPE_LANES)
    pos = jnp.broadcast_to(positions.reshape(T, 1), (T, ROPE_LANES))
    tm = 1024
    spec = pl.BlockSpec((tm, ROPE_LANES), lambda i: (i, 0))
    return pl.pallas_call(
        _rope_table_kernel,
        grid=(T // tm,),
        in_specs=[spec, pl.BlockSpec((1, ROPE_LANES), lambda i: (0, 0))],
        out_specs=[spec] * 3,
        out_shape=[jax.ShapeDtypeStruct((T, ROPE_LANES), _F32)] * 3,
        name="rope_tables",
    )(pos, inv)


def _attn_prep_kernel(cq_ref, ck_ref, cv_ref, dq_ref, d1_ref, d2_ref, cos_ref, slo_ref, shi_ref, gains_ref,
                      cqo_ref, cko_ref, cvo_ref, qn_ref, qr_ref, kc_ref, vc_ref, ks_ref, vst_ref, kw_ref, vwt_ref):
    G, hd = GROUP_W, HEAD_DIM
    wide = lambda t: jnp.concatenate([t] * (G // ROPE_LANES), axis=1)
    cos, slo, shi = wide(cos_ref[...]), wide(slo_ref[...]), wide(shi_ref[...])
    head_sum = _group_indicator(G, hd)
    gains = gains_ref[...]
    scale = hd ** -0.5

    def norm(t, n):
        return t * lax.rsqrt(_dot_split(t * t, head_sum) * (1.0 / hd) + NORM_EPS) * gains[n:n + 1]

    def rope(t):
        return t * cos + pltpu.roll(t, G - hd // 2, axis=1) * slo + pltpu.roll(t, hd // 2, axis=1) * shi

    cqo_ref[0] = (rope(norm(cq_ref[0], 0)) * (scale * LOG2_E)).astype(_BF16)
    cko_ref[0] = rope(norm(ck_ref[0], 1)).astype(_BF16)
    cvo_ref[0] = cv_ref[0].astype(_BF16)
    qn = norm(dq_ref[0], 2)
    qn_ref[0] = (qn * scale).astype(_BF16)
    qr_ref[0] = (rope(qn) * (scale * LOG2_E)).astype(_BF16)
    d1 = d1_ref[0]
    d2 = d2_ref[0]
    kc_ref[0] = d1[:, 0:hd]
    vc_ref[0] = d1[:, hd:2 * hd]
    ks_ref[0] = rope(norm(d1, 3))[:, 2 * hd:3 * hd].astype(_BF16)
    kw_ref[0] = rope(norm(d2, 4))[:, 0:hd].astype(_BF16)
    vst_ref[0] = d1.T[3 * hd:4 * hd, :].astype(_BF16)
    vwt_ref[0] = d2.T[hd:2 * hd, :].astype(_BF16)


def attn_prep_pallas(proj, tabs, dil_q_g, dil_k_g, nsa_q_g, nsa_ks_g, nsa_kw_g):
    B, S, _ = proj.shape
    tm = PREP_TM
    G, hd, H = GROUP_W, HEAD_DIM, N_HEADS
    c0 = (SECTION_WIDTHS[0] + SECTION_WIDTHS[1]) // G
    z = jnp.zeros((hd,), _F32)
    gains = jnp.stack([jnp.tile(dil_q_g, H), jnp.tile(dil_k_g, H), jnp.tile(nsa_q_g, H),
                       jnp.concatenate([z, z, nsa_ks_g, z]), jnp.concatenate([nsa_kw_g, z, z, z])])
    col = lambda c: pl.BlockSpec((1, tm, G), lambda b, i: (b, i, c))
    nt = S // tm
    tab = pl.BlockSpec((tm, ROPE_LANES), lambda b, i: (b * nt + i, 0))
    tok = lambda w, dt: (pl.BlockSpec((1, tm, w), lambda b, i: (b, i, 0)), jax.ShapeDtypeStruct((B, S, w), dt))
    tr = (pl.BlockSpec((1, hd, tm), lambda b, i: (b, 0, i)), jax.ShapeDtypeStruct((B, hd, S), _BF16))
    outs = [tok(G, _BF16), tok(G, _BF16), tok(G, _BF16), tok(G, _BF16), tok(G, _BF16), tok(hd, _F32), tok(hd, _F32), tok(hd, _BF16), tr, tok(hd, _BF16), tr]
    return pl.pallas_call(
        _attn_prep_kernel,
        grid=(B, nt),
        in_specs=[col(c0), col(c0 + 1), col(c0 + 2), col(c0 + 3), col(c0 + 4), col(c0 + 5), tab, tab, tab,
                  pl.BlockSpec(gains.shape, lambda b, i: (0, 0))],
        out_specs=[o[0] for o in outs],
        out_shape=[o[1] for o in outs],
        compiler_params=pltpu.CompilerParams(dimension_semantics=("parallel", "parallel")),
        name="attn_prep",
    )(proj, proj, proj, proj, proj, proj, *tabs, gains)


NSA_NC = 256
CMP_CHUNK = CMP_STRIDE * HEAD_DIM


def _nsa_compress_kernel(kc_ref, vc_ref, pek_ref, pev_ref, wk1_ref, wk2_ref, wv1_ref, wv2_ref, g_ref, ko_ref, vo_ref):
    def compress(x_ref, pe_ref, w1_ref, w2_ref):
        x = x_ref[0]
        top = _dot_lo(x + pe_ref[0:1], w1_ref[0:CMP_CHUNK, :])
        bot = _dot_lo(x + pe_ref[1:2], w1_ref[CMP_CHUNK:, :])
        hid = top + pltpu.roll(bot, NSA_NC - 1, axis=0)
        return _dot_lo(jax.nn.gelu(hid), w2_ref[...])

    k = compress(kc_ref, pek_ref, wk1_ref, wk2_ref)
    ko_ref[0] = k * lax.rsqrt(jnp.mean(k * k, axis=-1, keepdims=True) + NORM_EPS) * g_ref[...]
    vo_ref[0] = compress(vc_ref, pev_ref, wv1_ref, wv2_ref)


def nsa_compress_pallas(kc, vc, pe_k, pe_v, wk1, wk2, wv1, wv2, kc_g):
    B, S, hd = kc.shape
    assert S // CMP_STRIDE == NSA_NC
    chunks = lambda t: t.reshape(B, NSA_NC, CMP_CHUNK)
    pe2 = lambda p: p.reshape(2, CMP_CHUNK)
    const = lambda t: pl.BlockSpec(t.shape, lambda b: (0,) * t.ndim)
    x_spec = pl.BlockSpec((1, NSA_NC, CMP_CHUNK), lambda b: (b, 0, 0))
    o_spec = pl.BlockSpec((1, NSA_NC, hd), lambda b: (b, 0, 0))
    params = [pe2(pe_k), pe2(pe_v), wk1.astype(_BF16), wk2.astype(_BF16), wv1.astype(_BF16), wv2.astype(_BF16),
              kc_g.reshape(1, hd)]
    return pl.pallas_call(
        _nsa_compress_kernel,
        grid=(B,),
        in_specs=[x_spec, x_spec] + [const(p) for p in params],
        out_specs=[o_spec, o_spec],
        out_shape=[jax.ShapeDtypeStruct((B, NSA_NC, hd), _F32)] * 2,
        name="nsa_compress",
    )(chunks(kc), chunks(vc), *params)


def selection_overlap(n_c, n_sel):
    r = SEL_BLK // CMP_STRIDE
    m = CMP_LEN // CMP_STRIDE
    diff = np.arange(n_c)[:, None] - r * np.arange(n_sel)[None, :]
    offs = (np.arange(r)[:, None] - np.arange(m)[None, :]).reshape(-1)
    return (diff[..., None] == offs).sum(-1).astype(np.float32)


def _nsa_select_kernel(q_ref, kc_ref, vc_ref, ovt_ref, o_ref, selt_ref):
    T = ATT_T
    qi = pl.program_id(1)
    kc = kc_ref[0].astype(_BF16)
    vc = vc_ref[0].astype(_BF16)
    cblk = lax.broadcasted_iota(jnp.int32, (NSA_NC, T), 0)
    tq = qi * T + lax.broadcasted_iota(jnp.int32, (NSA_NC, T), 1)
    valid = cblk * CMP_STRIDE + (CMP_LEN - 1) <= tq
    any_valid = (tq[0:1, :] >= CMP_LEN - 1).astype(_F32)
    heads = range(N_HEADS)
    ss = [lax.dot_general(kc, q_ref[0, :, h * HEAD_DIM:(h + 1) * HEAD_DIM], _NT, preferred_element_type=_F32)
          for h in heads]
    ss = [jnp.where(valid, s, NEG) for s in ss]
    es = [jnp.exp(s - jnp.max(s, axis=0, keepdims=True)) for s in ss]
    ps = [e * (any_valid / jnp.sum(e, axis=0, keepdims=True)) for e in es]
    outs = [lax.dot_general(vc, p.astype(_BF16), _TN, preferred_element_type=_F32) for p in ps]
    p_sum = sum(ps[1:], ps[0])
    o_ref[0] = _heads_to_rows(jnp.concatenate(outs, axis=1))

    ovt = ovt_ref[...]
    hi = p_sum.astype(_BF16)
    lo = (p_sum - hi.astype(_F32)).astype(_BF16)
    imp = jnp.dot(ovt, hi, preferred_element_type=_F32) + jnp.dot(ovt, lo, preferred_element_type=_F32)
    jb = lax.broadcasted_iota(jnp.int32, (NSA_NSEL, T), 0)
    cur = (qi * T + lax.broadcasted_iota(jnp.int32, (NSA_NSEL, T), 1)) // SEL_BLK
    forced = (jb == 0) | (jb == cur) | (jb == cur - 1)
    score = jnp.where(jb > cur, -1.0, jnp.where(forced, FORCE_SCORE, imp))
    jbf = jb.astype(_F32)
    sel = jnp.zeros((NSA_NSEL, T), _F32)
    for _ in range(min(SEL_TOP, NSA_NSEL)):
        best = jnp.max(score, axis=0, keepdims=True)
        first = jnp.min(jnp.where(score == best, jbf, float(NSA_NSEL)), axis=0, keepdims=True)
        pick = jbf == first
        sel = jnp.where(pick, 1.0, sel)
        score = jnp.where(pick, NEG, score)
    selt_ref[0] = sel.astype(_BF16)


def nsa_select_pallas(q_n, k_cmp, v_cmp):
    B, S, W = q_n.shape
    T = ATT_T
    n_c = (S - CMP_LEN) // CMP_STRIDE + 1
    ovt = np.zeros((NSA_NSEL, NSA_NC), np.float32)
    ovt[:, :n_c] = selection_overlap(n_c, NSA_NSEL).T
    ovt = jnp.asarray(ovt, _BF16)
    c_spec = pl.BlockSpec((1, NSA_NC, HEAD_DIM), lambda b, i: (b, 0, 0))
    return pl.pallas_call(
        _nsa_select_kernel,
        grid=(B, S // T),
        in_specs=[pl.BlockSpec((1, T, W), lambda b, i: (b, i, 0)), c_spec, c_spec,
                  pl.BlockSpec(ovt.shape, lambda b, i: (0, 0))],
        out_specs=[pl.BlockSpec((1, T, W), lambda b, i: (b, i, 0)),
                   pl.BlockSpec((1, NSA_NSEL, T), lambda b, i: (b, 0, i))],
        out_shape=[jax.ShapeDtypeStruct((B, S, W), _F32), jax.ShapeDtypeStruct((B, NSA_NSEL, S), _BF16)],
        compiler_params=pltpu.CompilerParams(dimension_semantics=("parallel", "parallel")),
        name="nsa_select",
    )(q_n, k_cmp, v_cmp, ovt)


def kernel(x, c, positions, w_ada, b_ada, norm1_g, norm2_g, w_in, w_out, rwkv_mu, rwkv_w0, rwkv_w2, rwkv_a0, rwkv_a2, rwkv_g2, rwkv_kk, rwkv_ka, rwkv_rk, rwkv_gn_w, rwkv_gn_b, conv_w, dil_q_g, dil_k_g, nsa_q_g, nsa_kc_g, nsa_ks_g, nsa_kw_g, nsa_pe_k, nsa_pe_v, nsa_wk1, nsa_wk2, nsa_wv1, nsa_wv2, onorm_g, w_router, b_router, w_gu, b_gu, w_down, b_down):
    B, S, D = x.shape
    tabs = rope_tables(positions)
    for l in range(DEPTH):
        mod = matmul(jax.nn.silu(c), w_ada[l], tm=B, tn=512) + b_ada[l]
        sh1, sc1, gt1, sh2, sc2, gt2 = [m[:, None, :] for m in jnp.split(mod, 6, axis=-1)]
        proj = in_proj_pallas(x, norm1_g[l], sc1, sh1, w_in[l])
        y_a = rwkv7_pallas(proj, rwkv_mu[l], rwkv_w0[l], rwkv_w2[l], rwkv_a0[l], rwkv_a2[l], rwkv_g2[l],
                           rwkv_kk[l], rwkv_ka[l], rwkv_rk[l], rwkv_gn_w[l], rwkv_gn_b[l])
        cq, ck, cv, q_n, q_r, kc, vc, ks, vs_t, kw, vw_t = attn_prep_pallas(
            proj, tabs, dil_q_g[l], dil_k_g[l], nsa_q_g[l], nsa_ks_g[l], nsa_kw_g[l])
        dil_outs = [dil_attn_pallas(cq, ck, cv, dil) for _, dil in DIL_CFG]
        k_cmp, v_cmp = nsa_compress_pallas(kc, vc, nsa_pe_k[l], nsa_pe_v[l], nsa_wk1[l], nsa_wk2[l], nsa_wv1[l],
                                           nsa_wv2[l], nsa_kc_g[l])
        o_cmp, sel_t = nsa_select_pallas(q_n, k_cmp, v_cmp)
        o_slc, o_swa = nsa_attn_pallas(q_r, ks, vs_t, kw, vw_t, sel_t)
        x, hb, logits = mix_out_pallas(x, y_a, proj, dil_outs, o_cmp, o_slc, o_swa, conv_w[l], onorm_g[l], w_out[l], gt1,
                                       norm2_g[l], sc2, sh2, w_router[l], b_router[l])
        x = moe_pallas(x, gt2, hb, logits, l, w_gu, b_gu, w_down, b_down)
    return x
```

```python
import functools

import numpy as np
import jax
import jax.numpy as jnp
from jax import lax
from jax.experimental import pallas as pl
from jax.experimental.pallas import tpu as pltpu

D_MODEL = 1024
DEPTH = 2

HEAD_DIM = 64
N_MIXERS = 4
GROUP_W = D_MODEL // N_MIXERS
N_HEADS = GROUP_W // HEAD_DIM
MIX_W = N_MIXERS * GROUP_W

RWKV_DECAY_RANK = 64
RWKV_AAA_RANK = 64
RWKV_GATE_RANK = 128
RWKV_DECAY_SCALE = 0.606531
RWKV_GN_EPS = 64e-5

CONV_W = 3

DIL_CFG = ((128, 1), (512, 4), (2048, 16))

CMP_LEN = 32
CMP_STRIDE = 16
CMP_HIDDEN = 256
SEL_BLK = 64
SEL_TOP = 16
SWA_WIN = 512
FORCE_SCORE = 1e4

N_EXPERTS = 32
TOP_K = 4
EXPERT_FF = D_MODEL
SWIGLU_LIMIT = 7.0
SWIGLU_ALPHA = 1.702

ROPE_THETA = 10000.0
NORM_EPS = 1e-6
NEG = -1e30

A_WIDTHS = (GROUP_W, GROUP_W, GROUP_W, RWKV_DECAY_RANK, RWKV_AAA_RANK, RWKV_GATE_RANK)
B_WIDTHS = (GROUP_W, GROUP_W, GROUP_W)
C_WIDTHS = (GROUP_W, GROUP_W, GROUP_W)
D_WIDTHS = (GROUP_W, HEAD_DIM, HEAD_DIM, HEAD_DIM, HEAD_DIM, HEAD_DIM, HEAD_DIM, 3 * N_HEADS)
SECTION_WIDTHS = (sum(A_WIDTHS), sum(B_WIDTHS), sum(C_WIDTHS), sum(D_WIDTHS))
PROJ_W = sum(SECTION_WIDTHS)

_F32 = jnp.float32
_BF16 = jnp.bfloat16
_NT = (((1,), (1,)), ((), ()))
_TN = (((0,), (0,)), ((), ()))


def _dot_hi(a, b, dims=None):
    if dims is None:
        return jnp.dot(a, b, precision=lax.Precision.HIGHEST, preferred_element_type=_F32)
    return lax.dot_general(a, b, dims, precision=lax.Precision.HIGHEST, preferred_element_type=_F32)


def _dot_lo(a, b, dims=None):
    a, b = a.astype(_BF16), b.astype(_BF16)
    if dims is None:
        return jnp.dot(a, b, preferred_element_type=_F32)
    return lax.dot_general(a, b, dims, preferred_element_type=_F32)


def _bf16_parts(a, parts):
    out = []
    for _ in range(parts):
        p = a.astype(_BF16)
        out.append(p)
        a = a - p.astype(_F32)
    return out


def _dot_split(a, m, parts=2):
    mb = m.astype(_BF16)
    return sum(jnp.dot(p, mb, preferred_element_type=_F32) for p in _bf16_parts(a, parts))


def _dot_split_left(m, a, parts=2):
    mb = m.astype(_BF16)
    return sum(jnp.dot(mb, p, preferred_element_type=_F32) for p in _bf16_parts(a, parts))


def _dot_x3(a, b):
    ah, al = _bf16_parts(a, 2)
    bh, bl = _bf16_parts(b, 2)
    d = lambda x, y: jnp.dot(x, y, preferred_element_type=_F32)
    return d(ah, bh) + (d(ah, bl) + d(al, bh))


def _matmul_kernel(x_ref, w_ref, o_ref, *, exact):
    if exact:
        o_ref[...] = _dot_hi(x_ref[...], w_ref[...])
    else:
        o_ref[...] = jnp.dot(x_ref[...].astype(_BF16), w_ref[...], preferred_element_type=_F32)


def matmul(x, w, tm=512, tn=256, exact=False):
    M, K = x.shape
    N = w.shape[1]
    n_pad = -(-N // tn) * tn
    wb = jnp.pad(w if exact else w.astype(_BF16), ((0, 0), (0, n_pad - N)))
    out = pl.pallas_call(
        functools.partial(_matmul_kernel, exact=exact),
        grid=(M // tm, n_pad // tn),
        in_specs=[pl.BlockSpec((tm, K), lambda i, j: (i, 0)),
                  pl.BlockSpec((K, tn), lambda i, j: (0, j))],
        out_specs=pl.BlockSpec((tm, tn), lambda i, j: (i, j)),
        out_shape=jax.ShapeDtypeStruct((M, n_pad), _F32),
        name="matmul",
    )(x, wb)
    return out[:, :N]


PROJ_PAD = -(-PROJ_W // 256) * 256
PROJ_TM = 256
PROJ_TN = 512
ROUTER_PAD = 128
GATE_COL_BLOCK = (PROJ_W - 3 * N_HEADS) // 128
assert GATE_COL_BLOCK * 128 == PROJ_W - 3 * N_HEADS


def _norm_mod(x, g, sc, sh):
    y = x * lax.rsqrt(jnp.mean(x * x, axis=-1, keepdims=True) + NORM_EPS) * g
    return y * (1.0 + sc) + sh


def _in_proj_kernel(x_ref, g_ref, sc_ref, sh_ref, w_ref, o_ref):
    h = _norm_mod(x_ref[0], g_ref[...], sc_ref[0], sh_ref[0]).astype(_BF16)
    for n0 in range(0, PROJ_PAD, PROJ_TN):
        o_ref[0, :, n0:n0 + PROJ_TN] = jnp.dot(h, w_ref[:, n0:n0 + PROJ_TN], preferred_element_type=_F32)


def in_proj_pallas(x, g, sc, sh, w_in):
    B, S, D = x.shape
    tm = PROJ_TM
    w = jnp.pad(w_in.astype(_BF16), ((0, 0), (0, PROJ_PAD - PROJ_W)))
    per_batch = pl.BlockSpec((1, 1, D), lambda b, i: (b, 0, 0))
    return pl.pallas_call(
        _in_proj_kernel,
        grid=(B, S // tm),
        in_specs=[pl.BlockSpec((1, tm, D), lambda b, i: (b, i, 0)),
                  pl.BlockSpec((1, D), lambda b, i: (0, 0)), per_batch, per_batch,
                  pl.BlockSpec((D, PROJ_PAD), lambda b, i: (0, 0))],
        out_specs=pl.BlockSpec((1, tm, PROJ_PAD), lambda b, i: (b, i, 0)),
        out_shape=jax.ShapeDtypeStruct((B, S, PROJ_PAD), _F32),
        compiler_params=pltpu.CompilerParams(dimension_semantics=("parallel", "parallel"),
                                             vmem_limit_bytes=48 * 1024 * 1024),
        name="in_proj",
    )(x, g.reshape(1, D), sc, sh, w)


def _mix_out_kernel(x_ref, ya_ref, bg_ref, cg_ref, xin_ref, cgh_ref, xinh_ref, do0_ref, dl0_ref, do1_ref, dl1_ref,
                    do2_ref, dl2_ref, ocmp_ref, oslc_ref, oswa_ref,
                    gl_ref, convw_ref, ong_ref, wout_ref, gt1_ref, g2_ref, sc2_ref, sh2_ref, wr_ref, br_ref,
                    xo_ref, hb_ref, lg_ref):
    i = pl.program_id(1)
    G = GROUP_W
    u = cg_ref[0] * xin_ref[0]
    halo = jnp.where(i == 0, 0.0, cgh_ref[0] * xinh_ref[0])
    row = lax.broadcasted_iota(jnp.int32, u.shape, 0)
    u1 = jnp.where(row == 0, halo[7:8], pltpu.roll(u, 1, axis=0))
    u2 = jnp.where(row == 0, halo[6:7], jnp.where(row == 1, halo[7:8], pltpu.roll(u, 2, axis=0)))
    cw = convw_ref[...]
    yb = bg_ref[0] * (cw[0:1] * u2 + cw[1:2] * u1 + cw[2:3] * u)

    sg = jax.nn.sigmoid(gl_ref[0])
    er = lax.broadcasted_iota(jnp.int32, (128, G), 0)
    ec = lax.broadcasted_iota(jnp.int32, (128, G), 1) // HEAD_DIM
    yd = None
    for j, o_ref in enumerate((ocmp_ref, oslc_ref, oswa_ref)):
        gate = _dot_split(sg, er == 3 * ec + j)
        yd = gate * o_ref[0] if yd is None else yd + gate * o_ref[0]

    head_sum = _group_indicator(G, HEAD_DIM)
    ong = ong_ref[...]
    parts = [ya_ref[0]]
    dil = ((do0_ref, dl0_ref), (do1_ref, dl1_ref), (do2_ref, dl2_ref))
    top = functools.reduce(jnp.maximum, [l_ref[0] for _, l_ref in dil])
    wts = [jnp.exp2(l_ref[0] - top) for _, l_ref in dil]
    yc = sum(w * o_ref[0] for w, (o_ref, _) in zip(wts, dil)) / sum(wts)
    for n, y in enumerate((yb, yc, yd)):
        ms = _dot_split(y * y, head_sum) * (1.0 / HEAD_DIM)
        parts.append(y * lax.rsqrt(ms + NORM_EPS) * ong[:, n * G:(n + 1) * G])
    mixed = jnp.dot(jnp.concatenate(parts, axis=1).astype(_BF16), wout_ref[...], preferred_element_type=_F32)
    x = x_ref[0] + gt1_ref[0] * mixed
    xo_ref[0] = x
    h = _norm_mod(x, g2_ref[...], sc2_ref[0], sh2_ref[0])
    hb_ref[0] = h.astype(_BF16)
    lg_ref[0] = _dot_x3(h, wr_ref[...]) + br_ref[...]


def mix_out_pallas(x, y_a, proj, dil_outs, o_cmp, o_slc, o_swa, conv_w, onorm_g, w_out, gt1, g2, sc2, sh2, w_router, b_router):
    B, S, D = x.shape
    tm = PROJ_TM
    G = GROUP_W
    b0 = SECTION_WIDTHS[0] // G
    tile = lambda w, col: pl.BlockSpec((1, tm, w), lambda b, i: (b, i, col))
    halo = lambda col: pl.BlockSpec((1, 8, G), lambda b, i: (b, jnp.maximum(i * (tm // 8) - 1, 0), col))
    const = lambda t: pl.BlockSpec(t.shape, lambda b, i: (0,) * t.ndim)
    per_batch = pl.BlockSpec((1, 1, D), lambda b, i: (b, 0, 0))
    wr = jnp.pad(w_router, ((0, 0), (0, ROUTER_PAD - N_EXPERTS)))
    br = jnp.pad(b_router, (0, ROUTER_PAD - N_EXPERTS)).reshape(1, ROUTER_PAD)
    consts = [conv_w, onorm_g.reshape(1, 3 * G), w_out.astype(_BF16)]
    return pl.pallas_call(
        _mix_out_kernel,
        grid=(B, S // tm),
        in_specs=[tile(D, 0), tile(G, 0), tile(G, b0), tile(G, b0 + 1), tile(G, b0 + 2), halo(b0 + 1), halo(b0 + 2),
                  *([tile(G, 0)] * (2 * len(dil_outs) + 3)), tile(128, GATE_COL_BLOCK)]
                 + [const(t) for t in consts] + [per_batch, pl.BlockSpec((1, D), lambda b, i: (0, 0)), per_batch,
                                                 per_batch, const(wr), const(br)],
        out_specs=[tile(D, 0), tile(D, 0), tile(ROUTER_PAD, 0)],
        out_shape=[jax.ShapeDtypeStruct((B, S, D), _F32), jax.ShapeDtypeStruct((B, S, D), _BF16),
                   jax.ShapeDtypeStruct((B, S, ROUTER_PAD), _F32)],
        compiler_params=pltpu.CompilerParams(dimension_semantics=("parallel", "parallel")),
        name="mix_out",
    )(x, y_a, proj, proj, proj, proj, proj, *[t for pair in dil_outs for t in pair], o_cmp, o_slc, o_swa, proj, *consts, gt1, g2.reshape(1, D), sc2, sh2,
      wr, br)


RWKV_CHUNK = 64
RWKV_TILE = 512


def _group_indicator(n, group):
    r = lax.broadcasted_iota(jnp.int32, (n, n), 0) // group
    c = lax.broadcasted_iota(jnp.int32, (n, n), 1) // group
    return r == c


def _rwkv_kernel(za_ref, mu_ref, w0_ref, w2_ref, a0_ref, a2_ref, g2_ref, kk_ref, ka_ref, rk_ref,
                 gnw_ref, gnb_ref, o_ref, prev_sc, h_sc):
    C = RWKV_CHUNK
    TT = RWKV_TILE
    c = pl.program_id(1)

    @pl.when(c == 0)
    def _():
        prev_sc[...] = jnp.zeros_like(prev_sc)
        h_sc[...] = jnp.zeros_like(h_sc)

    z = za_ref[0]
    row = lax.broadcasted_iota(jnp.int32, z.shape, 0)
    zs = jnp.where(row == 0, prev_sc[...], pltpu.roll(z, 1, axis=0))
    prev_sc[...] = z[TT - 1:TT, :]
    z = z + (zs - z) * mu_ref[...]

    G = GROUP_W
    r, k, v = z[:, 0:G], z[:, G:2 * G], z[:, 2 * G:3 * G]
    o = 3 * G
    wd = z[:, o:o + RWKV_DECAY_RANK]
    ad = z[:, o + RWKV_DECAY_RANK:o + RWKV_DECAY_RANK + RWKV_AAA_RANK]
    gd = z[:, o + RWKV_DECAY_RANK + RWKV_AAA_RANK:]

    lw = -RWKV_DECAY_SCALE * jax.nn.sigmoid(w0_ref[...] + _dot_lo(jnp.tanh(wd), w2_ref[...]))
    a = jax.nn.sigmoid(a0_ref[...] + _dot_lo(ad, a2_ref[...]))
    g = _dot_lo(jax.nn.sigmoid(gd), g2_ref[...])

    head_sum = _group_indicator(G, HEAD_DIM)
    kk = k * kk_ref[...]
    kk = kk * lax.rsqrt(_dot_split(kk * kk, head_sum) + 1e-12)
    k = k * (1.0 + (a - 1.0) * ka_ref[...])
    b = kk * a

    ti = lax.broadcasted_iota(jnp.int32, (C, C), 0)
    tj = lax.broadcasted_iota(jnp.int32, (C, C), 1)
    incl = ti >= tj
    strict = ti > tj
    eye = ti == tj
    blk16 = (ti // 16) == (tj // 16)
    blk32 = (ti // 32) == (tj // 32)
    eye_f = eye.astype(_F32)

    ri = lax.broadcasted_iota(jnp.int32, (TT, TT), 0)
    rj = lax.broadcasted_iota(jnp.int32, (TT, TT), 1)
    chunk_tri = (ri >= rj) & (ri // C == rj // C)
    cum = _dot_split_left(chunk_tri, lw, parts=3)
    g_in = jnp.exp(cum)
    A_all = -kk * jnp.exp(cum - lw)
    R_all = r * g_in
    g_inv = jnp.exp(-cum)
    B_all = b * g_inv
    K_all = k * g_inv

    units = [(ci, h) for ci in range(TT // C) for h in range(N_HEADS)]

    def part(t, u):
        ci, h = u
        return t[ci * C:(ci + 1) * C, h * HEAD_DIM:(h + 1) * HEAD_DIM]

    def g_end(u):
        ci, h = u
        return g_in[(ci + 1) * C - 1:(ci + 1) * C, h * HEAD_DIM:(h + 1) * HEAD_DIM]

    A = [part(A_all, u) for u in units]
    R = [part(R_all, u) for u in units]
    B = [part(B_all, u) for u in units]
    Kt = [part(K_all, u) for u in units]
    V = [part(v, u) for u in units]
    n = range(len(units))
    gram = [_dot_lo(jnp.concatenate([A[i], R[i]], axis=0), jnp.concatenate([B[i], Kt[i]], axis=0), _NT) for i in n]
    l_ab = [jnp.where(strict, gram[i][0:C, 0:C], 0.0) for i in n]
    l_ak = [jnp.where(strict, gram[i][0:C, C:2 * C], 0.0) for i in n]
    m_rb = [jnp.where(incl, gram[i][C:2 * C, 0:C], 0.0) for i in n]
    m_rk = [jnp.where(incl, gram[i][C:2 * C, C:2 * C], 0.0) for i in n]
    p = [jnp.where(blk16, l_ab[i], 0.0) for i in n]
    x = [eye_f + p[i] for i in n]
    for _ in range(3):
        p = [_dot_lo(p[i], p[i]) for i in n]
        x = [_dot_lo(x[i], eye_f + p[i]) for i in n]
    for lvl in (blk32 & ~blk16, ~blk32):
        xl = [_dot_lo(x[i], jnp.where(lvl, l_ab[i], 0.0)) for i in n]
        x = [x[i] + _dot_lo(xl[i], x[i]) for i in n]
    lv = [_dot_lo(l_ak[i], V[i]) for i in n]
    tap = [_dot_lo(x[i], jnp.concatenate([A[i], lv[i]], axis=1)) for i in n]
    m1 = [_dot_lo(m_rb[i], tap[i]) for i in n]
    mv = [_dot_lo(m_rk[i], V[i]) for i in n]
    bt = [_dot_lo(B[i] * g_end(units[i]), tap[i], _TN) for i in n]
    kv = [_dot_lo(Kt[i] * g_end(units[i]), V[i], _TN) for i in n]
    w_yh = [jnp.concatenate([R[i] + m1[i][:, 0:HEAD_DIM],
                             jnp.where(eye, g_end(units[i]), 0.0) + bt[i][:, 0:HEAD_DIM]], axis=0) for i in n]
    y0 = [m1[i][:, HEAD_DIM:] + mv[i] for i in n]
    h_add = [bt[i][:, HEAD_DIM:] + kv[i] for i in n]
    state = [h_sc[h] for h in range(N_HEADS)]
    y_rows = []
    for ci in range(TT // C):
        ys = []
        for h in range(N_HEADS):
            i = ci * N_HEADS + h
            nxt = _dot_x3(w_yh[i], state[h])
            ys.append(nxt[0:C] + y0[i])
            state[h] = nxt[C:2 * C] + h_add[i]
        y_rows.append(jnp.concatenate(ys, axis=1))
    for h in range(N_HEADS):
        h_sc[h] = state[h]
    y = jnp.concatenate(y_rows, axis=0)

    mu = _dot_split(y, head_sum) * (1.0 / HEAD_DIM)
    d = y - mu
    var = _dot_split(d * d, head_sum) * (1.0 / HEAD_DIM)
    y = d * lax.rsqrt(var + RWKV_GN_EPS) * gnw_ref[...] + gnb_ref[...]
    y = y + _dot_split(r * k * rk_ref[...], head_sum) * v
    o_ref[0] = y * g


def rwkv7_pallas(za, mu, w0, w2, a0, a2, g2, k_k, k_a, r_k, gn_w, gn_b):
    B, S, _ = za.shape
    W = SECTION_WIDTHS[0]
    TT = RWKV_TILE
    row = lambda t: t.reshape(1, -1).astype(_F32)
    full = lambda t: pl.BlockSpec(t.shape, lambda b, c: (0,) * t.ndim)
    params = [row(mu), row(w0), w2.astype(_BF16), row(a0), a2.astype(_BF16), g2.astype(_BF16),
              row(k_k), row(k_a), row(r_k), row(gn_w), row(gn_b)]
    return pl.pallas_call(
        _rwkv_kernel,
        grid=(B, S // TT),
        in_specs=[pl.BlockSpec((1, TT, W), lambda b, c: (b, c, 0))] + [full(p) for p in params],
        out_specs=pl.BlockSpec((1, TT, GROUP_W), lambda b, c: (b, c, 0)),
        out_shape=jax.ShapeDtypeStruct((B, S, GROUP_W), _F32),
        scratch_shapes=[pltpu.VMEM((1, W), _F32), pltpu.VMEM((N_HEADS, HEAD_DIM, HEAD_DIM), _F32)],
        compiler_params=pltpu.CompilerParams(dimension_semantics=("parallel", "arbitrary")),
        name="rwkv7",
    )(za, *params)


MOE_TM = 512
MOE_FC = 512
MOE_CAST_ROWS = 128
MOE_KGROUPS = 2


def _moe_ffn_kernel(blk_e_ref, n_used_ref, x_ref, wgu_ref, bgu_ref, wd_ref, bd_ref, o_ref, wgu_sc, wd_sc):
    i = pl.program_id(0)
    F = EXPERT_FF
    new_expert = jnp.logical_or(i == 0, blk_e_ref[i] != blk_e_ref[jnp.maximum(i - 1, 0)])

    @pl.when(new_expert)
    def _():
        def cast(j, carry):
            rows = pl.ds(pl.multiple_of(j * MOE_CAST_ROWS, MOE_CAST_ROWS), MOE_CAST_ROWS)
            wgu_sc[rows, :] = wgu_ref[0, rows, :].astype(_BF16)
            wd_sc[rows, :] = wd_ref[0, rows, :].astype(_BF16)
            return carry
        lax.fori_loop(0, F // MOE_CAST_ROWS, cast, 0)

    @pl.when(i < n_used_ref[0])
    def _():
        x = x_ref[...]
        acc = None
        for c in range(F // MOE_FC):
            lo = c * MOE_FC
            gate = jnp.dot(x, wgu_sc[:, lo:lo + MOE_FC], preferred_element_type=_F32) + bgu_ref[0, :, lo:lo + MOE_FC]
            up = jnp.dot(x, wgu_sc[:, F + lo:F + lo + MOE_FC], preferred_element_type=_F32) + bgu_ref[0, :, F + lo:F + lo + MOE_FC]
            gate = jnp.minimum(gate, SWIGLU_LIMIT)
            up = jnp.clip(up, -SWIGLU_LIMIT, SWIGLU_LIMIT)
            act = gate * jax.nn.sigmoid(SWIGLU_ALPHA * gate) * (up + 1.0)
            part = jnp.dot(act.astype(_BF16), wd_sc[lo:lo + MOE_FC, :], preferred_element_type=_F32)
            acc = part if acc is None else acc + part
        o_ref[...] = (acc + bd_ref[0]).astype(o_ref.dtype)

    @pl.when(i >= n_used_ref[0])
    def _():
        o_ref[...] = jnp.zeros_like(o_ref)


def moe_ffn(xs, blk_e, n_used, layer, w_gu, b_gu, w_down, b_down):
    n_rows, D = xs.shape
    L, E, _, F2 = w_gu.shape
    assert D == EXPERT_FF and F2 == 2 * EXPERT_FF
    n_blk = n_rows // MOE_TM
    grid_spec = pltpu.PrefetchScalarGridSpec(
        num_scalar_prefetch=2,
        grid=(n_blk,),
        in_specs=[pl.BlockSpec((MOE_TM, D), lambda i, e, n: (i, 0)),
                  pl.BlockSpec((None, 1, D, F2), lambda i, e, n: (layer, e[i], 0, 0)),
                  pl.BlockSpec((None, 1, 1, F2), lambda i, e, n: (layer, e[i], 0, 0)),
                  pl.BlockSpec((None, 1, F2 // 2, D), lambda i, e, n: (layer, e[i], 0, 0)),
                  pl.BlockSpec((None, 1, 1, D), lambda i, e, n: (layer, e[i], 0, 0))],
        out_specs=pl.BlockSpec((MOE_TM, D), lambda i, e, n: (i, 0)),
        scratch_shapes=[pltpu.VMEM((D, F2), _BF16), pltpu.VMEM((F2 // 2, D), _BF16)],
    )
    return pl.pallas_call(
        _moe_ffn_kernel,
        grid_spec=grid_spec,
        out_shape=jax.ShapeDtypeStruct((n_rows, D), _BF16),
        compiler_params=pltpu.CompilerParams(dimension_semantics=("arbitrary",),
                                             vmem_limit_bytes=52 * 1024 * 1024),
        name="moe_ffn",
    )(blk_e, n_used, xs, w_gu, b_gu.reshape(L, E, 1, F2), w_down, b_down.reshape(L, E, 1, D))


def moe_pallas(x, gt2, hb, logits, layer, w_gu, b_gu, w_down, b_down):
    D = x.shape[-1]
    hb = hb.reshape(-1, D)
    logits = logits.reshape(hb.shape[0], -1)
    top_val, top_idx = lax.top_k(logits[:, :N_EXPERTS], TOP_K)
    gates = jax.nn.softmax(top_val, axis=-1)
    kg = TOP_K // MOE_KGROUPS
    ys = [_moe_dispatch(hb, top_idx[:, g * kg:(g + 1) * kg], layer, w_gu, b_gu, w_down, b_down)
          for g in range(MOE_KGROUPS)]
    return moe_combine(x, ys, gates, gt2)


def _moe_dispatch(hb, top_idx, layer, w_gu, b_gu, w_down, b_down):
    T, D = hb.shape
    TK = T * top_idx.shape[1]
    TM = MOE_TM
    E = N_EXPERTS
    i32 = jnp.int32
    e_flat = top_idx.T.reshape(TK).astype(i32)
    counts = jnp.sum((jnp.arange(E, dtype=i32)[:, None] == e_flat[None, :]).astype(i32), axis=1)
    need = jnp.repeat((-counts) % TM, TM)
    d_idx = jnp.arange(E * TM, dtype=i32)
    d_key = jnp.where(d_idx % TM < need, d_idx // TM, E)
    n_rows = TK + E * TM
    keys = jnp.concatenate([e_flat, d_key])
    toks = jnp.concatenate([jnp.arange(TK, dtype=i32) % T, jnp.zeros((E * TM,), i32)])
    rows = jnp.arange(n_rows, dtype=i32)
    s_keys, src_tok, s_slot = lax.sort((keys, toks, rows), num_keys=1)
    _, row_of_slot = lax.sort((s_slot, rows), num_keys=1)
    blk_e = s_keys[::TM]
    n_used = jnp.sum((blk_e < E).astype(i32)).reshape(1)
    blk_e = jnp.where(blk_e < E, blk_e, blk_e[jnp.maximum(n_used[0] - 1, 0)])
    xs = jnp.take(hb, src_tok, axis=0, mode="clip")
    ybuf = moe_ffn(xs, blk_e, n_used, layer, w_gu, b_gu, w_down, b_down)
    return jnp.take(ybuf, row_of_slot, axis=0, mode="clip")


def _moe_combine_kernel(*refs):
    *y_refs, g_ref, x_ref, gt2_ref, o_ref = refs
    g = g_ref[...]
    acc = None
    for k, y_ref in enumerate(y_refs):
        term = y_ref[...].astype(_F32) * g[:, k:k + 1]
        acc = term if acc is None else acc + term
    o_ref[0] = x_ref[0] + gt2_ref[0] * acc


def moe_combine(x, ys, gates, gt2):
    B, S, D = x.shape
    tm = PROJ_TM
    nt = S // tm
    kg = TOP_K // len(ys)
    g = jnp.pad(gates, ((0, 0), (0, 128 - TOP_K)))
    y_spec = lambda kk: pl.BlockSpec((tm, D), lambda b, i: ((kk * B + b) * nt + i, 0))
    return pl.pallas_call(
        _moe_combine_kernel,
        grid=(B, nt),
        in_specs=[y_spec(k % kg) for k in range(TOP_K)]
                 + [pl.BlockSpec((tm, 128), lambda b, i: (b * nt + i, 0)),
                    pl.BlockSpec((1, tm, D), lambda b, i: (b, i, 0)),
                    pl.BlockSpec((1, 1, D), lambda b, i: (b, 0, 0))],
        out_specs=pl.BlockSpec((1, tm, D), lambda b, i: (b, i, 0)),
        out_shape=jax.ShapeDtypeStruct((B, S, D), _F32),
        compiler_params=pltpu.CompilerParams(dimension_semantics=("parallel", "parallel")),
        name="moe_combine",
    )(*[ys[k // kg] for k in range(TOP_K)], g, x, gt2)


ATT_T = 128
ATT_TK = 512
ATT_R = ATT_TK // ATT_T
NSA_NSEL = 64
NSA_TQ = 256


def _window_table(window, weight_of_distance, tq):
    T, TK = ATT_T, ATT_TK
    n = (window + TK) // T + 1
    d = (np.arange(n)[:, None, None] * T + np.arange(tq)[None, None, :] - np.arange(TK)[None, :, None])
    return jnp.asarray(weight_of_distance(d).astype(np.float32))


def _first_key_tile(q_start, window):
    return jnp.maximum(q_start - window, 0) // ATT_TK


def _swa_weight(d):
    return (d >= 0) & (d <= SWA_WIN - 1)


def _softmax_tile(s, w, m_ref, l_ref):
    s = jnp.where(w > 0.0, s, NEG)
    m_prev = m_ref[...]
    m_new = jnp.maximum(m_prev, jnp.max(s, axis=0, keepdims=True))
    alpha = jnp.exp2(m_prev - m_new)
    p = jnp.exp2(s - m_new)
    l_ref[...] = alpha * l_ref[...] + jnp.sum(p, axis=0, keepdims=True)
    m_ref[...] = m_new
    return alpha, p.astype(_BF16)


def _init_stats(m_sc, l_sc, acc_sc):
    m_sc[...] = jnp.full_like(m_sc, NEG)
    l_sc[...] = jnp.zeros_like(l_sc)
    acc_sc[...] = jnp.zeros_like(acc_sc)


def _heads_to_rows(ot, T=ATT_T):
    pairs = []
    for h in range(0, N_HEADS, 2):
        two = jnp.concatenate([ot[:, h * T:(h + 1) * T], ot[:, (h + 1) * T:(h + 2) * T]], axis=0)
        pairs.append(two.T)
    return jnp.concatenate(pairs, axis=1)


def _nsa_attn_kernel(q_ref, ks_ref, vst_ref, kw_ref, vwt_ref, selt_ref, swa_ref, oslc_ref, oswa_ref,
                     m_sc, l_sc, acc_sc):
    T, TK = NSA_TQ, ATT_TK
    qi = pl.program_id(1)
    q = q_ref[0]
    q = jnp.concatenate([q[:, h * HEAD_DIM:(h + 1) * HEAD_DIM] for h in range(N_HEADS)], axis=0)
    selt = selt_ref[0]
    key_blk = lax.broadcasted_iota(jnp.int32, (TK, NSA_NSEL), 0) // SEL_BLK
    blk_id = lax.broadcasted_iota(jnp.int32, (TK, NSA_NSEL), 1)
    kr = lax.broadcasted_iota(jnp.int32, (TK, T), 0)
    qc = lax.broadcasted_iota(jnp.int32, (TK, T), 1)

    def step(k_ref, vt_ref, kt, w):
        base = pl.multiple_of(kt * TK, TK)
        s = lax.dot_general(k_ref[0, pl.ds(base, TK), :], q, _NT, preferred_element_type=_F32)
        alpha, p = _softmax_tile(s, jnp.concatenate([w] * N_HEADS, axis=1), m_sc, l_sc)
        acc_sc[...] = alpha * acc_sc[...] + jnp.dot(vt_ref[0, :, pl.ds(base, TK)], p, preferred_element_type=_F32)

    def finish(o_ref):
        o_ref[0] = _heads_to_rows(acc_sc[...] / l_sc[...], T)

    last = (qi * T + T - 1) // TK
    _init_stats(m_sc, l_sc, acc_sc)

    def slc_step(kt, carry):
        in_blk = (key_blk + kt * (TK // SEL_BLK) == blk_id).astype(_BF16)
        w = jnp.dot(in_blk, selt, preferred_element_type=_F32)
        w = jnp.where(kt * TK + kr <= qi * T + qc, w, 0.0)
        step(ks_ref, vst_ref, kt, w)
        return carry

    lax.fori_loop(0, last + 1, slc_step, 0)
    finish(oslc_ref)

    _init_stats(m_sc, l_sc, acc_sc)

    def swa_step(kt, carry):
        step(kw_ref, vwt_ref, kt, swa_ref[qi * (T // ATT_T) - kt * ATT_R])
        return carry

    lax.fori_loop(_first_key_tile(qi * T, SWA_WIN - 1), last + 1, swa_step, 0)
    finish(oswa_ref)


def nsa_attn_pallas(q_r, ks_r, vs_t, kw_r, vw_t, sel_t):
    B, S, W = q_r.shape
    assert S // SEL_BLK == NSA_NSEL
    T = NSA_TQ
    R = N_HEADS * T
    swa_tab = _window_table(SWA_WIN - 1, _swa_weight, T)
    k_spec = pl.BlockSpec((1, S, HEAD_DIM), lambda b, i: (b, 0, 0))
    vt_spec = pl.BlockSpec((1, HEAD_DIM, S), lambda b, i: (b, 0, 0))
    q_spec = pl.BlockSpec((1, T, W), lambda b, i: (b, i, 0))
    return pl.pallas_call(
        _nsa_attn_kernel,
        grid=(B, S // T),
        in_specs=[q_spec, k_spec, vt_spec, k_spec, vt_spec,
                  pl.BlockSpec((1, NSA_NSEL, T), lambda b, i: (b, 0, i)),
                  pl.BlockSpec(swa_tab.shape, lambda b, i: (0, 0, 0))],
        out_specs=[q_spec, q_spec],
        out_shape=[jax.ShapeDtypeStruct((B, S, W), _F32)] * 2,
        scratch_shapes=[pltpu.VMEM((1, R), _F32), pltpu.VMEM((1, R), _F32), pltpu.VMEM((HEAD_DIM, R), _F32)],
        compiler_params=pltpu.CompilerParams(dimension_semantics=("parallel", "arbitrary")),
        name="nsa_attn",
    )(q_r, ks_r, vs_t, kw_r, vw_t, sel_t, swa_tab)


DIL_TQ = 256
DIL_WIN = DIL_CFG[0][0] // DIL_CFG[0][1]
assert all(w // d == DIL_WIN for w, d in DIL_CFG) and DIL_WIN == ATT_T and DIL_TQ == 2 * ATT_T


def _dil_attn_kernel(q_ref, kp_ref, k0_ref, k1_ref, vp_ref, v0_ref, v1_ref, o_ref, lse_ref):
    TQ, W = DIL_TQ, DIL_WIN
    i = pl.program_id(1)
    k = jnp.concatenate([kp_ref[0], k0_ref[0], k1_ref[0]], axis=0)
    v = jnp.concatenate([vp_ref[0], v0_ref[0], v1_ref[0]], axis=0)
    row = lax.broadcasted_iota(jnp.int32, (W + TQ, TQ), 0)
    col = lax.broadcasted_iota(jnp.int32, (W + TQ, TQ), 1)
    dist = W + col - row
    valid = (dist >= 0) & (dist <= W) & ((i > 0) | (row >= W))
    heads = [slice(h * HEAD_DIM, (h + 1) * HEAD_DIM) for h in range(N_HEADS)]
    ss = [lax.dot_general(k[:, hs], q_ref[0, :, hs], _NT, preferred_element_type=_F32) for hs in heads]
    ss = [jnp.where(valid, s, NEG) for s in ss]
    ms = [jnp.max(s, axis=0, keepdims=True) for s in ss]
    ps = [jnp.exp2(s - m) for s, m in zip(ss, ms)]
    ls = [jnp.sum(p, axis=0, keepdims=True) for p in ps]
    os_ = [lax.dot_general(v[:, hs], p.astype(_BF16), _TN, preferred_element_type=_F32) for hs, p in zip(heads, ps)]
    outs = [o / l for o, l in zip(os_, ls)]
    lses = [jnp.broadcast_to(m + jnp.log2(l), (HEAD_DIM, TQ)) for m, l in zip(ms, ls)]
    for t, ref in ((outs, o_ref), (lses, lse_ref)):
        for half in range(TQ // ATT_T):
            cols = slice(half * ATT_T, (half + 1) * ATT_T)
            ref[0, cols, :] = _heads_to_rows(jnp.concatenate([x[:, cols] for x in t], axis=1))


def dil_attn_pallas(q_r, k_r, v, dil):
    B, S, W = q_r.shape
    Sd = S // dil
    TQ, T = DIL_TQ, ATT_T
    R = TQ // T
    view = lambda t: t.reshape(B, Sd, dil * W)
    q_spec = pl.BlockSpec((1, TQ, W), lambda b, i, r: (b, i, r))
    prev = pl.BlockSpec((1, T, W), lambda b, i, r: (b, jnp.maximum(i * R - 1, 0), r))
    cur = [pl.BlockSpec((1, T, W), functools.partial(lambda b, i, r, j: (b, i * R + j, r), j=j)) for j in range(R)]
    o, lse = pl.pallas_call(
        _dil_attn_kernel,
        grid=(B, Sd // TQ, dil),
        in_specs=[q_spec, prev, *cur, prev, *cur],
        out_specs=[q_spec, q_spec],
        out_shape=[jax.ShapeDtypeStruct((B, Sd, dil * W), _F32)] * 2,
        compiler_params=pltpu.CompilerParams(dimension_semantics=("parallel", "parallel", "parallel")),
        name="dil_attn",
    )(view(q_r), *([view(k_r)] * (R + 1)), *([view(v)] * (R + 1)))
    return o.reshape(B, S, W), lse.reshape(B, S, W)


LOG2_E = 1.4426950408889634
ROPE_LANES = 128
PREP_TM = 256


def _rope_table_kernel(pos_ref, inv_ref, cos_ref, slo_ref, shi_ref):
    ang = pos_ref[...].astype(_F32) * inv_ref[...]
    cos, sin = jnp.cos(ang), jnp.sin(ang)
    first_half = (lax.broadcasted_iota(jnp.int32, ang.shape, 1) % HEAD_DIM) < HEAD_DIM // 2
    cos_ref[...] = cos
    slo_ref[...] = jnp.where(first_half, -sin, 0.0)
    shi_ref[...] = jnp.where(first_half, 0.0, sin)


def rope_tables(positions):
    B, S = positions.shape
    T = B * S
    half = HEAD_DIM // 2
    inv = ROPE_THETA ** (-jnp.arange(half, dtype=_F32) / half)
    inv = jnp.tile(inv, ROPE_LANES // half).reshape(1, ROPE_LANES)
    pos = jnp.broadcast_to(positions.reshape(T, 1), (T, ROPE_LANES))
    tm = 1024
    spec = pl.BlockSpec((tm, ROPE_LANES), lambda i: (i, 0))
    return pl.pallas_call(
        _rope_table_kernel,
        grid=(T // tm,),
        in_specs=[spec, pl.BlockSpec((1, ROPE_LANES), lambda i: (0, 0))],
        out_specs=[spec] * 3,
        out_shape=[jax.ShapeDtypeStruct((T, ROPE_LANES), _F32)] * 3,
        name="rope_tables",
    )(pos, inv)


def _attn_prep_kernel(cq_ref, ck_ref, cv_ref, dq_ref, d1_ref, d2_ref, cos_ref, slo_ref, shi_ref, gains_ref,
                      cqo_ref, cko_ref, cvo_ref, qn_ref, qr_ref, kc_ref, vc_ref, ks_ref, vst_ref, kw_ref, vwt_ref):
    G, hd = GROUP_W, HEAD_DIM
    wide = lambda t: jnp.concatenate([t] * (G // ROPE_LANES), axis=1)
    cos, slo, shi = wide(cos_ref[...]), wide(slo_ref[...]), wide(shi_ref[...])
    head_sum = _group_indicator(G, hd)
    gains = gains_ref[...]
    scale = hd ** -0.5

    def norm(t, n):
        return t * lax.rsqrt(_dot_split(t * t, head_sum) * (1.0 / hd) + NORM_EPS) * gains[n:n + 1]

    def rope(t):
        return t * cos + pltpu.roll(t, G - hd // 2, axis=1) * slo + pltpu.roll(t, hd // 2, axis=1) * shi

    cqo_ref[0] = (rope(norm(cq_ref[0], 0)) * (scale * LOG2_E)).astype(_BF16)
    cko_ref[0] = rope(norm(ck_ref[0], 1)).astype(_BF16)
    cvo_ref[0] = cv_ref[0].astype(_BF16)
    qn = norm(dq_ref[0], 2)
    qn_ref[0] = (qn * scale).astype(_BF16)
    qr_ref[0] = (rope(qn) * (scale * LOG2_E)).astype(_BF16)
    d1 = d1_ref[0]
    d2 = d2_ref[0]
    kc_ref[0] = d1[:, 0:hd]
    vc_ref[0] = d1[:, hd:2 * hd]
    ks_ref[0] = rope(norm(d1, 3))[:, 2 * hd:3 * hd].astype(_BF16)
    kw_ref[0] = rope(norm(d2, 4))[:, 0:hd].astype(_BF16)
    vst_ref[0] = d1.T[3 * hd:4 * hd, :].astype(_BF16)
    vwt_ref[0] = d2.T[hd:2 * hd, :].astype(_BF16)


def attn_prep_pallas(proj, tabs, dil_q_g, dil_k_g, nsa_q_g, nsa_ks_g, nsa_kw_g):
    B, S, _ = proj.shape
    tm = PREP_TM
    G, hd, H = GROUP_W, HEAD_DIM, N_HEADS
    c0 = (SECTION_WIDTHS[0] + SECTION_WIDTHS[1]) // G
    z = jnp.zeros((hd,), _F32)
    gains = jnp.stack([jnp.tile(dil_q_g, H), jnp.tile(dil_k_g, H), jnp.tile(nsa_q_g, H),
                       jnp.concatenate([z, z, nsa_ks_g, z]), jnp.concatenate([nsa_kw_g, z, z, z])])
    col = lambda c: pl.BlockSpec((1, tm, G), lambda b, i: (b, i, c))
    nt = S // tm
    tab = pl.BlockSpec((tm, ROPE_LANES), lambda b, i: (b * nt + i, 0))
    tok = lambda w, dt: (pl.BlockSpec((1, tm, w), lambda b, i: (b, i, 0)), jax.ShapeDtypeStruct((B, S, w), dt))
    tr = (pl.BlockSpec((1, hd, tm), lambda b, i: (b, 0, i)), jax.ShapeDtypeStruct((B, hd, S), _BF16))
    outs = [tok(G, _BF16), tok(G, _BF16), tok(G, _BF16), tok(G, _BF16), tok(G, _BF16), tok(hd, _F32), tok(hd, _F32), tok(hd, _BF16), tr, tok(hd, _BF16), tr]
    return pl.pallas_call(
        _attn_prep_kernel,
        grid=(B, nt),
        in_specs=[col(c0), col(c0 + 1), col(c0 + 2), col(c0 + 3), col(c0 + 4), col(c0 + 5), tab, tab, tab,
                  pl.BlockSpec(gains.shape, lambda b, i: (0, 0))],
        out_specs=[o[0] for o in outs],
        out_shape=[o[1] for o in outs],
        compiler_params=pltpu.CompilerParams(dimension_semantics=("parallel", "parallel")),
        name="attn_prep",
    )(proj, proj, proj, proj, proj, proj, *tabs, gains)


NSA_NC = 256
CMP_CHUNK = CMP_STRIDE * HEAD_DIM


def _nsa_compress_kernel(kc_ref, vc_ref, pek_ref, pev_ref, wk1_ref, wk2_ref, wv1_ref, wv2_ref, g_ref, ko_ref, vo_ref):
    def compress(x_ref, pe_ref, w1_ref, w2_ref):
        x = x_ref[0]
        top = _dot_lo(x + pe_ref[0:1], w1_ref[0:CMP_CHUNK, :])
        bot = _dot_lo(x + pe_ref[1:2], w1_ref[CMP_CHUNK:, :])
        hid = top + pltpu.roll(bot, NSA_NC - 1, axis=0)
        return _dot_lo(jax.nn.gelu(hid), w2_ref[...])

    k = compress(kc_ref, pek_ref, wk1_ref, wk2_ref)
    ko_ref[0] = k * lax.rsqrt(jnp.mean(k * k, axis=-1, keepdims=True) + NORM_EPS) * g_ref[...]
    vo_ref[0] = compress(vc_ref, pev_ref, wv1_ref, wv2_ref)


def nsa_compress_pallas(kc, vc, pe_k, pe_v, wk1, wk2, wv1, wv2, kc_g):
    B, S, hd = kc.shape
    assert S // CMP_STRIDE == NSA_NC
    chunks = lambda t: t.reshape(B, NSA_NC, CMP_CHUNK)
    pe2 = lambda p: p.reshape(2, CMP_CHUNK)
    const = lambda t: pl.BlockSpec(t.shape, lambda b: (0,) * t.ndim)
    x_spec = pl.BlockSpec((1, NSA_NC, CMP_CHUNK), lambda b: (b, 0, 0))
    o_spec = pl.BlockSpec((1, NSA_NC, hd), lambda b: (b, 0, 0))
    params = [pe2(pe_k), pe2(pe_v), wk1.astype(_BF16), wk2.astype(_BF16), wv1.astype(_BF16), wv2.astype(_BF16),
              kc_g.reshape(1, hd)]
    return pl.pallas_call(
        _nsa_compress_kernel,
        grid=(B,),
        in_specs=[x_spec, x_spec] + [const(p) for p in params],
        out_specs=[o_spec, o_spec],
        out_shape=[jax.ShapeDtypeStruct((B, NSA_NC, hd), _F32)] * 2,
        name="nsa_compress",
    )(chunks(kc), chunks(vc), *params)


def selection_overlap(n_c, n_sel):
    r = SEL_BLK // CMP_STRIDE
    m = CMP_LEN // CMP_STRIDE
    diff = np.arange(n_c)[:, None] - r * np.arange(n_sel)[None, :]
    offs = (np.arange(r)[:, None] - np.arange(m)[None, :]).reshape(-1)
    return (diff[..., None] == offs).sum(-1).astype(np.float32)


def _nsa_select_kernel(q_ref, kc_ref, vc_ref, ovt_ref, o_ref, selt_ref):
    T = ATT_T
    qi = pl.program_id(1)
    kc = kc_ref[0].astype(_BF16)
    vc = vc_ref[0].astype(_BF16)
    cblk = lax.broadcasted_iota(jnp.int32, (NSA_NC, T), 0)
    tq = qi * T + lax.broadcasted_iota(jnp.int32, (NSA_NC, T), 1)
    valid = cblk * CMP_STRIDE + (CMP_LEN - 1) <= tq
    any_valid = (tq[0:1, :] >= CMP_LEN - 1).astype(_F32)
    heads = range(N_HEADS)
    ss = [lax.dot_general(kc, q_ref[0, :, h * HEAD_DIM:(h + 1) * HEAD_DIM], _NT, preferred_element_type=_F32)
          for h in heads]
    ss = [jnp.where(valid, s, NEG) for s in ss]
    es = [jnp.exp(s - jnp.max(s, axis=0, keepdims=True)) for s in ss]
    ps = [e * (any_valid / jnp.sum(e, axis=0, keepdims=True)) for e in es]
    outs = [lax.dot_general(vc, p.astype(_BF16), _TN, preferred_element_type=_F32) for p in ps]
    p_sum = sum(ps[1:], ps[0])
    o_ref[0] = _heads_to_rows(jnp.concatenate(outs, axis=1))

    ovt = ovt_ref[...]
    hi = p_sum.astype(_BF16)
    lo = (p_sum - hi.astype(_F32)).astype(_BF16)
    imp = jnp.dot(ovt, hi, preferred_element_type=_F32) + jnp.dot(ovt, lo, preferred_element_type=_F32)
    jb = lax.broadcasted_iota(jnp.int32, (NSA_NSEL, T), 0)
    cur = (qi * T + lax.broadcasted_iota(jnp.int32, (NSA_NSEL, T), 1)) // SEL_BLK
    forced = (jb == 0) | (jb == cur) | (jb == cur - 1)
    score = jnp.where(jb > cur, -1.0, jnp.where(forced, FORCE_SCORE, imp))
    jbf = jb.astype(_F32)
    sel = jnp.zeros((NSA_NSEL, T), _F32)
    for _ in range(min(SEL_TOP, NSA_NSEL)):
        best = jnp.max(score, axis=0, keepdims=True)
        first = jnp.min(jnp.where(score == best, jbf, float(NSA_NSEL)), axis=0, keepdims=True)
        pick = jbf == first
        sel = jnp.where(pick, 1.0, sel)
        score = jnp.where(pick, NEG, score)
    selt_ref[0] = sel.astype(_BF16)


def nsa_select_pallas(q_n, k_cmp, v_cmp):
    B, S, W = q_n.shape
    T = ATT_T
    n_c = (S - CMP_LEN) // CMP_STRIDE + 1
    ovt = np.zeros((NSA_NSEL, NSA_NC), np.float32)
    ovt[:, :n_c] = selection_overlap(n_c, NSA_NSEL).T
    ovt = jnp.asarray(ovt, _BF16)
    c_spec = pl.BlockSpec((1, NSA_NC, HEAD_DIM), lambda b, i: (b, 0, 0))
    return pl.pallas_call(
        _nsa_select_kernel,
        grid=(B, S // T),
        in_specs=[pl.BlockSpec((1, T, W), lambda b, i: (b, i, 0)), c_spec, c_spec,
                  pl.BlockSpec(ovt.shape, lambda b, i: (0, 0))],
        out_specs=[pl.BlockSpec((1, T, W), lambda b, i: (b, i, 0)),
                   pl.BlockSpec((1, NSA_NSEL, T), lambda b, i: (b, 0, i))],
        out_shape=[jax.ShapeDtypeStruct((B, S, W), _F32), jax.ShapeDtypeStruct((B, NSA_NSEL, S), _BF16)],
        compiler_params=pltpu.CompilerParams(dimension_semantics=("parallel", "parallel")),
        name="nsa_select",
    )(q_n, k_cmp, v_cmp, ovt)


def kernel(x, c, positions, w_ada, b_ada, norm1_g, norm2_g, w_in, w_out, rwkv_mu, rwkv_w0, rwkv_w2, rwkv_a0, rwkv_a2, rwkv_g2, rwkv_kk, rwkv_ka, rwkv_rk, rwkv_gn_w, rwkv_gn_b, conv_w, dil_q_g, dil_k_g, nsa_q_g, nsa_kc_g, nsa_ks_g, nsa_kw_g, nsa_pe_k, nsa_pe_v, nsa_wk1, nsa_wk2, nsa_wv1, nsa_wv2, onorm_g, w_router, b_router, w_gu, b_gu, w_down, b_down):
    B, S, D = x.shape
    tabs = rope_tables(positions)
    for l in range(DEPTH):
        mod = matmul(jax.nn.silu(c), w_ada[l], tm=B, tn=512) + b_ada[l]
        sh1, sc1, gt1, sh2, sc2, gt2 = [m[:, None, :] for m in jnp.split(mod, 6, axis=-1)]
        proj = in_proj_pallas(x, norm1_g[l], sc1, sh1, w_in[l])
        y_a = rwkv7_pallas(proj, rwkv_mu[l], rwkv_w0[l], rwkv_w2[l], rwkv_a0[l], rwkv_a2[l], rwkv_g2[l],
                           rwkv_kk[l], rwkv_ka[l], rwkv_rk[l], rwkv_gn_w[l], rwkv_gn_b[l])
        cq, ck, cv, q_n, q_r, kc, vc, ks, vs_t, kw, vw_t = attn_prep_pallas(
            proj, tabs, dil_q_g[l], dil_k_g[l], nsa_q_g[l], nsa_ks_g[l], nsa_kw_g[l])
        dil_outs = [dil_attn_pallas(cq, ck, cv, dil) for _, dil in DIL_CFG]
        k_cmp, v_cmp = nsa_compress_pallas(kc, vc, nsa_pe_k[l], nsa_pe_v[l], nsa_wk1[l], nsa_wk2[l], nsa_wv1[l],
                                           nsa_wv2[l], nsa_kc_g[l])
        o_cmp, sel_t = nsa_select_pallas(q_n, k_cmp, v_cmp)
        o_slc, o_swa = nsa_attn_pallas(q_r, ks, vs_t, kw, vw_t, sel_t)
        x, hb, logits = mix_out_pallas(x, y_a, proj, dil_outs, o_cmp, o_slc, o_swa, conv_w[l], onorm_g[l], w_out[l], gt1,
                                       norm2_g[l], sc2, sh2, w_router[l], b_router[l])
        x = moe_pallas(x, gt2, hb, logits, l, w_gu, b_gu, w_down, b_down)
    return x
```

```python
import functools

import numpy as np
import jax
import jax.numpy as jnp
from jax import lax
from jax.experimental import pallas as pl
from jax.experimental.pallas import tpu as pltpu

D_MODEL = 1024
DEPTH = 2

HEAD_DIM = 64
N_MIXERS = 4
GROUP_W = D_MODEL // N_MIXERS
N_HEADS = GROUP_W // HEAD_DIM
MIX_W = N_MIXERS * GROUP_W

RWKV_DECAY_RANK = 64
RWKV_AAA_RANK = 64
RWKV_GATE_RANK = 128
RWKV_DECAY_SCALE = 0.606531
RWKV_GN_EPS = 64e-5

CONV_W = 3

DIL_CFG = ((128, 1), (512, 4), (2048, 16))

CMP_LEN = 32
CMP_STRIDE = 16
CMP_HIDDEN = 256
SEL_BLK = 64
SEL_TOP = 16
SWA_WIN = 512
FORCE_SCORE = 1e4

N_EXPERTS = 32
TOP_K = 4
EXPERT_FF = D_MODEL
SWIGLU_LIMIT = 7.0
SWIGLU_ALPHA = 1.702

ROPE_THETA = 10000.0
NORM_EPS = 1e-6
NEG = -1e30

A_WIDTHS = (GROUP_W, GROUP_W, GROUP_W, RWKV_DECAY_RANK, RWKV_AAA_RANK, RWKV_GATE_RANK)
B_WIDTHS = (GROUP_W, GROUP_W, GROUP_W)
C_WIDTHS = (GROUP_W, GROUP_W, GROUP_W)
D_WIDTHS = (GROUP_W, HEAD_DIM, HEAD_DIM, HEAD_DIM, HEAD_DIM, HEAD_DIM, HEAD_DIM, 3 * N_HEADS)
SECTION_WIDTHS = (sum(A_WIDTHS), sum(B_WIDTHS), sum(C_WIDTHS), sum(D_WIDTHS))
PROJ_W = sum(SECTION_WIDTHS)

_F32 = jnp.float32
_BF16 = jnp.bfloat16
_NT = (((1,), (1,)), ((), ()))
_TN = (((0,), (0,)), ((), ()))


def _dot_hi(a, b, dims=None):
    if dims is None:
        return jnp.dot(a, b, precision=lax.Precision.HIGHEST, preferred_element_type=_F32)
    return lax.dot_general(a, b, dims, precision=lax.Precision.HIGHEST, preferred_element_type=_F32)


def _dot_lo(a, b, dims=None):
    a, b = a.astype(_BF16), b.astype(_BF16)
    if dims is None:
        return jnp.dot(a, b, preferred_element_type=_F32)
    return lax.dot_general(a, b, dims, preferred_element_type=_F32)


def _bf16_parts(a, parts):
    out = []
    for _ in range(parts):
        p = a.astype(_BF16)
        out.append(p)
        a = a - p.astype(_F32)
    return out


def _dot_split(a, m, parts=2):
    mb = m.astype(_BF16)
    return sum(jnp.dot(p, mb, preferred_element_type=_F32) for p in _bf16_parts(a, parts))


def _dot_split_left(m, a, parts=2):
    mb = m.astype(_BF16)
    return sum(jnp.dot(mb, p, preferred_element_type=_F32) for p in _bf16_parts(a, parts))


def _dot_x3(a, b):
    ah, al = _bf16_parts(a, 2)
    bh, bl = _bf16_parts(b, 2)
    d = lambda x, y: jnp.dot(x, y, preferred_element_type=_F32)
    return d(ah, bh) + (d(ah, bl) + d(al, bh))


def _matmul_kernel(x_ref, w_ref, o_ref, *, exact):
    if exact:
        o_ref[...] = _dot_hi(x_ref[...], w_ref[...])
    else:
        o_ref[...] = jnp.dot(x_ref[...].astype(_BF16), w_ref[...], preferred_element_type=_F32)


def matmul(x, w, tm=512, tn=256, exact=False):
    M, K = x.shape
    N = w.shape[1]
    n_pad = -(-N // tn) * tn
    wb = jnp.pad(w if exact else w.astype(_BF16), ((0, 0), (0, n_pad - N)))
    out = pl.pallas_call(
        functools.partial(_matmul_kernel, exact=exact),
        grid=(M // tm, n_pad // tn),
        in_specs=[pl.BlockSpec((tm, K), lambda i, j: (i, 0)),
                  pl.BlockSpec((K, tn), lambda i, j: (0, j))],
        out_specs=pl.BlockSpec((tm, tn), lambda i, j: (i, j)),
        out_shape=jax.ShapeDtypeStruct((M, n_pad), _F32),
        name="matmul",
    )(x, wb)
    return out[:, :N]


PROJ_PAD = -(-PROJ_W // 256) * 256
PROJ_TM = 256
PROJ_TN = 512
ROUTER_PAD = 128
GATE_COL_BLOCK = (PROJ_W - 3 * N_HEADS) // 128
assert GATE_COL_BLOCK * 128 == PROJ_W - 3 * N_HEADS


def _norm_mod(x, g, sc, sh):
    y = x * lax.rsqrt(jnp.mean(x * x, axis=-1, keepdims=True) + NORM_EPS) * g
    return y * (1.0 + sc) + sh


def _in_proj_kernel(x_ref, g_ref, sc_ref, sh_ref, w_ref, o_ref):
    h = _norm_mod(x_ref[0], g_ref[...], sc_ref[0], sh_ref[0]).astype(_BF16)
    for n0 in range(0, PROJ_PAD, PROJ_TN):
        o_ref[0, :, n0:n0 + PROJ_TN] = jnp.dot(h, w_ref[:, n0:n0 + PROJ_TN], preferred_element_type=_F32)


def in_proj_pallas(x, g, sc, sh, w_in):
    B, S, D = x.shape
    tm = PROJ_TM
    w = jnp.pad(w_in.astype(_BF16), ((0, 0), (0, PROJ_PAD - PROJ_W)))
    per_batch = pl.BlockSpec((1, 1, D), lambda b, i: (b, 0, 0))
    return pl.pallas_call(
        _in_proj_kernel,
        grid=(B, S // tm),
        in_specs=[pl.BlockSpec((1, tm, D), lambda b, i: (b, i, 0)),
                  pl.BlockSpec((1, D), lambda b, i: (0, 0)), per_batch, per_batch,
                  pl.BlockSpec((D, PROJ_PAD), lambda b, i: (0, 0))],
        out_specs=pl.BlockSpec((1, tm, PROJ_PAD), lambda b, i: (b, i, 0)),
        out_shape=jax.ShapeDtypeStruct((B, S, PROJ_PAD), _F32),
        compiler_params=pltpu.CompilerParams(dimension_semantics=("parallel", "parallel"),
                                             vmem_limit_bytes=48 * 1024 * 1024),
        name="in_proj",
    )(x, g.reshape(1, D), sc, sh, w)


def _mix_out_kernel(x_ref, ya_ref, bg_ref, cg_ref, xin_ref, cgh_ref, xinh_ref, do0_ref, dl0_ref, do1_ref, dl1_ref,
                    do2_ref, dl2_ref, ocmp_ref, oslc_ref, oswa_ref,
                    gl_ref, convw_ref, ong_ref, wout_ref, gt1_ref, g2_ref, sc2_ref, sh2_ref, wr_ref, br_ref,
                    xo_ref, hb_ref, lg_ref):
    i = pl.program_id(1)
    G = GROUP_W
    u = cg_ref[0] * xin_ref[0]
    halo = jnp.where(i == 0, 0.0, cgh_ref[0] * xinh_ref[0])
    row = lax.broadcasted_iota(jnp.int32, u.shape, 0)
    u1 = jnp.where(row == 0, halo[7:8], pltpu.roll(u, 1, axis=0))
    u2 = jnp.where(row == 0, halo[6:7], jnp.where(row == 1, halo[7:8], pltpu.roll(u, 2, axis=0)))
    cw = convw_ref[...]
    yb = bg_ref[0] * (cw[0:1] * u2 + cw[1:2] * u1 + cw[2:3] * u)

    sg = jax.nn.sigmoid(gl_ref[0])
    er = lax.broadcasted_iota(jnp.int32, (128, G), 0)
    ec = lax.broadcasted_iota(jnp.int32, (128, G), 1) // HEAD_DIM
    yd = None
    for j, o_ref in enumerate((ocmp_ref, oslc_ref, oswa_ref)):
        gate = _dot_split(sg, er == 3 * ec + j)
        yd = gate * o_ref[0] if yd is None else yd + gate * o_ref[0]

    head_sum = _group_indicator(G, HEAD_DIM)
    ong = ong_ref[...]
    parts = [ya_ref[0]]
    dil = ((do0_ref, dl0_ref), (do1_ref, dl1_ref), (do2_ref, dl2_ref))
    top = functools.reduce(jnp.maximum, [l_ref[0] for _, l_ref in dil])
    wts = [jnp.exp2(l_ref[0] - top) for _, l_ref in dil]
    yc = sum(w * o_ref[0] for w, (o_ref, _) in zip(wts, dil)) / sum(wts)
    for n, y in enumerate((yb, yc, yd)):
        ms = _dot_split(y * y, head_sum) * (1.0 / HEAD_DIM)
        parts.append(y * lax.rsqrt(ms + NORM_EPS) * ong[:, n * G:(n + 1) * G])
    mixed = jnp.dot(jnp.concatenate(parts, axis=1).astype(_BF16), wout_ref[...], preferred_element_type=_F32)
    x = x_ref[0] + gt1_ref[0] * mixed
    xo_ref[0] = x
    h = _norm_mod(x, g2_ref[...], sc2_ref[0], sh2_ref[0])
    hb_ref[0] = h.astype(_BF16)
    lg_ref[0] = _dot_x3(h, wr_ref[...]) + br_ref[...]


def mix_out_pallas(x, y_a, proj, dil_outs, o_cmp, o_slc, o_swa, conv_w, onorm_g, w_out, gt1, g2, sc2, sh2, w_router, b_router):
    B, S, D = x.shape
    tm = PROJ_TM
    G = GROUP_W
    b0 = SECTION_WIDTHS[0] // G
    tile = lambda w, col: pl.BlockSpec((1, tm, w), lambda b, i: (b, i, col))
    halo = lambda col: pl.BlockSpec((1, 8, G), lambda b, i: (b, jnp.maximum(i * (tm // 8) - 1, 0), col))
    const = lambda t: pl.BlockSpec(t.shape, lambda b, i: (0,) * t.ndim)
    per_batch = pl.BlockSpec((1, 1, D), lambda b, i: (b, 0, 0))
    wr = jnp.pad(w_router, ((0, 0), (0, ROUTER_PAD - N_EXPERTS)))
    br = jnp.pad(b_router, (0, ROUTER_PAD - N_EXPERTS)).reshape(1, ROUTER_PAD)
    consts = [conv_w, onorm_g.reshape(1, 3 * G), w_out.astype(_BF16)]
    return pl.pallas_call(
        _mix_out_kernel,
        grid=(B, S // tm),
        in_specs=[tile(D, 0), tile(G, 0), tile(G, b0), tile(G, b0 + 1), tile(G, b0 + 2), halo(b0 + 1), halo(b0 + 2),
                  *([tile(G, 0)] * (2 * len(dil_outs) + 3)), tile(128, GATE_COL_BLOCK)]
                 + [const(t) for t in consts] + [per_batch, pl.BlockSpec((1, D), lambda b, i: (0, 0)), per_batch,
                                                 per_batch, const(wr), const(br)],
        out_specs=[tile(D, 0), tile(D, 0), tile(ROUTER_PAD, 0)],
        out_shape=[jax.ShapeDtypeStruct((B, S, D), _F32), jax.ShapeDtypeStruct((B, S, D), _BF16),
                   jax.ShapeDtypeStruct((B, S, ROUTER_PAD), _F32)],
        compiler_params=pltpu.CompilerParams(dimension_semantics=("parallel", "parallel")),
        name="mix_out",
    )(x, y_a, proj, proj, proj, proj, proj, *[t for pair in dil_outs for t in pair], o_cmp, o_slc, o_swa, proj, *consts, gt1, g2.reshape(1, D), sc2, sh2,
      wr, br)


RWKV_CHUNK = 64
RWKV_TILE = 512


def _group_indicator(n, group):
    r = lax.broadcasted_iota(jnp.int32, (n, n), 0) // group
    c = lax.broadcasted_iota(jnp.int32, (n, n), 1) // group
    return r == c


def _rwkv_kernel(za_ref, mu_ref, w0_ref, w2_ref, a0_ref, a2_ref, g2_ref, kk_ref, ka_ref, rk_ref,
                 gnw_ref, gnb_ref, o_ref, prev_sc, h_sc):
    C = RWKV_CHUNK
    TT = RWKV_TILE
    c = pl.program_id(1)

    @pl.when(c == 0)
    def _():
        prev_sc[...] = jnp.zeros_like(prev_sc)
        h_sc[...] = jnp.zeros_like(h_sc)

    z = za_ref[0]
    row = lax.broadcasted_iota(jnp.int32, z.shape, 0)
    zs = jnp.where(row == 0, prev_sc[...], pltpu.roll(z, 1, axis=0))
    prev_sc[...] = z[TT - 1:TT, :]
    z = z + (zs - z) * mu_ref[...]

    G = GROUP_W
    r, k, v = z[:, 0:G], z[:, G:2 * G], z[:, 2 * G:3 * G]
    o = 3 * G
    wd = z[:, o:o + RWKV_DECAY_RANK]
    ad = z[:, o + RWKV_DECAY_RANK:o + RWKV_DECAY_RANK + RWKV_AAA_RANK]
    gd = z[:, o + RWKV_DECAY_RANK + RWKV_AAA_RANK:]

    lw = -RWKV_DECAY_SCALE * jax.nn.sigmoid(w0_ref[...] + _dot_lo(jnp.tanh(wd), w2_ref[...]))
    a = jax.nn.sigmoid(a0_ref[...] + _dot_lo(ad, a2_ref[...]))
    g = _dot_lo(jax.nn.sigmoid(gd), g2_ref[...])

    head_sum = _group_indicator(G, HEAD_DIM)
    kk = k * kk_ref[...]
    kk = kk * lax.rsqrt(_dot_split(kk * kk, head_sum) + 1e-12)
    k = k * (1.0 + (a - 1.0) * ka_ref[...])
    b = kk * a

    ti = lax.broadcasted_iota(jnp.int32, (C, C), 0)
    tj = lax.broadcasted_iota(jnp.int32, (C, C), 1)
    incl = ti >= tj
    strict = ti > tj
    eye = ti == tj
    blk16 = (ti // 16) == (tj // 16)
    blk32 = (ti // 32) == (tj // 32)
    eye_f = eye.astype(_F32)

    ri = lax.broadcasted_iota(jnp.int32, (TT, TT), 0)
    rj = lax.broadcasted_iota(jnp.int32, (TT, TT), 1)
    chunk_tri = (ri >= rj) & (ri // C == rj // C)
    cum = _dot_split_left(chunk_tri, lw, parts=3)
    g_in = jnp.exp(cum)
    A_all = -kk * jnp.exp(cum - lw)
    R_all = r * g_in
    g_inv = jnp.exp(-cum)
    B_all = b * g_inv
    K_all = k * g_inv

    units = [(ci, h) for ci in range(TT // C) for h in range(N_HEADS)]

    def part(t, u):
        ci, h = u
        return t[ci * C:(ci + 1) * C, h * HEAD_DIM:(h + 1) * HEAD_DIM]

    def g_end(u):
        ci, h = u
        return g_in[(ci + 1) * C - 1:(ci + 1) * C, h * HEAD_DIM:(h + 1) * HEAD_DIM]

    A = [part(A_all, u) for u in units]
    R = [part(R_all, u) for u in units]
    B = [part(B_all, u) for u in units]
    Kt = [part(K_all, u) for u in units]
    V = [part(v, u) for u in units]
    n = range(len(units))
    gram = [_dot_lo(jnp.concatenate([A[i], R[i]], axis=0), jnp.concatenate([B[i], Kt[i]], axis=0), _NT) for i in n]
    l_ab = [jnp.where(strict, gram[i][0:C, 0:C], 0.0) for i in n]
    l_ak = [jnp.where(strict, gram[i][0:C, C:2 * C], 0.0) for i in n]
    m_rb = [jnp.where(incl, gram[i][C:2 * C, 0:C], 0.0) for i in n]
    m_rk = [jnp.where(incl, gram[i][C:2 * C, C:2 * C], 0.0) for i in n]
    p = [jnp.where(blk16, l_ab[i], 0.0) for i in n]
    x = [eye_f + p[i] for i in n]
    for _ in range(3):
        p = [_dot_lo(p[i], p[i]) for i in n]
        x = [_dot_lo(x[i], eye_f + p[i]) for i in n]
    for lvl in (blk32 & ~blk16, ~blk32):
        xl = [_dot_lo(x[i], jnp.where(lvl, l_ab[i], 0.0)) for i in n]
        x = [x[i] + _dot_lo(xl[i], x[i]) for i in n]
    lv = [_dot_lo(l_ak[i], V[i]) for i in n]
    tap = [_dot_lo(x[i], jnp.concatenate([A[i], lv[i]], axis=1)) for i in n]
    m1 = [_dot_lo(m_rb[i], tap[i]) for i in n]
    mv = [_dot_lo(m_rk[i], V[i]) for i in n]
    bt = [_dot_lo(B[i] * g_end(units[i]), tap[i], _TN) for i in n]
    kv = [_dot_lo(Kt[i] * g_end(units[i]), V[i], _TN) for i in n]
    w_yh = [jnp.concatenate([R[i] + m1[i][:, 0:HEAD_DIM],
                             jnp.where(eye, g_end(units[i]), 0.0) + bt[i][:, 0:HEAD_DIM]], axis=0) for i in n]
    y0 = [m1[i][:, HEAD_DIM:] + mv[i] for i in n]
    h_add = [bt[i][:, HEAD_DIM:] + kv[i] for i in n]
    state = [h_sc[h] for h in range(N_HEADS)]
    y_rows = []
    for ci in range(TT // C):
        ys = []
        for h in range(N_HEADS):
            i = ci * N_HEADS + h
            nxt = _dot_x3(w_yh[i], state[h])
            ys.append(nxt[0:C] + y0[i])
            state[h] = nxt[C:2 * C] + h_add[i]
        y_rows.append(jnp.concatenate(ys, axis=1))
    for h in range(N_HEADS):
        h_sc[h] = state[h]
    y = jnp.concatenate(y_rows, axis=0)

    mu = _dot_split(y, head_sum) * (1.0 / HEAD_DIM)
    d = y - mu
    var = _dot_split(d * d, head_sum) * (1.0 / HEAD_DIM)
    y = d * lax.rsqrt(var + RWKV_GN_EPS) * gnw_ref[...] + gnb_ref[...]
    y = y + _dot_split(r * k * rk_ref[...], head_sum) * v
    o_ref[0] = y * g


def rwkv7_pallas(za, mu, w0, w2, a0, a2, g2, k_k, k_a, r_k, gn_w, gn_b):
    B, S, _ = za.shape
    W = SECTION_WIDTHS[0]
    TT = RWKV_TILE
    row = lambda t: t.reshape(1, -1).astype(_F32)
    full = lambda t: pl.BlockSpec(t.shape, lambda b, c: (0,) * t.ndim)
    params = [row(mu), row(w0), w2.astype(_BF16), row(a0), a2.astype(_BF16), g2.astype(_BF16),
              row(k_k), row(k_a), row(r_k), row(gn_w), row(gn_b)]
    return pl.pallas_call(
        _rwkv_kernel,
        grid=(B, S // TT),
        in_specs=[pl.BlockSpec((1, TT, W), lambda b, c: (b, c, 0))] + [full(p) for p in params],
        out_specs=pl.BlockSpec((1, TT, GROUP_W), lambda b, c: (b, c, 0)),
        out_shape=jax.ShapeDtypeStruct((B, S, GROUP_W), _F32),
        scratch_shapes=[pltpu.VMEM((1, W), _F32), pltpu.VMEM((N_HEADS, HEAD_DIM, HEAD_DIM), _F32)],
        compiler_params=pltpu.CompilerParams(dimension_semantics=("parallel", "arbitrary")),
        name="rwkv7",
    )(za, *params)


MOE_TM = 512
MOE_FC = 512
MOE_CAST_ROWS = 128


def _moe_ffn_kernel(blk_e_ref, n_used_ref, x_ref, wgu_ref, bgu_ref, wd_ref, bd_ref, o_ref, wgu_sc, wd_sc):
    i = pl.program_id(0)
    F = EXPERT_FF
    new_expert = jnp.logical_or(i == 0, blk_e_ref[i] != blk_e_ref[jnp.maximum(i - 1, 0)])

    @pl.when(new_expert)
    def _():
        def cast(j, carry):
            rows = pl.ds(pl.multiple_of(j * MOE_CAST_ROWS, MOE_CAST_ROWS), MOE_CAST_ROWS)
            wgu_sc[rows, :] = wgu_ref[0, rows, :].astype(_BF16)
            wd_sc[rows, :] = wd_ref[0, rows, :].astype(_BF16)
            return carry
        lax.fori_loop(0, F // MOE_CAST_ROWS, cast, 0)

    @pl.when(i < n_used_ref[0])
    def _():
        x = x_ref[...]
        acc = None
        for c in range(F // MOE_FC):
            lo = c * MOE_FC
            gate = jnp.dot(x, wgu_sc[:, lo:lo + MOE_FC], preferred_element_type=_F32) + bgu_ref[0, :, lo:lo + MOE_FC]
            up = jnp.dot(x, wgu_sc[:, F + lo:F + lo + MOE_FC], preferred_element_type=_F32) + bgu_ref[0, :, F + lo:F + lo + MOE_FC]
            gate = jnp.minimum(gate, SWIGLU_LIMIT)
            up = jnp.clip(up, -SWIGLU_LIMIT, SWIGLU_LIMIT)
            act = gate * jax.nn.sigmoid(SWIGLU_ALPHA * gate) * (up + 1.0)
            part = jnp.dot(act.astype(_BF16), wd_sc[lo:lo + MOE_FC, :], preferred_element_type=_F32)
            acc = part if acc is None else acc + part
        o_ref[...] = (acc + bd_ref[0]).astype(o_ref.dtype)

    @pl.when(i >= n_used_ref[0])
    def _():
        o_ref[...] = jnp.zeros_like(o_ref)


def moe_ffn(xs, blk_e, n_used, layer, w_gu, b_gu, w_down, b_down):
    n_rows, D = xs.shape
    L, E, _, F2 = w_gu.shape
    assert D == EXPERT_FF and F2 == 2 * EXPERT_FF
    n_blk = n_rows // MOE_TM
    grid_spec = pltpu.PrefetchScalarGridSpec(
        num_scalar_prefetch=2,
        grid=(n_blk,),
        in_specs=[pl.BlockSpec((MOE_TM, D), lambda i, e, n: (i, 0)),
                  pl.BlockSpec((None, 1, D, F2), lambda i, e, n: (layer, e[i], 0, 0)),
                  pl.BlockSpec((None, 1, 1, F2), lambda i, e, n: (layer, e[i], 0, 0)),
                  pl.BlockSpec((None, 1, F2 // 2, D), lambda i, e, n: (layer, e[i], 0, 0)),
                  pl.BlockSpec((None, 1, 1, D), lambda i, e, n: (layer, e[i], 0, 0))],
        out_specs=pl.BlockSpec((MOE_TM, D), lambda i, e, n: (i, 0)),
        scratch_shapes=[pltpu.VMEM((D, F2), _BF16), pltpu.VMEM((F2 // 2, D), _BF16)],
    )
    return pl.pallas_call(
        _moe_ffn_kernel,
        grid_spec=grid_spec,
        out_shape=jax.ShapeDtypeStruct((n_rows, D), _BF16),
        compiler_params=pltpu.CompilerParams(dimension_semantics=("arbitrary",),
                                             vmem_limit_bytes=52 * 1024 * 1024),
        name="moe_ffn",
    )(blk_e, n_used, xs, w_gu, b_gu.reshape(L, E, 1, F2), w_down, b_down.reshape(L, E, 1, D))


def moe_pallas(x, gt2, hb, logits, layer, w_gu, b_gu, w_down, b_down):
    D = x.shape[-1]
    hb = hb.reshape(-1, D)
    logits = logits.reshape(hb.shape[0], -1)
    T = hb.shape[0]
    TK = T * TOP_K
    TM = MOE_TM
    E = N_EXPERTS
    i32 = jnp.int32
    top_val, top_idx = lax.top_k(logits[:, :E], TOP_K)
    gates = jax.nn.softmax(top_val, axis=-1)
    e_flat = top_idx.T.reshape(TK).astype(i32)
    counts = jnp.sum((jnp.arange(E, dtype=i32)[:, None] == e_flat[None, :]).astype(i32), axis=1)
    need = jnp.repeat((-counts) % TM, TM)
    d_idx = jnp.arange(E * TM, dtype=i32)
    d_key = jnp.where(d_idx % TM < need, d_idx // TM, E)
    n_rows = TK + E * TM
    keys = jnp.concatenate([e_flat, d_key])
    toks = jnp.concatenate([jnp.arange(TK, dtype=i32) % T, jnp.zeros((E * TM,), i32)])
    rows = jnp.arange(n_rows, dtype=i32)
    s_keys, src_tok, s_slot = lax.sort((keys, toks, rows), num_keys=1)
    _, row_of_slot = lax.sort((s_slot, rows), num_keys=1)
    row_of = row_of_slot[:TK]
    blk_e = s_keys[::TM]
    n_used = jnp.sum((blk_e < E).astype(i32)).reshape(1)
    blk_e = jnp.where(blk_e < E, blk_e, blk_e[jnp.maximum(n_used[0] - 1, 0)])
    xs = jnp.take(hb, src_tok, axis=0, mode="clip")
    ybuf = moe_ffn(xs, blk_e, n_used, layer, w_gu, b_gu, w_down, b_down)
    y = jnp.take(ybuf, row_of, axis=0, mode="clip").reshape(TOP_K, T, D)
    return moe_combine(x, y, gates, gt2)


def _moe_combine_kernel(y_ref, g_ref, x_ref, gt2_ref, o_ref):
    g = g_ref[...]
    acc = None
    for k in range(TOP_K):
        term = y_ref[k].astype(_F32) * g[:, k:k + 1]
        acc = term if acc is None else acc + term
    o_ref[0] = x_ref[0] + gt2_ref[0] * acc


def moe_combine(x, y, gates, gt2):
    B, S, D = x.shape
    tm = PROJ_TM
    nt = S // tm
    g = jnp.pad(gates, ((0, 0), (0, 128 - TOP_K)))
    return pl.pallas_call(
        _moe_combine_kernel,
        grid=(B, nt),
        in_specs=[pl.BlockSpec((TOP_K, tm, D), lambda b, i: (0, b * nt + i, 0)),
                  pl.BlockSpec((tm, 128), lambda b, i: (b * nt + i, 0)),
                  pl.BlockSpec((1, tm, D), lambda b, i: (b, i, 0)),
                  pl.BlockSpec((1, 1, D), lambda b, i: (b, 0, 0))],
        out_specs=pl.BlockSpec((1, tm, D), lambda b, i: (b, i, 0)),
        out_shape=jax.ShapeDtypeStruct((B, S, D), _F32),
        compiler_params=pltpu.CompilerParams(dimension_semantics=("parallel", "parallel")),
        name="moe_combine",
    )(y, g, x, gt2)


ATT_T = 128
ATT_TK = 512
ATT_R = ATT_TK // ATT_T
NSA_NSEL = 64
NSA_TQ = 256


def _window_table(window, weight_of_distance, tq):
    T, TK = ATT_T, ATT_TK
    n = (window + TK) // T + 1
    d = (np.arange(n)[:, None, None] * T + np.arange(tq)[None, None, :] - np.arange(TK)[None, :, None])
    return jnp.asarray(weight_of_distance(d).astype(np.float32))


def _first_key_tile(q_start, window):
    return jnp.maximum(q_start - window, 0) // ATT_TK


def _swa_weight(d):
    return (d >= 0) & (d <= SWA_WIN - 1)


def _softmax_tile(s, w, m_ref, l_ref):
    s = jnp.where(w > 0.0, s, NEG)
    m_prev = m_ref[...]
    m_new = jnp.maximum(m_prev, jnp.max(s, axis=0, keepdims=True))
    alpha = jnp.exp2(m_prev - m_new)
    p = jnp.exp2(s - m_new)
    l_ref[...] = alpha * l_ref[...] + jnp.sum(p, axis=0, keepdims=True)
    m_ref[...] = m_new
    return alpha, p.astype(_BF16)


def _init_stats(m_sc, l_sc, acc_sc):
    m_sc[...] = jnp.full_like(m_sc, NEG)
    l_sc[...] = jnp.zeros_like(l_sc)
    acc_sc[...] = jnp.zeros_like(acc_sc)


def _heads_to_rows(ot, T=ATT_T):
    pairs = []
    for h in range(0, N_HEADS, 2):
        two = jnp.concatenate([ot[:, h * T:(h + 1) * T], ot[:, (h + 1) * T:(h + 2) * T]], axis=0)
        pairs.append(two.T)
    return jnp.concatenate(pairs, axis=1)


def _nsa_attn_kernel(q_ref, ks_ref, vst_ref, kw_ref, vwt_ref, selt_ref, swa_ref, oslc_ref, oswa_ref,
                     m_sc, l_sc, acc_sc):
    T, TK = NSA_TQ, ATT_TK
    qi = pl.program_id(1)
    q = q_ref[0]
    q = jnp.concatenate([q[:, h * HEAD_DIM:(h + 1) * HEAD_DIM] for h in range(N_HEADS)], axis=0)
    selt = selt_ref[0]
    key_blk = lax.broadcasted_iota(jnp.int32, (TK, NSA_NSEL), 0) // SEL_BLK
    blk_id = lax.broadcasted_iota(jnp.int32, (TK, NSA_NSEL), 1)
    kr = lax.broadcasted_iota(jnp.int32, (TK, T), 0)
    qc = lax.broadcasted_iota(jnp.int32, (TK, T), 1)

    def step(k_ref, vt_ref, kt, w):
        base = pl.multiple_of(kt * TK, TK)
        s = lax.dot_general(k_ref[0, pl.ds(base, TK), :], q, _NT, preferred_element_type=_F32)
        alpha, p = _softmax_tile(s, jnp.concatenate([w] * N_HEADS, axis=1), m_sc, l_sc)
        acc_sc[...] = alpha * acc_sc[...] + jnp.dot(vt_ref[0, :, pl.ds(base, TK)], p, preferred_element_type=_F32)

    def finish(o_ref):
        o_ref[0] = _heads_to_rows(acc_sc[...] / l_sc[...], T)

    last = (qi * T + T - 1) // TK
    _init_stats(m_sc, l_sc, acc_sc)

    def slc_step(kt, carry):
        in_blk = (key_blk + kt * (TK // SEL_BLK) == blk_id).astype(_BF16)
        w = jnp.dot(in_blk, selt, preferred_element_type=_F32)
        w = jnp.where(kt * TK + kr <= qi * T + qc, w, 0.0)
        step(ks_ref, vst_ref, kt, w)
        return carry

    lax.fori_loop(0, last + 1, slc_step, 0)
    finish(oslc_ref)

    _init_stats(m_sc, l_sc, acc_sc)

    def swa_step(kt, carry):
        step(kw_ref, vwt_ref, kt, swa_ref[qi * (T // ATT_T) - kt * ATT_R])
        return carry

    lax.fori_loop(_first_key_tile(qi * T, SWA_WIN - 1), last + 1, swa_step, 0)
    finish(oswa_ref)


def nsa_attn_pallas(q_r, ks_r, vs_t, kw_r, vw_t, sel_t):
    B, S, W = q_r.shape
    assert S // SEL_BLK == NSA_NSEL
    T = NSA_TQ
    R = N_HEADS * T
    swa_tab = _window_table(SWA_WIN - 1, _swa_weight, T)
    k_spec = pl.BlockSpec((1, S, HEAD_DIM), lambda b, i: (b, 0, 0))
    vt_spec = pl.BlockSpec((1, HEAD_DIM, S), lambda b, i: (b, 0, 0))
    q_spec = pl.BlockSpec((1, T, W), lambda b, i: (b, i, 0))
    return pl.pallas_call(
        _nsa_attn_kernel,
        grid=(B, S // T),
        in_specs=[q_spec, k_spec, vt_spec, k_spec, vt_spec,
                  pl.BlockSpec((1, NSA_NSEL, T), lambda b, i: (b, 0, i)),
                  pl.BlockSpec(swa_tab.shape, lambda b, i: (0, 0, 0))],
        out_specs=[q_spec, q_spec],
        out_shape=[jax.ShapeDtypeStruct((B, S, W), _F32)] * 2,
        scratch_shapes=[pltpu.VMEM((1, R), _F32), pltpu.VMEM((1, R), _F32), pltpu.VMEM((HEAD_DIM, R), _F32)],
        compiler_params=pltpu.CompilerParams(dimension_semantics=("parallel", "arbitrary")),
        name="nsa_attn",
    )(q_r, ks_r, vs_t, kw_r, vw_t, sel_t, swa_tab)


DIL_TQ = 256
DIL_WIN = DIL_CFG[0][0] // DIL_CFG[0][1]
assert all(w // d == DIL_WIN for w, d in DIL_CFG) and DIL_WIN == ATT_T and DIL_TQ == 2 * ATT_T


def _dil_attn_kernel(q_ref, kp_ref, k0_ref, k1_ref, vp_ref, v0_ref, v1_ref, o_ref, lse_ref):
    TQ, W = DIL_TQ, DIL_WIN
    i = pl.program_id(1)
    k = jnp.concatenate([kp_ref[0], k0_ref[0], k1_ref[0]], axis=0)
    v = jnp.concatenate([vp_ref[0], v0_ref[0], v1_ref[0]], axis=0)
    row = lax.broadcasted_iota(jnp.int32, (W + TQ, TQ), 0)
    col = lax.broadcasted_iota(jnp.int32, (W + TQ, TQ), 1)
    dist = W + col - row
    valid = (dist >= 0) & (dist <= W) & ((i > 0) | (row >= W))
    heads = [slice(h * HEAD_DIM, (h + 1) * HEAD_DIM) for h in range(N_HEADS)]
    ss = [lax.dot_general(k[:, hs], q_ref[0, :, hs], _NT, preferred_element_type=_F32) for hs in heads]
    ss = [jnp.where(valid, s, NEG) for s in ss]
    ms = [jnp.max(s, axis=0, keepdims=True) for s in ss]
    ps = [jnp.exp2(s - m) for s, m in zip(ss, ms)]
    ls = [jnp.sum(p, axis=0, keepdims=True) for p in ps]
    os_ = [lax.dot_general(v[:, hs], p.astype(_BF16), _TN, preferred_element_type=_F32) for hs, p in zip(heads, ps)]
    outs = [o / l for o, l in zip(os_, ls)]
    lses = [jnp.broadcast_to(m + jnp.log2(l), (HEAD_DIM, TQ)) for m, l in zip(ms, ls)]
    for t, ref in ((outs, o_ref), (lses, lse_ref)):
        for half in range(TQ // ATT_T):
            cols = slice(half * ATT_T, (half + 1) * ATT_T)
            ref[0, cols, :] = _heads_to_rows(jnp.concatenate([x[:, cols] for x in t], axis=1))


def dil_attn_pallas(q_r, k_r, v, dil):
    B, S, W = q_r.shape
    Sd = S // dil
    TQ, T = DIL_TQ, ATT_T
    R = TQ // T
    view = lambda t: t.reshape(B, Sd, dil * W)
    q_spec = pl.BlockSpec((1, TQ, W), lambda b, i, r: (b, i, r))
    prev = pl.BlockSpec((1, T, W), lambda b, i, r: (b, jnp.maximum(i * R - 1, 0), r))
    cur = [pl.BlockSpec((1, T, W), functools.partial(lambda b, i, r, j: (b, i * R + j, r), j=j)) for j in range(R)]
    o, lse = pl.pallas_call(
        _dil_attn_kernel,
        grid=(B, Sd // TQ, dil),
        in_specs=[q_spec, prev, *cur, prev, *cur],
        out_specs=[q_spec, q_spec],
        out_shape=[jax.ShapeDtypeStruct((B, Sd, dil * W), _F32)] * 2,
        compiler_params=pltpu.CompilerParams(dimension_semantics=("parallel", "parallel", "parallel")),
        name="dil_attn",
    )(view(q_r), *([view(k_r)] * (R + 1)), *([view(v)] * (R + 1)))
    return o.reshape(B, S, W), lse.reshape(B, S, W)


LOG2_E = 1.4426950408889634
ROPE_LANES = 128
PREP_TM = 256


def _rope_table_kernel(pos_ref, inv_ref, cos_ref, slo_ref, shi_ref):
    ang = pos_ref[...].astype(_F32) * inv_ref[...]
    cos, sin = jnp.cos(ang), jnp.sin(ang)
    first_half = (lax.broadcasted_iota(jnp.int32, ang.shape, 1) % HEAD_DIM) < HEAD_DIM // 2
    cos_ref[...] = cos
    slo_ref[...] = jnp.where(first_half, -sin, 0.0)
    shi_ref[...] = jnp.where(first_half, 0.0, sin)


def rope_tables(positions):
    B, S = positions.shape
    T = B * S
    half = HEAD_DIM // 2
    inv = ROPE_THETA ** (-jnp.arange(half, dtype=_F32) / half)
    inv = jnp.tile(inv, ROPE_LANES // half).reshape(1, ROPE_LANES)
    pos = jnp.broadcast_to(positions.reshape(T, 1), (T, ROPE_LANES))
    tm = 1024
    spec = pl.BlockSpec((tm, ROPE_LANES), lambda i: (i, 0))
    return pl.pallas_call(
        _rope_table_kernel,
        grid=(T // tm,),
        in_specs=[spec, pl.BlockSpec((1, ROPE_LANES), lambda i: (0, 0))],
        out_specs=[spec] * 3,
        out_shape=[jax.ShapeDtypeStruct((T, ROPE_LANES), _F32)] * 3,
        name="rope_tables",
    )(pos, inv)


def _attn_prep_kernel(cq_ref, ck_ref, cv_ref, dq_ref, d1_ref, d2_ref, cos_ref, slo_ref, shi_ref, gains_ref,
                      cqo_ref, cko_ref, cvo_ref, qn_ref, qr_ref, kc_ref, vc_ref, ks_ref, vst_ref, kw_ref, vwt_ref):
    G, hd = GROUP_W, HEAD_DIM
    wide = lambda t: jnp.concatenate([t] * (G // ROPE_LANES), axis=1)
    cos, slo, shi = wide(cos_ref[...]), wide(slo_ref[...]), wide(shi_ref[...])
    head_sum = _group_indicator(G, hd)
    gains = gains_ref[...]
    scale = hd ** -0.5

    def norm(t, n):
        return t * lax.rsqrt(_dot_split(t * t, head_sum) * (1.0 / hd) + NORM_EPS) * gains[n:n + 1]

    def rope(t):
        return t * cos + pltpu.roll(t, G - hd // 2, axis=1) * slo + pltpu.roll(t, hd // 2, axis=1) * shi

    cqo_ref[0] = (rope(norm(cq_ref[0], 0)) * (scale * LOG2_E)).astype(_BF16)
    cko_ref[0] = rope(norm(ck_ref[0], 1)).astype(_BF16)
    cvo_ref[0] = cv_ref[0].astype(_BF16)
    qn = norm(dq_ref[0], 2)
    qn_ref[0] = (qn * scale).astype(_BF16)
    qr_ref[0] = (rope(qn) * (scale * LOG2_E)).astype(_BF16)
    d1 = d1_ref[0]
    d2 = d2_ref[0]
    kc_ref[0] = d1[:, 0:hd]
    vc_ref[0] = d1[:, hd:2 * hd]
    ks_ref[0] = rope(norm(d1, 3))[:, 2 * hd:3 * hd].astype(_BF16)
    kw_ref[0] = rope(norm(d2, 4))[:, 0:hd].astype(_BF16)
    vst_ref[0] = d1.T[3 * hd:4 * hd, :].astype(_BF16)
    vwt_ref[0] = d2.T[hd:2 * hd, :].astype(_BF16)


def attn_prep_pallas(proj, tabs, dil_q_g, dil_k_g, nsa_q_g, nsa_ks_g, nsa_kw_g):
    B, S, _ = proj.shape
    tm = PREP_TM
    G, hd, H = GROUP_W, HEAD_DIM, N_HEADS
    c0 = (SECTION_WIDTHS[0] + SECTION_WIDTHS[1]) // G
    z = jnp.zeros((hd,), _F32)
    gains = jnp.stack([jnp.tile(dil_q_g, H), jnp.tile(dil_k_g, H), jnp.tile(nsa_q_g, H),
                       jnp.concatenate([z, z, nsa_ks_g, z]), jnp.concatenate([nsa_kw_g, z, z, z])])
    col = lambda c: pl.BlockSpec((1, tm, G), lambda b, i: (b, i, c))
    nt = S // tm
    tab = pl.BlockSpec((tm, ROPE_LANES), lambda b, i: (b * nt + i, 0))
    tok = lambda w, dt: (pl.BlockSpec((1, tm, w), lambda b, i: (b, i, 0)), jax.ShapeDtypeStruct((B, S, w), dt))
    tr = (pl.BlockSpec((1, hd, tm), lambda b, i: (b, 0, i)), jax.ShapeDtypeStruct((B, hd, S), _BF16))
    outs = [tok(G, _BF16), tok(G, _BF16), tok(G, _BF16), tok(G, _BF16), tok(G, _BF16), tok(hd, _F32), tok(hd, _F32), tok(hd, _BF16), tr, tok(hd, _BF16), tr]
    return pl.pallas_call(
        _attn_prep_kernel,
        grid=(B, nt),
        in_specs=[col(c0), col(c0 + 1), col(c0 + 2), col(c0 + 3), col(c0 + 4), col(c0 + 5), tab, tab, tab,
                  pl.BlockSpec(gains.shape, lambda b, i: (0, 0))],
        out_specs=[o[0] for o in outs],
        out_shape=[o[1] for o in outs],
        compiler_params=pltpu.CompilerParams(dimension_semantics=("parallel", "parallel")),
        name="attn_prep",
    )(proj, proj, proj, proj, proj, proj, *tabs, gains)


NSA_NC = 256
CMP_CHUNK = CMP_STRIDE * HEAD_DIM


def _nsa_compress_kernel(kc_ref, vc_ref, pek_ref, pev_ref, wk1_ref, wk2_ref, wv1_ref, wv2_ref, g_ref, ko_ref, vo_ref):
    def compress(x_ref, pe_ref, w1_ref, w2_ref):
        x = x_ref[0]
        top = _dot_lo(x + pe_ref[0:1], w1_ref[0:CMP_CHUNK, :])
        bot = _dot_lo(x + pe_ref[1:2], w1_ref[CMP_CHUNK:, :])
        hid = top + pltpu.roll(bot, NSA_NC - 1, axis=0)
        return _dot_lo(jax.nn.gelu(hid), w2_ref[...])

    k = compress(kc_ref, pek_ref, wk1_ref, wk2_ref)
    ko_ref[0] = k * lax.rsqrt(jnp.mean(k * k, axis=-1, keepdims=True) + NORM_EPS) * g_ref[...]
    vo_ref[0] = compress(vc_ref, pev_ref, wv1_ref, wv2_ref)


def nsa_compress_pallas(kc, vc, pe_k, pe_v, wk1, wk2, wv1, wv2, kc_g):
    B, S, hd = kc.shape
    assert S // CMP_STRIDE == NSA_NC
    chunks = lambda t: t.reshape(B, NSA_NC, CMP_CHUNK)
    pe2 = lambda p: p.reshape(2, CMP_CHUNK)
    const = lambda t: pl.BlockSpec(t.shape, lambda b: (0,) * t.ndim)
    x_spec = pl.BlockSpec((1, NSA_NC, CMP_CHUNK), lambda b: (b, 0, 0))
    o_spec = pl.BlockSpec((1, NSA_NC, hd), lambda b: (b, 0, 0))
    params = [pe2(pe_k), pe2(pe_v), wk1.astype(_BF16), wk2.astype(_BF16), wv1.astype(_BF16), wv2.astype(_BF16),
              kc_g.reshape(1, hd)]
    return pl.pallas_call(
        _nsa_compress_kernel,
        grid=(B,),
        in_specs=[x_spec, x_spec] + [const(p) for p in params],
        out_specs=[o_spec, o_spec],
        out_shape=[jax.ShapeDtypeStruct((B, NSA_NC, hd), _F32)] * 2,
        name="nsa_compress",
    )(chunks(kc), chunks(vc), *params)


def selection_overlap(n_c, n_sel):
    r = SEL_BLK // CMP_STRIDE
    m = CMP_LEN // CMP_STRIDE
    diff = np.arange(n_c)[:, None] - r * np.arange(n_sel)[None, :]
    offs = (np.arange(r)[:, None] - np.arange(m)[None, :]).reshape(-1)
    return (diff[..., None] == offs).sum(-1).astype(np.float32)


def _nsa_select_kernel(q_ref, kc_ref, vc_ref, ovt_ref, o_ref, selt_ref):
    T = ATT_T
    qi = pl.program_id(1)
    kc = kc_ref[0].astype(_BF16)
    vc = vc_ref[0].astype(_BF16)
    cblk = lax.broadcasted_iota(jnp.int32, (NSA_NC, T), 0)
    tq = qi * T + lax.broadcasted_iota(jnp.int32, (NSA_NC, T), 1)
    valid = cblk * CMP_STRIDE + (CMP_LEN - 1) <= tq
    any_valid = (tq[0:1, :] >= CMP_LEN - 1).astype(_F32)
    heads = range(N_HEADS)
    ss = [lax.dot_general(kc, q_ref[0, :, h * HEAD_DIM:(h + 1) * HEAD_DIM], _NT, preferred_element_type=_F32)
          for h in heads]
    ss = [jnp.where(valid, s, NEG) for s in ss]
    es = [jnp.exp(s - jnp.max(s, axis=0, keepdims=True)) for s in ss]
    ps = [e * (any_valid / jnp.sum(e, axis=0, keepdims=True)) for e in es]
    outs = [lax.dot_general(vc, p.astype(_BF16), _TN, preferred_element_type=_F32) for p in ps]
    p_sum = sum(ps[1:], ps[0])
    o_ref[0] = _heads_to_rows(jnp.concatenate(outs, axis=1))

    ovt = ovt_ref[...]
    hi = p_sum.astype(_BF16)
    lo = (p_sum - hi.astype(_F32)).astype(_BF16)
    imp = jnp.dot(ovt, hi, preferred_element_type=_F32) + jnp.dot(ovt, lo, preferred_element_type=_F32)
    jb = lax.broadcasted_iota(jnp.int32, (NSA_NSEL, T), 0)
    cur = (qi * T + lax.broadcasted_iota(jnp.int32, (NSA_NSEL, T), 1)) // SEL_BLK
    forced = (jb == 0) | (jb == cur) | (jb == cur - 1)
    score = jnp.where(jb > cur, -1.0, jnp.where(forced, FORCE_SCORE, imp))
    jbf = jb.astype(_F32)
    sel = jnp.zeros((NSA_NSEL, T), _F32)
    for _ in range(min(SEL_TOP, NSA_NSEL)):
        best = jnp.max(score, axis=0, keepdims=True)
        first = jnp.min(jnp.where(score == best, jbf, float(NSA_NSEL)), axis=0, keepdims=True)
        pick = jbf == first
        sel = jnp.where(pick, 1.0, sel)
        score = jnp.where(pick, NEG, score)
    selt_ref[0] = sel.astype(_BF16)


def nsa_select_pallas(q_n, k_cmp, v_cmp):
    B, S, W = q_n.shape
    T = ATT_T
    n_c = (S - CMP_LEN) // CMP_STRIDE + 1
    ovt = np.zeros((NSA_NSEL, NSA_NC), np.float32)
    ovt[:, :n_c] = selection_overlap(n_c, NSA_NSEL).T
    ovt = jnp.asarray(ovt, _BF16)
    c_spec = pl.BlockSpec((1, NSA_NC, HEAD_DIM), lambda b, i: (b, 0, 0))
    return pl.pallas_call(
        _nsa_select_kernel,
        grid=(B, S // T),
        in_specs=[pl.BlockSpec((1, T, W), lambda b, i: (b, i, 0)), c_spec, c_spec,
                  pl.BlockSpec(ovt.shape, lambda b, i: (0, 0))],
        out_specs=[pl.BlockSpec((1, T, W), lambda b, i: (b, i, 0)),
                   pl.BlockSpec((1, NSA_NSEL, T), lambda b, i: (b, 0, i))],
        out_shape=[jax.ShapeDtypeStruct((B, S, W), _F32), jax.ShapeDtypeStruct((B, NSA_NSEL, S), _BF16)],
        compiler_params=pltpu.CompilerParams(dimension_semantics=("parallel", "parallel")),
        name="nsa_select",
    )(q_n, k_cmp, v_cmp, ovt)


def kernel(x, c, positions, w_ada, b_ada, norm1_g, norm2_g, w_in, w_out, rwkv_mu, rwkv_w0, rwkv_w2, rwkv_a0, rwkv_a2, rwkv_g2, rwkv_kk, rwkv_ka, rwkv_rk, rwkv_gn_w, rwkv_gn_b, conv_w, dil_q_g, dil_k_g, nsa_q_g, nsa_kc_g, nsa_ks_g, nsa_kw_g, nsa_pe_k, nsa_pe_v, nsa_wk1, nsa_wk2, nsa_wv1, nsa_wv2, onorm_g, w_router, b_router, w_gu, b_gu, w_down, b_down):
    B, S, D = x.shape
    tabs = rope_tables(positions)
    for l in range(DEPTH):
        mod = matmul(jax.nn.silu(c), w_ada[l], tm=B, tn=512) + b_ada[l]
        sh1, sc1, gt1, sh2, sc2, gt2 = [m[:, None, :] for m in jnp.split(mod, 6, axis=-1)]
        proj = in_proj_pallas(x, norm1_g[l], sc1, sh1, w_in[l])
        y_a = rwkv7_pallas(proj, rwkv_mu[l], rwkv_w0[l], rwkv_w2[l], rwkv_a0[l], rwkv_a2[l], rwkv_g2[l],
                           rwkv_kk[l], rwkv_ka[l], rwkv_rk[l], rwkv_gn_w[l], rwkv_gn_b[l])
        cq, ck, cv, q_n, q_r, kc, vc, ks, vs_t, kw, vw_t = attn_prep_pallas(
            proj, tabs, dil_q_g[l], dil_k_g[l], nsa_q_g[l], nsa_ks_g[l], nsa_kw_g[l])
        dil_outs = [dil_attn_pallas(cq, ck, cv, dil) for _, dil in DIL_CFG]
        k_cmp, v_cmp = nsa_compress_pallas(kc, vc, nsa_pe_k[l], nsa_pe_v[l], nsa_wk1[l], nsa_wk2[l], nsa_wv1[l],
                                           nsa_wv2[l], nsa_kc_g[l])
        o_cmp, sel_t = nsa_select_pallas(q_n, k_cmp, v_cmp)
        o_slc, o_swa = nsa_attn_pallas(q_r, ks, vs_t, kw, vw_t, sel_t)
        x, hb, logits = mix_out_pallas(x, y_a, proj, dil_outs, o_cmp, o_slc, o_swa, conv_w[l], onorm_g[l], w_out[l], gt1,
                                       norm2_g[l], sc2, sh2, w_router[l], b_router[l])
        x = moe_pallas(x, gt2, hb, logits, l, w_gu, b_gu, w_down, b_down)
    return x
```

```python
import functools

import numpy as np
import jax
import jax.numpy as jnp
from jax import lax
from jax.experimental import pallas as pl
from jax.experimental.pallas import tpu as pltpu

D_MODEL = 1024
DEPTH = 2

HEAD_DIM = 64
N_MIXERS = 4
GROUP_W = D_MODEL // N_MIXERS
N_HEADS = GROUP_W // HEAD_DIM
MIX_W = N_MIXERS * GROUP_W

RWKV_DECAY_RANK = 64
RWKV_AAA_RANK = 64
RWKV_GATE_RANK = 128
RWKV_DECAY_SCALE = 0.606531
RWKV_GN_EPS = 64e-5

CONV_W = 3

DIL_CFG = ((128, 1), (512, 4), (2048, 16))

CMP_LEN = 32
CMP_STRIDE = 16
CMP_HIDDEN = 256
SEL_BLK = 64
SEL_TOP = 16
SWA_WIN = 512
FORCE_SCORE = 1e4

N_EXPERTS = 32
TOP_K = 4
EXPERT_FF = D_MODEL
SWIGLU_LIMIT = 7.0
SWIGLU_ALPHA = 1.702

ROPE_THETA = 10000.0
NORM_EPS = 1e-6
NEG = -1e30

A_WIDTHS = (GROUP_W, GROUP_W, GROUP_W, RWKV_DECAY_RANK, RWKV_AAA_RANK, RWKV_GATE_RANK)
B_WIDTHS = (GROUP_W, GROUP_W, GROUP_W)
C_WIDTHS = (GROUP_W, GROUP_W, GROUP_W)
D_WIDTHS = (GROUP_W, HEAD_DIM, HEAD_DIM, HEAD_DIM, HEAD_DIM, HEAD_DIM, HEAD_DIM, 3 * N_HEADS)
SECTION_WIDTHS = (sum(A_WIDTHS), sum(B_WIDTHS), sum(C_WIDTHS), sum(D_WIDTHS))
PROJ_W = sum(SECTION_WIDTHS)

_F32 = jnp.float32
_BF16 = jnp.bfloat16
_NT = (((1,), (1,)), ((), ()))
_TN = (((0,), (0,)), ((), ()))


def _dot_hi(a, b, dims=None):
    if dims is None:
        return jnp.dot(a, b, precision=lax.Precision.HIGHEST, preferred_element_type=_F32)
    return lax.dot_general(a, b, dims, precision=lax.Precision.HIGHEST, preferred_element_type=_F32)


def _dot_lo(a, b, dims=None):
    a, b = a.astype(_BF16), b.astype(_BF16)
    if dims is None:
        return jnp.dot(a, b, preferred_element_type=_F32)
    return lax.dot_general(a, b, dims, preferred_element_type=_F32)


def _bf16_parts(a, parts):
    out = []
    for _ in range(parts):
        p = a.astype(_BF16)
        out.append(p)
        a = a - p.astype(_F32)
    return out


def _dot_split(a, m, parts=2):
    mb = m.astype(_BF16)
    return sum(jnp.dot(p, mb, preferred_element_type=_F32) for p in _bf16_parts(a, parts))


def _dot_split_left(m, a, parts=2):
    mb = m.astype(_BF16)
    return sum(jnp.dot(mb, p, preferred_element_type=_F32) for p in _bf16_parts(a, parts))


def _dot_x3(a, b):
    ah, al = _bf16_parts(a, 2)
    bh, bl = _bf16_parts(b, 2)
    d = lambda x, y: jnp.dot(x, y, preferred_element_type=_F32)
    return d(ah, bh) + (d(ah, bl) + d(al, bh))


def _matmul_kernel(x_ref, w_ref, o_ref, *, exact):
    if exact:
        o_ref[...] = _dot_hi(x_ref[...], w_ref[...])
    else:
        o_ref[...] = jnp.dot(x_ref[...].astype(_BF16), w_ref[...], preferred_element_type=_F32)


def matmul(x, w, tm=512, tn=256, exact=False):
    M, K = x.shape
    N = w.shape[1]
    n_pad = -(-N // tn) * tn
    wb = jnp.pad(w if exact else w.astype(_BF16), ((0, 0), (0, n_pad - N)))
    out = pl.pallas_call(
        functools.partial(_matmul_kernel, exact=exact),
        grid=(M // tm, n_pad // tn),
        in_specs=[pl.BlockSpec((tm, K), lambda i, j: (i, 0)),
                  pl.BlockSpec((K, tn), lambda i, j: (0, j))],
        out_specs=pl.BlockSpec((tm, tn), lambda i, j: (i, j)),
        out_shape=jax.ShapeDtypeStruct((M, n_pad), _F32),
        name="matmul",
    )(x, wb)
    return out[:, :N]


PROJ_PAD = -(-PROJ_W // 256) * 256
PROJ_TM = 256
PROJ_TN = 512
ROUTER_PAD = 128
GATE_COL_BLOCK = (PROJ_W - 3 * N_HEADS) // 128
assert GATE_COL_BLOCK * 128 == PROJ_W - 3 * N_HEADS


def _norm_mod(x, g, sc, sh):
    y = x * lax.rsqrt(jnp.mean(x * x, axis=-1, keepdims=True) + NORM_EPS) * g
    return y * (1.0 + sc) + sh


def _in_proj_kernel(x_ref, g_ref, sc_ref, sh_ref, w_ref, o_ref):
    h = _norm_mod(x_ref[0], g_ref[...], sc_ref[0], sh_ref[0]).astype(_BF16)
    for n0 in range(0, PROJ_PAD, PROJ_TN):
        o_ref[0, :, n0:n0 + PROJ_TN] = jnp.dot(h, w_ref[:, n0:n0 + PROJ_TN], preferred_element_type=_F32)


def in_proj_pallas(x, g, sc, sh, w_in):
    B, S, D = x.shape
    tm = PROJ_TM
    w = jnp.pad(w_in.astype(_BF16), ((0, 0), (0, PROJ_PAD - PROJ_W)))
    per_batch = pl.BlockSpec((1, 1, D), lambda b, i: (b, 0, 0))
    return pl.pallas_call(
        _in_proj_kernel,
        grid=(B, S // tm),
        in_specs=[pl.BlockSpec((1, tm, D), lambda b, i: (b, i, 0)),
                  pl.BlockSpec((1, D), lambda b, i: (0, 0)), per_batch, per_batch,
                  pl.BlockSpec((D, PROJ_PAD), lambda b, i: (0, 0))],
        out_specs=pl.BlockSpec((1, tm, PROJ_PAD), lambda b, i: (b, i, 0)),
        out_shape=jax.ShapeDtypeStruct((B, S, PROJ_PAD), _F32),
        compiler_params=pltpu.CompilerParams(dimension_semantics=("parallel", "parallel"),
                                             vmem_limit_bytes=48 * 1024 * 1024),
        name="in_proj",
    )(x, g.reshape(1, D), sc, sh, w)


def _mix_out_kernel(x_ref, ya_ref, bg_ref, cg_ref, xin_ref, cgh_ref, xinh_ref, do0_ref, dl0_ref, do1_ref, dl1_ref,
                    do2_ref, dl2_ref, ocmp_ref, oslc_ref, oswa_ref,
                    gl_ref, convw_ref, ong_ref, wout_ref, gt1_ref, g2_ref, sc2_ref, sh2_ref, wr_ref, br_ref,
                    xo_ref, hb_ref, lg_ref):
    i = pl.program_id(1)
    G = GROUP_W
    u = cg_ref[0] * xin_ref[0]
    halo = jnp.where(i == 0, 0.0, cgh_ref[0] * xinh_ref[0])
    row = lax.broadcasted_iota(jnp.int32, u.shape, 0)
    u1 = jnp.where(row == 0, halo[7:8], pltpu.roll(u, 1, axis=0))
    u2 = jnp.where(row == 0, halo[6:7], jnp.where(row == 1, halo[7:8], pltpu.roll(u, 2, axis=0)))
    cw = convw_ref[...]
    yb = bg_ref[0] * (cw[0:1] * u2 + cw[1:2] * u1 + cw[2:3] * u)

    sg = jax.nn.sigmoid(gl_ref[0])
    er = lax.broadcasted_iota(jnp.int32, (128, G), 0)
    ec = lax.broadcasted_iota(jnp.int32, (128, G), 1) // HEAD_DIM
    yd = None
    for j, o_ref in enumerate((ocmp_ref, oslc_ref, oswa_ref)):
        gate = _dot_split(sg, er == 3 * ec + j)
        yd = gate * o_ref[0] if yd is None else yd + gate * o_ref[0]

    head_sum = _group_indicator(G, HEAD_DIM)
    ong = ong_ref[...]
    parts = [ya_ref[0]]
    dil = ((do0_ref, dl0_ref), (do1_ref, dl1_ref), (do2_ref, dl2_ref))
    top = functools.reduce(jnp.maximum, [l_ref[0] for _, l_ref in dil])
    wts = [jnp.exp2(l_ref[0] - top) for _, l_ref in dil]
    yc = sum(w * o_ref[0] for w, (o_ref, _) in zip(wts, dil)) / sum(wts)
    for n, y in enumerate((yb, yc, yd)):
        ms = _dot_split(y * y, head_sum) * (1.0 / HEAD_DIM)
        parts.append(y * lax.rsqrt(ms + NORM_EPS) * ong[:, n * G:(n + 1) * G])
    mixed = jnp.dot(jnp.concatenate(parts, axis=1).astype(_BF16), wout_ref[...], preferred_element_type=_F32)
    x = x_ref[0] + gt1_ref[0] * mixed
    xo_ref[0] = x
    h = _norm_mod(x, g2_ref[...], sc2_ref[0], sh2_ref[0])
    hb_ref[0] = h.astype(_BF16)
    lg_ref[0] = _dot_x3(h, wr_ref[...]) + br_ref[...]


def mix_out_pallas(x, y_a, proj, dil_outs, o_cmp, o_slc, o_swa, conv_w, onorm_g, w_out, gt1, g2, sc2, sh2, w_router, b_router):
    B, S, D = x.shape
    tm = PROJ_TM
    G = GROUP_W
    b0 = SECTION_WIDTHS[0] // G
    tile = lambda w, col: pl.BlockSpec((1, tm, w), lambda b, i: (b, i, col))
    halo = lambda col: pl.BlockSpec((1, 8, G), lambda b, i: (b, jnp.maximum(i * (tm // 8) - 1, 0), col))
    const = lambda t: pl.BlockSpec(t.shape, lambda b, i: (0,) * t.ndim)
    per_batch = pl.BlockSpec((1, 1, D), lambda b, i: (b, 0, 0))
    wr = jnp.pad(w_router, ((0, 0), (0, ROUTER_PAD - N_EXPERTS)))
    br = jnp.pad(b_router, (0, ROUTER_PAD - N_EXPERTS)).reshape(1, ROUTER_PAD)
    consts = [conv_w, onorm_g.reshape(1, 3 * G), w_out.astype(_BF16)]
    return pl.pallas_call(
        _mix_out_kernel,
        grid=(B, S // tm),
        in_specs=[tile(D, 0), tile(G, 0), tile(G, b0), tile(G, b0 + 1), tile(G, b0 + 2), halo(b0 + 1), halo(b0 + 2),
                  *([tile(G, 0)] * (2 * len(dil_outs) + 3)), tile(128, GATE_COL_BLOCK)]
                 + [const(t) for t in consts] + [per_batch, pl.BlockSpec((1, D), lambda b, i: (0, 0)), per_batch,
                                                 per_batch, const(wr), const(br)],
        out_specs=[tile(D, 0), tile(D, 0), tile(ROUTER_PAD, 0)],
        out_shape=[jax.ShapeDtypeStruct((B, S, D), _F32), jax.ShapeDtypeStruct((B, S, D), _BF16),
                   jax.ShapeDtypeStruct((B, S, ROUTER_PAD), _F32)],
        compiler_params=pltpu.CompilerParams(dimension_semantics=("parallel", "parallel")),
        name="mix_out",
    )(x, y_a, proj, proj, proj, proj, proj, *[t for pair in dil_outs for t in pair], o_cmp, o_slc, o_swa, proj, *consts, gt1, g2.reshape(1, D), sc2, sh2,
      wr, br)


RWKV_CHUNK = 64
RWKV_TILE = 512


def _group_indicator(n, group):
    r = lax.broadcasted_iota(jnp.int32, (n, n), 0) // group
    c = lax.broadcasted_iota(jnp.int32, (n, n), 1) // group
    return r == c


def _rwkv_kernel(za_ref, mu_ref, w0_ref, w2_ref, a0_ref, a2_ref, g2_ref, kk_ref, ka_ref, rk_ref,
                 gnw_ref, gnb_ref, o_ref, prev_sc, h_sc):
    C = RWKV_CHUNK
    TT = RWKV_TILE
    c = pl.program_id(1)

    @pl.when(c == 0)
    def _():
        prev_sc[...] = jnp.zeros_like(prev_sc)
        h_sc[...] = jnp.zeros_like(h_sc)

    z = za_ref[0]
    row = lax.broadcasted_iota(jnp.int32, z.shape, 0)
    zs = jnp.where(row == 0, prev_sc[...], pltpu.roll(z, 1, axis=0))
    prev_sc[...] = z[TT - 1:TT, :]
    z = z + (zs - z) * mu_ref[...]

    G = GROUP_W
    r, k, v = z[:, 0:G], z[:, G:2 * G], z[:, 2 * G:3 * G]
    o = 3 * G
    wd = z[:, o:o + RWKV_DECAY_RANK]
    ad = z[:, o + RWKV_DECAY_RANK:o + RWKV_DECAY_RANK + RWKV_AAA_RANK]
    gd = z[:, o + RWKV_DECAY_RANK + RWKV_AAA_RANK:]

    lw = -RWKV_DECAY_SCALE * jax.nn.sigmoid(w0_ref[...] + _dot_lo(jnp.tanh(wd), w2_ref[...]))
    a = jax.nn.sigmoid(a0_ref[...] + _dot_lo(ad, a2_ref[...]))
    g = _dot_lo(jax.nn.sigmoid(gd), g2_ref[...])

    head_sum = _group_indicator(G, HEAD_DIM)
    kk = k * kk_ref[...]
    kk = kk * lax.rsqrt(_dot_split(kk * kk, head_sum) + 1e-12)
    k = k * (1.0 + (a - 1.0) * ka_ref[...])
    b = kk * a

    ti = lax.broadcasted_iota(jnp.int32, (C, C), 0)
    tj = lax.broadcasted_iota(jnp.int32, (C, C), 1)
    incl = ti >= tj
    strict = ti > tj
    eye = ti == tj
    blk16 = (ti // 16) == (tj // 16)
    blk32 = (ti // 32) == (tj // 32)
    eye_f = eye.astype(_F32)

    ri = lax.broadcasted_iota(jnp.int32, (TT, TT), 0)
    rj = lax.broadcasted_iota(jnp.int32, (TT, TT), 1)
    chunk_tri = (ri >= rj) & (ri // C == rj // C)
    cum = _dot_split_left(chunk_tri, lw, parts=3)
    g_in = jnp.exp(cum)
    A_all = -kk * jnp.exp(cum - lw)
    R_all = r * g_in
    g_inv = jnp.exp(-cum)
    B_all = b * g_inv
    K_all = k * g_inv

    units = [(ci, h) for ci in range(TT // C) for h in range(N_HEADS)]

    def part(t, u):
        ci, h = u
        return t[ci * C:(ci + 1) * C, h * HEAD_DIM:(h + 1) * HEAD_DIM]

    def g_end(u):
        ci, h = u
        return g_in[(ci + 1) * C - 1:(ci + 1) * C, h * HEAD_DIM:(h + 1) * HEAD_DIM]

    A = [part(A_all, u) for u in units]
    R = [part(R_all, u) for u in units]
    B = [part(B_all, u) for u in units]
    Kt = [part(K_all, u) for u in units]
    V = [part(v, u) for u in units]
    n = range(len(units))
    gram = [_dot_lo(jnp.concatenate([A[i], R[i]], axis=0), jnp.concatenate([B[i], Kt[i]], axis=0), _NT) for i in n]
    l_ab = [jnp.where(strict, gram[i][0:C, 0:C], 0.0) for i in n]
    l_ak = [jnp.where(strict, gram[i][0:C, C:2 * C], 0.0) for i in n]
    m_rb = [jnp.where(incl, gram[i][C:2 * C, 0:C], 0.0) for i in n]
    m_rk = [jnp.where(incl, gram[i][C:2 * C, C:2 * C], 0.0) for i in n]
    p = [jnp.where(blk16, l_ab[i], 0.0) for i in n]
    x = [eye_f + p[i] for i in n]
    for _ in range(3):
        p = [_dot_lo(p[i], p[i]) for i in n]
        x = [_dot_lo(x[i], eye_f + p[i]) for i in n]
    for lvl in (blk32 & ~blk16, ~blk32):
        xl = [_dot_lo(x[i], jnp.where(lvl, l_ab[i], 0.0)) for i in n]
        x = [x[i] + _dot_lo(xl[i], x[i]) for i in n]
    lv = [_dot_lo(l_ak[i], V[i]) for i in n]
    tap = [_dot_lo(x[i], jnp.concatenate([A[i], lv[i]], axis=1)) for i in n]
    m1 = [_dot_lo(m_rb[i], tap[i]) for i in n]
    mv = [_dot_lo(m_rk[i], V[i]) for i in n]
    bt = [_dot_lo(B[i] * g_end(units[i]), tap[i], _TN) for i in n]
    kv = [_dot_lo(Kt[i] * g_end(units[i]), V[i], _TN) for i in n]
    w_yh = [jnp.concatenate([R[i] + m1[i][:, 0:HEAD_DIM],
                             jnp.where(eye, g_end(units[i]), 0.0) + bt[i][:, 0:HEAD_DIM]], axis=0) for i in n]
    y0 = [m1[i][:, HEAD_DIM:] + mv[i] for i in n]
    h_add = [bt[i][:, HEAD_DIM:] + kv[i] for i in n]
    state = [h_sc[h] for h in range(N_HEADS)]
    y_rows = []
    for ci in range(TT // C):
        ys = []
        for h in range(N_HEADS):
            i = ci * N_HEADS + h
            nxt = _dot_x3(w_yh[i], state[h])
            ys.append(nxt[0:C] + y0[i])
            state[h] = nxt[C:2 * C] + h_add[i]
        y_rows.append(jnp.concatenate(ys, axis=1))
    for h in range(N_HEADS):
        h_sc[h] = state[h]
    y = jnp.concatenate(y_rows, axis=0)

    mu = _dot_split(y, head_sum) * (1.0 / HEAD_DIM)
    d = y - mu
    var = _dot_split(d * d, head_sum) * (1.0 / HEAD_DIM)
    y = d * lax.rsqrt(var + RWKV_GN_EPS) * gnw_ref[...] + gnb_ref[...]
    y = y + _dot_split(r * k * rk_ref[...], head_sum) * v
    o_ref[0] = y * g


def rwkv7_pallas(za, mu, w0, w2, a0, a2, g2, k_k, k_a, r_k, gn_w, gn_b):
    B, S, _ = za.shape
    W = SECTION_WIDTHS[0]
    TT = RWKV_TILE
    row = lambda t: t.reshape(1, -1).astype(_F32)
    full = lambda t: pl.BlockSpec(t.shape, lambda b, c: (0,) * t.ndim)
    params = [row(mu), row(w0), w2.astype(_BF16), row(a0), a2.astype(_BF16), g2.astype(_BF16),
              row(k_k), row(k_a), row(r_k), row(gn_w), row(gn_b)]
    return pl.pallas_call(
        _rwkv_kernel,
        grid=(B, S // TT),
        in_specs=[pl.BlockSpec((1, TT, W), lambda b, c: (b, c, 0))] + [full(p) for p in params],
        out_specs=pl.BlockSpec((1, TT, GROUP_W), lambda b, c: (b, c, 0)),
        out_shape=jax.ShapeDtypeStruct((B, S, GROUP_W), _F32),
        scratch_shapes=[pltpu.VMEM((1, W), _F32), pltpu.VMEM((N_HEADS, HEAD_DIM, HEAD_DIM), _F32)],
        compiler_params=pltpu.CompilerParams(dimension_semantics=("parallel", "arbitrary")),
        name="rwkv7",
    )(za, *params)


MOE_TM = 512
MOE_FC = 512
MOE_CAST_ROWS = 128


def _moe_ffn_kernel(blk_e_ref, n_used_ref, x_ref, wgu_ref, bgu_ref, wd_ref, bd_ref, o_ref, wgu_sc, wd_sc):
    i = pl.program_id(0)
    F = EXPERT_FF
    new_expert = jnp.logical_or(i == 0, blk_e_ref[i] != blk_e_ref[jnp.maximum(i - 1, 0)])

    @pl.when(new_expert)
    def _():
        def cast(j, carry):
            rows = pl.ds(pl.multiple_of(j * MOE_CAST_ROWS, MOE_CAST_ROWS), MOE_CAST_ROWS)
            wgu_sc[rows, :] = wgu_ref[0, rows, :].astype(_BF16)
            wd_sc[rows, :] = wd_ref[0, rows, :].astype(_BF16)
            return carry
        lax.fori_loop(0, F // MOE_CAST_ROWS, cast, 0)

    @pl.when(i < n_used_ref[0])
    def _():
        x = x_ref[...]
        acc = None
        for c in range(F // MOE_FC):
            lo = c * MOE_FC
            gate = jnp.dot(x, wgu_sc[:, lo:lo + MOE_FC], preferred_element_type=_F32) + bgu_ref[0, :, lo:lo + MOE_FC]
            up = jnp.dot(x, wgu_sc[:, F + lo:F + lo + MOE_FC], preferred_element_type=_F32) + bgu_ref[0, :, F + lo:F + lo + MOE_FC]
            gate = jnp.minimum(gate, SWIGLU_LIMIT)
            up = jnp.clip(up, -SWIGLU_LIMIT, SWIGLU_LIMIT)
            act = gate * jax.nn.sigmoid(SWIGLU_ALPHA * gate) * (up + 1.0)
            part = jnp.dot(act.astype(_BF16), wd_sc[lo:lo + MOE_FC, :], preferred_element_type=_F32)
            acc = part if acc is None else acc + part
        o_ref[...] = (acc + bd_ref[0]).astype(o_ref.dtype)

    @pl.when(i >= n_used_ref[0])
    def _():
        o_ref[...] = jnp.zeros_like(o_ref)


def moe_ffn(xs, blk_e, n_used, layer, w_gu, b_gu, w_down, b_down):
    n_rows, D = xs.shape
    L, E, _, F2 = w_gu.shape
    assert D == EXPERT_FF and F2 == 2 * EXPERT_FF
    n_blk = n_rows // MOE_TM
    grid_spec = pltpu.PrefetchScalarGridSpec(
        num_scalar_prefetch=2,
        grid=(n_blk,),
        in_specs=[pl.BlockSpec((MOE_TM, D), lambda i, e, n: (i, 0)),
                  pl.BlockSpec((None, 1, D, F2), lambda i, e, n: (layer, e[i], 0, 0)),
                  pl.BlockSpec((None, 1, 1, F2), lambda i, e, n: (layer, e[i], 0, 0)),
                  pl.BlockSpec((None, 1, F2 // 2, D), lambda i, e, n: (layer, e[i], 0, 0)),
                  pl.BlockSpec((None, 1, 1, D), lambda i, e, n: (layer, e[i], 0, 0))],
        out_specs=pl.BlockSpec((MOE_TM, D), lambda i, e, n: (i, 0)),
        scratch_shapes=[pltpu.VMEM((D, F2), _BF16), pltpu.VMEM((F2 // 2, D), _BF16)],
    )
    return pl.pallas_call(
        _moe_ffn_kernel,
        grid_spec=grid_spec,
        out_shape=jax.ShapeDtypeStruct((n_rows, D), _BF16),
        compiler_params=pltpu.CompilerParams(dimension_semantics=("arbitrary",),
                                             vmem_limit_bytes=52 * 1024 * 1024),
        name="moe_ffn",
    )(blk_e, n_used, xs, w_gu, b_gu.reshape(L, E, 1, F2), w_down, b_down.reshape(L, E, 1, D))


def moe_pallas(x, gt2, hb, logits, layer, w_gu, b_gu, w_down, b_down):
    D = x.shape[-1]
    hb = hb.reshape(-1, D)
    logits = logits.reshape(hb.shape[0], -1)
    T = hb.shape[0]
    TK = T * TOP_K
    TM = MOE_TM
    E = N_EXPERTS
    i32 = jnp.int32
    top_val, top_idx = lax.top_k(logits[:, :E], TOP_K)
    gates = jax.nn.softmax(top_val, axis=-1)
    e_flat = top_idx.T.reshape(TK).astype(i32)
    counts = jnp.sum((jnp.arange(E, dtype=i32)[:, None] == e_flat[None, :]).astype(i32), axis=1)
    need = jnp.repeat((-counts) % TM, TM)
    d_idx = jnp.arange(E * TM, dtype=i32)
    d_key = jnp.where(d_idx % TM < need, d_idx // TM, E)
    n_rows = TK + E * TM
    keys = jnp.concatenate([e_flat, d_key])
    toks = jnp.concatenate([jnp.arange(TK, dtype=i32) % T, jnp.zeros((E * TM,), i32)])
    rows = jnp.arange(n_rows, dtype=i32)
    s_keys, src_tok, s_slot = lax.sort((keys, toks, rows), num_keys=1)
    _, row_of_slot = lax.sort((s_slot, rows), num_keys=1)
    row_of = row_of_slot[:TK]
    blk_e = s_keys[::TM]
    n_used = jnp.sum((blk_e < E).astype(i32)).reshape(1)
    blk_e = jnp.where(blk_e < E, blk_e, blk_e[jnp.maximum(n_used[0] - 1, 0)])
    xs = jnp.take(hb, src_tok, axis=0, mode="clip")
    ybuf = moe_ffn(xs, blk_e, n_used, layer, w_gu, b_gu, w_down, b_down)
    y = jnp.take(ybuf, row_of, axis=0, mode="clip").reshape(TOP_K, T, D)
    return moe_combine(x, y, gates, gt2)


def _moe_combine_kernel(y_ref, g_ref, x_ref, gt2_ref, o_ref):
    g = g_ref[...]
    acc = None
    for k in range(TOP_K):
        term = y_ref[k].astype(_F32) * g[:, k:k + 1]
        acc = term if acc is None else acc + term
    o_ref[0] = x_ref[0] + gt2_ref[0] * acc


def moe_combine(x, y, gates, gt2):
    B, S, D = x.shape
    tm = PROJ_TM
    nt = S // tm
    g = jnp.pad(gates, ((0, 0), (0, 128 - TOP_K)))
    return pl.pallas_call(
        _moe_combine_kernel,
        grid=(B, nt),
        in_specs=[pl.BlockSpec((TOP_K, tm, D), lambda b, i: (0, b * nt + i, 0)),
                  pl.BlockSpec((tm, 128), lambda b, i: (b * nt + i, 0)),
                  pl.BlockSpec((1, tm, D), lambda b, i: (b, i, 0)),
                  pl.BlockSpec((1, 1, D), lambda b, i: (b, 0, 0))],
        out_specs=pl.BlockSpec((1, tm, D), lambda b, i: (b, i, 0)),
        out_shape=jax.ShapeDtypeStruct((B, S, D), _F32),
        compiler_params=pltpu.CompilerParams(dimension_semantics=("parallel", "parallel")),
        name="moe_combine",
    )(y, g, x, gt2)


ATT_T = 128
ATT_TK = 512
ATT_R = ATT_TK // ATT_T
NSA_NSEL = 64
NSA_TQ = 256


def _window_table(window, weight_of_distance, tq):
    T, TK = ATT_T, ATT_TK
    n = (window + TK) // T + 1
    d = (np.arange(n)[:, None, None] * T + np.arange(tq)[None, None, :] - np.arange(TK)[None, :, None])
    return jnp.asarray(weight_of_distance(d).astype(np.float32))


def _first_key_tile(q_start, window):
    return jnp.maximum(q_start - window, 0) // ATT_TK


def _swa_weight(d):
    return (d >= 0) & (d <= SWA_WIN - 1)


def _softmax_tile(s, w, m_ref, l_ref):
    s = jnp.where(w > 0.0, s, NEG)
    m_prev = m_ref[...]
    m_new = jnp.maximum(m_prev, jnp.max(s, axis=0, keepdims=True))
    alpha = jnp.exp2(m_prev - m_new)
    p = jnp.exp2(s - m_new)
    l_ref[...] = alpha * l_ref[...] + jnp.sum(p, axis=0, keepdims=True)
    m_ref[...] = m_new
    return alpha, p.astype(_BF16)


def _init_stats(m_sc, l_sc, acc_sc):
    m_sc[...] = jnp.full_like(m_sc, NEG)
    l_sc[...] = jnp.zeros_like(l_sc)
    acc_sc[...] = jnp.zeros_like(acc_sc)


def _heads_to_rows(ot, T=ATT_T):
    pairs = []
    for h in range(0, N_HEADS, 2):
        two = jnp.concatenate([ot[:, h * T:(h + 1) * T], ot[:, (h + 1) * T:(h + 2) * T]], axis=0)
        pairs.append(two.T)
    return jnp.concatenate(pairs, axis=1)


def _nsa_attn_kernel(q_ref, ks_ref, vst_ref, kw_ref, vwt_ref, selt_ref, swa_ref, oslc_ref, oswa_ref,
                     m_sc, l_sc, acc_sc):
    T, TK = NSA_TQ, ATT_TK
    qi = pl.program_id(1)
    q = q_ref[0]
    q = jnp.concatenate([q[:, h * HEAD_DIM:(h + 1) * HEAD_DIM] for h in range(N_HEADS)], axis=0)
    selt = selt_ref[0]
    key_blk = lax.broadcasted_iota(jnp.int32, (TK, NSA_NSEL), 0) // SEL_BLK
    blk_id = lax.broadcasted_iota(jnp.int32, (TK, NSA_NSEL), 1)
    kr = lax.broadcasted_iota(jnp.int32, (TK, T), 0)
    qc = lax.broadcasted_iota(jnp.int32, (TK, T), 1)

    def step(k_ref, vt_ref, kt, w):
        base = pl.multiple_of(kt * TK, TK)
        s = lax.dot_general(k_ref[0, pl.ds(base, TK), :], q, _NT, preferred_element_type=_F32)
        alpha, p = _softmax_tile(s, jnp.concatenate([w] * N_HEADS, axis=1), m_sc, l_sc)
        acc_sc[...] = alpha * acc_sc[...] + jnp.dot(vt_ref[0, :, pl.ds(base, TK)], p, preferred_element_type=_F32)

    def finish(o_ref):
        o_ref[0] = _heads_to_rows(acc_sc[...] / l_sc[...], T)

    last = (qi * T + T - 1) // TK
    _init_stats(m_sc, l_sc, acc_sc)

    def slc_step(kt, carry):
        in_blk = (key_blk + kt * (TK // SEL_BLK) == blk_id).astype(_BF16)
        w = jnp.dot(in_blk, selt, preferred_element_type=_F32)
        w = jnp.where(kt * TK + kr <= qi * T + qc, w, 0.0)
        step(ks_ref, vst_ref, kt, w)
        return carry

    lax.fori_loop(0, last + 1, slc_step, 0)
    finish(oslc_ref)

    _init_stats(m_sc, l_sc, acc_sc)

    def swa_step(kt, carry):
        step(kw_ref, vwt_ref, kt, swa_ref[qi * (T // ATT_T) - kt * ATT_R])
        return carry

    lax.fori_loop(_first_key_tile(qi * T, SWA_WIN - 1), last + 1, swa_step, 0)
    finish(oswa_ref)


def nsa_attn_pallas(q_r, ks_r, vs_t, kw_r, vw_t, sel_t):
    B, S, W = q_r.shape
    assert S // SEL_BLK == NSA_NSEL
    T = NSA_TQ
    R = N_HEADS * T
    swa_tab = _window_table(SWA_WIN - 1, _swa_weight, T)
    k_spec = pl.BlockSpec((1, S, HEAD_DIM), lambda b, i: (b, 0, 0))
    vt_spec = pl.BlockSpec((1, HEAD_DIM, S), lambda b, i: (b, 0, 0))
    q_spec = pl.BlockSpec((1, T, W), lambda b, i: (b, i, 0))
    return pl.pallas_call(
        _nsa_attn_kernel,
        grid=(B, S // T),
        in_specs=[q_spec, k_spec, vt_spec, k_spec, vt_spec,
                  pl.BlockSpec((1, NSA_NSEL, T), lambda b, i: (b, 0, i)),
                  pl.BlockSpec(swa_tab.shape, lambda b, i: (0, 0, 0))],
        out_specs=[q_spec, q_spec],
        out_shape=[jax.ShapeDtypeStruct((B, S, W), _F32)] * 2,
        scratch_shapes=[pltpu.VMEM((1, R), _F32), pltpu.VMEM((1, R), _F32), pltpu.VMEM((HEAD_DIM, R), _F32)],
        compiler_params=pltpu.CompilerParams(dimension_semantics=("parallel", "arbitrary")),
        name="nsa_attn",
    )(q_r, ks_r, vs_t, kw_r, vw_t, sel_t, swa_tab)


DIL_TQ = 256
DIL_WIN = DIL_CFG[0][0] // DIL_CFG[0][1]
assert all(w // d == DIL_WIN for w, d in DIL_CFG) and DIL_WIN == ATT_T and DIL_TQ == 2 * ATT_T


def _dil_attn_kernel(q_ref, kp_ref, k0_ref, k1_ref, vp_ref, v0_ref, v1_ref, o_ref, lse_ref):
    TQ, W = DIL_TQ, DIL_WIN
    i = pl.program_id(1)
    k = jnp.concatenate([kp_ref[0], k0_ref[0], k1_ref[0]], axis=0)
    v = jnp.concatenate([vp_ref[0], v0_ref[0], v1_ref[0]], axis=0)
    row = lax.broadcasted_iota(jnp.int32, (W + TQ, TQ), 0)
    col = lax.broadcasted_iota(jnp.int32, (W + TQ, TQ), 1)
    dist = W + col - row
    valid = (dist >= 0) & (dist <= W) & ((i > 0) | (row >= W))
    heads = [slice(h * HEAD_DIM, (h + 1) * HEAD_DIM) for h in range(N_HEADS)]
    ss = [lax.dot_general(k[:, hs], q_ref[0, :, hs], _NT, preferred_element_type=_F32) for hs in heads]
    ss = [jnp.where(valid, s, NEG) for s in ss]
    ms = [jnp.max(s, axis=0, keepdims=True) for s in ss]
    ps = [jnp.exp2(s - m) for s, m in zip(ss, ms)]
    ls = [jnp.sum(p, axis=0, keepdims=True) for p in ps]
    os_ = [lax.dot_general(v[:, hs], p.astype(_BF16), _TN, preferred_element_type=_F32) for hs, p in zip(heads, ps)]
    outs = [o / l for o, l in zip(os_, ls)]
    lses = [jnp.broadcast_to(m + jnp.log2(l), (HEAD_DIM, TQ)) for m, l in zip(ms, ls)]
    for t, ref in ((outs, o_ref), (lses, lse_ref)):
        for half in range(TQ // ATT_T):
            cols = slice(half * ATT_T, (half + 1) * ATT_T)
            ref[0, cols, :] = _heads_to_rows(jnp.concatenate([x[:, cols] for x in t], axis=1))


def dil_attn_pallas(q_r, k_r, v, dil):
    B, S, W = q_r.shape
    Sd = S // dil
    TQ, T = DIL_TQ, ATT_T
    R = TQ // T
    view = lambda t: t.reshape(B, Sd, dil * W)
    q_spec = pl.BlockSpec((1, TQ, W), lambda b, i, r: (b, i, r))
    prev = pl.BlockSpec((1, T, W), lambda b, i, r: (b, jnp.maximum(i * R - 1, 0), r))
    cur = [pl.BlockSpec((1, T, W), functools.partial(lambda b, i, r, j: (b, i * R + j, r), j=j)) for j in range(R)]
    o, lse = pl.pallas_call(
        _dil_attn_kernel,
        grid=(B, Sd // TQ, dil),
        in_specs=[q_spec, prev, *cur, prev, *cur],
        out_specs=[q_spec, q_spec],
        out_shape=[jax.ShapeDtypeStruct((B, Sd, dil * W), _F32)] * 2,
        compiler_params=pltpu.CompilerParams(dimension_semantics=("parallel", "parallel", "parallel")),
        name="dil_attn",
    )(view(q_r), *([view(k_r)] * (R + 1)), *([view(v)] * (R + 1)))
    return o.reshape(B, S, W), lse.reshape(B, S, W)


LOG2_E = 1.4426950408889634
ROPE_LANES = 128
PREP_TM = 256


def _rope_table_kernel(pos_ref, inv_ref, cos_ref, slo_ref, shi_ref):
    ang = pos_ref[...].astype(_F32) * inv_ref[...]
    cos, sin = jnp.cos(ang), jnp.sin(ang)
    first_half = (lax.broadcasted_iota(jnp.int32, ang.shape, 1) % HEAD_DIM) < HEAD_DIM // 2
    cos_ref[...] = cos
    slo_ref[...] = jnp.where(first_half, -sin, 0.0)
    shi_ref[...] = jnp.where(first_half, 0.0, sin)


def rope_tables(positions):
    B, S = positions.shape
    T = B * S
    half = HEAD_DIM // 2
    inv = ROPE_THETA ** (-jnp.arange(half, dtype=_F32) / half)
    inv = jnp.tile(inv, ROPE_LANES // half).reshape(1, ROPE_LANES)
    pos = jnp.broadcast_to(positions.reshape(T, 1), (T, ROPE_LANES))
    tm = 1024
    spec = pl.BlockSpec((tm, ROPE_LANES), lambda i: (i, 0))
    return pl.pallas_call(
        _rope_table_kernel,
        grid=(T // tm,),
        in_specs=[spec, pl.BlockSpec((1, ROPE_LANES), lambda i: (0, 0))],
        out_specs=[spec] * 3,
        out_shape=[jax.ShapeDtypeStruct((T, ROPE_LANES), _F32)] * 3,
        name="rope_tables",
    )(pos, inv)


def _attn_prep_kernel(cq_ref, ck_ref, cv_ref, dq_ref, d1_ref, d2_ref, cos_ref, slo_ref, shi_ref, gains_ref,
                      cqo_ref, cko_ref, cvo_ref, qn_ref, qr_ref, kc_ref, vc_ref, ks_ref, vst_ref, kw_ref, vwt_ref):
    G, hd = GROUP_W, HEAD_DIM
    wide = lambda t: jnp.concatenate([t] * (G // ROPE_LANES), axis=1)
    cos, slo, shi = wide(cos_ref[...]), wide(slo_ref[...]), wide(shi_ref[...])
    head_sum = _group_indicator(G, hd)
    gains = gains_ref[...]
    scale = hd ** -0.5

    def norm(t, n):
        return t * lax.rsqrt(_dot_split(t * t, head_sum) * (1.0 / hd) + NORM_EPS) * gains[n:n + 1]

    def rope(t):
        return t * cos + pltpu.roll(t, G - hd // 2, axis=1) * slo + pltpu.roll(t, hd // 2, axis=1) * shi

    cqo_ref[0] = (rope(norm(cq_ref[0], 0)) * (scale * LOG2_E)).astype(_BF16)
    cko_ref[0] = rope(norm(ck_ref[0], 1)).astype(_BF16)
    cvo_ref[0] = cv_ref[0].astype(_BF16)
    qn = norm(dq_ref[0], 2)
    qn_ref[0] = (qn * scale).astype(_BF16)
    qr_ref[0] = (rope(qn) * (scale * LOG2_E)).astype(_BF16)
    d1 = d1_ref[0]
    d2 = d2_ref[0]
    kc_ref[0] = d1[:, 0:hd]
    vc_ref[0] = d1[:, hd:2 * hd]
    ks_ref[0] = rope(norm(d1, 3))[:, 2 * hd:3 * hd].astype(_BF16)
    kw_ref[0] = rope(norm(d2, 4))[:, 0:hd].astype(_BF16)
    vst_ref[0] = d1.T[3 * hd:4 * hd, :].astype(_BF16)
    vwt_ref[0] = d2.T[hd:2 * hd, :].astype(_BF16)


def attn_prep_pallas(proj, tabs, dil_q_g, dil_k_g, nsa_q_g, nsa_ks_g, nsa_kw_g):
    B, S, _ = proj.shape
    tm = PREP_TM
    G, hd, H = GROUP_W, HEAD_DIM, N_HEADS
    c0 = (SECTION_WIDTHS[0] + SECTION_WIDTHS[1]) // G
    z = jnp.zeros((hd,), _F32)
    gains = jnp.stack([jnp.tile(dil_q_g, H), jnp.tile(dil_k_g, H), jnp.tile(nsa_q_g, H),
                       jnp.concatenate([z, z, nsa_ks_g, z]), jnp.concatenate([nsa_kw_g, z, z, z])])
    col = lambda c: pl.BlockSpec((1, tm, G), lambda b, i: (b, i, c))
    nt = S // tm
    tab = pl.BlockSpec((tm, ROPE_LANES), lambda b, i: (b * nt + i, 0))
    tok = lambda w, dt: (pl.BlockSpec((1, tm, w), lambda b, i: (b, i, 0)), jax.ShapeDtypeStruct((B, S, w), dt))
    tr = (pl.BlockSpec((1, hd, tm), lambda b, i: (b, 0, i)), jax.ShapeDtypeStruct((B, hd, S), _BF16))
    outs = [tok(G, _BF16), tok(G, _BF16), tok(G, _BF16), tok(G, _BF16), tok(G, _BF16), tok(hd, _F32), tok(hd, _F32), tok(hd, _BF16), tr, tok(hd, _BF16), tr]
    return pl.pallas_call(
        _attn_prep_kernel,
        grid=(B, nt),
        in_specs=[col(c0), col(c0 + 1), col(c0 + 2), col(c0 + 3), col(c0 + 4), col(c0 + 5), tab, tab, tab,
                  pl.BlockSpec(gains.shape, lambda b, i: (0, 0))],
        out_specs=[o[0] for o in outs],
        out_shape=[o[1] for o in outs],
        compiler_params=pltpu.CompilerParams(dimension_semantics=("parallel", "parallel")),
        name="attn_prep",
    )(proj, proj, proj, proj, proj, proj, *tabs, gains)


NSA_NC = 256
CMP_CHUNK = CMP_STRIDE * HEAD_DIM


def _nsa_compress_kernel(kc_ref, vc_ref, pek_ref, pev_ref, wk1_ref, wk2_ref, wv1_ref, wv2_ref, g_ref, ko_ref, vo_ref):
    def compress(x_ref, pe_ref, w1_ref, w2_ref):
        x = x_ref[0]
        top = _dot_lo(x + pe_ref[0:1], w1_ref[0:CMP_CHUNK, :])
        bot = _dot_lo(x + pe_ref[1:2], w1_ref[CMP_CHUNK:, :])
        hid = top + pltpu.roll(bot, NSA_NC - 1, axis=0)
        return _dot_lo(jax.nn.gelu(hid), w2_ref[...])

    k = compress(kc_ref, pek_ref, wk1_ref, wk2_ref)
    ko_ref[0] = k * lax.rsqrt(jnp.mean(k * k, axis=-1, keepdims=True) + NORM_EPS) * g_ref[...]
    vo_ref[0] = compress(vc_ref, pev_ref, wv1_ref, wv2_ref)


def nsa_compress_pallas(kc, vc, pe_k, pe_v, wk1, wk2, wv1, wv2, kc_g):
    B, S, hd = kc.shape
    assert S // CMP_STRIDE == NSA_NC
    chunks = lambda t: t.reshape(B, NSA_NC, CMP_CHUNK)
    pe2 = lambda p: p.reshape(2, CMP_CHUNK)
    const = lambda t: pl.BlockSpec(t.shape, lambda b: (0,) * t.ndim)
    x_spec = pl.BlockSpec((1, NSA_NC, CMP_CHUNK), lambda b: (b, 0, 0))
    o_spec = pl.BlockSpec((1, NSA_NC, hd), lambda b: (b, 0, 0))
    params = [pe2(pe_k), pe2(pe_v), wk1.astype(_BF16), wk2.astype(_BF16), wv1.astype(_BF16), wv2.astype(_BF16),
              kc_g.reshape(1, hd)]
    return pl.pallas_call(
        _nsa_compress_kernel,
        grid=(B,),
        in_specs=[x_spec, x_spec] + [const(p) for p in params],
        out_specs=[o_spec, o_spec],
        out_shape=[jax.ShapeDtypeStruct((B, NSA_NC, hd), _F32)] * 2,
        name="nsa_compress",
    )(chunks(kc), chunks(vc), *params)


def selection_overlap(n_c, n_sel):
    r = SEL_BLK // CMP_STRIDE
    m = CMP_LEN // CMP_STRIDE
    diff = np.arange(n_c)[:, None] - r * np.arange(n_sel)[None, :]
    offs = (np.arange(r)[:, None] - np.arange(m)[None, :]).reshape(-1)
    return (diff[..., None] == offs).sum(-1).astype(np.float32)


def _nsa_select_kernel(q_ref, kc_ref, vc_ref, ovt_ref, o_ref, selt_ref):
    T = NSA_TQ
    qi = pl.program_id(1)
    kc = kc_ref[0].astype(_BF16)
    vc = vc_ref[0].astype(_BF16)
    cblk = lax.broadcasted_iota(jnp.int32, (NSA_NC, T), 0)
    tq = qi * T + lax.broadcasted_iota(jnp.int32, (NSA_NC, T), 1)
    valid = cblk * CMP_STRIDE + (CMP_LEN - 1) <= tq
    any_valid = (tq[0:1, :] >= CMP_LEN - 1).astype(_F32)
    heads = range(N_HEADS)
    ss = [lax.dot_general(kc, q_ref[0, :, h * HEAD_DIM:(h + 1) * HEAD_DIM], _NT, preferred_element_type=_F32)
          for h in heads]
    ss = [jnp.where(valid, s, NEG) for s in ss]
    es = [jnp.exp(s - jnp.max(s, axis=0, keepdims=True)) for s in ss]
    ps = [e * (any_valid / jnp.sum(e, axis=0, keepdims=True)) for e in es]
    outs = [lax.dot_general(vc, p.astype(_BF16), _TN, preferred_element_type=_F32) for p in ps]
    p_sum = sum(ps[1:], ps[0])
    o_ref[0] = _heads_to_rows(jnp.concatenate(outs, axis=1), T)

    ovt = ovt_ref[...]
    hi = p_sum.astype(_BF16)
    lo = (p_sum - hi.astype(_F32)).astype(_BF16)
    imp = jnp.dot(ovt, hi, preferred_element_type=_F32) + jnp.dot(ovt, lo, preferred_element_type=_F32)
    jb = lax.broadcasted_iota(jnp.int32, (NSA_NSEL, T), 0)
    cur = (qi * T + lax.broadcasted_iota(jnp.int32, (NSA_NSEL, T), 1)) // SEL_BLK
    forced = (jb == 0) | (jb == cur) | (jb == cur - 1)
    score = jnp.where(jb > cur, -1.0, jnp.where(forced, FORCE_SCORE, imp))
    jbf = jb.astype(_F32)
    sel = jnp.zeros((NSA_NSEL, T), _F32)
    for _ in range(min(SEL_TOP, NSA_NSEL)):
        best = jnp.max(score, axis=0, keepdims=True)
        first = jnp.min(jnp.where(score == best, jbf, float(NSA_NSEL)), axis=0, keepdims=True)
        pick = jbf == first
        sel = jnp.where(pick, 1.0, sel)
        score = jnp.where(pick, NEG, score)
    selt_ref[0] = sel.astype(_BF16)


def nsa_select_pallas(q_n, k_cmp, v_cmp):
    B, S, W = q_n.shape
    T = NSA_TQ
    n_c = (S - CMP_LEN) // CMP_STRIDE + 1
    ovt = np.zeros((NSA_NSEL, NSA_NC), np.float32)
    ovt[:, :n_c] = selection_overlap(n_c, NSA_NSEL).T
    ovt = jnp.asarray(ovt, _BF16)
    c_spec = pl.BlockSpec((1, NSA_NC, HEAD_DIM), lambda b, i: (b, 0, 0))
    return pl.pallas_call(
        _nsa_select_kernel,
        grid=(B, S // T),
        in_specs=[pl.BlockSpec((1, T, W), lambda b, i: (b, i, 0)), c_spec, c_spec,
                  pl.BlockSpec(ovt.shape, lambda b, i: (0, 0))],
        out_specs=[pl.BlockSpec((1, T, W), lambda b, i: (b, i, 0)),
                   pl.BlockSpec((1, NSA_NSEL, T), lambda b, i: (b, 0, i))],
        out_shape=[jax.ShapeDtypeStruct((B, S, W), _F32), jax.ShapeDtypeStruct((B, NSA_NSEL, S), _BF16)],
        compiler_params=pltpu.CompilerParams(dimension_semantics=("parallel", "parallel")),
        name="nsa_select",
    )(q_n, k_cmp, v_cmp, ovt)


def kernel(x, c, positions, w_ada, b_ada, norm1_g, norm2_g, w_in, w_out, rwkv_mu, rwkv_w0, rwkv_w2, rwkv_a0, rwkv_a2, rwkv_g2, rwkv_kk, rwkv_ka, rwkv_rk, rwkv_gn_w, rwkv_gn_b, conv_w, dil_q_g, dil_k_g, nsa_q_g, nsa_kc_g, nsa_ks_g, nsa_kw_g, nsa_pe_k, nsa_pe_v, nsa_wk1, nsa_wk2, nsa_wv1, nsa_wv2, onorm_g, w_router, b_router, w_gu, b_gu, w_down, b_down):
    B, S, D = x.shape
    tabs = rope_tables(positions)
    for l in range(DEPTH):
        mod = matmul(jax.nn.silu(c), w_ada[l], tm=B, tn=512) + b_ada[l]
        sh1, sc1, gt1, sh2, sc2, gt2 = [m[:, None, :] for m in jnp.split(mod, 6, axis=-1)]
        proj = in_proj_pallas(x, norm1_g[l], sc1, sh1, w_in[l])
        y_a = rwkv7_pallas(proj, rwkv_mu[l], rwkv_w0[l], rwkv_w2[l], rwkv_a0[l], rwkv_a2[l], rwkv_g2[l],
                           rwkv_kk[l], rwkv_ka[l], rwkv_rk[l], rwkv_gn_w[l], rwkv_gn_b[l])
        cq, ck, cv, q_n, q_r, kc, vc, ks, vs_t, kw, vw_t = attn_prep_pallas(
            proj, tabs, dil_q_g[l], dil_k_g[l], nsa_q_g[l], nsa_ks_g[l], nsa_kw_g[l])
        dil_outs = [dil_attn_pallas(cq, ck, cv, dil) for _, dil in DIL_CFG]
        k_cmp, v_cmp = nsa_compress_pallas(kc, vc, nsa_pe_k[l], nsa_pe_v[l], nsa_wk1[l], nsa_wk2[l], nsa_wv1[l],
                                           nsa_wv2[l], nsa_kc_g[l])
        o_cmp, sel_t = nsa_select_pallas(q_n, k_cmp, v_cmp)
        o_slc, o_swa = nsa_attn_pallas(q_r, ks, vs_t, kw, vw_t, sel_t)
        x, hb, logits = mix_out_pallas(x, y_a, proj, dil_outs, o_cmp, o_slc, o_swa, conv_w[l], onorm_g[l], w_out[l], gt1,
                                       norm2_g[l], sc2, sh2, w_router[l], b_router[l])
        x = moe_pallas(x, gt2, hb, logits, l, w_gu, b_gu, w_down, b_down)
    return x
```
